```python
import jax, jax.numpy as jnp
from jax import lax
import numpy as np

D_MODEL = 1024
BATCH = 8
SEQ = 8192
DEPTH = 2

N_MIXERS = 2
CONV_WIDTH = 31
POOL_WINDOWS = (2, 4, 8, 16)
N_POOL_GROUPS = len(POOL_WINDOWS)
POOL_GROUP_DIM = D_MODEL // N_POOL_GROUPS
D_FF = ((8 * D_MODEL // 3 + 255) // 256) * 256
N_ADA = 6
EPS = 1e-6
N_CONV_LAYERS = (DEPTH + 1) // 2
N_POOL_LAYERS = DEPTH // 2

kernel_name = "hybrid_conformer_conv_multiscale_pool_trunk"


def rms_norm(x, g):
    xf = x.astype(jnp.float32)
    y = xf * lax.rsqrt(jnp.mean(xf * xf, axis=-1, keepdims=True) + EPS)
    return (y * g.astype(jnp.float32)).astype(x.dtype)


def layer_norm(x, g, b):
    xf = x.astype(jnp.float32)
    mu = jnp.mean(xf, axis=-1, keepdims=True)
    var = jnp.mean(jnp.square(xf - mu), axis=-1, keepdims=True)
    y = (xf - mu) * lax.rsqrt(var + EPS)
    return (y * g.astype(jnp.float32) + b.astype(jnp.float32)).astype(x.dtype)


def modulate(h, shift, scale):
    return h * (1.0 + scale[:, None, :]) + shift[:, None, :]


def conformer_conv(h, w1, b1, w_dw, b_dw, ln_g, ln_b, w2, b2):
    u = h @ w1 + b1
    a, g = jnp.split(u, 2, axis=-1)
    u = a * jax.nn.sigmoid(g)
    u = lax.conv_general_dilated(
        u, w_dw[:, None, :].astype(u.dtype),
        window_strides=(1,), padding=[(CONV_WIDTH - 1, 0)],
        dimension_numbers=("NWC", "WIO", "NWC"),
        feature_group_count=D_MODEL) + b_dw
    u = jax.nn.silu(layer_norm(u, ln_g, ln_b))
    return u @ w2 + b2


def multiscale_pool(h, w_grp, ls):
    B, S, D = h.shape
    hf = h.astype(jnp.float32).reshape(B, S, N_POOL_GROUPS, POOL_GROUP_DIM)
    cs = jnp.cumsum(hf, axis=1)
    cs = jnp.concatenate([jnp.zeros_like(cs[:, :1]), cs], axis=1)
    t = jnp.arange(S)
    pooled = []
    for gi, w in enumerate(POOL_WINDOWS):
        lo = jnp.maximum(t + 1 - w, 0)
        win_sum = cs[:, t + 1, gi] - cs[:, lo, gi]
        cnt = (t + 1 - lo).astype(jnp.float32)
        pooled.append(win_sum / cnt[None, :, None])
    pooled = jnp.stack(pooled, axis=2)
    mixed = (pooled - hf).astype(h.dtype)
    y = jnp.einsum("bsgc,gcd->bsgd", mixed, w_grp).reshape(B, S, D)
    return y * ls


def swiglu_ffn(h, w_gate, w_up, w_down):
    return (jax.nn.silu(h @ w_gate) * (h @ w_up)) @ w_down


def _fwd_setup_inputs(seed: int = 0) -> dict:
    key = jax.random.key(seed)
    ks = iter(jax.random.split(key, 32))
    D, F = D_MODEL, D_FF
    nrm = lambda shape, s: jax.random.normal(next(ks), shape, jnp.float32) * s
    return {
        "x": nrm((BATCH, SEQ, D), 1.0),
        "c": nrm((BATCH, D), 1.0),
        "ada_w": nrm((DEPTH, D, N_ADA * D), 0.5 * D ** -0.5),
        "ada_b": nrm((DEPTH, N_ADA * D), 0.01),
        "norm_mix_g": 1.0 + nrm((DEPTH, D), 0.02),
        "norm_ffn_g": 1.0 + nrm((DEPTH, D), 0.02),
        "conv_w1": nrm((N_CONV_LAYERS, D, 2 * D), D ** -0.5),
        "conv_b1": nrm((N_CONV_LAYERS, 2 * D), 0.01),
        "conv_wdw": nrm((N_CONV_LAYERS, CONV_WIDTH, D), CONV_WIDTH ** -0.5),
        "conv_bdw": nrm((N_CONV_LAYERS, D), 0.01),
        "conv_ln_g": 1.0 + nrm((N_CONV_LAYERS, D), 0.02),
        "conv_ln_b": nrm((N_CONV_LAYERS, D), 0.01),
        "conv_w2": nrm((N_CONV_LAYERS, D, D), D ** -0.5),
        "conv_b2": nrm((N_CONV_LAYERS, D), 0.01),
        "pool_w": nrm((N_POOL_LAYERS, N_POOL_GROUPS, POOL_GROUP_DIM, POOL_GROUP_DIM), POOL_GROUP_DIM ** -0.5),
        "pool_ls": 1.0 + nrm((N_POOL_LAYERS, D), 0.02),
        "ffn_w_gate": nrm((DEPTH, D, F), D ** -0.5),
        "ffn_w_up": nrm((DEPTH, D, F), D ** -0.5),
        "ffn_w_down": nrm((DEPTH, F, D), F ** -0.5),
        "final_g": 1.0 + nrm((D,), 0.02),
    }


def _fwd_reference(x, c, ada_w, ada_b, norm_mix_g, norm_ffn_g,
              conv_w1, conv_b1, conv_wdw, conv_bdw, conv_ln_g, conv_ln_b,
              conv_w2, conv_b2, pool_w, pool_ls,
              ffn_w_gate, ffn_w_up, ffn_w_down, final_g):
    c_act = jax.nn.silu(c)
    for i in range(DEPTH):
        mod = c_act @ ada_w[i] + ada_b[i]
        sh_m, sc_m, g_m, sh_f, sc_f, g_f = jnp.split(mod, N_ADA, axis=-1)
        h = modulate(rms_norm(x, norm_mix_g[i]), sh_m, sc_m)
        j = i // N_MIXERS
        if i % N_MIXERS == 0:
            y = conformer_conv(h, conv_w1[j], conv_b1[j], conv_wdw[j], conv_bdw[j],
                               conv_ln_g[j], conv_ln_b[j], conv_w2[j], conv_b2[j])
        else:
            y = multiscale_pool(h, pool_w[j], pool_ls[j])
        x = x + (1.0 + g_m)[:, None, :] * y
        h = modulate(rms_norm(x, norm_ffn_g[i]), sh_f, sc_f)
        y = swiglu_ffn(h, ffn_w_gate[i], ffn_w_up[i], ffn_w_down[i])
        x = x + (1.0 + g_f)[:, None, :] * y
    return rms_norm(x, final_g)


import jax as _jax
import jax.numpy as _jnp

TWIN_FORMAT = 'train_step'
FWD_PARAMS = ['x', 'c', 'ada_w', 'ada_b', 'norm_mix_g', 'norm_ffn_g', 'conv_w1', 'conv_b1', 'conv_wdw', 'conv_bdw', 'conv_ln_g', 'conv_ln_b', 'conv_w2', 'conv_b2', 'pool_w', 'pool_ls', 'ffn_w_gate', 'ffn_w_up', 'ffn_w_down', 'final_g']
TWIN_WEIGHTS = ['ada_w', 'ada_b', 'norm_mix_g', 'norm_ffn_g', 'conv_w1', 'conv_b1', 'conv_wdw', 'conv_bdw', 'conv_ln_g', 'conv_ln_b', 'conv_w2', 'conv_b2', 'pool_w', 'pool_ls', 'ffn_w_gate', 'ffn_w_up', 'ffn_w_down', 'final_g']
TWIN_DIFF_INPUT = 'x'
TWIN_INPUTS = ['x', 'c', 'ada_w', 'ada_b', 'norm_mix_g', 'norm_ffn_g', 'conv_w1', 'conv_b1', 'conv_wdw', 'conv_bdw', 'conv_ln_g', 'conv_ln_b', 'conv_w2', 'conv_b2', 'pool_w', 'pool_ls', 'ffn_w_gate', 'ffn_w_up', 'ffn_w_down', 'final_g', 'loss_target', 'm_ada_w', 'm_ada_b', 'm_norm_mix_g', 'm_norm_ffn_g', 'm_conv_w1', 'm_conv_b1', 'm_conv_wdw', 'm_conv_bdw', 'm_conv_ln_g', 'm_conv_ln_b', 'm_conv_w2', 'm_conv_b2', 'm_pool_w', 'm_pool_ls', 'm_ffn_w_gate', 'm_ffn_w_up', 'm_ffn_w_down', 'm_final_g', 'v_ada_w', 'v_ada_b', 'v_norm_mix_g', 'v_norm_ffn_g', 'v_conv_w1', 'v_conv_b1', 'v_conv_wdw', 'v_conv_bdw', 'v_conv_ln_g', 'v_conv_ln_b', 'v_conv_w2', 'v_conv_b2', 'v_pool_w', 'v_pool_ls', 'v_ffn_w_gate', 'v_ffn_w_up', 'v_ffn_w_down', 'v_final_g']
TWIN_OUTPUTS = ['loss', 'grad_x', 'grad_ada_w', 'grad_ada_b', 'grad_norm_mix_g', 'grad_norm_ffn_g', 'grad_conv_w1', 'grad_conv_b1', 'grad_conv_wdw', 'grad_conv_bdw', 'grad_conv_ln_g', 'grad_conv_ln_b', 'grad_conv_w2', 'grad_conv_b2', 'grad_pool_w', 'grad_pool_ls', 'grad_ffn_w_gate', 'grad_ffn_w_up', 'grad_ffn_w_down', 'grad_final_g', 'delta_ada_w', 'delta_ada_b', 'delta_norm_mix_g', 'delta_norm_ffn_g', 'delta_conv_w1', 'delta_conv_b1', 'delta_conv_wdw', 'delta_conv_bdw', 'delta_conv_ln_g', 'delta_conv_ln_b', 'delta_conv_w2', 'delta_conv_b2', 'delta_pool_w', 'delta_pool_ls', 'delta_ffn_w_gate', 'delta_ffn_w_up', 'delta_ffn_w_down', 'delta_final_g', 'new_m_ada_w', 'new_m_ada_b', 'new_m_norm_mix_g', 'new_m_norm_ffn_g', 'new_m_conv_w1', 'new_m_conv_b1', 'new_m_conv_wdw', 'new_m_conv_bdw', 'new_m_conv_ln_g', 'new_m_conv_ln_b', 'new_m_conv_w2', 'new_m_conv_b2', 'new_m_pool_w', 'new_m_pool_ls', 'new_m_ffn_w_gate', 'new_m_ffn_w_up', 'new_m_ffn_w_down', 'new_m_final_g', 'new_v_ada_w', 'new_v_ada_b', 'new_v_norm_mix_g', 'new_v_norm_ffn_g', 'new_v_conv_w1', 'new_v_conv_b1', 'new_v_conv_wdw', 'new_v_conv_bdw', 'new_v_conv_ln_g', 'new_v_conv_ln_b', 'new_v_conv_w2', 'new_v_conv_b2', 'new_v_pool_w', 'new_v_pool_ls', 'new_v_ffn_w_gate', 'new_v_ffn_w_up', 'new_v_ffn_w_down', 'new_v_final_g']
TWIN_LEAF_KINDS = {'loss': 'loss', 'grad_x': 'grad_x', 'grad_ada_w': 'grad_w', 'grad_ada_b': 'grad_w', 'grad_norm_mix_g': 'grad_w', 'grad_norm_ffn_g': 'grad_w', 'grad_conv_w1': 'grad_w', 'grad_conv_b1': 'grad_w', 'grad_conv_wdw': 'grad_w', 'grad_conv_bdw': 'grad_w', 'grad_conv_ln_g': 'grad_w', 'grad_conv_ln_b': 'grad_w', 'grad_conv_w2': 'grad_w', 'grad_conv_b2': 'grad_w', 'grad_pool_w': 'grad_w', 'grad_pool_ls': 'grad_w', 'grad_ffn_w_gate': 'grad_w', 'grad_ffn_w_up': 'grad_w', 'grad_ffn_w_down': 'grad_w', 'grad_final_g': 'grad_w', 'delta_ada_w': 'delta_w', 'delta_ada_b': 'delta_w', 'delta_norm_mix_g': 'delta_w', 'delta_norm_ffn_g': 'delta_w', 'delta_conv_w1': 'delta_w', 'delta_conv_b1': 'delta_w', 'delta_conv_wdw': 'delta_w', 'delta_conv_bdw': 'delta_w', 'delta_conv_ln_g': 'delta_w', 'delta_conv_ln_b': 'delta_w', 'delta_conv_w2': 'delta_w', 'delta_conv_b2': 'delta_w', 'delta_pool_w': 'delta_w', 'delta_pool_ls': 'delta_w', 'delta_ffn_w_gate': 'delta_w', 'delta_ffn_w_up': 'delta_w', 'delta_ffn_w_down': 'delta_w', 'delta_final_g': 'delta_w', 'new_m_ada_w': 'new_m', 'new_m_ada_b': 'new_m', 'new_m_norm_mix_g': 'new_m', 'new_m_norm_ffn_g': 'new_m', 'new_m_conv_w1': 'new_m', 'new_m_conv_b1': 'new_m', 'new_m_conv_wdw': 'new_m', 'new_m_conv_bdw': 'new_m', 'new_m_conv_ln_g': 'new_m', 'new_m_conv_ln_b': 'new_m', 'new_m_conv_w2': 'new_m', 'new_m_conv_b2': 'new_m', 'new_m_pool_w': 'new_m', 'new_m_pool_ls': 'new_m', 'new_m_ffn_w_gate': 'new_m', 'new_m_ffn_w_up': 'new_m', 'new_m_ffn_w_down': 'new_m', 'new_m_final_g': 'new_m', 'new_v_ada_w': 'new_v', 'new_v_ada_b': 'new_v', 'new_v_norm_mix_g': 'new_v', 'new_v_norm_ffn_g': 'new_v', 'new_v_conv_w1': 'new_v', 'new_v_conv_b1': 'new_v', 'new_v_conv_wdw': 'new_v', 'new_v_conv_bdw': 'new_v', 'new_v_conv_ln_g': 'new_v', 'new_v_conv_ln_b': 'new_v', 'new_v_conv_w2': 'new_v', 'new_v_conv_b2': 'new_v', 'new_v_pool_w': 'new_v', 'new_v_pool_ls': 'new_v', 'new_v_ffn_w_gate': 'new_v', 'new_v_ffn_w_up': 'new_v', 'new_v_ffn_w_down': 'new_v', 'new_v_final_g': 'new_v'}


def _forward(args):
    return _fwd_reference(*[args[k] for k in FWD_PARAMS])


def _output_shape():
    def fwd():
        inp = _fwd_setup_inputs(0)
        return _fwd_reference(*[inp[k] for k in FWD_PARAMS])
    out = _jax.eval_shape(fwd)
    return out.shape, out.dtype

N_MICROBATCH = 1
ADAM_LR = 0.001
ADAM_B1 = 0.9
ADAM_B2 = 0.999
ADAM_EPS = 1e-08
ADAM_WD = 0.01
ADAM_STEP = 10
PER_EXAMPLE_BATCH_AXIS = {'x': 0, 'c': 0, 'loss_target': 0}
SHARED_INPUTS = []
_WEIGHT_DTYPES = {'ada_w': _jnp.float32, 'ada_b': _jnp.float32, 'norm_mix_g': _jnp.float32, 'norm_ffn_g': _jnp.float32, 'conv_w1': _jnp.float32, 'conv_b1': _jnp.float32, 'conv_wdw': _jnp.float32, 'conv_bdw': _jnp.float32, 'conv_ln_g': _jnp.float32, 'conv_ln_b': _jnp.float32, 'conv_w2': _jnp.float32, 'conv_b2': _jnp.float32, 'pool_w': _jnp.float32, 'pool_ls': _jnp.float32, 'ffn_w_gate': _jnp.float32, 'ffn_w_up': _jnp.float32, 'ffn_w_down': _jnp.float32, 'final_g': _jnp.float32}
MOMENT_SCALE = {'ada_w': 9.714234e-02, 'ada_b': 2.035678e-01, 'norm_mix_g': 1.745152e-01, 'norm_ffn_g': 1.783142e-01, 'conv_w1': 1.277553e-01, 'conv_b1': 1.126151e-01, 'conv_wdw': 1.679827e-01, 'conv_bdw': 2.958716e-01, 'conv_ln_g': 1.938702e-01, 'conv_ln_b': 1.552078e-01, 'conv_w2': 1.606217e-01, 'conv_b2': 2.620817e-01, 'pool_w': 1.528608e-01, 'pool_ls': 5.489703e-01, 'ffn_w_gate': 7.824812e-02, 'ffn_w_up': 7.599498e-02, 'ffn_w_down': 1.255941e-01, 'final_g': 6.449961e+01}


def _to_microbatches(a, axis):
    t = _jnp.moveaxis(a, axis, 0)
    t = t.reshape((N_MICROBATCH, t.shape[0] // N_MICROBATCH) + t.shape[1:])
    return _jnp.moveaxis(t, 1, axis + 1)


def setup_inputs(seed: int = 0) -> dict:
    inp = _fwd_setup_inputs(seed)
    key = _jax.random.fold_in(_jax.random.key(seed), 7919)
    shape, _ = _output_shape()
    out = dict(inp)
    out["loss_target"] = _jax.random.normal(_jax.random.fold_in(key, 0), shape, _jnp.float32)
    for i, name in enumerate(TWIN_WEIGHTS):
        w = inp[name].astype(_jnp.float32)
        if MOMENT_SCALE is None:
            s = _jnp.sqrt(_jnp.mean(_jnp.square(w)) + 1e-30)
        else:
            s = MOMENT_SCALE[name]
        km, kv = _jax.random.split(_jax.random.fold_in(key, i + 1))
        out[name] = w
        out["m_" + name] = s * _jax.random.normal(km, w.shape, _jnp.float32)
        out["v_" + name] = (s * s) * _jax.random.uniform(kv, w.shape, _jnp.float32, 0.5, 1.5)
    if N_MICROBATCH > 1:
        for name, axis in PER_EXAMPLE_BATCH_AXIS.items():
            out[name] = _to_microbatches(out[name], axis)
    return {'x': out['x'], 'c': out['c'], 'ada_w': out['ada_w'], 'ada_b': out['ada_b'], 'norm_mix_g': out['norm_mix_g'], 'norm_ffn_g': out['norm_ffn_g'], 'conv_w1': out['conv_w1'], 'conv_b1': out['conv_b1'], 'conv_wdw': out['conv_wdw'], 'conv_bdw': out['conv_bdw'], 'conv_ln_g': out['conv_ln_g'], 'conv_ln_b': out['conv_ln_b'], 'conv_w2': out['conv_w2'], 'conv_b2': out['conv_b2'], 'pool_w': out['pool_w'], 'pool_ls': out['pool_ls'], 'ffn_w_gate': out['ffn_w_gate'], 'ffn_w_up': out['ffn_w_up'], 'ffn_w_down': out['ffn_w_down'], 'final_g': out['final_g'], 'loss_target': out['loss_target'], 'm_ada_w': out['m_ada_w'], 'm_ada_b': out['m_ada_b'], 'm_norm_mix_g': out['m_norm_mix_g'], 'm_norm_ffn_g': out['m_norm_ffn_g'], 'm_conv_w1': out['m_conv_w1'], 'm_conv_b1': out['m_conv_b1'], 'm_conv_wdw': out['m_conv_wdw'], 'm_conv_bdw': out['m_conv_bdw'], 'm_conv_ln_g': out['m_conv_ln_g'], 'm_conv_ln_b': out['m_conv_ln_b'], 'm_conv_w2': out['m_conv_w2'], 'm_conv_b2': out['m_conv_b2'], 'm_pool_w': out['m_pool_w'], 'm_pool_ls': out['m_pool_ls'], 'm_ffn_w_gate': out['m_ffn_w_gate'], 'm_ffn_w_up': out['m_ffn_w_up'], 'm_ffn_w_down': out['m_ffn_w_down'], 'm_final_g': out['m_final_g'], 'v_ada_w': out['v_ada_w'], 'v_ada_b': out['v_ada_b'], 'v_norm_mix_g': out['v_norm_mix_g'], 'v_norm_ffn_g': out['v_norm_ffn_g'], 'v_conv_w1': out['v_conv_w1'], 'v_conv_b1': out['v_conv_b1'], 'v_conv_wdw': out['v_conv_wdw'], 'v_conv_bdw': out['v_conv_bdw'], 'v_conv_ln_g': out['v_conv_ln_g'], 'v_conv_ln_b': out['v_conv_ln_b'], 'v_conv_w2': out['v_conv_w2'], 'v_conv_b2': out['v_conv_b2'], 'v_pool_w': out['v_pool_w'], 'v_pool_ls': out['v_pool_ls'], 'v_ffn_w_gate': out['v_ffn_w_gate'], 'v_ffn_w_up': out['v_ffn_w_up'], 'v_ffn_w_down': out['v_ffn_w_down'], 'v_final_g': out['v_final_g']}


def _loss(weights, diff, rest, loss_target):
    with _jax.named_scope("forward"):
        args = {**rest, TWIN_DIFF_INPUT: diff, **{k: w.astype(_WEIGHT_DTYPES[k]) for k, w in weights.items()}}
        y = _forward(args)
    with _jax.named_scope("loss_head"):
        err = _jnp.square(y.astype(_jnp.float32) - loss_target)
        return 0.5 * _jnp.sum(_jnp.mean(err, axis=-1)) if err.ndim else 0.5 * err


def _adamw(w, g, m, v):
    m = ADAM_B1 * m + (1.0 - ADAM_B1) * g
    v = ADAM_B2 * v + (1.0 - ADAM_B2) * _jnp.square(g)
    m_hat = m / (1.0 - ADAM_B1 ** ADAM_STEP)
    v_hat = v / (1.0 - ADAM_B2 ** ADAM_STEP)
    delta = -ADAM_LR * (m_hat / (_jnp.sqrt(v_hat) + ADAM_EPS) + ADAM_WD * w)
    return delta, m, v


def reference(x, c, ada_w, ada_b, norm_mix_g, norm_ffn_g, conv_w1, conv_b1, conv_wdw, conv_bdw, conv_ln_g, conv_ln_b, conv_w2, conv_b2, pool_w, pool_ls, ffn_w_gate, ffn_w_up, ffn_w_down, final_g, loss_target, m_ada_w, m_ada_b, m_norm_mix_g, m_norm_ffn_g, m_conv_w1, m_conv_b1, m_conv_wdw, m_conv_bdw, m_conv_ln_g, m_conv_ln_b, m_conv_w2, m_conv_b2, m_pool_w, m_pool_ls, m_ffn_w_gate, m_ffn_w_up, m_ffn_w_down, m_final_g, v_ada_w, v_ada_b, v_norm_mix_g, v_norm_ffn_g, v_conv_w1, v_conv_b1, v_conv_wdw, v_conv_bdw, v_conv_ln_g, v_conv_ln_b, v_conv_w2, v_conv_b2, v_pool_w, v_pool_ls, v_ffn_w_gate, v_ffn_w_up, v_ffn_w_down, v_final_g):
    given = dict(x=x, c=c, ada_w=ada_w, ada_b=ada_b, norm_mix_g=norm_mix_g, norm_ffn_g=norm_ffn_g, conv_w1=conv_w1, conv_b1=conv_b1, conv_wdw=conv_wdw, conv_bdw=conv_bdw, conv_ln_g=conv_ln_g, conv_ln_b=conv_ln_b, conv_w2=conv_w2, conv_b2=conv_b2, pool_w=pool_w, pool_ls=pool_ls, ffn_w_gate=ffn_w_gate, ffn_w_up=ffn_w_up, ffn_w_down=ffn_w_down, final_g=final_g, loss_target=loss_target, m_ada_w=m_ada_w, m_ada_b=m_ada_b, m_norm_mix_g=m_norm_mix_g, m_norm_ffn_g=m_norm_ffn_g, m_conv_w1=m_conv_w1, m_conv_b1=m_conv_b1, m_conv_wdw=m_conv_wdw, m_conv_bdw=m_conv_bdw, m_conv_ln_g=m_conv_ln_g, m_conv_ln_b=m_conv_ln_b, m_conv_w2=m_conv_w2, m_conv_b2=m_conv_b2, m_pool_w=m_pool_w, m_pool_ls=m_pool_ls, m_ffn_w_gate=m_ffn_w_gate, m_ffn_w_up=m_ffn_w_up, m_ffn_w_down=m_ffn_w_down, m_final_g=m_final_g, v_ada_w=v_ada_w, v_ada_b=v_ada_b, v_norm_mix_g=v_norm_mix_g, v_norm_ffn_g=v_norm_ffn_g, v_conv_w1=v_conv_w1, v_conv_b1=v_conv_b1, v_conv_wdw=v_conv_wdw, v_conv_bdw=v_conv_bdw, v_conv_ln_g=v_conv_ln_g, v_conv_ln_b=v_conv_ln_b, v_conv_w2=v_conv_w2, v_conv_b2=v_conv_b2, v_pool_w=v_pool_w, v_pool_ls=v_pool_ls, v_ffn_w_gate=v_ffn_w_gate, v_ffn_w_up=v_ffn_w_up, v_ffn_w_down=v_ffn_w_down, v_final_g=v_final_g)
    weights = {n: given[n] for n in TWIN_WEIGHTS}
    shared = {n: given[n] for n in SHARED_INPUTS}
    per_example = {n: given[n] for n in ['x', 'c']}
    grad_fn = _jax.value_and_grad(_loss, argnums=(0, 1))

    def one_microbatch(ex, loss_target):
        ex = dict(ex)
        diff = ex.pop(TWIN_DIFF_INPUT)
        return grad_fn(weights, diff, {**shared, **ex}, loss_target)

    if N_MICROBATCH == 1:
        loss, (grad_w, grad_x) = one_microbatch(per_example, given["loss_target"])
    else:
        def body(carry, xs):
            loss_sum, grad_sum = carry
            l_k, (gw_k, gx_k) = one_microbatch(xs[0], xs[1])
            with _jax.named_scope("update"):
                return (loss_sum + l_k, _jax.tree.map(_jnp.add, grad_sum, gw_k)), gx_k

        init = (_jnp.zeros((), _jnp.float32), _jax.tree.map(_jnp.zeros_like, weights))
        (loss, grad_w), grad_x = _jax.lax.scan(body, init, (per_example, given["loss_target"]))
    with _jax.named_scope("update"):
        delta_w, new_m, new_v = {}, {}, {}
        for n in TWIN_WEIGHTS:
            delta_w[n], new_m[n], new_v[n] = _adamw(weights[n], grad_w[n], given["m_" + n], given["v_" + n])
    return (loss, grad_x, *[grad_w[n] for n in TWIN_WEIGHTS], *[delta_w[n] for n in TWIN_WEIGHTS],
            *[new_m[n] for n in TWIN_WEIGHTS], *[new_v[n] for n in TWIN_WEIGHTS])
```

```python
import functools

import jax
import jax.numpy as jnp
from jax import lax
from jax.experimental import pallas as pl
from jax.experimental.pallas import tpu as pltpu

F32 = jnp.float32
_LOWP = jnp.bfloat16
EPS = 1e-6
CONV_WIDTH = 31
POOL_WINDOWS = (2, 4, 8, 16)
ADAM_LR = 0.001
ADAM_B1 = 0.9
ADAM_B2 = 0.999
ADAM_EPS = 1e-08
ADAM_WD = 0.01
ADAM_STEP = 10

N_CHIPS = 4
N_DEV = 8
LANES = 128
SUBLANES = 8
CONV_HALO = 32
POOL_HALO = 16
TILE_ROWS = 64
VMEM_LIMIT = 48 * 1024 * 1024
ADAM_BLOCK_BYTES = 1 << 20
MESH = pl.DeviceIdType.MESH
SDS = jax.ShapeDtypeStruct


def _cparams(*sem):
    return pltpu.CompilerParams(dimension_semantics=sem if sem else None, vmem_limit_bytes=VMEM_LIMIT)


def _row_block(s):
    for tm in (512, 256, 128, 64, 32, 16):
        if s % tm == 0:
            return tm
    raise ValueError(f"sequence length {s} must be a multiple of 16")


def _rows_spec(tm, ncols):
    return pl.BlockSpec((tm, ncols), lambda i, *_: (i, 0))


def _const_spec(shape):
    nd = len(shape)
    return pl.BlockSpec(shape, lambda *_: (0,) * nd)


def _dot(a, b):
    return lax.dot_general(a.astype(_LOWP), b.astype(_LOWP), (((1,), (0,)), ((), ())), preferred_element_type=F32)


def _dot_nt(a, b):
    return lax.dot_general(a.astype(_LOWP), b.astype(_LOWP), (((1,), (1,)), ((), ())), preferred_element_type=F32)


def _dot_tn(a, b):
    return lax.dot_general(a.astype(_LOWP), b.astype(_LOWP), (((0,), (0,)), ((), ())), preferred_element_type=F32)


def _rms(x, gamma):
    r = lax.rsqrt(jnp.mean(x * x, axis=-1, keepdims=True) + EPS)
    xh = x * r
    return xh, r, xh * gamma


def _rms_bwd(dn, xh, r, gamma):
    dxh = dn * gamma
    return r * (dxh - xh * jnp.mean(dxh * xh, axis=-1, keepdims=True))


def _colsum(v):
    return jnp.sum(v, axis=0, keepdims=True)


def _sigmoid(v):
    return jax.nn.sigmoid(v)


def _dsilu(v):
    sg = _sigmoid(v)
    return sg * (1.0 + v * (1.0 - sg))


def _for_tiles(nrows, col0, ncols, fn):
    def step(t, carry):
        r0 = pl.multiple_of(t * TILE_ROWS, TILE_ROWS)
        for col in range(col0, col0 + ncols, LANES):
            fn(r0, col)
        return carry

    lax.fori_loop(0, nrows // TILE_ROWS, step, 0)


def _shift_up(win, b):
    return pltpu.roll(win, win.shape[0] - b, 0) if b else win


def _shift_down(win, b):
    return pltpu.roll(win, b, 0) if b else win


def _tiles3(v):
    return v.reshape(v.shape[0] // SUBLANES, SUBLANES, v.shape[1])


def _mesh_pos():
    return lax.axis_index("x"), lax.axis_index("y"), lax.axis_index("c")


def _other_chips(x, y):
    return [(1 - x, y), (x, 1 - y), (1 - x, 1 - y)]


def _all_gather8(v, name):
    m, n = v.shape

    def body(x_ref, out_ref, send_sems, recv_sems, local_sem):
        x, y, c = _mesh_pos()
        me, sibling = (x, y, c), (x, y, 1 - c)
        chips = _other_chips(x, y)

        def rows(px, py, pc):
            return out_ref.at[pl.ds((4 * px + 2 * py + pc) * m, m), :]

        def copy(k, block, to, src=None):
            return pltpu.make_async_remote_copy(
                src_ref=rows(*block) if src is None else src, dst_ref=rows(*block),
                send_sem=send_sems.at[k], recv_sem=recv_sems.at[k], device_id=to, device_id_type=MESH)

        mine = pltpu.make_async_copy(x_ref, rows(*me), local_sem)
        mine.start()
        first = [copy(0, me, sibling, src=x_ref)]
        first += [copy(1 + j, me, (*chip, c), src=x_ref) for j, chip in enumerate(chips)]
        for cp in first:
            cp.start()
        passed = [copy(4 + j, (*chip, c), sibling) for j, chip in enumerate(chips)]
        for j, chip in enumerate(chips):
            copy(1 + j, (*chip, c), me).wait_recv()
            passed[j].start()
        copy(0, sibling, me).wait_recv()
        for j, chip in enumerate(chips):
            copy(4 + j, (*chip, 1 - c), me).wait_recv()
        for cp in first + passed:
            cp.wait_send()
        mine.wait()

    return pl.pallas_call(
        body, name=name,
        out_shape=SDS((N_DEV * m, n), v.dtype),
        in_specs=[pl.BlockSpec(memory_space=pltpu.VMEM)],
        out_specs=pl.BlockSpec(memory_space=pltpu.VMEM),
        scratch_shapes=[pltpu.SemaphoreType.DMA((7,)), pltpu.SemaphoreType.DMA((7,)), pltpu.SemaphoreType.DMA],
    )(v)


def _gather_quarters(shards):
    n = len(shards)

    def body(*refs):
        ins, outs = refs[:n], refs[n:2 * n]
        send_sems, recv_sems, local_sems = refs[2 * n:]
        x, y, c = _mesh_pos()
        myq = 2 * x + y
        sibling = (x, y, 1 - c)
        chips = _other_chips(x, y)

        def half(a, q, h):
            hr = ins[a].shape[0] // 2
            return outs[a].at[q, pl.ds(h * hr, hr)]

        def copy(a, k, q, h, to, src=None):
            return pltpu.make_async_remote_copy(
                src_ref=half(a, q, h) if src is None else src, dst_ref=half(a, q, h),
                send_sem=send_sems.at[a, k], recv_sem=recv_sems.at[a, k], device_id=to, device_id_type=MESH)

        sends = []
        for a in range(n):
            hr = ins[a].shape[0] // 2
            pltpu.make_async_copy(ins[a], outs[a].at[myq], local_sems.at[a]).start()
            for j, chip in enumerate(chips):
                cp = copy(a, j, myq, c, (*chip, c), src=ins[a].at[pl.ds(c * hr, hr)])
                cp.start()
                sends.append(cp)
        for a in range(n):
            for j, (px, py) in enumerate(chips):
                q = 2 * px + py
                copy(a, j, q, c, sibling).wait_recv()
                cp = copy(a, 3 + j, q, c, sibling)
                cp.start()
                sends.append(cp)
        for a in range(n):
            for j, (px, py) in enumerate(chips):
                copy(a, 3 + j, 2 * px + py, 1 - c, sibling).wait_recv()
        for cp in sends:
            cp.wait_send()
        for a in range(n):
            pltpu.make_async_copy(ins[a], outs[a].at[myq], local_sems.at[a]).wait()

    any_spec = pl.BlockSpec(memory_space=pl.ANY)
    return pl.pallas_call(
        body, name="gather_weights",
        out_shape=[SDS((N_CHIPS,) + s.shape, s.dtype) for s in shards],
        in_specs=[any_spec] * n, out_specs=[any_spec] * n,
        scratch_shapes=[pltpu.SemaphoreType.DMA((n, 6)), pltpu.SemaphoreType.DMA((n, 6)), pltpu.SemaphoreType.DMA((n,))],
    )(*shards)


def _sibling_swap_halves(p):
    nq, r, n = p.shape
    hr = r // 2

    def body(p_ref, out_ref, send_sem, recv_sem):
        x, y, c = _mesh_pos()
        cp = pltpu.make_async_remote_copy(
            src_ref=p_ref.at[:, pl.ds((1 - c) * hr, hr)], dst_ref=out_ref,
            send_sem=send_sem, recv_sem=recv_sem, device_id=(x, y, 1 - c), device_id_type=MESH)
        cp.start()
        cp.wait()

    any_spec = pl.BlockSpec(memory_space=pl.ANY)
    return pl.pallas_call(
        body, name="grad_sibling_swap", out_shape=SDS((nq, hr, n), p.dtype),
        in_specs=[any_spec], out_specs=any_spec,
        scratch_shapes=[pltpu.SemaphoreType.DMA, pltpu.SemaphoreType.DMA],
    )(p)


def _scatter_to_owner_chips(cs):
    nq, h, n = cs.shape

    def body(cs_ref, out_ref, send_sems, recv_sems, local_sem):
        x, y, c = _mesh_pos()
        chips = _other_chips(x, y)
        mine = pltpu.make_async_copy(cs_ref.at[2 * x + y], out_ref.at[3], local_sem)
        mine.start()
        cps = []
        for j, (px, py) in enumerate(chips):
            cp = pltpu.make_async_remote_copy(
                src_ref=cs_ref.at[2 * px + py], dst_ref=out_ref.at[j],
                send_sem=send_sems.at[j], recv_sem=recv_sems.at[j], device_id=(px, py, c), device_id_type=MESH)
            cp.start()
            cps.append(cp)
        for cp in cps:
            cp.wait()
        mine.wait()

    any_spec = pl.BlockSpec(memory_space=pl.ANY)
    return pl.pallas_call(
        body, name="grad_scatter_chips", out_shape=SDS((nq, h, n), cs.dtype),
        in_specs=[any_spec], out_specs=any_spec,
        scratch_shapes=[pltpu.SemaphoreType.DMA((3,)), pltpu.SemaphoreType.DMA((3,)), pltpu.SemaphoreType.DMA],
    )(cs)


def _sibling_join_halves(fh):
    h, n = fh.shape

    def body(fh_ref, out_ref, send_sem, recv_sem, local_sem):
        x, y, c = _mesh_pos()

        def rows(hh):
            return out_ref.at[pl.ds(hh * h, h)]

        mine = pltpu.make_async_copy(fh_ref, rows(c), local_sem)
        mine.start()
        cp = pltpu.make_async_remote_copy(
            src_ref=fh_ref, dst_ref=rows(c), send_sem=send_sem, recv_sem=recv_sem,
            device_id=(x, y, 1 - c), device_id_type=MESH)
        cp.start()
        cp.wait_send()
        pltpu.make_async_remote_copy(
            src_ref=fh_ref, dst_ref=rows(1 - c), send_sem=send_sem, recv_sem=recv_sem,
            device_id=(x, y, 1 - c), device_id_type=MESH).wait_recv()
        mine.wait()

    any_spec = pl.BlockSpec(memory_space=pl.ANY)
    return pl.pallas_call(
        body, name="grad_sibling_join", out_shape=SDS((2 * h, n), fh.dtype),
        in_specs=[any_spec], out_specs=any_spec,
        scratch_shapes=[pltpu.SemaphoreType.DMA, pltpu.SemaphoreType.DMA, pltpu.SemaphoreType.DMA],
    )(fh)


def _pack_block(h):
    for rb in (2528, 1264, 632, 512, 256, 128, 64, 32, 16):
        if h % rb == 0:
            return rb
    raise ValueError(h)


def _add_sibling_half(p, rb1, c_idx):
    nq, r, n = p.shape
    h = r // 2
    rb = _pack_block(h)
    nb = h // rb

    def body(c_ref, p_ref, r_ref, o_ref):
        o_ref[...] = (p_ref[...].astype(F32) + r_ref[...].astype(F32)).astype(o_ref.dtype)

    grid_spec = pltpu.PrefetchScalarGridSpec(
        num_scalar_prefetch=1, grid=(nq, nb),
        in_specs=[pl.BlockSpec((1, rb, n), lambda q, i, c: (q, c[0] * nb + i, 0)),
                  pl.BlockSpec((1, rb, n), lambda q, i, c: (q, i, 0))],
        out_specs=pl.BlockSpec((1, rb, n), lambda q, i, c: (q, i, 0)))
    return pl.pallas_call(
        body, name="grad_add_sibling", grid_spec=grid_spec, out_shape=SDS((nq, h, n), p.dtype),
        compiler_params=_cparams("parallel", "parallel"),
    )(c_idx, p, rb1)


def _add_chip_sums(rb2):
    nq, h, n = rb2.shape
    rb = _pack_block(h)

    def body(r_ref, o_ref):
        v = r_ref[...].astype(F32)
        o_ref[...] = ((v[3] + v[0]) + v[1]) + v[2]

    return pl.pallas_call(
        body, name="grad_add_chips", grid=(h // rb,),
        in_specs=[pl.BlockSpec((nq, rb, n), lambda i: (0, i, 0))],
        out_specs=pl.BlockSpec((rb, n), lambda i: (i, 0)),
        out_shape=SDS((h, n), F32), compiler_params=_cparams("parallel"),
    )(rb2)


def _sum_devices(g):
    nd, m, n = g.shape

    def body(g_ref, o_ref):
        acc = g_ref[0]
        for d in range(1, nd):
            acc = acc + g_ref[d]
        o_ref[...] = acc

    return pl.pallas_call(body, name="small_grad_sum", out_shape=SDS((m, n), F32))(g)


def _ada_forward(c_all, ada_w, ada_b_cols):
    nl, d, ncol = ada_w.shape
    nb = c_all.shape[0]
    tn = 512 if ncol % 512 == 0 else ncol

    def body(c_ref, w_ref, b_ref, o_ref):
        cv = c_ref[...]
        ca = cv * _sigmoid(cv)
        o_ref[0] = _dot(ca, w_ref[0]) + b_ref[0]

    return pl.pallas_call(
        body, name="ada_forward", grid=(nl, ncol // tn),
        in_specs=[pl.BlockSpec((nb, d), lambda l, j: (0, 0)),
                  pl.BlockSpec((1, d, tn), lambda l, j: (l, 0, j)),
                  pl.BlockSpec((1, 1, tn), lambda l, j: (l, 0, j))],
        out_specs=pl.BlockSpec((1, nb, tn), lambda l, j: (l, 0, j)),
        out_shape=SDS((nl, nb, ncol), F32), compiler_params=_cparams("parallel", "parallel"),
    )(c_all, ada_w, ada_b_cols)


def _ada_backward(c_all_t, dmod_cols):
    d, nb = c_all_t.shape
    nl, _, ncol = dmod_cols.shape
    tn = 512 if ncol % 512 == 0 else ncol

    def body(c_ref, g_ref, o_ref):
        cv = c_ref[...]
        ca = cv * _sigmoid(cv)
        o_ref[0] = _dot(ca, g_ref[0])

    return pl.pallas_call(
        body, name="ada_backward", grid=(nl, ncol // tn),
        in_specs=[pl.BlockSpec((d, nb), lambda l, j: (0, 0)),
                  pl.BlockSpec((1, nb, tn), lambda l, j: (l, 0, j))],
        out_specs=pl.BlockSpec((1, d, tn), lambda l, j: (l, 0, j)),
        out_shape=SDS((nl, d, ncol), F32), compiler_params=_cparams("parallel", "parallel"),
    )(c_all_t, dmod_cols)


def _conv_a_fwd(x, vec, w1, b1):
    s, d = x.shape
    nq, _, hq = w1.shape
    tm = _row_block(s)

    def body(x_ref, vec_ref, w1_ref, b1_ref, h_ref, a_ref, g_ref, glu_ref):
        vec = vec_ref[...]
        _, _, n = _rms(x_ref[...], vec[0:1])
        hb = (n * (1.0 + vec[1:2]) + vec[2:3]).astype(_LOWP)
        h_ref[...] = hb
        u = [_dot(hb, w1_ref[q]) + b1_ref[:, q * hq:(q + 1) * hq] for q in range(nq)]
        for k in range(nq // 2):
            av, gv = u[k], u[nq // 2 + k]
            cols = slice(k * hq, (k + 1) * hq)
            a_ref[:, cols] = av.astype(_LOWP)
            g_ref[:, cols] = gv.astype(_LOWP)
            glu_ref[:, cols] = av * _sigmoid(gv)

    return pl.pallas_call(
        body, name="conv_a_fwd", grid=(s // tm,),
        in_specs=[_rows_spec(tm, d), _const_spec(vec.shape), _const_spec(w1.shape), _const_spec(b1.shape)],
        out_specs=[_rows_spec(tm, d)] * 4,
        out_shape=[SDS((s, d), _LOWP)] * 3 + [SDS((s, d), F32)],
        compiler_params=_cparams("parallel"),
    )(x, vec, w1, b1)


def _conv_b_fwd(glu, w8, vec, w2, x):
    s, d = x.shape
    tm = _row_block(s)
    hb = tm // CONV_HALO

    def body(glu_ref, halo_ref, w8_ref, vec_ref, w2_ref, x_ref, cw_ref, s_ref, y_ref, xo_ref, ext_ref):
        i = pl.program_id(0)
        vec = vec_ref[...]
        ext_ref[0:CONV_HALO, :] = jnp.where(i > 0, halo_ref[...], 0.0)
        ext_ref[CONV_HALO:, :] = glu_ref[...]

        def tile(r0, c0):
            lanes = slice(c0, c0 + LANES)
            win = ext_ref[pl.ds(r0, TILE_ROWS + CONV_HALO), lanes]
            acc = jnp.zeros((TILE_ROWS // SUBLANES, SUBLANES, LANES), F32)
            for b in range(SUBLANES):
                sh = _shift_up(win, b)
                for a in range(CONV_HALO // SUBLANES + 1):
                    k = SUBLANES * a + b - (CONV_HALO - CONV_WIDTH + 1)
                    if 0 <= k < CONV_WIDTH:
                        acc = acc + w8_ref[k, :, lanes][None] * _tiles3(sh[SUBLANES * a:SUBLANES * a + TILE_ROWS])
            cw_ref[pl.ds(r0, TILE_ROWS), lanes] = acc.reshape(TILE_ROWS, LANES)

        _for_tiles(tm, 0, d, tile)
        cw = cw_ref[...] + vec[0:1]
        cw_ref[...] = cw
        cc = cw - jnp.mean(cw, axis=-1, keepdims=True)
        ch = cc * lax.rsqrt(jnp.mean(cc * cc, axis=-1, keepdims=True) + EPS)
        lo = ch * vec[1:2] + vec[2:3]
        sv = (lo * _sigmoid(lo)).astype(_LOWP)
        s_ref[...] = sv
        y = _dot(sv, w2_ref[...]) + vec[3:4]
        y_ref[...] = y.astype(_LOWP)
        xo_ref[...] = x_ref[...] + (1.0 + vec[4:5]) * y

    return pl.pallas_call(
        body, name="conv_b_fwd", grid=(s // tm,),
        in_specs=[_rows_spec(tm, d),
                  pl.BlockSpec((CONV_HALO, d), lambda i: (jnp.maximum(i * hb - 1, 0), 0)),
                  _const_spec(w8.shape), _const_spec(vec.shape), _const_spec(w2.shape), _rows_spec(tm, d)],
        out_specs=[_rows_spec(tm, d)] * 4,
        out_shape=[SDS((s, d), F32), SDS((s, d), _LOWP), SDS((s, d), _LOWP), SDS((s, d), F32)],
        scratch_shapes=[pltpu.VMEM((tm + CONV_HALO, d), F32)],
        compiler_params=_cparams("parallel"),
    )(glu, glu, w8, vec, w2, x)


def _ffn_fwd(x, vec, wg, wu, wd, name):
    s, d = x.shape
    nq, _, fq = wg.shape
    tm = _row_block(s)

    def body(x_ref, vec_ref, wg_ref, wu_ref, wd_ref, h_ref, gt_ref, up_ref, act_ref, y_ref, xo_ref, hb_ref, yacc_ref):
        q = pl.program_id(1)
        vec = vec_ref[...]

        @pl.when(q == 0)
        def _():
            _, _, n = _rms(x_ref[...], vec[0:1])
            hb = (n * (1.0 + vec[1:2]) + vec[2:3]).astype(_LOWP)
            hb_ref[...] = hb
            h_ref[...] = hb
            yacc_ref[...] = jnp.zeros_like(yacc_ref)

        hb = hb_ref[...]
        gt = _dot(hb, wg_ref[0])
        up = _dot(hb, wu_ref[0])
        act = (gt * _sigmoid(gt) * up).astype(_LOWP)
        gt_ref[0] = gt.astype(_LOWP)
        up_ref[0] = up.astype(_LOWP)
        act_ref[0] = act
        yacc_ref[...] += _dot(act, wd_ref[0])

        @pl.when(q == nq - 1)
        def _():
            y = yacc_ref[...]
            y_ref[...] = y.astype(_LOWP)
            xo_ref[...] = x_ref[...] + (1.0 + vec[3:4]) * y

    rows = pl.BlockSpec((tm, d), lambda i, q: (i, 0))
    hid = pl.BlockSpec((1, tm, fq), lambda i, q: (q, i, 0))
    return pl.pallas_call(
        body, name=name, grid=(s // tm, nq),
        in_specs=[rows, pl.BlockSpec(vec.shape, lambda i, q: (0, 0)),
                  pl.BlockSpec((1, d, fq), lambda i, q: (q, 0, 0)),
                  pl.BlockSpec((1, d, fq), lambda i, q: (q, 0, 0)),
                  pl.BlockSpec((1, fq, d), lambda i, q: (q, 0, 0))],
        out_specs=[rows, hid, hid, hid, rows, rows],
        out_shape=[SDS((s, d), _LOWP)] + [SDS((nq, s, fq), _LOWP)] * 3 + [SDS((s, d), _LOWP), SDS((s, d), F32)],
        scratch_shapes=[pltpu.VMEM((tm, d), _LOWP), pltpu.VMEM((tm, d), F32)],
        compiler_params=_cparams("parallel", "arbitrary"),
    )(x, vec, wg, wu, wd)


def _pool_inv_count(t0, nrows, w):
    t = t0 + lax.broadcasted_iota(jnp.int32, (nrows, LANES), 0)
    return 1.0 / jnp.minimum(t + 1, w).astype(F32)


def _pool_fwd(x, vec, pw):
    s, d = x.shape
    ng, gd, _ = pw.shape
    tm = _row_block(s)
    hb = tm // POOL_HALO

    def body(x_ref, halo_ref, vec_ref, pw_ref, mixed_ref, yp_ref, xo_ref, hext_ref):
        i = pl.program_id(0)
        vec = vec_ref[...]

        def modulated(xv):
            _, _, n = _rms(xv, vec[0:1])
            return n * (1.0 + vec[1:2]) + vec[2:3]

        hext_ref[0:POOL_HALO, :] = jnp.where(i > 0, modulated(halo_ref[...]), 0.0)
        hext_ref[POOL_HALO:, :] = modulated(x_ref[...])

        for g in range(ng):
            def tile(r0, c0, g=g):
                lanes = slice(c0, c0 + LANES)
                win = hext_ref[pl.ds(r0, TILE_ROWS + POOL_HALO), lanes]
                acc = win
                for step in range(g + 1):
                    acc = acc + _shift_down(acc, 2 ** step)
                inv = _pool_inv_count(i * tm + r0, TILE_ROWS, POOL_WINDOWS[g])
                mixed = acc[POOL_HALO:] * inv - win[POOL_HALO:]
                mixed_ref[pl.ds(r0, TILE_ROWS), lanes] = mixed.astype(_LOWP)

            _for_tiles(tm, g * gd, gd, tile)

        for g in range(ng):
            cols = slice(g * gd, (g + 1) * gd)
            yp = _dot(mixed_ref[:, cols], pw_ref[g])
            yp_ref[:, cols] = yp.astype(_LOWP)
            xo_ref[:, cols] = x_ref[:, cols] + (1.0 + vec[3:4, cols]) * (yp * vec[4:5, cols])

    return pl.pallas_call(
        body, name="pool_fwd", grid=(s // tm,),
        in_specs=[_rows_spec(tm, d),
                  pl.BlockSpec((POOL_HALO, d), lambda i: (jnp.maximum(i * hb - 1, 0), 0)),
                  _const_spec(vec.shape), _const_spec(pw.shape)],
        out_specs=[_rows_spec(tm, d)] * 3,
        out_shape=[SDS((s, d), _LOWP), SDS((s, d), _LOWP), SDS((s, d), F32)],
        scratch_shapes=[pltpu.VMEM((tm + POOL_HALO, d), F32)],
        compiler_params=_cparams("parallel"),
    )(x, x, vec, pw)


def _final_fwd_bwd(x, tgt, gamma):
    s, d = x.shape
    tm = _row_block(s)
    nsteps = s // tm

    def body(x_ref, t_ref, g_ref, dx_ref, sums_ref):
        i = pl.program_id(0)
        gamma_v = g_ref[...]
        xh, r, out = _rms(x_ref[...], gamma_v)
        err = out - t_ref[...]
        dout = err * (1.0 / d)

        @pl.when(i == 0)
        def _():
            sums_ref[...] = jnp.zeros_like(sums_ref)

        sums_ref[0:1, :] += _colsum(dout * xh)
        sums_ref[1:2, :] += _colsum(0.5 * err * dout)
        dx_ref[...] = _rms_bwd(dout, xh, r, gamma_v)

        @pl.when(i == nsteps - 1)
        def _():
            sums_ref[1:2, :] = jnp.broadcast_to(jnp.sum(sums_ref[1:2, :], axis=-1, keepdims=True), (1, d))

    return pl.pallas_call(
        body, name="final_loss", grid=(nsteps,),
        in_specs=[_rows_spec(tm, d), _rows_spec(tm, d), _const_spec(gamma.shape)],
        out_specs=[_rows_spec(tm, d), _const_spec((2, d))],
        out_shape=[SDS((s, d), F32), SDS((2, d), F32)],
        compiler_params=_cparams("arbitrary"),
    )(x, tgt, gamma)


def _ffn_bwd(dxo, x, vec, gt, up, y, wg, wu, wd, name):
    s, d = x.shape
    nq, _, fq = wg.shape
    tm = _row_block(s)

    def body(dxo_ref, x_ref, vec_ref, gt_ref, up_ref, y_ref, wg_ref, wu_ref, wd_ref,
             dgt_ref, dup_ref, dy_ref, dx_ref, sums_ref, dyb_ref, dh_ref):
        i, q = pl.program_id(0), pl.program_id(1)
        vec = vec_ref[...]

        @pl.when((i == 0) & (q == 0))
        def _():
            sums_ref[...] = jnp.zeros_like(sums_ref)

        @pl.when(q == 0)
        def _():
            dxv = dxo_ref[...]
            dyb = (dxv * (1.0 + vec[3:4])).astype(_LOWP)
            dyb_ref[...] = dyb
            dy_ref[...] = dyb
            dh_ref[...] = jnp.zeros_like(dh_ref)
            sums_ref[3:4, :] += _colsum(dxv * y_ref[...].astype(F32))

        dact = _dot_nt(dyb_ref[...], wd_ref[0])
        gtv = gt_ref[0].astype(F32)
        sg = _sigmoid(gtv)
        dgt = (dact * up_ref[0].astype(F32) * (sg * (1.0 + gtv * (1.0 - sg)))).astype(_LOWP)
        dup = (dact * (gtv * sg)).astype(_LOWP)
        dgt_ref[0] = dgt
        dup_ref[0] = dup
        dh_ref[...] += _dot_nt(dgt, wg_ref[0]) + _dot_nt(dup, wu_ref[0])

        @pl.when(q == nq - 1)
        def _():
            dh = dh_ref[...]
            xh, r, n = _rms(x_ref[...], vec[0:1])
            sums_ref[2:3, :] += _colsum(dh)
            sums_ref[1:2, :] += _colsum(dh * n)
            dn = dh * (1.0 + vec[1:2])
            sums_ref[0:1, :] += _colsum(dn * xh)
            dx_ref[...] = dxo_ref[...] + _rms_bwd(dn, xh, r, vec[0:1])

    rows = pl.BlockSpec((tm, d), lambda i, q: (i, 0))
    hid = pl.BlockSpec((1, tm, fq), lambda i, q: (q, i, 0))
    return pl.pallas_call(
        body, name=name, grid=(s // tm, nq),
        in_specs=[rows, rows, pl.BlockSpec(vec.shape, lambda i, q: (0, 0)), hid, hid, rows,
                  pl.BlockSpec((1, d, fq), lambda i, q: (q, 0, 0)),
                  pl.BlockSpec((1, d, fq), lambda i, q: (q, 0, 0)),
                  pl.BlockSpec((1, fq, d), lambda i, q: (q, 0, 0))],
        out_specs=[hid, hid, rows, rows, pl.BlockSpec((4, d), lambda i, q: (0, 0))],
        out_shape=[SDS((nq, s, fq), _LOWP)] * 2 + [SDS((s, d), _LOWP), SDS((s, d), F32), SDS((4, d), F32)],
        scratch_shapes=[pltpu.VMEM((tm, d), _LOWP), pltpu.VMEM((tm, d), F32)],
        compiler_params=_cparams("arbitrary", "arbitrary"),
    )(dxo, x, vec, gt, up, y, wg, wu, wd)


def _wgrad(a, b, a_spec, b_spec, nq, ka, nb, s, name):
    tk = min(2048, s)
    nk = s // tk

    def body(a_ref, b_ref, o_ref, acc_ref):
        k = pl.program_id(1)

        @pl.when(k == 0)
        def _():
            acc_ref[...] = jnp.zeros_like(acc_ref)

        av = a_ref[...].reshape(tk, ka)
        bv = b_ref[...].reshape(tk, nb)
        acc_ref[...] += _dot_tn(av, bv)

        @pl.when(k == nk - 1)
        def _():
            o_ref[0] = acc_ref[...].astype(o_ref.dtype)

    return pl.pallas_call(
        body, name=name, grid=(nq, nk),
        in_specs=[a_spec(tk), b_spec(tk)],
        out_specs=pl.BlockSpec((1, ka, nb), lambda q, k: (q, 0, 0)),
        out_shape=SDS((nq, ka, nb), _LOWP),
        scratch_shapes=[pltpu.VMEM((ka, nb), F32)],
        compiler_params=_cparams("parallel", "arbitrary"),
    )(a, b)


def _shared_rows(ncols):
    return lambda tk: pl.BlockSpec((tk, ncols), lambda q, k: (k, 0))


def _column_block(ncols):
    return lambda tk: pl.BlockSpec((tk, ncols), lambda q, k: (k, q))


def _quarter_major(ncols):
    return lambda tk: pl.BlockSpec((1, tk, ncols), lambda q, k: (q, k, 0))


def _pool_bwd(dxo, x, yp, vec, pw):
    s, d = x.shape
    ng, gd, _ = pw.shape
    tm = _row_block(s)
    hb = tm // POOL_HALO
    nsteps = s // tm
    last_halo = s // POOL_HALO - 1

    def body(dxo_ref, halo_ref, x_ref, yp_ref, vec_ref, pw_ref, dyp_ref, dx_ref, sums_ref, eext_ref, dh_ref):
        i = pl.program_id(0)
        vec = vec_ref[...]

        @pl.when(i == 0)
        def _():
            sums_ref[...] = jnp.zeros_like(sums_ref)

        dxv = dxo_ref[...]
        ypv = yp_ref[...].astype(F32)
        gate1 = 1.0 + vec[3:4]
        sums_ref[3:4, :] += _colsum(dxv * (ypv * vec[4:5]))
        sums_ref[4:5, :] += _colsum(dxv * gate1 * ypv)
        dyp = (dxv * gate1 * vec[4:5]).astype(_LOWP)
        dyp_ref[...] = dyp
        dyp_halo = (jnp.where(i < nsteps - 1, halo_ref[...], 0.0) * gate1 * vec[4:5]).astype(_LOWP)
        for g in range(ng):
            cols = slice(g * gd, (g + 1) * gd)
            dm = _dot_nt(dyp[:, cols], pw_ref[g])
            dh_ref[:, cols] = dm
            inv = _pool_inv_count(i * tm, tm, POOL_WINDOWS[g])
            eext_ref[0:tm, cols] = dm * jnp.concatenate([inv] * (gd // LANES), axis=1)
            eext_ref[tm:, cols] = _dot_nt(dyp_halo[:, cols], pw_ref[g]) * (1.0 / POOL_WINDOWS[g])

        for g in range(ng):
            def tile(r0, c0, g=g):
                acc = eext_ref[pl.ds(r0, TILE_ROWS + POOL_HALO), c0:c0 + LANES]
                for step in range(g + 1):
                    acc = acc + _shift_up(acc, 2 ** step)
                here = (pl.ds(r0, TILE_ROWS), slice(c0, c0 + LANES))
                dh_ref[here] = acc[:TILE_ROWS] - dh_ref[here]

            _for_tiles(tm, g * gd, gd, tile)

        dh = dh_ref[...]
        xh, r, n = _rms(x_ref[...], vec[0:1])
        sums_ref[2:3, :] += _colsum(dh)
        sums_ref[1:2, :] += _colsum(dh * n)
        dn = dh * (1.0 + vec[1:2])
        sums_ref[0:1, :] += _colsum(dn * xh)
        dx_ref[...] = dxv + _rms_bwd(dn, xh, r, vec[0:1])

    return pl.pallas_call(
        body, name="pool_bwd", grid=(nsteps,),
        in_specs=[_rows_spec(tm, d),
                  pl.BlockSpec((POOL_HALO, d), lambda i: (jnp.minimum((i + 1) * hb, last_halo), 0)),
                  _rows_spec(tm, d), _rows_spec(tm, d), _const_spec(vec.shape), _const_spec(pw.shape)],
        out_specs=[_rows_spec(tm, d), _rows_spec(tm, d), _const_spec((5, d))],
        out_shape=[SDS((s, d), _LOWP), SDS((s, d), F32), SDS((5, d), F32)],
        scratch_shapes=[pltpu.VMEM((tm + POOL_HALO, d), F32), pltpu.VMEM((tm, d), F32)],
        compiler_params=_cparams("arbitrary"),
    )(dxo, dxo, x, yp, vec, pw)


def _conv_b_bwd(dxo, cw, y, vec, w2):
    s, d = dxo.shape
    tm = _row_block(s)

    def body(dxo_ref, cw_ref, y_ref, vec_ref, w2_ref, dy_ref, dcw_ref, sums_ref):
        i = pl.program_id(0)
        vec = vec_ref[...]

        @pl.when(i == 0)
        def _():
            sums_ref[...] = jnp.zeros_like(sums_ref)

        dxv = dxo_ref[...]
        dy = dxv * (1.0 + vec[4:5])
        sums_ref[0:1, :] += _colsum(dxv * y_ref[...].astype(F32))
        sums_ref[1:2, :] += _colsum(dy)
        dyb = dy.astype(_LOWP)
        dy_ref[...] = dyb
        ds = _dot_nt(dyb, w2_ref[...])
        cw = cw_ref[...]
        cc = cw - jnp.mean(cw, axis=-1, keepdims=True)
        rs = lax.rsqrt(jnp.mean(cc * cc, axis=-1, keepdims=True) + EPS)
        ch = cc * rs
        lo = ch * vec[1:2] + vec[2:3]
        dlo = ds * _dsilu(lo)
        sums_ref[2:3, :] += _colsum(dlo * ch)
        sums_ref[3:4, :] += _colsum(dlo)
        dch = dlo * vec[1:2]
        dcw = rs * (dch - jnp.mean(dch, axis=-1, keepdims=True) - ch * jnp.mean(dch * ch, axis=-1, keepdims=True))
        sums_ref[4:5, :] += _colsum(dcw)
        dcw_ref[...] = dcw

    return pl.pallas_call(
        body, name="conv_b_bwd", grid=(s // tm,),
        in_specs=[_rows_spec(tm, d), _rows_spec(tm, d), _rows_spec(tm, d), _const_spec(vec.shape), _const_spec(w2.shape)],
        out_specs=[_rows_spec(tm, d), _rows_spec(tm, d), _const_spec((5, d))],
        out_shape=[SDS((s, d), _LOWP), SDS((s, d), F32), SDS((5, d), F32)],
        compiler_params=_cparams("arbitrary"),
    )(dxo, cw, y, vec, w2)


def _conv_a_bwd(dcw, glu, a, g, x, dxo, vec, w8, w1):
    s, d = x.shape
    nq, _, hq = w1.shape
    tm = _row_block(s)
    hb = tm // CONV_HALO
    nsteps = s // tm
    last_halo = s // CONV_HALO - 1
    lead = CONV_HALO - CONV_WIDTH + 1

    def body(dcw_ref, dhalo_ref, glu_ref, ghalo_ref, a_ref, g_ref, x_ref, dxo_ref, vec_ref, w8_ref, w1_ref,
             du_ref, dx_ref, sums_ref, db1_ref, dwdw_ref, dext_ref, gext_ref, dglu_ref, wacc_ref):
        i = pl.program_id(0)
        vec = vec_ref[...]

        @pl.when(i == 0)
        def _():
            sums_ref[...] = jnp.zeros_like(sums_ref)
            db1_ref[...] = jnp.zeros_like(db1_ref)
            wacc_ref[...] = jnp.zeros_like(wacc_ref)

        dext_ref[0:tm, :] = dcw_ref[...]
        dext_ref[tm:, :] = jnp.where(i < nsteps - 1, dhalo_ref[...], 0.0)
        gext_ref[0:CONV_HALO, :] = jnp.where(i > 0, ghalo_ref[...], 0.0)
        gext_ref[CONV_HALO:, :] = glu_ref[...]

        def tile(r0, c0):
            lanes = slice(c0, c0 + LANES)
            dwin = dext_ref[pl.ds(r0, TILE_ROWS + CONV_HALO), lanes]
            acc = jnp.zeros((TILE_ROWS // SUBLANES, SUBLANES, LANES), F32)
            for b in range(SUBLANES):
                sh = _shift_up(dwin, b)
                for a8 in range(CONV_HALO // SUBLANES):
                    k = CONV_WIDTH - 1 - (SUBLANES * a8 + b)
                    if 0 <= k < CONV_WIDTH:
                        acc = acc + w8_ref[k, :, lanes][None] * _tiles3(sh[SUBLANES * a8:SUBLANES * a8 + TILE_ROWS])
            dglu_ref[pl.ds(r0, TILE_ROWS), lanes] = acc.reshape(TILE_ROWS, LANES)

            dc3 = _tiles3(dwin[:TILE_ROWS])
            gwin = gext_ref[pl.ds(r0, TILE_ROWS + CONV_HALO), lanes]
            for b in range(SUBLANES):
                sh = _shift_up(gwin, b)
                for a8 in range(CONV_HALO // SUBLANES + 1):
                    k = SUBLANES * a8 + b - lead
                    if 0 <= k < CONV_WIDTH:
                        part = jnp.sum(dc3 * _tiles3(sh[SUBLANES * a8:SUBLANES * a8 + TILE_ROWS]), axis=0)
                        wacc_ref[k, :, lanes] += part

        _for_tiles(tm, 0, d, tile)

        dglu = dglu_ref[...]
        av, gv = a_ref[...].astype(F32), g_ref[...].astype(F32)
        sg = _sigmoid(gv)
        da = dglu * sg
        dg = dglu * av * sg * (1.0 - sg)
        du_ref[:, 0:d] = da.astype(_LOWP)
        du_ref[:, d:2 * d] = dg.astype(_LOWP)
        db1_ref[:, 0:d] += _colsum(da)
        db1_ref[:, d:2 * d] += _colsum(dg)
        dh = jnp.zeros((tm, d), F32)
        for q in range(nq):
            dh = dh + _dot_nt(du_ref[:, q * hq:(q + 1) * hq], w1_ref[q])
        xh, r, n = _rms(x_ref[...], vec[0:1])
        sums_ref[2:3, :] += _colsum(dh)
        sums_ref[1:2, :] += _colsum(dh * n)
        dn = dh * (1.0 + vec[1:2])
        sums_ref[0:1, :] += _colsum(dn * xh)
        dx_ref[...] = dxo_ref[...] + _rms_bwd(dn, xh, r, vec[0:1])

        @pl.when(i == nsteps - 1)
        def _():
            dwdw_ref[...] = jnp.sum(wacc_ref[...], axis=1)

    return pl.pallas_call(
        body, name="conv_a_bwd", grid=(nsteps,),
        in_specs=[_rows_spec(tm, d),
                  pl.BlockSpec((CONV_HALO, d), lambda i: (jnp.minimum((i + 1) * hb, last_halo), 0)),
                  _rows_spec(tm, d),
                  pl.BlockSpec((CONV_HALO, d), lambda i: (jnp.maximum(i * hb - 1, 0), 0)),
                  _rows_spec(tm, d), _rows_spec(tm, d), _rows_spec(tm, d), _rows_spec(tm, d),
                  _const_spec(vec.shape), _const_spec(w8.shape), _const_spec(w1.shape)],
        out_specs=[_rows_spec(tm, 2 * d), _rows_spec(tm, d), _const_spec((3, d)), _const_spec((1, 2 * d)),
                   _const_spec((CONV_WIDTH, d))],
        out_shape=[SDS((s, 2 * d), _LOWP), SDS((s, d), F32), SDS((3, d), F32), SDS((1, 2 * d), F32),
                   SDS((CONV_WIDTH, d), F32)],
        scratch_shapes=[pltpu.VMEM((tm + CONV_HALO, d), F32), pltpu.VMEM((tm + CONV_HALO, d), F32),
                        pltpu.VMEM((tm, d), F32), pltpu.VMEM((CONV_WIDTH, SUBLANES, d), F32)],
        compiler_params=_cparams("arbitrary"),
    )(dcw, dcw, glu, glu, a, g, x, dxo, vec, w8, w1)


def _adamw(w, g, m, v, name):
    r, c = w.shape
    rb = r
    for cand in (512, 352, 256, 184, 128, 64, 32, 16, 8):
        if r % cand == 0 and cand * c * 4 <= ADAM_BLOCK_BYTES:
            rb = cand
            break

    def body(w_ref, g_ref, m_ref, v_ref, d_ref, mo_ref, vo_ref):
        gv = g_ref[...]
        mn = ADAM_B1 * m_ref[...] + (1.0 - ADAM_B1) * gv
        vn = ADAM_B2 * v_ref[...] + (1.0 - ADAM_B2) * (gv * gv)
        m_hat = mn / (1.0 - ADAM_B1 ** ADAM_STEP)
        v_hat = vn / (1.0 - ADAM_B2 ** ADAM_STEP)
        d_ref[...] = -ADAM_LR * (m_hat / (jnp.sqrt(v_hat) + ADAM_EPS) + ADAM_WD * w_ref[...])
        mo_ref[...] = mn
        vo_ref[...] = vn

    spec = pl.BlockSpec((rb, c), lambda i: (i, 0))
    return pl.pallas_call(
        body, name=name, grid=(r // rb,), in_specs=[spec] * 4, out_specs=[spec] * 3,
        out_shape=[SDS((r, c), F32)] * 3, compiler_params=_cparams("parallel"),
    )(w, g, m, v)


def _adamw_nd(w, g, m, v, name):
    shape = w.shape
    two_d = (-1, shape[-1])
    outs = _adamw(w.reshape(two_d), g.reshape(two_d), m.reshape(two_d), v.reshape(two_d), name)
    return [o.reshape(shape) for o in outs]


def _rows_of(*vs):
    return jnp.concatenate([v.reshape(-1, v.shape[-1]) for v in vs], axis=0)


def kernel(x, c, ada_w, ada_b, norm_mix_g, norm_ffn_g, conv_w1, conv_b1, conv_wdw, conv_bdw, conv_ln_g, conv_ln_b, conv_w2, conv_b2, pool_w, pool_ls, ffn_w_gate, ffn_w_up, ffn_w_down, final_g, loss_target, m_ada_w, m_ada_b, m_norm_mix_g, m_norm_ffn_g, m_conv_w1, m_conv_b1, m_conv_wdw, m_conv_bdw, m_conv_ln_g, m_conv_ln_b, m_conv_w2, m_conv_b2, m_pool_w, m_pool_ls, m_ffn_w_gate, m_ffn_w_up, m_ffn_w_down, m_final_g, v_ada_w, v_ada_b, v_norm_mix_g, v_norm_ffn_g, v_conv_w1, v_conv_b1, v_conv_wdw, v_conv_bdw, v_conv_ln_g, v_conv_ln_b, v_conv_w2, v_conv_b2, v_pool_w, v_pool_ls, v_ffn_w_gate, v_ffn_w_up, v_ffn_w_down, v_final_g):
    s, d = x.shape[1], x.shape[2]
    depth = ada_w.shape[0]
    assert depth == 2 and conv_w1.shape[0] == 1 and pool_w.shape[0] == 1, "one conv layer then one pool layer"
    fq = ffn_w_gate.shape[2]
    n_ada = ada_b.shape[1] // d
    ada_cols = ada_w.shape[2]
    ng, gq, gd = pool_w.shape[1], pool_w.shape[2], pool_w.shape[3]
    cq = conv_wdw.shape[2]
    ax, ay, ac = _mesh_pos()
    myq = 2 * ax + ay
    me = 4 * ax + 2 * ay + ac
    x0 = x.reshape(s, d)
    tgt = loss_target.reshape(s, d)

    c_all = _all_gather8(c.reshape(d // LANES, LANES), "gather_c").reshape(N_DEV, d)
    ada_b_cols = lax.dynamic_slice_in_dim(ada_b, myq * ada_cols, ada_cols, axis=1).reshape(depth, 1, ada_cols)
    mod_part = _ada_forward(c_all, ada_w, ada_b_cols)
    mod_all = _all_gather8(mod_part.reshape(-1, LANES), "gather_mod").reshape(N_DEV, depth, N_DEV, ada_cols)
    mod_mine = lax.dynamic_index_in_dim(mod_all[0::2], me, axis=2, keepdims=False)
    mod = jnp.transpose(mod_mine, (1, 0, 2)).reshape(depth, n_ada, d)

    shards = [conv_w1[0], conv_w2[0], pool_w[0].reshape(ng * gq, gd),
              ffn_w_gate.reshape(depth * d, fq), ffn_w_up.reshape(depth * d, fq), ffn_w_down.reshape(depth * fq, d)]
    w1_all, w2_all, pw_all, wg_all, wu_all, wd_all = _gather_quarters([w.astype(_LOWP) for w in shards])
    w1 = w1_all
    w2 = w2_all.reshape(d, d)
    pw = jnp.transpose(pw_all.reshape(N_CHIPS, ng, gq, gd), (1, 0, 2, 3)).reshape(ng, N_CHIPS * gq, gd)
    wg = wg_all.reshape(N_CHIPS, depth, d, fq)
    wu = wu_all.reshape(N_CHIPS, depth, d, fq)
    wd = wd_all.reshape(N_CHIPS, depth, fq, d)
    wdw_full_shards = _all_gather8(
        jnp.concatenate([conv_wdw[0], pool_ls, jnp.zeros((SUBLANES * 4 - CONV_WIDTH - 1, cq), F32)], axis=0).reshape(-1, LANES),
        "gather_small_shards").reshape(N_DEV, SUBLANES * 4, cq)[0::2]
    wdw = jnp.transpose(wdw_full_shards[:, :CONV_WIDTH], (1, 0, 2)).reshape(CONV_WIDTH, d)
    pls = wdw_full_shards[:, CONV_WIDTH].reshape(1, d)
    w8 = jnp.broadcast_to(wdw[:, None, :], (CONV_WIDTH, SUBLANES, d))

    vec_ca = _rows_of(norm_mix_g[0], mod[0, 1], mod[0, 0])
    h0, a0, g0, glu = _conv_a_fwd(x0, vec_ca, w1, conv_b1)
    vec_cb = _rows_of(conv_bdw, conv_ln_g, conv_ln_b, conv_b2, mod[0, 2])
    cw, s0, y0, x1 = _conv_b_fwd(glu, w8, vec_cb, w2, x0)
    vec_f0 = _rows_of(norm_ffn_g[0], mod[0, 4], mod[0, 3], mod[0, 5])
    h2a, gta, upa, acta, y2a, x2 = _ffn_fwd(x1, vec_f0, wg[:, 0], wu[:, 0], wd[:, 0], "ffn0_fwd")
    vec_p = _rows_of(norm_mix_g[1], mod[1, 1], mod[1, 0], mod[1, 2], pls)
    mixed, yp, x3 = _pool_fwd(x2, vec_p, pw)
    vec_f1 = _rows_of(norm_ffn_g[1], mod[1, 4], mod[1, 3], mod[1, 5])
    h2b, gtb, upb, actb, y2b, x4 = _ffn_fwd(x3, vec_f1, wg[:, 1], wu[:, 1], wd[:, 1], "ffn1_fwd")
    dx4, fin_sums = _final_fwd_bwd(x4, tgt, final_g.reshape(1, d))
    loss = lax.psum(fin_sums[1, 0], ("x", "y", "c"))

    dgtb, dupb, dyb, dx3, sums_f1 = _ffn_bwd(dx4, x3, vec_f1, gtb, upb, y2b, wg[:, 1], wu[:, 1], wd[:, 1], "ffn1_bwd")
    dwg1 = _wgrad(h2b, dgtb, _shared_rows(d), _quarter_major(fq), N_CHIPS, d, fq, s, "wgrad_gate1")
    dwu1 = _wgrad(h2b, dupb, _shared_rows(d), _quarter_major(fq), N_CHIPS, d, fq, s, "wgrad_up1")
    dwd1 = _wgrad(actb, dyb, _quarter_major(fq), _shared_rows(d), N_CHIPS, fq, d, s, "wgrad_down1")
    dyp, dx2, sums_p = _pool_bwd(dx3, x2, yp, vec_p, pw)
    dpw = _wgrad(mixed, dyp, _column_block(gd), _column_block(gd), ng, gd, gd, s, "wgrad_pool")
    dgta, dupa, dya, dx1, sums_f0 = _ffn_bwd(dx2, x1, vec_f0, gta, upa, y2a, wg[:, 0], wu[:, 0], wd[:, 0], "ffn0_bwd")
    dwg0 = _wgrad(h2a, dgta, _shared_rows(d), _quarter_major(fq), N_CHIPS, d, fq, s, "wgrad_gate0")
    dwu0 = _wgrad(h2a, dupa, _shared_rows(d), _quarter_major(fq), N_CHIPS, d, fq, s, "wgrad_up0")
    dwd0 = _wgrad(acta, dya, _quarter_major(fq), _shared_rows(d), N_CHIPS, fq, d, s, "wgrad_down0")
    dy0, dcw, sums_cb = _conv_b_bwd(dx1, cw, y0, vec_cb, w2)
    dw2 = _wgrad(s0, dy0, _column_block(d // N_CHIPS), _shared_rows(d), N_CHIPS, d // N_CHIPS, d, s, "wgrad_conv2")
    du, dx0, sums_ca, db1, dwdw = _conv_a_bwd(dcw, glu, a0, g0, x0, dx1, vec_ca, w8, w1)
    dw1 = _wgrad(h0, du, _shared_rows(d), _column_block(2 * d // N_CHIPS), N_CHIPS, d, 2 * d // N_CHIPS, s, "wgrad_conv1")

    dpw_q = jnp.transpose(dpw.reshape(ng, N_CHIPS, gq, gd), (1, 0, 2, 3))
    parts = [dw1, dw2, dpw_q, dwg0, dwg1, dwu0, dwu1, dwd0, dwd1]
    pack = jnp.concatenate([p.reshape(N_CHIPS, -1) for p in parts], axis=1)
    pack = pack.reshape(N_CHIPS, -1, LANES)
    from_sibling = _sibling_swap_halves(pack)
    chip_sums = _add_sibling_half(pack, from_sibling, ac.reshape(1).astype(jnp.int32))
    from_chips = _scatter_to_owner_chips(chip_sums)
    my_half = _add_chip_sums(from_chips)
    gpack = _sibling_join_halves(my_half).reshape(-1)
    sizes = [p.size // N_CHIPS for p in parts]
    offs = [sum(sizes[:k]) for k in range(len(sizes) + 1)]
    gparts = [gpack[offs[k]:offs[k + 1]] for k in range(len(sizes))]
    g_conv_w1 = gparts[0].reshape(conv_w1.shape)
    g_conv_w2 = gparts[1].reshape(conv_w2.shape)
    g_pool_w = gparts[2].reshape(pool_w.shape)
    g_ffn_gate = jnp.stack([gparts[3].reshape(d, fq), gparts[4].reshape(d, fq)])
    g_ffn_up = jnp.stack([gparts[5].reshape(d, fq), gparts[6].reshape(d, fq)])
    g_ffn_down = jnp.stack([gparts[7].reshape(fq, d), gparts[8].reshape(fq, d)])

    dmod = jnp.stack([
        _rows_of(sums_ca[2], sums_ca[1], sums_cb[0], sums_f0[2], sums_f0[1], sums_f0[3]),
        _rows_of(sums_p[2], sums_p[1], sums_p[3], sums_f1[2], sums_f1[1], sums_f1[3])])
    small = _rows_of(dmod.reshape(-1, d), sums_ca[0], sums_p[0], sums_f0[0], sums_f1[0], db1.reshape(2, d),
                     sums_cb[4], sums_cb[2], sums_cb[3], sums_cb[1], fin_sums[0], sums_p[4], dwdw)
    n_small = small.shape[0]
    pad = (-n_small) % SUBLANES
    small = jnp.concatenate([small, jnp.zeros((pad, d), F32)], axis=0) if pad else small
    small_all = _all_gather8(small.reshape(-1, LANES), "gather_small_grads").reshape(N_DEV, -1, LANES)
    small_sum = _sum_devices(small_all).reshape(-1, d)
    n_dm = depth * n_ada
    g_ada_b = small_sum[0:n_dm].reshape(depth, n_ada * d)
    g_norm_mix = small_sum[n_dm:n_dm + 2]
    g_norm_ffn = small_sum[n_dm + 2:n_dm + 4]
    g_b1 = small_sum[n_dm + 4:n_dm + 6].reshape(1, 2 * d)
    g_bdw, g_lng, g_lnb, g_b2 = (small_sum[n_dm + 6 + k].reshape(1, d) for k in range(4))
    g_final = small_sum[n_dm + 10]
    g_pls = lax.dynamic_slice_in_dim(small_sum[n_dm + 11].reshape(1, d), myq * cq, cq, axis=1)
    g_wdw = lax.dynamic_slice_in_dim(small_sum[n_dm + 12:n_dm + 12 + CONV_WIDTH], myq * cq, cq, axis=1).reshape(conv_wdw.shape)
    dmod_all = small_all.reshape(N_DEV, -1, d)[:, 0:n_dm].reshape(N_DEV, depth, n_ada * d)
    dmod_cols = lax.dynamic_slice_in_dim(jnp.transpose(dmod_all, (1, 0, 2)), myq * ada_cols, ada_cols, axis=2)
    g_ada_w = _ada_backward(c_all.T, dmod_cols)

    def small_pack(*vs):
        return jnp.concatenate([v.reshape(-1) for v in vs]).reshape(-1, LANES)

    names = ("ada_b", "norm_mix_g", "norm_ffn_g", "conv_b1", "conv_bdw", "conv_ln_g", "conv_ln_b", "conv_b2", "final_g")
    small_w = (ada_b, norm_mix_g, norm_ffn_g, conv_b1, conv_bdw, conv_ln_g, conv_ln_b, conv_b2, final_g)
    small_g = (g_ada_b, g_norm_mix, g_norm_ffn, g_b1, g_bdw, g_lng, g_lnb, g_b2, g_final)
    small_m = (m_ada_b, m_norm_mix_g, m_norm_ffn_g, m_conv_b1, m_conv_bdw, m_conv_ln_g, m_conv_ln_b, m_conv_b2, m_final_g)
    small_v = (v_ada_b, v_norm_mix_g, v_norm_ffn_g, v_conv_b1, v_conv_bdw, v_conv_ln_g, v_conv_ln_b, v_conv_b2, v_final_g)
    sd, sm, sv = _adamw(small_pack(*small_w), small_pack(*small_g), small_pack(*small_m), small_pack(*small_v), "adamw_small")

    def unpack(flat2d):
        flat = flat2d.reshape(-1)
        out, off = {}, 0
        for nm, wv in zip(names, small_w):
            out[nm] = flat[off:off + wv.size].reshape(wv.shape)
            off += wv.size
        return out

    sd, sm, sv = unpack(sd), unpack(sm), unpack(sv)
    grads = {
        "ada_w": g_ada_w, "ada_b": g_ada_b, "norm_mix_g": g_norm_mix, "norm_ffn_g": g_norm_ffn,
        "conv_w1": g_conv_w1, "conv_b1": g_b1, "conv_wdw": g_wdw, "conv_bdw": g_bdw, "conv_ln_g": g_lng,
        "conv_ln_b": g_lnb, "conv_w2": g_conv_w2, "conv_b2": g_b2, "pool_w": g_pool_w, "pool_ls": g_pls,
        "ffn_w_gate": g_ffn_gate, "ffn_w_up": g_ffn_up, "ffn_w_down": g_ffn_down, "final_g": g_final,
    }
    big = {
        "ada_w": (ada_w, m_ada_w, v_ada_w), "conv_w1": (conv_w1, m_conv_w1, v_conv_w1),
        "conv_wdw": (conv_wdw, m_conv_wdw, v_conv_wdw), "conv_w2": (conv_w2, m_conv_w2, v_conv_w2),
        "pool_w": (pool_w, m_pool_w, v_pool_w), "pool_ls": (pool_ls, m_pool_ls, v_pool_ls),
        "ffn_w_gate": (ffn_w_gate, m_ffn_w_gate, v_ffn_w_gate), "ffn_w_up": (ffn_w_up, m_ffn_w_up, v_ffn_w_up),
        "ffn_w_down": (ffn_w_down, m_ffn_w_down, v_ffn_w_down),
    }
    order = ("ada_w", "ada_b", "norm_mix_g", "norm_ffn_g", "conv_w1", "conv_b1", "conv_wdw", "conv_bdw", "conv_ln_g",
             "conv_ln_b", "conv_w2", "conv_b2", "pool_w", "pool_ls", "ffn_w_gate", "ffn_w_up", "ffn_w_down", "final_g")
    delta, new_m, new_v = {}, {}, {}
    for nm in order:
        if nm in big:
            wv, mv, vv = big[nm]
            delta[nm], new_m[nm], new_v[nm] = _adamw_nd(wv, grads[nm].reshape(wv.shape), mv, vv, "adamw_" + nm)
        else:
            delta[nm], new_m[nm], new_v[nm] = sd[nm], sm[nm], sv[nm]
        grads[nm] = grads[nm].reshape((big[nm][0] if nm in big else dict(zip(names, small_w))[nm]).shape)

    return (loss, dx0.reshape(x.shape), *[grads[n] for n in order], *[delta[n] for n in order],
            *[new_m[n] for n in order], *[new_v[n] for n in order])
```

```python
import functools

import jax
import jax.numpy as jnp
from jax import lax
from jax.experimental import pallas as pl
from jax.experimental.pallas import tpu as pltpu

F32 = jnp.float32
_LOWP = jnp.bfloat16
EPS = 1e-6
CONV_WIDTH = 31
POOL_WINDOWS = (2, 4, 8, 16)
ADAM_LR = 0.001
ADAM_B1 = 0.9
ADAM_B2 = 0.999
ADAM_EPS = 1e-08
ADAM_WD = 0.01
ADAM_STEP = 10

N_CHIPS = 4
N_DEV = 8
LANES = 128
SUBLANES = 8
CONV_HALO = 32
POOL_HALO = 16
TILE_ROWS = 64
VMEM_LIMIT = 48 * 1024 * 1024
ADAM_BLOCK_BYTES = 1 << 20
MESH = pl.DeviceIdType.MESH
SDS = jax.ShapeDtypeStruct


def _cparams(*sem):
    return pltpu.CompilerParams(dimension_semantics=sem if sem else None, vmem_limit_bytes=VMEM_LIMIT)


def _row_block(s):
    for tm in (512, 256, 128, 64, 32, 16):
        if s % tm == 0:
            return tm
    raise ValueError(f"sequence length {s} must be a multiple of 16")


def _rows_spec(tm, ncols):
    return pl.BlockSpec((tm, ncols), lambda i, *_: (i, 0))


def _const_spec(shape):
    nd = len(shape)
    return pl.BlockSpec(shape, lambda *_: (0,) * nd)


def _dot(a, b):
    return lax.dot_general(a.astype(_LOWP), b.astype(_LOWP), (((1,), (0,)), ((), ())), preferred_element_type=F32)


def _dot_nt(a, b):
    return lax.dot_general(a.astype(_LOWP), b.astype(_LOWP), (((1,), (1,)), ((), ())), preferred_element_type=F32)


def _dot_tn(a, b):
    return lax.dot_general(a.astype(_LOWP), b.astype(_LOWP), (((0,), (0,)), ((), ())), preferred_element_type=F32)


def _rms(x, gamma):
    r = lax.rsqrt(jnp.mean(x * x, axis=-1, keepdims=True) + EPS)
    xh = x * r
    return xh, r, xh * gamma


def _rms_bwd(dn, xh, r, gamma):
    dxh = dn * gamma
    return r * (dxh - xh * jnp.mean(dxh * xh, axis=-1, keepdims=True))


def _colsum(v):
    return jnp.sum(v, axis=0, keepdims=True)


def _sigmoid(v):
    return jax.nn.sigmoid(v)


def _dsilu(v):
    sg = _sigmoid(v)
    return sg * (1.0 + v * (1.0 - sg))


def _for_tiles(nrows, col0, ncols, fn):
    def step(t, carry):
        r0 = pl.multiple_of(t * TILE_ROWS, TILE_ROWS)
        for col in range(col0, col0 + ncols, LANES):
            fn(r0, col)
        return carry

    lax.fori_loop(0, nrows // TILE_ROWS, step, 0)


def _shift_up(win, b):
    return pltpu.roll(win, win.shape[0] - b, 0) if b else win


def _shift_down(win, b):
    return pltpu.roll(win, b, 0) if b else win


def _tiles3(v):
    return v.reshape(v.shape[0] // SUBLANES, SUBLANES, v.shape[1])


def _mesh_pos():
    return lax.axis_index("x"), lax.axis_index("y"), lax.axis_index("c")


def _other_chips(x, y):
    return [(1 - x, y), (x, 1 - y), (1 - x, 1 - y)]


def _all_gather8(v, name):
    m, n = v.shape

    def body(x_ref, out_ref, send_sems, recv_sems, local_sem):
        x, y, c = _mesh_pos()
        me, sibling = (x, y, c), (x, y, 1 - c)
        chips = _other_chips(x, y)

        def rows(px, py, pc):
            return out_ref.at[pl.ds((4 * px + 2 * py + pc) * m, m), :]

        def copy(k, block, to, src=None):
            return pltpu.make_async_remote_copy(
                src_ref=rows(*block) if src is None else src, dst_ref=rows(*block),
                send_sem=send_sems.at[k], recv_sem=recv_sems.at[k], device_id=to, device_id_type=MESH)

        mine = pltpu.make_async_copy(x_ref, rows(*me), local_sem)
        mine.start()
        first = [copy(0, me, sibling, src=x_ref)]
        first += [copy(1 + j, me, (*chip, c), src=x_ref) for j, chip in enumerate(chips)]
        for cp in first:
            cp.start()
        passed = [copy(4 + j, (*chip, c), sibling) for j, chip in enumerate(chips)]
        for j, chip in enumerate(chips):
            copy(1 + j, (*chip, c), me).wait_recv()
            passed[j].start()
        copy(0, sibling, me).wait_recv()
        for j, chip in enumerate(chips):
            copy(4 + j, (*chip, 1 - c), me).wait_recv()
        for cp in first + passed:
            cp.wait_send()
        mine.wait()

    return pl.pallas_call(
        body, name=name,
        out_shape=SDS((N_DEV * m, n), v.dtype),
        in_specs=[pl.BlockSpec(memory_space=pltpu.VMEM)],
        out_specs=pl.BlockSpec(memory_space=pltpu.VMEM),
        scratch_shapes=[pltpu.SemaphoreType.DMA((7,)), pltpu.SemaphoreType.DMA((7,)), pltpu.SemaphoreType.DMA],
    )(v)


class _Exchange:
    N_SEMS = 7

    def __init__(self, kind, arrays):
        self.kind, self.n = kind, len(arrays)
        if kind == "gather":
            self.out_shape = [SDS((N_CHIPS,) + a.shape, a.dtype) for a in arrays]
        else:
            self.out_shape = [SDS((N_CHIPS, 2 * a.shape[1], a.shape[2]), a.dtype) for a in arrays]
        self.scratch = [pltpu.SemaphoreType.DMA((self.n, self.N_SEMS)), pltpu.SemaphoreType.DMA((self.n, self.N_SEMS))]

    def _ctx(self, ins, outs, send_sems, recv_sems):
        x, y, c = _mesh_pos()
        myq = 2 * x + y
        sibling = (x, y, 1 - c)
        chips = _other_chips(x, y)
        gather = self.kind == "gather"

        def place(a, j, h):
            hr = outs[a].shape[1] // 2
            px, py = chips[j] if j < 3 else (x, y)
            return outs[a].at[2 * px + py if gather else j, pl.ds(h * hr, hr)]

        def copy(a, k, src, dst, to):
            return pltpu.make_async_remote_copy(src_ref=src, dst_ref=dst, send_sem=send_sems.at[a, k],
                                                recv_sem=recv_sems.at[a, k], device_id=to, device_id_type=MESH)

        def first(a, j):
            hr = outs[a].shape[1] // 2
            if gather:
                src = ins[a].at[pl.ds(c * hr, hr)]
                dst = outs[a].at[myq, pl.ds(c * hr, hr)]
            else:
                px, py = chips[j]
                src = ins[a].at[2 * px + py]
                dst = outs[a].at[j, pl.ds(c * hr, hr)]
            return copy(a, j, src, dst, (*chips[j], c))

        def landed(a, j):
            return copy(a, j, place(a, j, c), place(a, j, c), sibling)

        def relay(a, j):
            return copy(a, 3 + j, place(a, j, c), place(a, j, c), sibling)

        def relayed(a, j):
            return copy(a, 3 + j, place(a, j, 1 - c), place(a, j, 1 - c), sibling)

        def own(a):
            return copy(a, 6, ins[a].at[myq], place(a, 3, c), sibling)

        def own_landed(a):
            return copy(a, 6, place(a, 3, 1 - c), place(a, 3, 1 - c), sibling)

        return first, landed, relay, relayed, own, own_landed

    def start(self, ins, outs, send_sems, recv_sems):
        first, _, _, _, own, _ = self._ctx(ins, outs, send_sems, recv_sems)
        for a in range(self.n):
            for j in range(3):
                first(a, j).start()
            if self.kind == "reduce":
                own(a).start()

    def relay(self, ins, outs, send_sems, recv_sems):
        _, landed, relay, _, _, _ = self._ctx(ins, outs, send_sems, recv_sems)
        for a in range(self.n):
            for j in range(3):
                landed(a, j).wait_recv()
                relay(a, j).start()

    def finish(self, ins, outs, send_sems, recv_sems):
        first, _, relay, relayed, own, own_landed = self._ctx(ins, outs, send_sems, recv_sems)
        for a in range(self.n):
            for j in range(3):
                relayed(a, j).wait_recv()
            if self.kind == "reduce":
                own_landed(a).wait_recv()
        for a in range(self.n):
            for j in range(3):
                first(a, j).wait_send()
                relay(a, j).wait_send()
            if self.kind == "reduce":
                own(a).wait_send()

    def alone(self, arrays, name):
        n = self.n

        def body(*refs):
            args = (refs[:n], refs[n:2 * n], refs[2 * n], refs[2 * n + 1])
            self.start(*args)
            self.relay(*args)
            self.finish(*args)

        any_spec = pl.BlockSpec(memory_space=pl.ANY)
        return pl.pallas_call(body, name=name, out_shape=self.out_shape, in_specs=[any_spec] * n,
                              out_specs=[any_spec] * n, scratch_shapes=self.scratch)(*arrays)


RELAY_AT = (5, 8)


def _call(body, *args, carry=None, name, grid, in_specs, out_specs, out_shape, scratch_shapes=(), sem):
    if carry is None:
        outs = pl.pallas_call(body, name=name, grid=grid, in_specs=in_specs, out_specs=out_specs, out_shape=out_shape,
                              scratch_shapes=list(scratch_shapes), compiler_params=_cparams(*sem))(*args)
        return outs, None
    ex, arrays = carry
    n_in, n_out, n_scr, n = len(in_specs), len(out_specs), len(scratch_shapes), ex.n
    total = 1
    for g in grid:
        total *= g
    relay_step = (total * RELAY_AT[0]) // RELAY_AT[1]

    def hosted(*refs):
        ins, cins = refs[:n_in], refs[n_in:n_in + n]
        outs, couts = refs[n_in + n:n_in + n + n_out], refs[n_in + n + n_out:n_in + 2 * n + n_out]
        scr = refs[n_in + 2 * n + n_out:n_in + 2 * n + n_out + n_scr]
        sems = refs[n_in + 2 * n + n_out + n_scr:]
        step = 0
        for ax, g in enumerate(grid):
            step = step * g + pl.program_id(ax)

        @pl.when(step == 0)
        def _():
            ex.start(cins, couts, *sems)

        body(*ins, *outs, *scr)

        @pl.when(step == relay_step)
        def _():
            ex.relay(cins, couts, *sems)

        @pl.when(step == total - 1)
        def _():
            ex.finish(cins, couts, *sems)

    any_spec = pl.BlockSpec(memory_space=pl.ANY)
    res = pl.pallas_call(
        hosted, name=name, grid=grid, in_specs=list(in_specs) + [any_spec] * n,
        out_specs=list(out_specs) + [any_spec] * n, out_shape=list(out_shape) + ex.out_shape,
        scratch_shapes=list(scratch_shapes) + ex.scratch, compiler_params=_cparams(*(("arbitrary",) * len(grid))),
    )(*args, *arrays)
    return res[:n_out], res[n_out:]


def _sibling_swap_halves(ps, name):
    n = len(ps)

    def body(*refs):
        ins, outs, send_sems, recv_sems = refs[:n], refs[n:2 * n], refs[2 * n], refs[2 * n + 1]
        x, y, c = _mesh_pos()
        cps = []
        for a in range(n):
            hr = ins[a].shape[1] // 2
            cp = pltpu.make_async_remote_copy(
                src_ref=ins[a].at[:, pl.ds((1 - c) * hr, hr)], dst_ref=outs[a],
                send_sem=send_sems.at[a], recv_sem=recv_sems.at[a], device_id=(x, y, 1 - c), device_id_type=MESH)
            cp.start()
            cps.append(cp)
        for cp in cps:
            cp.wait()

    any_spec = pl.BlockSpec(memory_space=pl.ANY)
    return pl.pallas_call(
        body, name=name, out_shape=[SDS((p.shape[0], p.shape[1] // 2, p.shape[2]), p.dtype) for p in ps],
        in_specs=[any_spec] * n, out_specs=[any_spec] * n,
        scratch_shapes=[pltpu.SemaphoreType.DMA((n,)), pltpu.SemaphoreType.DMA((n,))],
    )(*ps)


def _divisor_block(rows, cands=(512, 352, 256, 176, 128, 64, 32, 16)):
    for rb in cands:
        if rows % rb == 0:
            return rb
    raise ValueError(rows)


def _add_sibling_half(p, rb1, pos, name):
    nq, r, n = p.shape
    h = r // 2
    rb = _divisor_block(h)
    nb = h // rb

    def body(pos_ref, p_ref, r_ref, o_ref):
        o_ref[...] = (p_ref[...].astype(F32) + r_ref[...].astype(F32)).astype(o_ref.dtype)

    grid_spec = pltpu.PrefetchScalarGridSpec(
        num_scalar_prefetch=1, grid=(nq, nb),
        in_specs=[pl.BlockSpec((1, rb, n), lambda q, i, pos: (q, pos[0] * nb + i, 0)),
                  pl.BlockSpec((1, rb, n), lambda q, i, pos: (q, i, 0))],
        out_specs=pl.BlockSpec((1, rb, n), lambda q, i, pos: (q, i, 0)))
    return pl.pallas_call(
        body, name=name, grid_spec=grid_spec, out_shape=SDS((nq, h, n), p.dtype),
        compiler_params=_cparams("parallel", "parallel"),
    )(pos, p, rb1)


def _sum_devices(g):
    nd, m, n = g.shape

    def body(g_ref, o_ref):
        acc = g_ref[0]
        for d in range(1, nd):
            acc = acc + g_ref[d]
        o_ref[...] = acc

    return pl.pallas_call(body, name="small_grad_sum", out_shape=SDS((m, n), F32))(g)


def _ada_forward(c_all, ada_w, ada_b_cols):
    nl, d, ncol = ada_w.shape
    nb = c_all.shape[0]
    tn = 512 if ncol % 512 == 0 else ncol

    def body(c_ref, w_ref, b_ref, o_ref):
        cv = c_ref[...]
        ca = cv * _sigmoid(cv)
        o_ref[0] = _dot(ca, w_ref[0]) + b_ref[0]

    return pl.pallas_call(
        body, name="ada_forward", grid=(nl, ncol // tn),
        in_specs=[pl.BlockSpec((nb, d), lambda l, j: (0, 0)),
                  pl.BlockSpec((1, d, tn), lambda l, j: (l, 0, j)),
                  pl.BlockSpec((1, 1, tn), lambda l, j: (l, 0, j))],
        out_specs=pl.BlockSpec((1, nb, tn), lambda l, j: (l, 0, j)),
        out_shape=SDS((nl, nb, ncol), F32), compiler_params=_cparams("parallel", "parallel"),
    )(c_all, ada_w, ada_b_cols)


def _ada_backward(c_all_t, dmod_cols):
    d, nb = c_all_t.shape
    nl, _, ncol = dmod_cols.shape
    tn = 512 if ncol % 512 == 0 else ncol

    def body(c_ref, g_ref, o_ref):
        cv = c_ref[...]
        ca = cv * _sigmoid(cv)
        o_ref[0] = _dot(ca, g_ref[0])

    return pl.pallas_call(
        body, name="ada_backward", grid=(nl, ncol // tn),
        in_specs=[pl.BlockSpec((d, nb), lambda l, j: (0, 0)),
                  pl.BlockSpec((1, nb, tn), lambda l, j: (l, 0, j))],
        out_specs=pl.BlockSpec((1, d, tn), lambda l, j: (l, 0, j)),
        out_shape=SDS((nl, d, ncol), F32), compiler_params=_cparams("parallel", "parallel"),
    )(c_all_t, dmod_cols)


def _conv_a_fwd(x, vec, w1, b1):
    s, d = x.shape
    nq, _, hq = w1.shape
    tm = _row_block(s)

    def body(x_ref, vec_ref, w1_ref, b1_ref, h_ref, a_ref, g_ref, glu_ref):
        vec = vec_ref[...]
        _, _, n = _rms(x_ref[...], vec[0:1])
        hb = (n * (1.0 + vec[1:2]) + vec[2:3]).astype(_LOWP)
        h_ref[...] = hb
        u = [_dot(hb, w1_ref[q]) + b1_ref[:, q * hq:(q + 1) * hq] for q in range(nq)]
        for k in range(nq // 2):
            av, gv = u[k], u[nq // 2 + k]
            cols = slice(k * hq, (k + 1) * hq)
            a_ref[:, cols] = av.astype(_LOWP)
            g_ref[:, cols] = gv.astype(_LOWP)
            glu_ref[:, cols] = av * _sigmoid(gv)

    return pl.pallas_call(
        body, name="conv_a_fwd", grid=(s // tm,),
        in_specs=[_rows_spec(tm, d), _const_spec(vec.shape), _const_spec(w1.shape), _const_spec(b1.shape)],
        out_specs=[_rows_spec(tm, d)] * 4,
        out_shape=[SDS((s, d), _LOWP)] * 3 + [SDS((s, d), F32)],
        compiler_params=_cparams("parallel"),
    )(x, vec, w1, b1)


def _conv_b_fwd(glu, w8, vec, w2, x, carry=None):
    s, d = x.shape
    tm = _row_block(s)
    hb = tm // CONV_HALO

    def body(glu_ref, halo_ref, w8_ref, vec_ref, w2_ref, x_ref, cw_ref, s_ref, y_ref, xo_ref, ext_ref):
        i = pl.program_id(0)
        vec = vec_ref[...]
        ext_ref[0:CONV_HALO, :] = jnp.where(i > 0, halo_ref[...], 0.0)
        ext_ref[CONV_HALO:, :] = glu_ref[...]

        def tile(r0, c0):
            lanes = slice(c0, c0 + LANES)
            win = ext_ref[pl.ds(r0, TILE_ROWS + CONV_HALO), lanes]
            acc = jnp.zeros((TILE_ROWS // SUBLANES, SUBLANES, LANES), F32)
            for b in range(SUBLANES):
                sh = _shift_up(win, b)
                for a in range(CONV_HALO // SUBLANES + 1):
                    k = SUBLANES * a + b - (CONV_HALO - CONV_WIDTH + 1)
                    if 0 <= k < CONV_WIDTH:
                        acc = acc + w8_ref[k, :, lanes][None] * _tiles3(sh[SUBLANES * a:SUBLANES * a + TILE_ROWS])
            cw_ref[pl.ds(r0, TILE_ROWS), lanes] = acc.reshape(TILE_ROWS, LANES)

        _for_tiles(tm, 0, d, tile)
        cw = cw_ref[...] + vec[0:1]
        cw_ref[...] = cw
        cc = cw - jnp.mean(cw, axis=-1, keepdims=True)
        ch = cc * lax.rsqrt(jnp.mean(cc * cc, axis=-1, keepdims=True) + EPS)
        lo = ch * vec[1:2] + vec[2:3]
        sv = (lo * _sigmoid(lo)).astype(_LOWP)
        s_ref[...] = sv
        y = _dot(sv, w2_ref[...]) + vec[3:4]
        y_ref[...] = y.astype(_LOWP)
        xo_ref[...] = x_ref[...] + (1.0 + vec[4:5]) * y

    return _call(
        body, glu, glu, w8, vec, w2, x, carry=carry, name="conv_b_fwd", grid=(s // tm,),
        in_specs=[_rows_spec(tm, d),
                  pl.BlockSpec((CONV_HALO, d), lambda i: (jnp.maximum(i * hb - 1, 0), 0)),
                  _const_spec(w8.shape), _const_spec(vec.shape), _const_spec(w2.shape), _rows_spec(tm, d)],
        out_specs=[_rows_spec(tm, d)] * 4,
        out_shape=[SDS((s, d), F32), SDS((s, d), _LOWP), SDS((s, d), _LOWP), SDS((s, d), F32)],
        scratch_shapes=[pltpu.VMEM((tm + CONV_HALO, d), F32)], sem=("parallel",))


def _ffn_fwd(x, vec, wg, wu, wd, name, carry=None):
    s, d = x.shape
    nq, _, fq = wg.shape
    tm = _row_block(s)

    def body(x_ref, vec_ref, wg_ref, wu_ref, wd_ref, h_ref, gt_ref, up_ref, act_ref, y_ref, xo_ref, hb_ref, yacc_ref):
        q = pl.program_id(1)
        vec = vec_ref[...]

        @pl.when(q == 0)
        def _():
            _, _, n = _rms(x_ref[...], vec[0:1])
            hb = (n * (1.0 + vec[1:2]) + vec[2:3]).astype(_LOWP)
            hb_ref[...] = hb
            h_ref[...] = hb
            yacc_ref[...] = jnp.zeros_like(yacc_ref)

        hb = hb_ref[...]
        gt = _dot(hb, wg_ref[0])
        up = _dot(hb, wu_ref[0])
        act = (gt * _sigmoid(gt) * up).astype(_LOWP)
        gt_ref[0] = gt.astype(_LOWP)
        up_ref[0] = up.astype(_LOWP)
        act_ref[0] = act
        yacc_ref[...] += _dot(act, wd_ref[0])

        @pl.when(q == nq - 1)
        def _():
            y = yacc_ref[...]
            y_ref[...] = y.astype(_LOWP)
            xo_ref[...] = x_ref[...] + (1.0 + vec[3:4]) * y

    rows = pl.BlockSpec((tm, d), lambda i, q: (i, 0))
    hid = pl.BlockSpec((1, tm, fq), lambda i, q: (q, i, 0))
    return _call(
        body, x, vec, wg, wu, wd, carry=carry, name=name, grid=(s // tm, nq),
        in_specs=[rows, pl.BlockSpec(vec.shape, lambda i, q: (0, 0)),
                  pl.BlockSpec((1, d, fq), lambda i, q: (q, 0, 0)),
                  pl.BlockSpec((1, d, fq), lambda i, q: (q, 0, 0)),
                  pl.BlockSpec((1, fq, d), lambda i, q: (q, 0, 0))],
        out_specs=[rows, hid, hid, hid, rows, rows],
        out_shape=[SDS((s, d), _LOWP)] + [SDS((nq, s, fq), _LOWP)] * 3 + [SDS((s, d), _LOWP), SDS((s, d), F32)],
        scratch_shapes=[pltpu.VMEM((tm, d), _LOWP), pltpu.VMEM((tm, d), F32)], sem=("parallel", "arbitrary"))


def _pool_inv_count(t0, nrows, w):
    t = t0 + lax.broadcasted_iota(jnp.int32, (nrows, LANES), 0)
    return 1.0 / jnp.minimum(t + 1, w).astype(F32)


def _pool_fwd(x, vec, pw):
    s, d = x.shape
    ng, gd, _ = pw.shape
    tm = _row_block(s)
    hb = tm // POOL_HALO

    def body(x_ref, halo_ref, vec_ref, pw_ref, mixed_ref, yp_ref, xo_ref, hext_ref):
        i = pl.program_id(0)
        vec = vec_ref[...]

        def modulated(xv):
            _, _, n = _rms(xv, vec[0:1])
            return n * (1.0 + vec[1:2]) + vec[2:3]

        hext_ref[0:POOL_HALO, :] = jnp.where(i > 0, modulated(halo_ref[...]), 0.0)
        hext_ref[POOL_HALO:, :] = modulated(x_ref[...])

        for g in range(ng):
            def tile(r0, c0, g=g):
                lanes = slice(c0, c0 + LANES)
                win = hext_ref[pl.ds(r0, TILE_ROWS + POOL_HALO), lanes]
                acc = win
                for step in range(g + 1):
                    acc = acc + _shift_down(acc, 2 ** step)
                inv = _pool_inv_count(i * tm + r0, TILE_ROWS, POOL_WINDOWS[g])
                mixed = acc[POOL_HALO:] * inv - win[POOL_HALO:]
                mixed_ref[pl.ds(r0, TILE_ROWS), lanes] = mixed.astype(_LOWP)

            _for_tiles(tm, g * gd, gd, tile)

        for g in range(ng):
            cols = slice(g * gd, (g + 1) * gd)
            yp = _dot(mixed_ref[:, cols], pw_ref[g])
            yp_ref[:, cols] = yp.astype(_LOWP)
            xo_ref[:, cols] = x_ref[:, cols] + (1.0 + vec[3:4, cols]) * (yp * vec[4:5, cols])

    return pl.pallas_call(
        body, name="pool_fwd", grid=(s // tm,),
        in_specs=[_rows_spec(tm, d),
                  pl.BlockSpec((POOL_HALO, d), lambda i: (jnp.maximum(i * hb - 1, 0), 0)),
                  _const_spec(vec.shape), _const_spec(pw.shape)],
        out_specs=[_rows_spec(tm, d)] * 3,
        out_shape=[SDS((s, d), _LOWP), SDS((s, d), _LOWP), SDS((s, d), F32)],
        scratch_shapes=[pltpu.VMEM((tm + POOL_HALO, d), F32)],
        compiler_params=_cparams("parallel"),
    )(x, x, vec, pw)


def _final_fwd_bwd(x, tgt, gamma):
    s, d = x.shape
    tm = _row_block(s)
    nsteps = s // tm

    def body(x_ref, t_ref, g_ref, dx_ref, sums_ref):
        i = pl.program_id(0)
        gamma_v = g_ref[...]
        xh, r, out = _rms(x_ref[...], gamma_v)
        err = out - t_ref[...]
        dout = err * (1.0 / d)

        @pl.when(i == 0)
        def _():
            sums_ref[...] = jnp.zeros_like(sums_ref)

        sums_ref[0:1, :] += _colsum(dout * xh)
        sums_ref[1:2, :] += _colsum(0.5 * err * dout)
        dx_ref[...] = _rms_bwd(dout, xh, r, gamma_v)

        @pl.when(i == nsteps - 1)
        def _():
            sums_ref[1:2, :] = jnp.broadcast_to(jnp.sum(sums_ref[1:2, :], axis=-1, keepdims=True), (1, d))

    return pl.pallas_call(
        body, name="final_loss", grid=(nsteps,),
        in_specs=[_rows_spec(tm, d), _rows_spec(tm, d), _const_spec(gamma.shape)],
        out_specs=[_rows_spec(tm, d), _const_spec((2, d))],
        out_shape=[SDS((s, d), F32), SDS((2, d), F32)],
        compiler_params=_cparams("arbitrary"),
    )(x, tgt, gamma)


def _ffn_bwd(dxo, x, vec, gt, up, y, wg, wu, wd, name, carry=None):
    s, d = x.shape
    nq, _, fq = wg.shape
    tm = _row_block(s)

    def body(dxo_ref, x_ref, vec_ref, gt_ref, up_ref, y_ref, wg_ref, wu_ref, wd_ref,
             dgt_ref, dup_ref, dy_ref, dx_ref, sums_ref, dyb_ref, dh_ref):
        i, q = pl.program_id(0), pl.program_id(1)
        vec = vec_ref[...]

        @pl.when((i == 0) & (q == 0))
        def _():
            sums_ref[...] = jnp.zeros_like(sums_ref)

        @pl.when(q == 0)
        def _():
            dxv = dxo_ref[...]
            dyb = (dxv * (1.0 + vec[3:4])).astype(_LOWP)
            dyb_ref[...] = dyb
            dy_ref[...] = dyb
            dh_ref[...] = jnp.zeros_like(dh_ref)
            sums_ref[3:4, :] += _colsum(dxv * y_ref[...].astype(F32))

        dact = _dot_nt(dyb_ref[...], wd_ref[0])
        gtv = gt_ref[0].astype(F32)
        sg = _sigmoid(gtv)
        dgt = (dact * up_ref[0].astype(F32) * (sg * (1.0 + gtv * (1.0 - sg)))).astype(_LOWP)
        dup = (dact * (gtv * sg)).astype(_LOWP)
        dgt_ref[0] = dgt
        dup_ref[0] = dup
        dh_ref[...] += _dot_nt(dgt, wg_ref[0]) + _dot_nt(dup, wu_ref[0])

        @pl.when(q == nq - 1)
        def _():
            dh = dh_ref[...]
            xh, r, n = _rms(x_ref[...], vec[0:1])
            sums_ref[2:3, :] += _colsum(dh)
            sums_ref[1:2, :] += _colsum(dh * n)
            dn = dh * (1.0 + vec[1:2])
            sums_ref[0:1, :] += _colsum(dn * xh)
            dx_ref[...] = dxo_ref[...] + _rms_bwd(dn, xh, r, vec[0:1])

    rows = pl.BlockSpec((tm, d), lambda i, q: (i, 0))
    hid = pl.BlockSpec((1, tm, fq), lambda i, q: (q, i, 0))
    return _call(
        body, dxo, x, vec, gt, up, y, wg, wu, wd, carry=carry, name=name, grid=(s // tm, nq),
        in_specs=[rows, rows, pl.BlockSpec(vec.shape, lambda i, q: (0, 0)), hid, hid, rows,
                  pl.BlockSpec((1, d, fq), lambda i, q: (q, 0, 0)),
                  pl.BlockSpec((1, d, fq), lambda i, q: (q, 0, 0)),
                  pl.BlockSpec((1, fq, d), lambda i, q: (q, 0, 0))],
        out_specs=[hid, hid, rows, rows, pl.BlockSpec((4, d), lambda i, q: (0, 0))],
        out_shape=[SDS((nq, s, fq), _LOWP)] * 2 + [SDS((s, d), _LOWP), SDS((s, d), F32), SDS((4, d), F32)],
        scratch_shapes=[pltpu.VMEM((tm, d), _LOWP), pltpu.VMEM((tm, d), F32)], sem=("arbitrary", "arbitrary"))


def _wgrad(a, b, a_spec, b_spec, nq, ka, nb, s, name, stack=None, stack_blocks=1, block=0):
    tk = min(2048, s)
    nk = s // tk

    def body(a_ref, b_ref, *rest):
        o_ref, acc_ref = rest[-2:]
        k = pl.program_id(1)

        @pl.when(k == 0)
        def _():
            acc_ref[...] = jnp.zeros_like(acc_ref)

        av = a_ref[...].reshape(tk, ka)
        bv = b_ref[...].reshape(tk, nb)
        acc_ref[...] += _dot_tn(av, bv)

        @pl.when(k == nk - 1)
        def _():
            o_ref[0] = acc_ref[...].astype(o_ref.dtype)

    extra = [] if stack is None else [stack]
    return pl.pallas_call(
        body, name=name, grid=(nq, nk),
        in_specs=[a_spec(tk), b_spec(tk)] + [pl.BlockSpec(memory_space=pl.ANY)] * len(extra),
        out_specs=pl.BlockSpec((1, ka, nb), lambda q, k: (q, block, 0)),
        out_shape=SDS((nq, stack_blocks * ka, nb), _LOWP),
        scratch_shapes=[pltpu.VMEM((ka, nb), F32)],
        input_output_aliases={2: 0} if extra else {},
        compiler_params=_cparams("parallel", "arbitrary"),
    )(a, b, *extra)


def _shared_rows(ncols):
    return lambda tk: pl.BlockSpec((tk, ncols), lambda q, k: (k, 0))


def _column_block(ncols):
    return lambda tk: pl.BlockSpec((tk, ncols), lambda q, k: (k, q))


def _quarter_major(ncols):
    return lambda tk: pl.BlockSpec((1, tk, ncols), lambda q, k: (q, k, 0))


def _pool_bwd(dxo, x, yp, vec, pw):
    s, d = x.shape
    ng, gd, _ = pw.shape
    tm = _row_block(s)
    hb = tm // POOL_HALO
    nsteps = s // tm
    last_halo = s // POOL_HALO - 1

    def body(dxo_ref, halo_ref, x_ref, yp_ref, vec_ref, pw_ref, dyp_ref, dx_ref, sums_ref, eext_ref, dh_ref):
        i = pl.program_id(0)
        vec = vec_ref[...]

        @pl.when(i == 0)
        def _():
            sums_ref[...] = jnp.zeros_like(sums_ref)

        dxv = dxo_ref[...]
        ypv = yp_ref[...].astype(F32)
        gate1 = 1.0 + vec[3:4]
        sums_ref[3:4, :] += _colsum(dxv * (ypv * vec[4:5]))
        sums_ref[4:5, :] += _colsum(dxv * gate1 * ypv)
        dyp = (dxv * gate1 * vec[4:5]).astype(_LOWP)
        dyp_ref[...] = dyp
        dyp_halo = (jnp.where(i < nsteps - 1, halo_ref[...], 0.0) * gate1 * vec[4:5]).astype(_LOWP)
        for g in range(ng):
            cols = slice(g * gd, (g + 1) * gd)
            dm = _dot_nt(dyp[:, cols], pw_ref[g])
            dh_ref[:, cols] = dm
            inv = _pool_inv_count(i * tm, tm, POOL_WINDOWS[g])
            eext_ref[0:tm, cols] = dm * jnp.concatenate([inv] * (gd // LANES), axis=1)
            eext_ref[tm:, cols] = _dot_nt(dyp_halo[:, cols], pw_ref[g]) * (1.0 / POOL_WINDOWS[g])

        for g in range(ng):
            def tile(r0, c0, g=g):
                acc = eext_ref[pl.ds(r0, TILE_ROWS + POOL_HALO), c0:c0 + LANES]
                for step in range(g + 1):
                    acc = acc + _shift_up(acc, 2 ** step)
                here = (pl.ds(r0, TILE_ROWS), slice(c0, c0 + LANES))
                dh_ref[here] = acc[:TILE_ROWS] - dh_ref[here]

            _for_tiles(tm, g * gd, gd, tile)

        dh = dh_ref[...]
        xh, r, n = _rms(x_ref[...], vec[0:1])
        sums_ref[2:3, :] += _colsum(dh)
        sums_ref[1:2, :] += _colsum(dh * n)
        dn = dh * (1.0 + vec[1:2])
        sums_ref[0:1, :] += _colsum(dn * xh)
        dx_ref[...] = dxv + _rms_bwd(dn, xh, r, vec[0:1])

    return pl.pallas_call(
        body, name="pool_bwd", grid=(nsteps,),
        in_specs=[_rows_spec(tm, d),
                  pl.BlockSpec((POOL_HALO, d), lambda i: (jnp.minimum((i + 1) * hb, last_halo), 0)),
                  _rows_spec(tm, d), _rows_spec(tm, d), _const_spec(vec.shape), _const_spec(pw.shape)],
        out_specs=[_rows_spec(tm, d), _rows_spec(tm, d), _const_spec((5, d))],
        out_shape=[SDS((s, d), _LOWP), SDS((s, d), F32), SDS((5, d), F32)],
        scratch_shapes=[pltpu.VMEM((tm + POOL_HALO, d), F32), pltpu.VMEM((tm, d), F32)],
        compiler_params=_cparams("arbitrary"),
    )(dxo, dxo, x, yp, vec, pw)


def _conv_b_bwd(dxo, cw, y, vec, w2):
    s, d = dxo.shape
    tm = _row_block(s)

    def body(dxo_ref, cw_ref, y_ref, vec_ref, w2_ref, dy_ref, dcw_ref, sums_ref):
        i = pl.program_id(0)
        vec = vec_ref[...]

        @pl.when(i == 0)
        def _():
            sums_ref[...] = jnp.zeros_like(sums_ref)

        dxv = dxo_ref[...]
        dy = dxv * (1.0 + vec[4:5])
        sums_ref[0:1, :] += _colsum(dxv * y_ref[...].astype(F32))
        sums_ref[1:2, :] += _colsum(dy)
        dyb = dy.astype(_LOWP)
        dy_ref[...] = dyb
        ds = _dot_nt(dyb, w2_ref[...])
        cw = cw_ref[...]
        cc = cw - jnp.mean(cw, axis=-1, keepdims=True)
        rs = lax.rsqrt(jnp.mean(cc * cc, axis=-1, keepdims=True) + EPS)
        ch = cc * rs
        lo = ch * vec[1:2] + vec[2:3]
        dlo = ds * _dsilu(lo)
        sums_ref[2:3, :] += _colsum(dlo * ch)
        sums_ref[3:4, :] += _colsum(dlo)
        dch = dlo * vec[1:2]
        dcw = rs * (dch - jnp.mean(dch, axis=-1, keepdims=True) - ch * jnp.mean(dch * ch, axis=-1, keepdims=True))
        sums_ref[4:5, :] += _colsum(dcw)
        dcw_ref[...] = dcw

    return pl.pallas_call(
        body, name="conv_b_bwd", grid=(s // tm,),
        in_specs=[_rows_spec(tm, d), _rows_spec(tm, d), _rows_spec(tm, d), _const_spec(vec.shape), _const_spec(w2.shape)],
        out_specs=[_rows_spec(tm, d), _rows_spec(tm, d), _const_spec((5, d))],
        out_shape=[SDS((s, d), _LOWP), SDS((s, d), F32), SDS((5, d), F32)],
        compiler_params=_cparams("arbitrary"),
    )(dxo, cw, y, vec, w2)


def _conv_a_bwd(dcw, glu, a, g, x, dxo, vec, w8, w1, carry=None):
    s, d = x.shape
    nq, _, hq = w1.shape
    tm = _row_block(s)
    hb = tm // CONV_HALO
    nsteps = s // tm
    last_halo = s // CONV_HALO - 1
    lead = CONV_HALO - CONV_WIDTH + 1

    def body(dcw_ref, dhalo_ref, glu_ref, ghalo_ref, a_ref, g_ref, x_ref, dxo_ref, vec_ref, w8_ref, w1_ref,
             du_ref, dx_ref, sums_ref, db1_ref, dwdw_ref, dext_ref, gext_ref, dglu_ref, wacc_ref):
        i = pl.program_id(0)
        vec = vec_ref[...]

        @pl.when(i == 0)
        def _():
            sums_ref[...] = jnp.zeros_like(sums_ref)
            db1_ref[...] = jnp.zeros_like(db1_ref)
            wacc_ref[...] = jnp.zeros_like(wacc_ref)

        dext_ref[0:tm, :] = dcw_ref[...]
        dext_ref[tm:, :] = jnp.where(i < nsteps - 1, dhalo_ref[...], 0.0)
        gext_ref[0:CONV_HALO, :] = jnp.where(i > 0, ghalo_ref[...], 0.0)
        gext_ref[CONV_HALO:, :] = glu_ref[...]

        def tile(r0, c0):
            lanes = slice(c0, c0 + LANES)
            dwin = dext_ref[pl.ds(r0, TILE_ROWS + CONV_HALO), lanes]
            acc = jnp.zeros((TILE_ROWS // SUBLANES, SUBLANES, LANES), F32)
            for b in range(SUBLANES):
                sh = _shift_up(dwin, b)
                for a8 in range(CONV_HALO // SUBLANES):
                    k = CONV_WIDTH - 1 - (SUBLANES * a8 + b)
                    if 0 <= k < CONV_WIDTH:
                        acc = acc + w8_ref[k, :, lanes][None] * _tiles3(sh[SUBLANES * a8:SUBLANES * a8 + TILE_ROWS])
            dglu_ref[pl.ds(r0, TILE_ROWS), lanes] = acc.reshape(TILE_ROWS, LANES)

            dc3 = _tiles3(dwin[:TILE_ROWS])
            gwin = gext_ref[pl.ds(r0, TILE_ROWS + CONV_HALO), lanes]
            for b in range(SUBLANES):
                sh = _shift_up(gwin, b)
                for a8 in range(CONV_HALO // SUBLANES + 1):
                    k = SUBLANES * a8 + b - lead
                    if 0 <= k < CONV_WIDTH:
                        part = jnp.sum(dc3 * _tiles3(sh[SUBLANES * a8:SUBLANES * a8 + TILE_ROWS]), axis=0)
                        wacc_ref[k, :, lanes] += part

        _for_tiles(tm, 0, d, tile)

        dglu = dglu_ref[...]
        av, gv = a_ref[...].astype(F32), g_ref[...].astype(F32)
        sg = _sigmoid(gv)
        da = dglu * sg
        dg = dglu * av * sg * (1.0 - sg)
        du_ref[:, 0:d] = da.astype(_LOWP)
        du_ref[:, d:2 * d] = dg.astype(_LOWP)
        db1_ref[:, 0:d] += _colsum(da)
        db1_ref[:, d:2 * d] += _colsum(dg)
        dh = jnp.zeros((tm, d), F32)
        for q in range(nq):
            dh = dh + _dot_nt(du_ref[:, q * hq:(q + 1) * hq], w1_ref[q])
        xh, r, n = _rms(x_ref[...], vec[0:1])
        sums_ref[2:3, :] += _colsum(dh)
        sums_ref[1:2, :] += _colsum(dh * n)
        dn = dh * (1.0 + vec[1:2])
        sums_ref[0:1, :] += _colsum(dn * xh)
        dx_ref[...] = dxo_ref[...] + _rms_bwd(dn, xh, r, vec[0:1])

        @pl.when(i == nsteps - 1)
        def _():
            dwdw_ref[...] = jnp.sum(wacc_ref[...], axis=1)

    return _call(
        body, dcw, dcw, glu, glu, a, g, x, dxo, vec, w8, w1, carry=carry, name="conv_a_bwd", grid=(nsteps,),
        in_specs=[_rows_spec(tm, d),
                  pl.BlockSpec((CONV_HALO, d), lambda i: (jnp.minimum((i + 1) * hb, last_halo), 0)),
                  _rows_spec(tm, d),
                  pl.BlockSpec((CONV_HALO, d), lambda i: (jnp.maximum(i * hb - 1, 0), 0)),
                  _rows_spec(tm, d), _rows_spec(tm, d), _rows_spec(tm, d), _rows_spec(tm, d),
                  _const_spec(vec.shape), _const_spec(w8.shape), _const_spec(w1.shape)],
        out_specs=[_rows_spec(tm, 2 * d), _rows_spec(tm, d), _const_spec((3, d)), _const_spec((1, 2 * d)),
                   _const_spec((CONV_WIDTH, d))],
        out_shape=[SDS((s, 2 * d), _LOWP), SDS((s, d), F32), SDS((3, d), F32), SDS((1, 2 * d), F32),
                   SDS((CONV_WIDTH, d), F32)],
        scratch_shapes=[pltpu.VMEM((tm + CONV_HALO, d), F32), pltpu.VMEM((tm + CONV_HALO, d), F32),
                        pltpu.VMEM((tm, d), F32), pltpu.VMEM((CONV_WIDTH, SUBLANES, d), F32)],
        sem=("arbitrary",))


def _adamw(w, g, m, v, name):
    r, c = w.shape
    rb = r
    for cand in (512, 352, 256, 184, 128, 64, 32, 16, 8):
        if r % cand == 0 and cand * c * 4 <= ADAM_BLOCK_BYTES:
            rb = cand
            break

    def body(w_ref, g_ref, m_ref, v_ref, d_ref, mo_ref, vo_ref):
        gv = g_ref[...]
        mn = ADAM_B1 * m_ref[...] + (1.0 - ADAM_B1) * gv
        vn = ADAM_B2 * v_ref[...] + (1.0 - ADAM_B2) * (gv * gv)
        m_hat = mn / (1.0 - ADAM_B1 ** ADAM_STEP)
        v_hat = vn / (1.0 - ADAM_B2 ** ADAM_STEP)
        d_ref[...] = -ADAM_LR * (m_hat / (jnp.sqrt(v_hat) + ADAM_EPS) + ADAM_WD * w_ref[...])
        mo_ref[...] = mn
        vo_ref[...] = vn

    spec = pl.BlockSpec((rb, c), lambda i: (i, 0))
    return pl.pallas_call(
        body, name=name, grid=(r // rb,), in_specs=[spec] * 4, out_specs=[spec] * 3,
        out_shape=[SDS((r, c), F32)] * 3, compiler_params=_cparams("parallel"),
    )(w, g, m, v)


def _adamw_nd(w, g, m, v, name):
    shape = w.shape
    two_d = (-1, shape[-1])
    outs = _adamw(w.reshape(two_d), g.reshape(two_d), m.reshape(two_d), v.reshape(two_d), name)
    return [o.reshape(shape) for o in outs]


def _adamw_from_slots(w, m, v, segs, pos, name):
    r, c_ = w.shape
    rb = _divisor_block(r, (256, 176, 128, 64, 32, 16))
    for slots, _, first, rows in segs:
        half = slots.shape[1] // 2
        while first % rb or rows % rb or half % rb:
            rb //= 2
    assert rb % 16 == 0, (name, rb)
    plan, start = [], 0
    for slots, _, first, rows in segs:
        plan.append((start, rows // rb, first // rb, slots.shape[1] // 2 // rb))
        start += rows // rb
    assert start * rb == r, (name, start, rb, r)

    def body(pos_ref, w_ref, m_ref, v_ref, *rest):
        seg_refs, (g_ref, d_ref, mo_ref, vo_ref) = rest[:2 * len(segs)], rest[2 * len(segs):]
        i = pl.program_id(0)
        for k, (s0, nblk, b0, nbh) in enumerate(plan):
            @pl.when((i >= s0) & (i < s0 + nblk))
            def _(k=k, s0=s0, b0=b0, nbh=nbh):
                slots = seg_refs[2 * k][...].astype(F32)
                blk = b0 + i - s0
                in_my_half = (blk >= pos_ref[0] * nbh) & (blk < (pos_ref[0] + 1) * nbh)
                own = jnp.where(in_my_half, seg_refs[2 * k + 1][0].astype(F32), slots[3])
                gv = ((own + slots[0]) + slots[1]) + slots[2]
                mn = ADAM_B1 * m_ref[...] + (1.0 - ADAM_B1) * gv
                vn = ADAM_B2 * v_ref[...] + (1.0 - ADAM_B2) * (gv * gv)
                m_hat = mn / (1.0 - ADAM_B1 ** ADAM_STEP)
                v_hat = vn / (1.0 - ADAM_B2 ** ADAM_STEP)
                g_ref[...] = gv
                d_ref[...] = -ADAM_LR * (m_hat / (jnp.sqrt(v_hat) + ADAM_EPS) + ADAM_WD * w_ref[...])
                mo_ref[...] = mn
                vo_ref[...] = vn

    spec = pl.BlockSpec((rb, c_), lambda i, pos: (i, 0))
    seg_specs, seg_args = [], []
    for (slots, mine, _, _), (s0, nblk, b0, nbh) in zip(segs, plan):
        def in_seg(i, s0=s0, nblk=nblk):
            return jnp.clip(i - s0, 0, nblk - 1)
        seg_specs.append(pl.BlockSpec((N_CHIPS, rb, c_), lambda i, pos, f=in_seg, b0=b0: (0, b0 + f(i), 0)))
        seg_specs.append(pl.BlockSpec(
            (1, rb, c_), lambda i, pos, f=in_seg, b0=b0, nbh=nbh: (pos[1], jnp.clip(b0 + f(i) - pos[0] * nbh, 0, nbh - 1), 0)))
        seg_args += [slots, mine]
    grid_spec = pltpu.PrefetchScalarGridSpec(
        num_scalar_prefetch=1, grid=(r // rb,), in_specs=[spec] * 3 + seg_specs, out_specs=[spec] * 4)
    return pl.pallas_call(
        body, name=name, grid_spec=grid_spec, out_shape=[SDS((r, c_), F32)] * 4,
        compiler_params=_cparams("arbitrary"),
    )(pos, w, m, v, *seg_args)


def _rows_of(*vs):
    return jnp.concatenate([v.reshape(-1, v.shape[-1]) for v in vs], axis=0)


def kernel(x, c, ada_w, ada_b, norm_mix_g, norm_ffn_g, conv_w1, conv_b1, conv_wdw, conv_bdw, conv_ln_g, conv_ln_b, conv_w2, conv_b2, pool_w, pool_ls, ffn_w_gate, ffn_w_up, ffn_w_down, final_g, loss_target, m_ada_w, m_ada_b, m_norm_mix_g, m_norm_ffn_g, m_conv_w1, m_conv_b1, m_conv_wdw, m_conv_bdw, m_conv_ln_g, m_conv_ln_b, m_conv_w2, m_conv_b2, m_pool_w, m_pool_ls, m_ffn_w_gate, m_ffn_w_up, m_ffn_w_down, m_final_g, v_ada_w, v_ada_b, v_norm_mix_g, v_norm_ffn_g, v_conv_w1, v_conv_b1, v_conv_wdw, v_conv_bdw, v_conv_ln_g, v_conv_ln_b, v_conv_w2, v_conv_b2, v_pool_w, v_pool_ls, v_ffn_w_gate, v_ffn_w_up, v_ffn_w_down, v_final_g):
    s, d = x.shape[1], x.shape[2]
    depth = ada_w.shape[0]
    assert depth == 2 and conv_w1.shape[0] == 1 and pool_w.shape[0] == 1, "one conv layer then one pool layer"
    fq = ffn_w_gate.shape[2]
    n_ada = ada_b.shape[1] // d
    ada_cols = ada_w.shape[2]
    ng, gq, gd = pool_w.shape[1], pool_w.shape[2], pool_w.shape[3]
    cq = conv_wdw.shape[2]
    ax, ay, ac = _mesh_pos()
    myq = 2 * ax + ay
    me = 4 * ax + 2 * ay + ac
    x0 = x.reshape(s, d)
    tgt = loss_target.reshape(s, d)

    c_all = _all_gather8(c.reshape(d // LANES, LANES), "gather_c").reshape(N_DEV, d)
    ada_b_cols = lax.dynamic_slice_in_dim(ada_b, myq * ada_cols, ada_cols, axis=1).reshape(depth, 1, ada_cols)
    mod_part = _ada_forward(c_all, ada_w, ada_b_cols)
    mod_all = _all_gather8(mod_part.reshape(-1, LANES), "gather_mod").reshape(N_DEV, depth, N_DEV, ada_cols)
    mod_mine = lax.dynamic_index_in_dim(mod_all[0::2], me, axis=2, keepdims=False)
    mod = jnp.transpose(mod_mine, (1, 0, 2)).reshape(depth, n_ada, d)

    def gathered(ex_out, shards):
        return [lax.dynamic_update_slice(g, sh[None], (myq, 0, 0)) for g, sh in zip(ex_out, shards)]

    pos = jnp.stack([ac, myq]).astype(jnp.int32)
    conv_shards = [conv_w1[0].astype(_LOWP), conv_w2[0].astype(_LOWP)]
    ffn0_shards = [ffn_w_gate[0].astype(_LOWP), ffn_w_up[0].astype(_LOWP), ffn_w_down[0].astype(_LOWP)]
    l1_shards = [pool_w[0].reshape(ng * gq, gd).astype(_LOWP), ffn_w_gate[1].astype(_LOWP), ffn_w_up[1].astype(_LOWP),
                 ffn_w_down[1].astype(_LOWP)]
    w1, w2_all = gathered(_Exchange("gather", conv_shards).alone(conv_shards, "gather_conv_weights"), conv_shards)
    w2 = w2_all.reshape(d, d)
    wdw_full_shards = _all_gather8(
        jnp.concatenate([conv_wdw[0], pool_ls, jnp.zeros((SUBLANES * 4 - CONV_WIDTH - 1, cq), F32)], axis=0).reshape(-1, LANES),
        "gather_small_shards").reshape(N_DEV, SUBLANES * 4, cq)[0::2]
    wdw = jnp.transpose(wdw_full_shards[:, :CONV_WIDTH], (1, 0, 2)).reshape(CONV_WIDTH, d)
    pls = wdw_full_shards[:, CONV_WIDTH].reshape(1, d)
    w8 = jnp.broadcast_to(wdw[:, None, :], (CONV_WIDTH, SUBLANES, d))

    vec_ca = _rows_of(norm_mix_g[0], mod[0, 1], mod[0, 0])
    h0, a0, g0, glu = _conv_a_fwd(x0, vec_ca, w1, conv_b1)
    vec_cb = _rows_of(conv_bdw, conv_ln_g, conv_ln_b, conv_b2, mod[0, 2])
    (cw, s0, y0, x1), ffn0_w = _conv_b_fwd(glu, w8, vec_cb, w2, x0, carry=(_Exchange("gather", ffn0_shards), ffn0_shards))
    wg0, wu0, wd0 = gathered(ffn0_w, ffn0_shards)
    vec_f0 = _rows_of(norm_ffn_g[0], mod[0, 4], mod[0, 3], mod[0, 5])
    (h2a, gta, upa, acta, y2a, x2), l1_w = _ffn_fwd(x1, vec_f0, wg0, wu0, wd0, "ffn0_fwd",
                                                   carry=(_Exchange("gather", l1_shards), l1_shards))
    pw_all, wg1, wu1, wd1 = gathered(l1_w, l1_shards)
    pw = jnp.transpose(pw_all.reshape(N_CHIPS, ng, gq, gd), (1, 0, 2, 3)).reshape(ng, N_CHIPS * gq, gd)
    vec_p = _rows_of(norm_mix_g[1], mod[1, 1], mod[1, 0], mod[1, 2], pls)
    mixed, yp, x3 = _pool_fwd(x2, vec_p, pw)
    vec_f1 = _rows_of(norm_ffn_g[1], mod[1, 4], mod[1, 3], mod[1, 5])
    (h2b, gtb, upb, actb, y2b, x4), _ = _ffn_fwd(x3, vec_f1, wg1, wu1, wd1, "ffn1_fwd")
    dx4, fin_sums = _final_fwd_bwd(x4, tgt, final_g.reshape(1, d))
    loss = lax.psum(fin_sums[1, 0], ("x", "y", "c"))

    def chip_sums(partials, tag):
        swapped = _sibling_swap_halves(partials, "grad_swap_" + tag)
        return [_add_sibling_half(p, r, pos, f"grad_add_{tag}{k}") for k, (p, r) in enumerate(zip(partials, swapped))]

    def ffn_wgrads(h2, dgt, dup, act, dy, tag):
        gate_up = _wgrad(h2, dgt, _shared_rows(d), _quarter_major(fq), N_CHIPS, d, fq, s, "wgrad_gate" + tag, stack_blocks=2)
        gate_up = _wgrad(h2, dup, _shared_rows(d), _quarter_major(fq), N_CHIPS, d, fq, s, "wgrad_up" + tag,
                         stack=gate_up, stack_blocks=2, block=1)
        down = _wgrad(act, dy, _quarter_major(fq), _shared_rows(d), N_CHIPS, fq, d, s, "wgrad_down" + tag)
        return [gate_up, down]

    (dgtb, dupb, dyb, dx3, sums_f1), _ = _ffn_bwd(dx4, x3, vec_f1, gtb, upb, y2b, wg1, wu1, wd1, "ffn1_bwd")
    cs_l1 = chip_sums(ffn_wgrads(h2b, dgtb, dupb, actb, dyb, "1"), "l1")
    dyp, dx2, sums_p = _pool_bwd(dx3, x2, yp, vec_p, pw)
    dpw = _wgrad(mixed, dyp, _column_block(gd), _column_block(gd), ng, gd, gd, s, "wgrad_pool")
    dpw_q = jnp.transpose(dpw.reshape(ng, N_CHIPS, gq, gd), (1, 0, 2, 3)).reshape(N_CHIPS, ng * gq, gd)
    (dgta, dupa, dya, dx1, sums_f0), slots_l1 = _ffn_bwd(dx2, x1, vec_f0, gta, upa, y2a, wg0, wu0, wd0, "ffn0_bwd",
                                                        carry=(_Exchange("reduce", cs_l1), cs_l1))
    cs_l0 = chip_sums(ffn_wgrads(h2a, dgta, dupa, acta, dya, "0") + [dpw_q], "l0")
    dy0, dcw, sums_cb = _conv_b_bwd(dx1, cw, y0, vec_cb, w2)
    dw2 = _wgrad(s0, dy0, _column_block(d // N_CHIPS), _shared_rows(d), N_CHIPS, d // N_CHIPS, d, s, "wgrad_conv2")
    (du, dx0, sums_ca, db1, dwdw), slots_l0 = _conv_a_bwd(dcw, glu, a0, g0, x0, dx1, vec_ca, w8, w1,
                                                          carry=(_Exchange("reduce", cs_l0), cs_l0))
    dw1 = _wgrad(h0, du, _shared_rows(d), _column_block(2 * d // N_CHIPS), N_CHIPS, d, 2 * d // N_CHIPS, s, "wgrad_conv1")
    cs_conv = chip_sums([dw1, dw2], "conv")
    slots_conv = _Exchange("reduce", cs_conv).alone(cs_conv, "grad_exchange_conv")

    dmod = jnp.stack([
        _rows_of(sums_ca[2], sums_ca[1], sums_cb[0], sums_f0[2], sums_f0[1], sums_f0[3]),
        _rows_of(sums_p[2], sums_p[1], sums_p[3], sums_f1[2], sums_f1[1], sums_f1[3])])
    small = _rows_of(dmod.reshape(-1, d), sums_ca[0], sums_p[0], sums_f0[0], sums_f1[0], db1.reshape(2, d),
                     sums_cb[4], sums_cb[2], sums_cb[3], sums_cb[1], fin_sums[0], sums_p[4], dwdw)
    n_small = small.shape[0]
    pad = (-n_small) % SUBLANES
    small = jnp.concatenate([small, jnp.zeros((pad, d), F32)], axis=0) if pad else small
    small_all = _all_gather8(small.reshape(-1, LANES), "gather_small_grads").reshape(N_DEV, -1, LANES)
    small_sum = _sum_devices(small_all).reshape(-1, d)
    n_dm = depth * n_ada
    g_ada_b = small_sum[0:n_dm].reshape(depth, n_ada * d)
    g_norm_mix = small_sum[n_dm:n_dm + 2]
    g_norm_ffn = small_sum[n_dm + 2:n_dm + 4]
    g_b1 = small_sum[n_dm + 4:n_dm + 6].reshape(1, 2 * d)
    g_bdw, g_lng, g_lnb, g_b2 = (small_sum[n_dm + 6 + k].reshape(1, d) for k in range(4))
    g_final = small_sum[n_dm + 10]
    g_pls = lax.dynamic_slice_in_dim(small_sum[n_dm + 11].reshape(1, d), myq * cq, cq, axis=1)
    g_wdw = lax.dynamic_slice_in_dim(small_sum[n_dm + 12:n_dm + 12 + CONV_WIDTH], myq * cq, cq, axis=1).reshape(conv_wdw.shape)
    dmod_all = small_all.reshape(N_DEV, -1, d)[:, 0:n_dm].reshape(N_DEV, depth, n_ada * d)
    dmod_cols = lax.dynamic_slice_in_dim(jnp.transpose(dmod_all, (1, 0, 2)), myq * ada_cols, ada_cols, axis=2)
    g_ada_w = _ada_backward(c_all.T, dmod_cols)

    def small_pack(*vs):
        return jnp.concatenate([v.reshape(-1) for v in vs]).reshape(-1, LANES)

    names = ("ada_b", "norm_mix_g", "norm_ffn_g", "conv_b1", "conv_bdw", "conv_ln_g", "conv_ln_b", "conv_b2", "final_g")
    small_w = (ada_b, norm_mix_g, norm_ffn_g, conv_b1, conv_bdw, conv_ln_g, conv_ln_b, conv_b2, final_g)
    small_g = (g_ada_b, g_norm_mix, g_norm_ffn, g_b1, g_bdw, g_lng, g_lnb, g_b2, g_final)
    small_m = (m_ada_b, m_norm_mix_g, m_norm_ffn_g, m_conv_b1, m_conv_bdw, m_conv_ln_g, m_conv_ln_b, m_conv_b2, m_final_g)
    small_v = (v_ada_b, v_norm_mix_g, v_norm_ffn_g, v_conv_b1, v_conv_bdw, v_conv_ln_g, v_conv_ln_b, v_conv_b2, v_final_g)
    sd, sm, sv = _adamw(small_pack(*small_w), small_pack(*small_g), small_pack(*small_m), small_pack(*small_v), "adamw_small")

    def unpack(flat2d):
        flat = flat2d.reshape(-1)
        out, off = {}, 0
        for nm, wv in zip(names, small_w):
            out[nm] = flat[off:off + wv.size].reshape(wv.shape)
            off += wv.size
        return out

    sd, sm, sv = unpack(sd), unpack(sm), unpack(sv)
    grads = {
        "ada_w": g_ada_w, "ada_b": g_ada_b, "norm_mix_g": g_norm_mix, "norm_ffn_g": g_norm_ffn,
        "conv_b1": g_b1, "conv_wdw": g_wdw, "conv_bdw": g_bdw, "conv_ln_g": g_lng,
        "conv_ln_b": g_lnb, "conv_b2": g_b2, "pool_ls": g_pls, "final_g": g_final,
    }
    plain = {
        "ada_w": (ada_w, m_ada_w, v_ada_w), "conv_wdw": (conv_wdw, m_conv_wdw, v_conv_wdw),
        "pool_ls": (pool_ls, m_pool_ls, v_pool_ls),
    }
    from_slots = {
        "conv_w1": (conv_w1, m_conv_w1, v_conv_w1, [(slots_conv[0], cs_conv[0], 0, d)]),
        "conv_w2": (conv_w2, m_conv_w2, v_conv_w2, [(slots_conv[1], cs_conv[1], 0, d // N_CHIPS)]),
        "pool_w": (pool_w, m_pool_w, v_pool_w, [(slots_l0[2], cs_l0[2], 0, ng * gq)]),
        "ffn_w_gate": (ffn_w_gate, m_ffn_w_gate, v_ffn_w_gate,
                       [(slots_l0[0], cs_l0[0], 0, d), (slots_l1[0], cs_l1[0], 0, d)]),
        "ffn_w_up": (ffn_w_up, m_ffn_w_up, v_ffn_w_up,
                     [(slots_l0[0], cs_l0[0], d, d), (slots_l1[0], cs_l1[0], d, d)]),
        "ffn_w_down": (ffn_w_down, m_ffn_w_down, v_ffn_w_down,
                       [(slots_l0[1], cs_l0[1], 0, fq), (slots_l1[1], cs_l1[1], 0, fq)]),
    }
    order = ("ada_w", "ada_b", "norm_mix_g", "norm_ffn_g", "conv_w1", "conv_b1", "conv_wdw", "conv_bdw", "conv_ln_g",
             "conv_ln_b", "conv_w2", "conv_b2", "pool_w", "pool_ls", "ffn_w_gate", "ffn_w_up", "ffn_w_down", "final_g")
    delta, new_m, new_v = {}, {}, {}
    for nm in order:
        if nm in from_slots:
            wv, mv, vv, segs = from_slots[nm]
            two_d = (-1, wv.shape[-1])
            outs = _adamw_from_slots(wv.reshape(two_d), mv.reshape(two_d), vv.reshape(two_d), segs, pos, "adamw_" + nm)
            grads[nm], delta[nm], new_m[nm], new_v[nm] = (o.reshape(wv.shape) for o in outs)
        elif nm in plain:
            wv, mv, vv = plain[nm]
            grads[nm] = grads[nm].reshape(wv.shape)
            delta[nm], new_m[nm], new_v[nm] = _adamw_nd(wv, grads[nm], mv, vv, "adamw_" + nm)
        else:
            delta[nm], new_m[nm], new_v[nm] = sd[nm], sm[nm], sv[nm]
            grads[nm] = grads[nm].reshape(dict(zip(names, small_w))[nm].shape)

    return (loss, dx0.reshape(x.shape), *[grads[n] for n in order], *[delta[n] for n in order],
            *[new_m[n] for n in order], *[new_v[n] for n in order])
```

```python
import functools

import jax
import jax.numpy as jnp
from jax import lax
from jax.experimental import pallas as pl
from jax.experimental.pallas import tpu as pltpu

F32 = jnp.float32
_LOWP = jnp.bfloat16
EPS = 1e-6
CONV_WIDTH = 31
POOL_WINDOWS = (2, 4, 8, 16)
ADAM_LR = 0.001
ADAM_B1 = 0.9
ADAM_B2 = 0.999
ADAM_EPS = 1e-08
ADAM_WD = 0.01
ADAM_STEP = 10

N_CHIPS = 4
N_DEV = 8
LANES = 128
SUBLANES = 8
CONV_HALO = 32
POOL_HALO = 16
TILE_ROWS = 64
VMEM_LIMIT = 48 * 1024 * 1024
ADAM_BLOCK_BYTES = 1 << 20
MESH = pl.DeviceIdType.MESH
SDS = jax.ShapeDtypeStruct


def _cparams(*sem):
    return pltpu.CompilerParams(dimension_semantics=sem if sem else None, vmem_limit_bytes=VMEM_LIMIT)


def _row_block(s):
    for tm in (512, 256, 128, 64, 32, 16):
        if s % tm == 0:
            return tm
    raise ValueError(f"sequence length {s} must be a multiple of 16")


def _rows_spec(tm, ncols):
    return pl.BlockSpec((tm, ncols), lambda i, *_: (i, 0))


def _const_spec(shape):
    nd = len(shape)
    return pl.BlockSpec(shape, lambda *_: (0,) * nd)


def _dot(a, b):
    return lax.dot_general(a.astype(_LOWP), b.astype(_LOWP), (((1,), (0,)), ((), ())), preferred_element_type=F32)


def _dot_nt(a, b):
    return lax.dot_general(a.astype(_LOWP), b.astype(_LOWP), (((1,), (1,)), ((), ())), preferred_element_type=F32)


def _dot_tn(a, b):
    return lax.dot_general(a.astype(_LOWP), b.astype(_LOWP), (((0,), (0,)), ((), ())), preferred_element_type=F32)


def _rms(x, gamma):
    r = lax.rsqrt(jnp.mean(x * x, axis=-1, keepdims=True) + EPS)
    xh = x * r
    return xh, r, xh * gamma


def _rms_bwd(dn, xh, r, gamma):
    dxh = dn * gamma
    return r * (dxh - xh * jnp.mean(dxh * xh, axis=-1, keepdims=True))


def _colsum(v):
    return jnp.sum(v, axis=0, keepdims=True)


def _sigmoid(v):
    return jax.nn.sigmoid(v)


def _dsilu(v):
    sg = _sigmoid(v)
    return sg * (1.0 + v * (1.0 - sg))


def _for_tiles(nrows, col0, ncols, fn):
    def step(t, carry):
        r0 = pl.multiple_of(t * TILE_ROWS, TILE_ROWS)
        for col in range(col0, col0 + ncols, LANES):
            fn(r0, col)
        return carry

    lax.fori_loop(0, nrows // TILE_ROWS, step, 0)


def _shift_up(win, b):
    return pltpu.roll(win, win.shape[0] - b, 0) if b else win


def _shift_down(win, b):
    return pltpu.roll(win, b, 0) if b else win


def _tiles3(v):
    return v.reshape(v.shape[0] // SUBLANES, SUBLANES, v.shape[1])


def _mesh_pos():
    return lax.axis_index("x"), lax.axis_index("y"), lax.axis_index("c")


def _other_chips(x, y):
    return [(1 - x, y), (x, 1 - y), (1 - x, 1 - y)]


def _all_gather8(v, name):
    m, n = v.shape

    def body(x_ref, out_ref, send_sems, recv_sems, local_sem):
        x, y, c = _mesh_pos()
        me, sibling = (x, y, c), (x, y, 1 - c)
        chips = _other_chips(x, y)

        def rows(px, py, pc):
            return out_ref.at[pl.ds((4 * px + 2 * py + pc) * m, m), :]

        def copy(k, block, to, src=None):
            return pltpu.make_async_remote_copy(
                src_ref=rows(*block) if src is None else src, dst_ref=rows(*block),
                send_sem=send_sems.at[k], recv_sem=recv_sems.at[k], device_id=to, device_id_type=MESH)

        mine = pltpu.make_async_copy(x_ref, rows(*me), local_sem)
        mine.start()
        first = [copy(0, me, sibling, src=x_ref)]
        first += [copy(1 + j, me, (*chip, c), src=x_ref) for j, chip in enumerate(chips)]
        for cp in first:
            cp.start()
        passed = [copy(4 + j, (*chip, c), sibling) for j, chip in enumerate(chips)]
        for j, chip in enumerate(chips):
            copy(1 + j, (*chip, c), me).wait_recv()
            passed[j].start()
        copy(0, sibling, me).wait_recv()
        for j, chip in enumerate(chips):
            copy(4 + j, (*chip, 1 - c), me).wait_recv()
        for cp in first + passed:
            cp.wait_send()
        mine.wait()

    return pl.pallas_call(
        body, name=name,
        out_shape=SDS((N_DEV * m, n), v.dtype),
        in_specs=[pl.BlockSpec(memory_space=pltpu.VMEM)],
        out_specs=pl.BlockSpec(memory_space=pltpu.VMEM),
        scratch_shapes=[pltpu.SemaphoreType.DMA((7,)), pltpu.SemaphoreType.DMA((7,)), pltpu.SemaphoreType.DMA],
    )(v)


class _Exchange:
    N_SEMS = 7

    def __init__(self, kind, arrays):
        self.kind, self.n = kind, len(arrays)
        if kind == "gather":
            self.out_shape = [SDS((N_CHIPS,) + a.shape, a.dtype) for a in arrays]
        else:
            self.out_shape = [SDS((N_CHIPS, 2 * a.shape[1], a.shape[2]), a.dtype) for a in arrays]
        self.scratch = [pltpu.SemaphoreType.DMA((self.n, self.N_SEMS)), pltpu.SemaphoreType.DMA((self.n, self.N_SEMS))]

    def _ctx(self, ins, outs, send_sems, recv_sems):
        x, y, c = _mesh_pos()
        myq = 2 * x + y
        sibling = (x, y, 1 - c)
        chips = _other_chips(x, y)
        gather = self.kind == "gather"

        def place(a, j, h):
            hr = outs[a].shape[1] // 2
            px, py = chips[j] if j < 3 else (x, y)
            return outs[a].at[2 * px + py if gather else j, pl.ds(h * hr, hr)]

        def copy(a, k, src, dst, to):
            return pltpu.make_async_remote_copy(src_ref=src, dst_ref=dst, send_sem=send_sems.at[a, k],
                                                recv_sem=recv_sems.at[a, k], device_id=to, device_id_type=MESH)

        def first(a, j):
            hr = outs[a].shape[1] // 2
            if gather:
                src = ins[a].at[pl.ds(c * hr, hr)]
                dst = outs[a].at[myq, pl.ds(c * hr, hr)]
            else:
                px, py = chips[j]
                src = ins[a].at[2 * px + py]
                dst = outs[a].at[j, pl.ds(c * hr, hr)]
            return copy(a, j, src, dst, (*chips[j], c))

        def landed(a, j):
            return copy(a, j, place(a, j, c), place(a, j, c), sibling)

        def relay(a, j):
            return copy(a, 3 + j, place(a, j, c), place(a, j, c), sibling)

        def relayed(a, j):
            return copy(a, 3 + j, place(a, j, 1 - c), place(a, j, 1 - c), sibling)

        def own(a):
            return copy(a, 6, ins[a].at[myq], place(a, 3, c), sibling)

        def own_landed(a):
            return copy(a, 6, place(a, 3, 1 - c), place(a, 3, 1 - c), sibling)

        return first, landed, relay, relayed, own, own_landed

    def start(self, ins, outs, send_sems, recv_sems):
        first, _, _, _, own, _ = self._ctx(ins, outs, send_sems, recv_sems)
        for a in range(self.n):
            for j in range(3):
                first(a, j).start()
            if self.kind == "reduce":
                own(a).start()

    def relay(self, ins, outs, send_sems, recv_sems):
        _, landed, relay, _, _, _ = self._ctx(ins, outs, send_sems, recv_sems)
        for a in range(self.n):
            for j in range(3):
                landed(a, j).wait_recv()
                relay(a, j).start()

    def finish(self, ins, outs, send_sems, recv_sems):
        first, _, relay, relayed, own, own_landed = self._ctx(ins, outs, send_sems, recv_sems)
        for a in range(self.n):
            for j in range(3):
                relayed(a, j).wait_recv()
            if self.kind == "reduce":
                own_landed(a).wait_recv()
        for a in range(self.n):
            for j in range(3):
                first(a, j).wait_send()
                relay(a, j).wait_send()
            if self.kind == "reduce":
                own(a).wait_send()

    def alone(self, arrays, name):
        n = self.n

        def body(*refs):
            args = (refs[:n], refs[n:2 * n], refs[2 * n], refs[2 * n + 1])
            self.start(*args)
            self.relay(*args)
            self.finish(*args)

        any_spec = pl.BlockSpec(memory_space=pl.ANY)
        return pl.pallas_call(body, name=name, out_shape=self.out_shape, in_specs=[any_spec] * n,
                              out_specs=[any_spec] * n, scratch_shapes=self.scratch)(*arrays)


RELAY_AT = (5, 8)


def _call(body, *args, carry=None, name, grid, in_specs, out_specs, out_shape, scratch_shapes=(), sem):
    if carry is None:
        outs = pl.pallas_call(body, name=name, grid=grid, in_specs=in_specs, out_specs=out_specs, out_shape=out_shape,
                              scratch_shapes=list(scratch_shapes), compiler_params=_cparams(*sem))(*args)
        return outs, None
    ex, arrays = carry
    n_in, n_out, n_scr, n = len(in_specs), len(out_specs), len(scratch_shapes), ex.n
    total = 1
    for g in grid:
        total *= g
    relay_step = (total * RELAY_AT[0]) // RELAY_AT[1]

    def hosted(*refs):
        ins, cins = refs[:n_in], refs[n_in:n_in + n]
        outs, couts = refs[n_in + n:n_in + n + n_out], refs[n_in + n + n_out:n_in + 2 * n + n_out]
        scr = refs[n_in + 2 * n + n_out:n_in + 2 * n + n_out + n_scr]
        sems = refs[n_in + 2 * n + n_out + n_scr:]
        step = 0
        for ax, g in enumerate(grid):
            step = step * g + pl.program_id(ax)

        @pl.when(step == 0)
        def _():
            ex.start(cins, couts, *sems)

        body(*ins, *outs, *scr)

        @pl.when(step == relay_step)
        def _():
            ex.relay(cins, couts, *sems)

        @pl.when(step == total - 1)
        def _():
            ex.finish(cins, couts, *sems)

    any_spec = pl.BlockSpec(memory_space=pl.ANY)
    res = pl.pallas_call(
        hosted, name=name, grid=grid, in_specs=list(in_specs) + [any_spec] * n,
        out_specs=list(out_specs) + [any_spec] * n, out_shape=list(out_shape) + ex.out_shape,
        scratch_shapes=list(scratch_shapes) + ex.scratch, compiler_params=_cparams(*(("arbitrary",) * len(grid))),
    )(*args, *arrays)
    return res[:n_out], res[n_out:]


def _sibling_swap_halves(ps, name):
    n = len(ps)

    def body(*refs):
        ins, outs, send_sems, recv_sems = refs[:n], refs[n:2 * n], refs[2 * n], refs[2 * n + 1]
        x, y, c = _mesh_pos()
        cps = []
        for a in range(n):
            hr = ins[a].shape[1] // 2
            cp = pltpu.make_async_remote_copy(
                src_ref=ins[a].at[:, pl.ds((1 - c) * hr, hr)], dst_ref=outs[a],
                send_sem=send_sems.at[a], recv_sem=recv_sems.at[a], device_id=(x, y, 1 - c), device_id_type=MESH)
            cp.start()
            cps.append(cp)
        for cp in cps:
            cp.wait()

    any_spec = pl.BlockSpec(memory_space=pl.ANY)
    return pl.pallas_call(
        body, name=name, out_shape=[SDS((p.shape[0], p.shape[1] // 2, p.shape[2]), p.dtype) for p in ps],
        in_specs=[any_spec] * n, out_specs=[any_spec] * n,
        scratch_shapes=[pltpu.SemaphoreType.DMA((n,)), pltpu.SemaphoreType.DMA((n,))],
    )(*ps)


def _divisor_block(rows, cands=(512, 352, 256, 176, 128, 64, 32, 16)):
    for rb in cands:
        if rows % rb == 0:
            return rb
    raise ValueError(rows)


def _add_sibling_half(p, rb1, pos, name):
    nq, r, n = p.shape
    h = r // 2
    rb = _divisor_block(h)
    nb = h // rb

    def body(pos_ref, p_ref, r_ref, o_ref):
        o_ref[...] = (p_ref[...].astype(F32) + r_ref[...].astype(F32)).astype(o_ref.dtype)

    grid_spec = pltpu.PrefetchScalarGridSpec(
        num_scalar_prefetch=1, grid=(nq, nb),
        in_specs=[pl.BlockSpec((1, rb, n), lambda q, i, pos: (q, pos[0] * nb + i, 0)),
                  pl.BlockSpec((1, rb, n), lambda q, i, pos: (q, i, 0))],
        out_specs=pl.BlockSpec((1, rb, n), lambda q, i, pos: (q, i, 0)))
    return pl.pallas_call(
        body, name=name, grid_spec=grid_spec, out_shape=SDS((nq, h, n), p.dtype),
        compiler_params=_cparams("parallel", "parallel"),
    )(pos, p, rb1)


def _sum_devices(g):
    nd, m, n = g.shape

    def body(g_ref, o_ref):
        acc = g_ref[0]
        for d in range(1, nd):
            acc = acc + g_ref[d]
        o_ref[...] = acc

    return pl.pallas_call(body, name="small_grad_sum", out_shape=SDS((m, n), F32))(g)


def _ada_forward(c_all, ada_w, ada_b_cols):
    nl, d, ncol = ada_w.shape
    nb = c_all.shape[0]
    tn = 512 if ncol % 512 == 0 else ncol

    def body(c_ref, w_ref, b_ref, o_ref):
        cv = c_ref[...]
        ca = cv * _sigmoid(cv)
        o_ref[0] = _dot(ca, w_ref[0]) + b_ref[0]

    return pl.pallas_call(
        body, name="ada_forward", grid=(nl, ncol // tn),
        in_specs=[pl.BlockSpec((nb, d), lambda l, j: (0, 0)),
                  pl.BlockSpec((1, d, tn), lambda l, j: (l, 0, j)),
                  pl.BlockSpec((1, 1, tn), lambda l, j: (l, 0, j))],
        out_specs=pl.BlockSpec((1, nb, tn), lambda l, j: (l, 0, j)),
        out_shape=SDS((nl, nb, ncol), F32), compiler_params=_cparams("parallel", "parallel"),
    )(c_all, ada_w, ada_b_cols)


def _ada_backward(c_all_t, dmod_cols):
    d, nb = c_all_t.shape
    nl, _, ncol = dmod_cols.shape
    tn = 512 if ncol % 512 == 0 else ncol

    def body(c_ref, g_ref, o_ref):
        cv = c_ref[...]
        ca = cv * _sigmoid(cv)
        o_ref[0] = _dot(ca, g_ref[0])

    return pl.pallas_call(
        body, name="ada_backward", grid=(nl, ncol // tn),
        in_specs=[pl.BlockSpec((d, nb), lambda l, j: (0, 0)),
                  pl.BlockSpec((1, nb, tn), lambda l, j: (l, 0, j))],
        out_specs=pl.BlockSpec((1, d, tn), lambda l, j: (l, 0, j)),
        out_shape=SDS((nl, d, ncol), F32), compiler_params=_cparams("parallel", "parallel"),
    )(c_all_t, dmod_cols)


def _conv_a_fwd(x, vec, w1, b1):
    s, d = x.shape
    nq, _, hq = w1.shape
    tm = _row_block(s)

    def body(x_ref, vec_ref, w1_ref, b1_ref, h_ref, a_ref, g_ref, glu_ref):
        vec = vec_ref[...]
        _, _, n = _rms(x_ref[...], vec[0:1])
        hb = (n * (1.0 + vec[1:2]) + vec[2:3]).astype(_LOWP)
        h_ref[...] = hb
        u = [_dot(hb, w1_ref[q]) + b1_ref[:, q * hq:(q + 1) * hq] for q in range(nq)]
        for k in range(nq // 2):
            av, gv = u[k], u[nq // 2 + k]
            cols = slice(k * hq, (k + 1) * hq)
            a_ref[:, cols] = av.astype(_LOWP)
            g_ref[:, cols] = gv.astype(_LOWP)
            glu_ref[:, cols] = av * _sigmoid(gv)

    return pl.pallas_call(
        body, name="conv_a_fwd", grid=(s // tm,),
        in_specs=[_rows_spec(tm, d), _const_spec(vec.shape), _const_spec(w1.shape), _const_spec(b1.shape)],
        out_specs=[_rows_spec(tm, d)] * 4,
        out_shape=[SDS((s, d), _LOWP)] * 3 + [SDS((s, d), F32)],
        compiler_params=_cparams("parallel"),
    )(x, vec, w1, b1)


def _conv_b_fwd(glu, w8, vec, w2, x, carry=None):
    s, d = x.shape
    tm = _row_block(s)
    hb = tm // CONV_HALO

    def body(glu_ref, halo_ref, w8_ref, vec_ref, w2_ref, x_ref, cw_ref, s_ref, y_ref, xo_ref, ext_ref):
        i = pl.program_id(0)
        vec = vec_ref[...]
        ext_ref[0:CONV_HALO, :] = jnp.where(i > 0, halo_ref[...], 0.0)
        ext_ref[CONV_HALO:, :] = glu_ref[...]

        def tile(r0, c0):
            lanes = slice(c0, c0 + LANES)
            win = ext_ref[pl.ds(r0, TILE_ROWS + CONV_HALO), lanes]
            acc = jnp.zeros((TILE_ROWS // SUBLANES, SUBLANES, LANES), F32)
            for b in range(SUBLANES):
                sh = _shift_up(win, b)
                for a in range(CONV_HALO // SUBLANES + 1):
                    k = SUBLANES * a + b - (CONV_HALO - CONV_WIDTH + 1)
                    if 0 <= k < CONV_WIDTH:
                        acc = acc + w8_ref[k, :, lanes][None] * _tiles3(sh[SUBLANES * a:SUBLANES * a + TILE_ROWS])
            cw_ref[pl.ds(r0, TILE_ROWS), lanes] = acc.reshape(TILE_ROWS, LANES)

        _for_tiles(tm, 0, d, tile)
        cw = cw_ref[...] + vec[0:1]
        cw_ref[...] = cw
        cc = cw - jnp.mean(cw, axis=-1, keepdims=True)
        ch = cc * lax.rsqrt(jnp.mean(cc * cc, axis=-1, keepdims=True) + EPS)
        lo = ch * vec[1:2] + vec[2:3]
        sv = (lo * _sigmoid(lo)).astype(_LOWP)
        s_ref[...] = sv
        y = _dot(sv, w2_ref[...]) + vec[3:4]
        y_ref[...] = y.astype(_LOWP)
        xo_ref[...] = x_ref[...] + (1.0 + vec[4:5]) * y

    return _call(
        body, glu, glu, w8, vec, w2, x, carry=carry, name="conv_b_fwd", grid=(s // tm,),
        in_specs=[_rows_spec(tm, d),
                  pl.BlockSpec((CONV_HALO, d), lambda i: (jnp.maximum(i * hb - 1, 0), 0)),
                  _const_spec(w8.shape), _const_spec(vec.shape), _const_spec(w2.shape), _rows_spec(tm, d)],
        out_specs=[_rows_spec(tm, d)] * 4,
        out_shape=[SDS((s, d), F32), SDS((s, d), _LOWP), SDS((s, d), _LOWP), SDS((s, d), F32)],
        scratch_shapes=[pltpu.VMEM((tm + CONV_HALO, d), F32)], sem=("parallel",))


def _ffn_fwd(x, vec, wg, wu, wd, name, carry=None):
    s, d = x.shape
    nq, _, fq = wg.shape
    tm = _row_block(s)

    def body(x_ref, vec_ref, wg_ref, wu_ref, wd_ref, h_ref, gt_ref, up_ref, act_ref, y_ref, xo_ref, hb_ref, yacc_ref):
        q = pl.program_id(1)
        vec = vec_ref[...]

        @pl.when(q == 0)
        def _():
            _, _, n = _rms(x_ref[...], vec[0:1])
            hb = (n * (1.0 + vec[1:2]) + vec[2:3]).astype(_LOWP)
            hb_ref[...] = hb
            h_ref[...] = hb
            yacc_ref[...] = jnp.zeros_like(yacc_ref)

        hb = hb_ref[...]
        gt = _dot(hb, wg_ref[0])
        up = _dot(hb, wu_ref[0])
        act = (gt * _sigmoid(gt) * up).astype(_LOWP)
        gt_ref[0] = gt.astype(_LOWP)
        up_ref[0] = up.astype(_LOWP)
        act_ref[0] = act
        yacc_ref[...] += _dot(act, wd_ref[0])

        @pl.when(q == nq - 1)
        def _():
            y = yacc_ref[...]
            y_ref[...] = y.astype(_LOWP)
            xo_ref[...] = x_ref[...] + (1.0 + vec[3:4]) * y

    rows = pl.BlockSpec((tm, d), lambda i, q: (i, 0))
    hid = pl.BlockSpec((1, tm, fq), lambda i, q: (q, i, 0))
    return _call(
        body, x, vec, wg, wu, wd, carry=carry, name=name, grid=(s // tm, nq),
        in_specs=[rows, pl.BlockSpec(vec.shape, lambda i, q: (0, 0)),
                  pl.BlockSpec((1, d, fq), lambda i, q: (q, 0, 0)),
                  pl.BlockSpec((1, d, fq), lambda i, q: (q, 0, 0)),
                  pl.BlockSpec((1, fq, d), lambda i, q: (q, 0, 0))],
        out_specs=[rows, hid, hid, hid, rows, rows],
        out_shape=[SDS((s, d), _LOWP)] + [SDS((nq, s, fq), _LOWP)] * 3 + [SDS((s, d), _LOWP), SDS((s, d), F32)],
        scratch_shapes=[pltpu.VMEM((tm, d), _LOWP), pltpu.VMEM((tm, d), F32)], sem=("parallel", "arbitrary"))


def _pool_inv_count(t0, nrows, w):
    t = t0 + lax.broadcasted_iota(jnp.int32, (nrows, LANES), 0)
    return 1.0 / jnp.minimum(t + 1, w).astype(F32)


def _pool_fwd(x, vec, pw):
    s, d = x.shape
    ng, gd, _ = pw.shape
    tm = _row_block(s)
    hb = tm // POOL_HALO

    def body(x_ref, halo_ref, vec_ref, pw_ref, mixed_ref, yp_ref, xo_ref, hext_ref):
        i = pl.program_id(0)
        vec = vec_ref[...]

        def modulated(xv):
            _, _, n = _rms(xv, vec[0:1])
            return n * (1.0 + vec[1:2]) + vec[2:3]

        hext_ref[0:POOL_HALO, :] = jnp.where(i > 0, modulated(halo_ref[...]), 0.0)
        hext_ref[POOL_HALO:, :] = modulated(x_ref[...])

        for g in range(ng):
            def tile(r0, c0, g=g):
                lanes = slice(c0, c0 + LANES)
                win = hext_ref[pl.ds(r0, TILE_ROWS + POOL_HALO), lanes]
                acc = win
                for step in range(g + 1):
                    acc = acc + _shift_down(acc, 2 ** step)
                inv = _pool_inv_count(i * tm + r0, TILE_ROWS, POOL_WINDOWS[g])
                mixed = acc[POOL_HALO:] * inv - win[POOL_HALO:]
                mixed_ref[pl.ds(r0, TILE_ROWS), lanes] = mixed.astype(_LOWP)

            _for_tiles(tm, g * gd, gd, tile)

        for g in range(ng):
            cols = slice(g * gd, (g + 1) * gd)
            yp = _dot(mixed_ref[:, cols], pw_ref[g])
            yp_ref[:, cols] = yp.astype(_LOWP)
            xo_ref[:, cols] = x_ref[:, cols] + (1.0 + vec[3:4, cols]) * (yp * vec[4:5, cols])

    return pl.pallas_call(
        body, name="pool_fwd", grid=(s // tm,),
        in_specs=[_rows_spec(tm, d),
                  pl.BlockSpec((POOL_HALO, d), lambda i: (jnp.maximum(i * hb - 1, 0), 0)),
                  _const_spec(vec.shape), _const_spec(pw.shape)],
        out_specs=[_rows_spec(tm, d)] * 3,
        out_shape=[SDS((s, d), _LOWP), SDS((s, d), _LOWP), SDS((s, d), F32)],
        scratch_shapes=[pltpu.VMEM((tm + POOL_HALO, d), F32)],
        compiler_params=_cparams("parallel"),
    )(x, x, vec, pw)


def _final_fwd_bwd(x, tgt, gamma):
    s, d = x.shape
    tm = _row_block(s)
    nsteps = s // tm

    def body(x_ref, t_ref, g_ref, dx_ref, sums_ref):
        i = pl.program_id(0)
        gamma_v = g_ref[...]
        xh, r, out = _rms(x_ref[...], gamma_v)
        err = out - t_ref[...]
        dout = err * (1.0 / d)

        @pl.when(i == 0)
        def _():
            sums_ref[...] = jnp.zeros_like(sums_ref)

        sums_ref[0:1, :] += _colsum(dout * xh)
        sums_ref[1:2, :] += _colsum(0.5 * err * dout)
        dx_ref[...] = _rms_bwd(dout, xh, r, gamma_v)

        @pl.when(i == nsteps - 1)
        def _():
            sums_ref[1:2, :] = jnp.broadcast_to(jnp.sum(sums_ref[1:2, :], axis=-1, keepdims=True), (1, d))

    return pl.pallas_call(
        body, name="final_loss", grid=(nsteps,),
        in_specs=[_rows_spec(tm, d), _rows_spec(tm, d), _const_spec(gamma.shape)],
        out_specs=[_rows_spec(tm, d), _const_spec((2, d))],
        out_shape=[SDS((s, d), F32), SDS((2, d), F32)],
        compiler_params=_cparams("arbitrary"),
    )(x, tgt, gamma)


def _ffn_bwd(dxo, x, vec, gt, up, y, wg, wu, wd, name, carry=None):
    s, d = x.shape
    nq, _, fq = wg.shape
    tm = _row_block(s)

    def body(dxo_ref, x_ref, vec_ref, gt_ref, up_ref, y_ref, wg_ref, wu_ref, wd_ref,
             dgt_ref, dup_ref, dy_ref, dx_ref, sums_ref, dyb_ref, dh_ref):
        i, q = pl.program_id(0), pl.program_id(1)
        vec = vec_ref[...]

        @pl.when((i == 0) & (q == 0))
        def _():
            sums_ref[...] = jnp.zeros_like(sums_ref)

        @pl.when(q == 0)
        def _():
            dxv = dxo_ref[...]
            dyb = (dxv * (1.0 + vec[3:4])).astype(_LOWP)
            dyb_ref[...] = dyb
            dy_ref[...] = dyb
            dh_ref[...] = jnp.zeros_like(dh_ref)
            sums_ref[3:4, :] += _colsum(dxv * y_ref[...].astype(F32))

        halves = [slice(k * (tm // 2), (k + 1) * (tm // 2)) for k in range(2)]
        dacts = [_dot_nt(dyb_ref[rs, :], wd_ref[0]) for rs in halves]
        dgus = []
        for rs, dact in zip(halves, dacts):
            gtv = gt_ref[0, rs, :].astype(F32)
            sg = _sigmoid(gtv)
            dgt = (dact * up_ref[0, rs, :].astype(F32) * (sg * (1.0 + gtv * (1.0 - sg)))).astype(_LOWP)
            dup = (dact * (gtv * sg)).astype(_LOWP)
            dgt_ref[0, rs, :] = dgt
            dup_ref[0, rs, :] = dup
            dgus.append((dgt, dup))
        for rs, (dgt, dup) in zip(halves, dgus):
            dh_ref[rs, :] += _dot_nt(dgt, wg_ref[0]) + _dot_nt(dup, wu_ref[0])

        @pl.when(q == nq - 1)
        def _():
            dh = dh_ref[...]
            xh, r, n = _rms(x_ref[...], vec[0:1])
            sums_ref[2:3, :] += _colsum(dh)
            sums_ref[1:2, :] += _colsum(dh * n)
            dn = dh * (1.0 + vec[1:2])
            sums_ref[0:1, :] += _colsum(dn * xh)
            dx_ref[...] = dxo_ref[...] + _rms_bwd(dn, xh, r, vec[0:1])

    rows = pl.BlockSpec((tm, d), lambda i, q: (i, 0))
    hid = pl.BlockSpec((1, tm, fq), lambda i, q: (q, i, 0))
    return _call(
        body, dxo, x, vec, gt, up, y, wg, wu, wd, carry=carry, name=name, grid=(s // tm, nq),
        in_specs=[rows, rows, pl.BlockSpec(vec.shape, lambda i, q: (0, 0)), hid, hid, rows,
                  pl.BlockSpec((1, d, fq), lambda i, q: (q, 0, 0)),
                  pl.BlockSpec((1, d, fq), lambda i, q: (q, 0, 0)),
                  pl.BlockSpec((1, fq, d), lambda i, q: (q, 0, 0))],
        out_specs=[hid, hid, rows, rows, pl.BlockSpec((4, d), lambda i, q: (0, 0))],
        out_shape=[SDS((nq, s, fq), _LOWP)] * 2 + [SDS((s, d), _LOWP), SDS((s, d), F32), SDS((4, d), F32)],
        scratch_shapes=[pltpu.VMEM((tm, d), _LOWP), pltpu.VMEM((tm, d), F32)], sem=("arbitrary", "arbitrary"))


def _wgrad(a, b, a_spec, b_spec, nq, ka, nb, s, name, stack=None, stack_blocks=1, block=0):
    tk = min(2048, s)
    nk = s // tk

    def body(a_ref, b_ref, *rest):
        o_ref, acc_ref = rest[-2:]
        k = pl.program_id(1)

        @pl.when(k == 0)
        def _():
            acc_ref[...] = jnp.zeros_like(acc_ref)

        av = a_ref[...].reshape(tk, ka)
        bv = b_ref[...].reshape(tk, nb)
        acc_ref[...] += _dot_tn(av, bv)

        @pl.when(k == nk - 1)
        def _():
            o_ref[0] = acc_ref[...].astype(o_ref.dtype)

    extra = [] if stack is None else [stack]
    return pl.pallas_call(
        body, name=name, grid=(nq, nk),
        in_specs=[a_spec(tk), b_spec(tk)] + [pl.BlockSpec(memory_space=pl.ANY)] * len(extra),
        out_specs=pl.BlockSpec((1, ka, nb), lambda q, k: (q, block, 0)),
        out_shape=SDS((nq, stack_blocks * ka, nb), _LOWP),
        scratch_shapes=[pltpu.VMEM((ka, nb), F32)],
        input_output_aliases={2: 0} if extra else {},
        compiler_params=_cparams("parallel", "arbitrary"),
    )(a, b, *extra)


def _shared_rows(ncols):
    return lambda tk: pl.BlockSpec((tk, ncols), lambda q, k: (k, 0))


def _column_block(ncols):
    return lambda tk: pl.BlockSpec((tk, ncols), lambda q, k: (k, q))


def _quarter_major(ncols):
    return lambda tk: pl.BlockSpec((1, tk, ncols), lambda q, k: (q, k, 0))


def _pool_bwd(dxo, x, yp, vec, pw):
    s, d = x.shape
    ng, gd, _ = pw.shape
    tm = _row_block(s)
    hb = tm // POOL_HALO
    nsteps = s // tm
    last_halo = s // POOL_HALO - 1

    def body(dxo_ref, halo_ref, x_ref, yp_ref, vec_ref, pw_ref, dyp_ref, dx_ref, sums_ref, eext_ref, dh_ref):
        i = pl.program_id(0)
        vec = vec_ref[...]

        @pl.when(i == 0)
        def _():
            sums_ref[...] = jnp.zeros_like(sums_ref)

        dxv = dxo_ref[...]
        ypv = yp_ref[...].astype(F32)
        gate1 = 1.0 + vec[3:4]
        sums_ref[3:4, :] += _colsum(dxv * (ypv * vec[4:5]))
        sums_ref[4:5, :] += _colsum(dxv * gate1 * ypv)
        dyp = (dxv * gate1 * vec[4:5]).astype(_LOWP)
        dyp_ref[...] = dyp
        dyp_halo = (jnp.where(i < nsteps - 1, halo_ref[...], 0.0) * gate1 * vec[4:5]).astype(_LOWP)
        for g in range(ng):
            cols = slice(g * gd, (g + 1) * gd)
            dm = _dot_nt(dyp[:, cols], pw_ref[g])
            dh_ref[:, cols] = dm
            inv = _pool_inv_count(i * tm, tm, POOL_WINDOWS[g])
            eext_ref[0:tm, cols] = dm * jnp.concatenate([inv] * (gd // LANES), axis=1)
            eext_ref[tm:, cols] = _dot_nt(dyp_halo[:, cols], pw_ref[g]) * (1.0 / POOL_WINDOWS[g])

        for g in range(ng):
            def tile(r0, c0, g=g):
                acc = eext_ref[pl.ds(r0, TILE_ROWS + POOL_HALO), c0:c0 + LANES]
                for step in range(g + 1):
                    acc = acc + _shift_up(acc, 2 ** step)
                here = (pl.ds(r0, TILE_ROWS), slice(c0, c0 + LANES))
                dh_ref[here] = acc[:TILE_ROWS] - dh_ref[here]

            _for_tiles(tm, g * gd, gd, tile)

        dh = dh_ref[...]
        xh, r, n = _rms(x_ref[...], vec[0:1])
        sums_ref[2:3, :] += _colsum(dh)
        sums_ref[1:2, :] += _colsum(dh * n)
        dn = dh * (1.0 + vec[1:2])
        sums_ref[0:1, :] += _colsum(dn * xh)
        dx_ref[...] = dxv + _rms_bwd(dn, xh, r, vec[0:1])

    return pl.pallas_call(
        body, name="pool_bwd", grid=(nsteps,),
        in_specs=[_rows_spec(tm, d),
                  pl.BlockSpec((POOL_HALO, d), lambda i: (jnp.minimum((i + 1) * hb, last_halo), 0)),
                  _rows_spec(tm, d), _rows_spec(tm, d), _const_spec(vec.shape), _const_spec(pw.shape)],
        out_specs=[_rows_spec(tm, d), _rows_spec(tm, d), _const_spec((5, d))],
        out_shape=[SDS((s, d), _LOWP), SDS((s, d), F32), SDS((5, d), F32)],
        scratch_shapes=[pltpu.VMEM((tm + POOL_HALO, d), F32), pltpu.VMEM((tm, d), F32)],
        compiler_params=_cparams("arbitrary"),
    )(dxo, dxo, x, yp, vec, pw)


def _conv_b_bwd(dxo, cw, y, vec, w2):
    s, d = dxo.shape
    tm = _row_block(s)

    def body(dxo_ref, cw_ref, y_ref, vec_ref, w2_ref, dy_ref, dcw_ref, sums_ref):
        i = pl.program_id(0)
        vec = vec_ref[...]

        @pl.when(i == 0)
        def _():
            sums_ref[...] = jnp.zeros_like(sums_ref)

        dxv = dxo_ref[...]
        dy = dxv * (1.0 + vec[4:5])
        sums_ref[0:1, :] += _colsum(dxv * y_ref[...].astype(F32))
        sums_ref[1:2, :] += _colsum(dy)
        dyb = dy.astype(_LOWP)
        dy_ref[...] = dyb
        ds = _dot_nt(dyb, w2_ref[...])
        cw = cw_ref[...]
        cc = cw - jnp.mean(cw, axis=-1, keepdims=True)
        rs = lax.rsqrt(jnp.mean(cc * cc, axis=-1, keepdims=True) + EPS)
        ch = cc * rs
        lo = ch * vec[1:2] + vec[2:3]
        dlo = ds * _dsilu(lo)
        sums_ref[2:3, :] += _colsum(dlo * ch)
        sums_ref[3:4, :] += _colsum(dlo)
        dch = dlo * vec[1:2]
        dcw = rs * (dch - jnp.mean(dch, axis=-1, keepdims=True) - ch * jnp.mean(dch * ch, axis=-1, keepdims=True))
        sums_ref[4:5, :] += _colsum(dcw)
        dcw_ref[...] = dcw

    return pl.pallas_call(
        body, name="conv_b_bwd", grid=(s // tm,),
        in_specs=[_rows_spec(tm, d), _rows_spec(tm, d), _rows_spec(tm, d), _const_spec(vec.shape), _const_spec(w2.shape)],
        out_specs=[_rows_spec(tm, d), _rows_spec(tm, d), _const_spec((5, d))],
        out_shape=[SDS((s, d), _LOWP), SDS((s, d), F32), SDS((5, d), F32)],
        compiler_params=_cparams("arbitrary"),
    )(dxo, cw, y, vec, w2)


def _conv_a_bwd(dcw, glu, a, g, x, dxo, vec, w8, w1, carry=None):
    s, d = x.shape
    nq, _, hq = w1.shape
    tm = _row_block(s)
    hb = tm // CONV_HALO
    nsteps = s // tm
    last_halo = s // CONV_HALO - 1
    lead = CONV_HALO - CONV_WIDTH + 1

    def body(dcw_ref, dhalo_ref, glu_ref, ghalo_ref, a_ref, g_ref, x_ref, dxo_ref, vec_ref, w8_ref, w1_ref,
             du_ref, dx_ref, sums_ref, db1_ref, dwdw_ref, dext_ref, gext_ref, dglu_ref, wacc_ref):
        i = pl.program_id(0)
        vec = vec_ref[...]

        @pl.when(i == 0)
        def _():
            sums_ref[...] = jnp.zeros_like(sums_ref)
            db1_ref[...] = jnp.zeros_like(db1_ref)
            wacc_ref[...] = jnp.zeros_like(wacc_ref)

        dext_ref[0:tm, :] = dcw_ref[...]
        dext_ref[tm:, :] = jnp.where(i < nsteps - 1, dhalo_ref[...], 0.0)
        gext_ref[0:CONV_HALO, :] = jnp.where(i > 0, ghalo_ref[...], 0.0)
        gext_ref[CONV_HALO:, :] = glu_ref[...]

        def tile(r0, c0):
            lanes = slice(c0, c0 + LANES)
            dwin = dext_ref[pl.ds(r0, TILE_ROWS + CONV_HALO), lanes]
            acc = jnp.zeros((TILE_ROWS // SUBLANES, SUBLANES, LANES), F32)
            for b in range(SUBLANES):
                sh = _shift_up(dwin, b)
                for a8 in range(CONV_HALO // SUBLANES):
                    k = CONV_WIDTH - 1 - (SUBLANES * a8 + b)
                    if 0 <= k < CONV_WIDTH:
                        acc = acc + w8_ref[k, :, lanes][None] * _tiles3(sh[SUBLANES * a8:SUBLANES * a8 + TILE_ROWS])
            dglu_ref[pl.ds(r0, TILE_ROWS), lanes] = acc.reshape(TILE_ROWS, LANES)

            dc3 = _tiles3(dwin[:TILE_ROWS])
            gwin = gext_ref[pl.ds(r0, TILE_ROWS + CONV_HALO), lanes]
            for b in range(SUBLANES):
                sh = _shift_up(gwin, b)
                for a8 in range(CONV_HALO // SUBLANES + 1):
                    k = SUBLANES * a8 + b - lead
                    if 0 <= k < CONV_WIDTH:
                        part = jnp.sum(dc3 * _tiles3(sh[SUBLANES * a8:SUBLANES * a8 + TILE_ROWS]), axis=0)
                        wacc_ref[k, :, lanes] += part

        _for_tiles(tm, 0, d, tile)

        dglu = dglu_ref[...]
        av, gv = a_ref[...].astype(F32), g_ref[...].astype(F32)
        sg = _sigmoid(gv)
        da = dglu * sg
        dg = dglu * av * sg * (1.0 - sg)
        du_ref[:, 0:d] = da.astype(_LOWP)
        du_ref[:, d:2 * d] = dg.astype(_LOWP)
        db1_ref[:, 0:d] += _colsum(da)
        db1_ref[:, d:2 * d] += _colsum(dg)
        dh = jnp.zeros((tm, d), F32)
        for q in range(nq):
            dh = dh + _dot_nt(du_ref[:, q * hq:(q + 1) * hq], w1_ref[q])
        xh, r, n = _rms(x_ref[...], vec[0:1])
        sums_ref[2:3, :] += _colsum(dh)
        sums_ref[1:2, :] += _colsum(dh * n)
        dn = dh * (1.0 + vec[1:2])
        sums_ref[0:1, :] += _colsum(dn * xh)
        dx_ref[...] = dxo_ref[...] + _rms_bwd(dn, xh, r, vec[0:1])

        @pl.when(i == nsteps - 1)
        def _():
            dwdw_ref[...] = jnp.sum(wacc_ref[...], axis=1)

    return _call(
        body, dcw, dcw, glu, glu, a, g, x, dxo, vec, w8, w1, carry=carry, name="conv_a_bwd", grid=(nsteps,),
        in_specs=[_rows_spec(tm, d),
                  pl.BlockSpec((CONV_HALO, d), lambda i: (jnp.minimum((i + 1) * hb, last_halo), 0)),
                  _rows_spec(tm, d),
                  pl.BlockSpec((CONV_HALO, d), lambda i: (jnp.maximum(i * hb - 1, 0), 0)),
                  _rows_spec(tm, d), _rows_spec(tm, d), _rows_spec(tm, d), _rows_spec(tm, d),
                  _const_spec(vec.shape), _const_spec(w8.shape), _const_spec(w1.shape)],
        out_specs=[_rows_spec(tm, 2 * d), _rows_spec(tm, d), _const_spec((3, d)), _const_spec((1, 2 * d)),
                   _const_spec((CONV_WIDTH, d))],
        out_shape=[SDS((s, 2 * d), _LOWP), SDS((s, d), F32), SDS((3, d), F32), SDS((1, 2 * d), F32),
                   SDS((CONV_WIDTH, d), F32)],
        scratch_shapes=[pltpu.VMEM((tm + CONV_HALO, d), F32), pltpu.VMEM((tm + CONV_HALO, d), F32),
                        pltpu.VMEM((tm, d), F32), pltpu.VMEM((CONV_WIDTH, SUBLANES, d), F32)],
        sem=("arbitrary",))


def _adamw(w, g, m, v, name):
    r, c = w.shape
    rb = r
    for cand in (512, 352, 256, 184, 128, 64, 32, 16, 8):
        if r % cand == 0 and cand * c * 4 <= ADAM_BLOCK_BYTES:
            rb = cand
            break

    def body(w_ref, g_ref, m_ref, v_ref, d_ref, mo_ref, vo_ref):
        gv = g_ref[...]
        mn = ADAM_B1 * m_ref[...] + (1.0 - ADAM_B1) * gv
        vn = ADAM_B2 * v_ref[...] + (1.0 - ADAM_B2) * (gv * gv)
        m_hat = mn / (1.0 - ADAM_B1 ** ADAM_STEP)
        v_hat = vn / (1.0 - ADAM_B2 ** ADAM_STEP)
        d_ref[...] = -ADAM_LR * (m_hat / (jnp.sqrt(v_hat) + ADAM_EPS) + ADAM_WD * w_ref[...])
        mo_ref[...] = mn
        vo_ref[...] = vn

    spec = pl.BlockSpec((rb, c), lambda i: (i, 0))
    return pl.pallas_call(
        body, name=name, grid=(r // rb,), in_specs=[spec] * 4, out_specs=[spec] * 3,
        out_shape=[SDS((r, c), F32)] * 3, compiler_params=_cparams("parallel"),
    )(w, g, m, v)


def _adamw_nd(w, g, m, v, name):
    shape = w.shape
    two_d = (-1, shape[-1])
    outs = _adamw(w.reshape(two_d), g.reshape(two_d), m.reshape(two_d), v.reshape(two_d), name)
    return [o.reshape(shape) for o in outs]


def _adamw_from_slots(w, m, v, segs, pos, name):
    r, c_ = w.shape
    rb = _divisor_block(r, (256, 176, 128, 64, 32, 16))
    for slots, _, first, rows in segs:
        half = slots.shape[1] // 2
        while first % rb or rows % rb or half % rb:
            rb //= 2
    assert rb % 16 == 0, (name, rb)
    plan, start = [], 0
    for slots, _, first, rows in segs:
        plan.append((start, rows // rb, first // rb, slots.shape[1] // 2 // rb))
        start += rows // rb
    assert start * rb == r, (name, start, rb, r)

    def body(pos_ref, w_ref, m_ref, v_ref, *rest):
        seg_refs, (g_ref, d_ref, mo_ref, vo_ref) = rest[:2 * len(segs)], rest[2 * len(segs):]
        i = pl.program_id(0)
        for k, (s0, nblk, b0, nbh) in enumerate(plan):
            @pl.when((i >= s0) & (i < s0 + nblk))
            def _(k=k, s0=s0, b0=b0, nbh=nbh):
                slots = seg_refs[2 * k][...].astype(F32)
                blk = b0 + i - s0
                in_my_half = (blk >= pos_ref[0] * nbh) & (blk < (pos_ref[0] + 1) * nbh)
                own = jnp.where(in_my_half, seg_refs[2 * k + 1][0].astype(F32), slots[3])
                gv = ((own + slots[0]) + slots[1]) + slots[2]
                mn = ADAM_B1 * m_ref[...] + (1.0 - ADAM_B1) * gv
                vn = ADAM_B2 * v_ref[...] + (1.0 - ADAM_B2) * (gv * gv)
                m_hat = mn / (1.0 - ADAM_B1 ** ADAM_STEP)
                v_hat = vn / (1.0 - ADAM_B2 ** ADAM_STEP)
                g_ref[...] = gv
                d_ref[...] = -ADAM_LR * (m_hat / (jnp.sqrt(v_hat) + ADAM_EPS) + ADAM_WD * w_ref[...])
                mo_ref[...] = mn
                vo_ref[...] = vn

    spec = pl.BlockSpec((rb, c_), lambda i, pos: (i, 0))
    seg_specs, seg_args = [], []
    for (slots, mine, _, _), (s0, nblk, b0, nbh) in zip(segs, plan):
        def in_seg(i, s0=s0, nblk=nblk):
            return jnp.clip(i - s0, 0, nblk - 1)
        seg_specs.append(pl.BlockSpec((N_CHIPS, rb, c_), lambda i, pos, f=in_seg, b0=b0: (0, b0 + f(i), 0)))
        seg_specs.append(pl.BlockSpec(
            (1, rb, c_), lambda i, pos, f=in_seg, b0=b0, nbh=nbh: (pos[1], jnp.clip(b0 + f(i) - pos[0] * nbh, 0, nbh - 1), 0)))
        seg_args += [slots, mine]
    grid_spec = pltpu.PrefetchScalarGridSpec(
        num_scalar_prefetch=1, grid=(r // rb,), in_specs=[spec] * 3 + seg_specs, out_specs=[spec] * 4)
    return pl.pallas_call(
        body, name=name, grid_spec=grid_spec, out_shape=[SDS((r, c_), F32)] * 4,
        compiler_params=_cparams("arbitrary"),
    )(pos, w, m, v, *seg_args)


def _rows_of(*vs):
    return jnp.concatenate([v.reshape(-1, v.shape[-1]) for v in vs], axis=0)


def kernel(x, c, ada_w, ada_b, norm_mix_g, norm_ffn_g, conv_w1, conv_b1, conv_wdw, conv_bdw, conv_ln_g, conv_ln_b, conv_w2, conv_b2, pool_w, pool_ls, ffn_w_gate, ffn_w_up, ffn_w_down, final_g, loss_target, m_ada_w, m_ada_b, m_norm_mix_g, m_norm_ffn_g, m_conv_w1, m_conv_b1, m_conv_wdw, m_conv_bdw, m_conv_ln_g, m_conv_ln_b, m_conv_w2, m_conv_b2, m_pool_w, m_pool_ls, m_ffn_w_gate, m_ffn_w_up, m_ffn_w_down, m_final_g, v_ada_w, v_ada_b, v_norm_mix_g, v_norm_ffn_g, v_conv_w1, v_conv_b1, v_conv_wdw, v_conv_bdw, v_conv_ln_g, v_conv_ln_b, v_conv_w2, v_conv_b2, v_pool_w, v_pool_ls, v_ffn_w_gate, v_ffn_w_up, v_ffn_w_down, v_final_g):
    s, d = x.shape[1], x.shape[2]
    depth = ada_w.shape[0]
    assert depth == 2 and conv_w1.shape[0] == 1 and pool_w.shape[0] == 1, "one conv layer then one pool layer"
    fq = ffn_w_gate.shape[2]
    n_ada = ada_b.shape[1] // d
    ada_cols = ada_w.shape[2]
    ng, gq, gd = pool_w.shape[1], pool_w.shape[2], pool_w.shape[3]
    cq = conv_wdw.shape[2]
    ax, ay, ac = _mesh_pos()
    myq = 2 * ax + ay
    me = 4 * ax + 2 * ay + ac
    x0 = x.reshape(s, d)
    tgt = loss_target.reshape(s, d)

    n_c = d // LANES
    small_in = jnp.concatenate(
        [c.reshape(n_c, LANES), conv_wdw[0].reshape(-1, LANES), pool_ls.reshape(-1, LANES)], axis=0)
    small_in_all = _all_gather8(small_in, "gather_c_and_small_shards").reshape(N_DEV, -1, LANES)
    c_all = small_in_all[:, :n_c].reshape(N_DEV, d)
    ada_b_cols = lax.dynamic_slice_in_dim(ada_b, myq * ada_cols, ada_cols, axis=1).reshape(depth, 1, ada_cols)
    mod_part = _ada_forward(c_all, ada_w, ada_b_cols)
    mod_all = _all_gather8(mod_part.reshape(-1, LANES), "gather_mod").reshape(N_DEV, depth, N_DEV, ada_cols)
    mod_mine = lax.dynamic_index_in_dim(mod_all[0::2], me, axis=2, keepdims=False)
    mod = jnp.transpose(mod_mine, (1, 0, 2)).reshape(depth, n_ada, d)

    def gathered(ex_out, shards):
        return [lax.dynamic_update_slice(g, sh[None], (myq, 0, 0)) for g, sh in zip(ex_out, shards)]

    pos = jnp.stack([ac, myq]).astype(jnp.int32)
    conv_shards = [conv_w1[0].astype(_LOWP), conv_w2[0].astype(_LOWP)]
    ffn0_shards = [ffn_w_gate[0].astype(_LOWP), ffn_w_up[0].astype(_LOWP), ffn_w_down[0].astype(_LOWP)]
    l1_shards = [pool_w[0].reshape(ng * gq, gd).astype(_LOWP), ffn_w_gate[1].astype(_LOWP), ffn_w_up[1].astype(_LOWP),
                 ffn_w_down[1].astype(_LOWP)]
    w1, w2_all = gathered(_Exchange("gather", conv_shards).alone(conv_shards, "gather_conv_weights"), conv_shards)
    w2 = w2_all.reshape(d, d)
    wdw_full_shards = small_in_all[0::2, n_c:].reshape(N_CHIPS, CONV_WIDTH + 1, cq)
    wdw = jnp.transpose(wdw_full_shards[:, :CONV_WIDTH], (1, 0, 2)).reshape(CONV_WIDTH, d)
    pls = wdw_full_shards[:, CONV_WIDTH].reshape(1, d)
    w8 = jnp.broadcast_to(wdw[:, None, :], (CONV_WIDTH, SUBLANES, d))

    vec_ca = _rows_of(norm_mix_g[0], mod[0, 1], mod[0, 0])
    h0, a0, g0, glu = _conv_a_fwd(x0, vec_ca, w1, conv_b1)
    vec_cb = _rows_of(conv_bdw, conv_ln_g, conv_ln_b, conv_b2, mod[0, 2])
    (cw, s0, y0, x1), ffn0_w = _conv_b_fwd(glu, w8, vec_cb, w2, x0, carry=(_Exchange("gather", ffn0_shards), ffn0_shards))
    wg0, wu0, wd0 = gathered(ffn0_w, ffn0_shards)
    vec_f0 = _rows_of(norm_ffn_g[0], mod[0, 4], mod[0, 3], mod[0, 5])
    (h2a, gta, upa, acta, y2a, x2), l1_w = _ffn_fwd(x1, vec_f0, wg0, wu0, wd0, "ffn0_fwd",
                                                   carry=(_Exchange("gather", l1_shards), l1_shards))
    pw_all, wg1, wu1, wd1 = gathered(l1_w, l1_shards)
    pw = jnp.transpose(pw_all.reshape(N_CHIPS, ng, gq, gd), (1, 0, 2, 3)).reshape(ng, N_CHIPS * gq, gd)
    vec_p = _rows_of(norm_mix_g[1], mod[1, 1], mod[1, 0], mod[1, 2], pls)
    mixed, yp, x3 = _pool_fwd(x2, vec_p, pw)
    vec_f1 = _rows_of(norm_ffn_g[1], mod[1, 4], mod[1, 3], mod[1, 5])
    (h2b, gtb, upb, actb, y2b, x4), _ = _ffn_fwd(x3, vec_f1, wg1, wu1, wd1, "ffn1_fwd")
    dx4, fin_sums = _final_fwd_bwd(x4, tgt, final_g.reshape(1, d))
    loss = lax.psum(fin_sums[1, 0], ("x", "y", "c"))

    def chip_sums(partials, tag):
        swapped = _sibling_swap_halves(partials, "grad_swap_" + tag)
        return [_add_sibling_half(p, r, pos, f"grad_add_{tag}{k}") for k, (p, r) in enumerate(zip(partials, swapped))]

    def ffn_wgrads(h2, dgt, dup, act, dy, tag):
        gate_up = _wgrad(h2, dgt, _shared_rows(d), _quarter_major(fq), N_CHIPS, d, fq, s, "wgrad_gate" + tag, stack_blocks=2)
        gate_up = _wgrad(h2, dup, _shared_rows(d), _quarter_major(fq), N_CHIPS, d, fq, s, "wgrad_up" + tag,
                         stack=gate_up, stack_blocks=2, block=1)
        down = _wgrad(act, dy, _quarter_major(fq), _shared_rows(d), N_CHIPS, fq, d, s, "wgrad_down" + tag)
        return [gate_up, down]

    (dgtb, dupb, dyb, dx3, sums_f1), _ = _ffn_bwd(dx4, x3, vec_f1, gtb, upb, y2b, wg1, wu1, wd1, "ffn1_bwd")
    cs_l1 = chip_sums(ffn_wgrads(h2b, dgtb, dupb, actb, dyb, "1"), "l1")
    dyp, dx2, sums_p = _pool_bwd(dx3, x2, yp, vec_p, pw)
    dpw = _wgrad(mixed, dyp, _column_block(gd), _column_block(gd), ng, gd, gd, s, "wgrad_pool")
    dpw_q = jnp.transpose(dpw.reshape(ng, N_CHIPS, gq, gd), (1, 0, 2, 3)).reshape(N_CHIPS, ng * gq, gd)
    (dgta, dupa, dya, dx1, sums_f0), slots_l1 = _ffn_bwd(dx2, x1, vec_f0, gta, upa, y2a, wg0, wu0, wd0, "ffn0_bwd",
                                                        carry=(_Exchange("reduce", cs_l1), cs_l1))
    dy0, dcw, sums_cb = _conv_b_bwd(dx1, cw, y0, vec_cb, w2)
    dw2 = _wgrad(s0, dy0, _column_block(d // N_CHIPS), _shared_rows(d), N_CHIPS, d // N_CHIPS, d, s, "wgrad_conv2")
    cs_l0 = chip_sums(ffn_wgrads(h2a, dgta, dupa, acta, dya, "0") + [dpw_q, dw2], "l0")
    (du, dx0, sums_ca, db1, dwdw), slots_l0 = _conv_a_bwd(dcw, glu, a0, g0, x0, dx1, vec_ca, w8, w1,
                                                          carry=(_Exchange("reduce", cs_l0), cs_l0))
    dw1 = _wgrad(h0, du, _shared_rows(d), _column_block(2 * d // N_CHIPS), N_CHIPS, d, 2 * d // N_CHIPS, s, "wgrad_conv1")
    cs_conv = chip_sums([dw1], "conv")
    slots_conv = _Exchange("reduce", cs_conv).alone(cs_conv, "grad_exchange_conv")

    dmod = jnp.stack([
        _rows_of(sums_ca[2], sums_ca[1], sums_cb[0], sums_f0[2], sums_f0[1], sums_f0[3]),
        _rows_of(sums_p[2], sums_p[1], sums_p[3], sums_f1[2], sums_f1[1], sums_f1[3])])
    small = _rows_of(dmod.reshape(-1, d), sums_ca[0], sums_p[0], sums_f0[0], sums_f1[0], db1.reshape(2, d),
                     sums_cb[4], sums_cb[2], sums_cb[3], sums_cb[1], fin_sums[0], sums_p[4], dwdw)
    n_small = small.shape[0]
    pad = (-n_small) % SUBLANES
    small = jnp.concatenate([small, jnp.zeros((pad, d), F32)], axis=0) if pad else small
    small_all = _all_gather8(small.reshape(-1, LANES), "gather_small_grads").reshape(N_DEV, -1, LANES)
    small_sum = _sum_devices(small_all).reshape(-1, d)
    n_dm = depth * n_ada
    g_ada_b = small_sum[0:n_dm].reshape(depth, n_ada * d)
    g_norm_mix = small_sum[n_dm:n_dm + 2]
    g_norm_ffn = small_sum[n_dm + 2:n_dm + 4]
    g_b1 = small_sum[n_dm + 4:n_dm + 6].reshape(1, 2 * d)
    g_bdw, g_lng, g_lnb, g_b2 = (small_sum[n_dm + 6 + k].reshape(1, d) for k in range(4))
    g_final = small_sum[n_dm + 10]
    g_pls = lax.dynamic_slice_in_dim(small_sum[n_dm + 11].reshape(1, d), myq * cq, cq, axis=1)
    g_wdw = lax.dynamic_slice_in_dim(small_sum[n_dm + 12:n_dm + 12 + CONV_WIDTH], myq * cq, cq, axis=1).reshape(conv_wdw.shape)
    dmod_all = small_all.reshape(N_DEV, -1, d)[:, 0:n_dm].reshape(N_DEV, depth, n_ada * d)
    dmod_cols = lax.dynamic_slice_in_dim(jnp.transpose(dmod_all, (1, 0, 2)), myq * ada_cols, ada_cols, axis=2)
    g_ada_w = _ada_backward(c_all.T, dmod_cols)

    def small_pack(*vs):
        return jnp.concatenate([v.reshape(-1) for v in vs]).reshape(-1, LANES)

    names = ("ada_b", "norm_mix_g", "norm_ffn_g", "conv_b1", "conv_bdw", "conv_ln_g", "conv_ln_b", "conv_b2", "final_g")
    small_w = (ada_b, norm_mix_g, norm_ffn_g, conv_b1, conv_bdw, conv_ln_g, conv_ln_b, conv_b2, final_g)
    small_g = (g_ada_b, g_norm_mix, g_norm_ffn, g_b1, g_bdw, g_lng, g_lnb, g_b2, g_final)
    small_m = (m_ada_b, m_norm_mix_g, m_norm_ffn_g, m_conv_b1, m_conv_bdw, m_conv_ln_g, m_conv_ln_b, m_conv_b2, m_final_g)
    small_v = (v_ada_b, v_norm_mix_g, v_norm_ffn_g, v_conv_b1, v_conv_bdw, v_conv_ln_g, v_conv_ln_b, v_conv_b2, v_final_g)
    sd, sm, sv = _adamw(small_pack(*small_w), small_pack(*small_g), small_pack(*small_m), small_pack(*small_v), "adamw_small")

    def unpack(flat2d):
        flat = flat2d.reshape(-1)
        out, off = {}, 0
        for nm, wv in zip(names, small_w):
            out[nm] = flat[off:off + wv.size].reshape(wv.shape)
            off += wv.size
        return out

    sd, sm, sv = unpack(sd), unpack(sm), unpack(sv)
    grads = {
        "ada_w": g_ada_w, "ada_b": g_ada_b, "norm_mix_g": g_norm_mix, "norm_ffn_g": g_norm_ffn,
        "conv_b1": g_b1, "conv_wdw": g_wdw, "conv_bdw": g_bdw, "conv_ln_g": g_lng,
        "conv_ln_b": g_lnb, "conv_b2": g_b2, "pool_ls": g_pls, "final_g": g_final,
    }
    plain = {
        "ada_w": (ada_w, m_ada_w, v_ada_w), "conv_wdw": (conv_wdw, m_conv_wdw, v_conv_wdw),
        "pool_ls": (pool_ls, m_pool_ls, v_pool_ls),
    }
    from_slots = {
        "conv_w1": (conv_w1, m_conv_w1, v_conv_w1, [(slots_conv[0], cs_conv[0], 0, d)]),
        "conv_w2": (conv_w2, m_conv_w2, v_conv_w2, [(slots_l0[3], cs_l0[3], 0, d // N_CHIPS)]),
        "pool_w": (pool_w, m_pool_w, v_pool_w, [(slots_l0[2], cs_l0[2], 0, ng * gq)]),
        "ffn_w_gate": (ffn_w_gate, m_ffn_w_gate, v_ffn_w_gate,
                       [(slots_l0[0], cs_l0[0], 0, d), (slots_l1[0], cs_l1[0], 0, d)]),
        "ffn_w_up": (ffn_w_up, m_ffn_w_up, v_ffn_w_up,
                     [(slots_l0[0], cs_l0[0], d, d), (slots_l1[0], cs_l1[0], d, d)]),
        "ffn_w_down": (ffn_w_down, m_ffn_w_down, v_ffn_w_down,
                       [(slots_l0[1], cs_l0[1], 0, fq), (slots_l1[1], cs_l1[1], 0, fq)]),
    }
    order = ("ada_w", "ada_b", "norm_mix_g", "norm_ffn_g", "conv_w1", "conv_b1", "conv_wdw", "conv_bdw", "conv_ln_g",
             "conv_ln_b", "conv_w2", "conv_b2", "pool_w", "pool_ls", "ffn_w_gate", "ffn_w_up", "ffn_w_down", "final_g")
    delta, new_m, new_v = {}, {}, {}
    for nm in order:
        if nm in from_slots:
            wv, mv, vv, segs = from_slots[nm]
            two_d = (-1, wv.shape[-1])
            outs = _adamw_from_slots(wv.reshape(two_d), mv.reshape(two_d), vv.reshape(two_d), segs, pos, "adamw_" + nm)
            grads[nm], delta[nm], new_m[nm], new_v[nm] = (o.reshape(wv.shape) for o in outs)
        elif nm in plain:
            wv, mv, vv = plain[nm]
            grads[nm] = grads[nm].reshape(wv.shape)
            delta[nm], new_m[nm], new_v[nm] = _adamw_nd(wv, grads[nm], mv, vv, "adamw_" + nm)
        else:
            delta[nm], new_m[nm], new_v[nm] = sd[nm], sm[nm], sv[nm]
            grads[nm] = grads[nm].reshape(dict(zip(names, small_w))[nm].shape)

    return (loss, dx0.reshape(x.shape), *[grads[n] for n in order], *[delta[n] for n in order],
            *[new_m[n] for n in order], *[new_v[n] for n in order])
```

```python
import functools

import jax
import jax.numpy as jnp
from jax import lax
from jax.experimental import pallas as pl
from jax.experimental.pallas import tpu as pltpu

F32 = jnp.float32
_LOWP = jnp.bfloat16
EPS = 1e-6
CONV_WIDTH = 31
POOL_WINDOWS = (2, 4, 8, 16)
ADAM_LR = 0.001
ADAM_B1 = 0.9
ADAM_B2 = 0.999
ADAM_EPS = 1e-08
ADAM_WD = 0.01
ADAM_STEP = 10

N_CHIPS = 4
N_DEV = 8
LANES = 128
SUBLANES = 8
CONV_HALO = 32
POOL_HALO = 16
TILE_ROWS = 64
VMEM_LIMIT = 56 * 1024 * 1024
ADAM_BLOCK_BYTES = 1 << 20
MESH = pl.DeviceIdType.MESH
SDS = jax.ShapeDtypeStruct


def _cparams(*sem):
    return pltpu.CompilerParams(dimension_semantics=sem if sem else None, vmem_limit_bytes=VMEM_LIMIT)


def _row_block(s):
    for tm in (512, 256, 128, 64, 32, 16):
        if s % tm == 0:
            return tm
    raise ValueError(f"sequence length {s} must be a multiple of 16")


def _rows_spec(tm, ncols):
    return pl.BlockSpec((tm, ncols), lambda i, *_: (i, 0))


def _const_spec(shape):
    nd = len(shape)
    return pl.BlockSpec(shape, lambda *_: (0,) * nd)


def _dot(a, b):
    return lax.dot_general(a.astype(_LOWP), b.astype(_LOWP), (((1,), (0,)), ((), ())), preferred_element_type=F32)


def _dot_nt(a, b):
    return lax.dot_general(a.astype(_LOWP), b.astype(_LOWP), (((1,), (1,)), ((), ())), preferred_element_type=F32)


def _dot_tn(a, b):
    return lax.dot_general(a.astype(_LOWP), b.astype(_LOWP), (((0,), (0,)), ((), ())), preferred_element_type=F32)


def _rms(x, gamma):
    r = lax.rsqrt(jnp.mean(x * x, axis=-1, keepdims=True) + EPS)
    xh = x * r
    return xh, r, xh * gamma


def _rms_bwd(dn, xh, r, gamma):
    dxh = dn * gamma
    return r * (dxh - xh * jnp.mean(dxh * xh, axis=-1, keepdims=True))


def _colsum(v):
    return jnp.sum(v, axis=0, keepdims=True)


def _sigmoid(v):
    return jax.nn.sigmoid(v)


def _dsilu(v):
    sg = _sigmoid(v)
    return sg * (1.0 + v * (1.0 - sg))


def _for_tiles(nrows, col0, ncols, fn):
    def step(t, carry):
        r0 = pl.multiple_of(t * TILE_ROWS, TILE_ROWS)
        for col in range(col0, col0 + ncols, LANES):
            fn(r0, col)
        return carry

    lax.fori_loop(0, nrows // TILE_ROWS, step, 0)


def _shift_up(win, b):
    return pltpu.roll(win, win.shape[0] - b, 0) if b else win


def _shift_down(win, b):
    return pltpu.roll(win, b, 0) if b else win


def _tiles3(v):
    return v.reshape(v.shape[0] // SUBLANES, SUBLANES, v.shape[1])


def _resident_scratch(arrays):
    return [pltpu.VMEM(a.shape, a.dtype) for a in arrays] + [pltpu.SemaphoreType.DMA((len(arrays),))]


def _load_resident(hbm_refs, vmem_refs, sems):
    copies = [pltpu.make_async_copy(src, dst, sems.at[k]) for k, (src, dst) in enumerate(zip(hbm_refs, vmem_refs))]
    for cp in copies:
        cp.start()
    for cp in copies:
        cp.wait()


def _mesh_pos():
    return lax.axis_index("x"), lax.axis_index("y"), lax.axis_index("c")


def _other_chips(x, y):
    return [(1 - x, y), (x, 1 - y), (1 - x, 1 - y)]


def _all_gather8(v, name):
    m, n = v.shape

    def body(x_ref, out_ref, send_sems, recv_sems, local_sem):
        x, y, c = _mesh_pos()
        me, sibling = (x, y, c), (x, y, 1 - c)
        chips = _other_chips(x, y)

        def rows(px, py, pc):
            return out_ref.at[pl.ds((4 * px + 2 * py + pc) * m, m), :]

        def copy(k, block, to, src=None):
            return pltpu.make_async_remote_copy(
                src_ref=rows(*block) if src is None else src, dst_ref=rows(*block),
                send_sem=send_sems.at[k], recv_sem=recv_sems.at[k], device_id=to, device_id_type=MESH)

        mine = pltpu.make_async_copy(x_ref, rows(*me), local_sem)
        mine.start()
        first = [copy(0, me, sibling, src=x_ref)]
        first += [copy(1 + j, me, (*chip, c), src=x_ref) for j, chip in enumerate(chips)]
        for cp in first:
            cp.start()
        passed = [copy(4 + j, (*chip, c), sibling) for j, chip in enumerate(chips)]
        for j, chip in enumerate(chips):
            copy(1 + j, (*chip, c), me).wait_recv()
            passed[j].start()
        copy(0, sibling, me).wait_recv()
        for j, chip in enumerate(chips):
            copy(4 + j, (*chip, 1 - c), me).wait_recv()
        for cp in first + passed:
            cp.wait_send()
        mine.wait()

    return pl.pallas_call(
        body, name=name,
        out_shape=SDS((N_DEV * m, n), v.dtype),
        in_specs=[pl.BlockSpec(memory_space=pltpu.VMEM)],
        out_specs=pl.BlockSpec(memory_space=pltpu.VMEM),
        scratch_shapes=[pltpu.SemaphoreType.DMA((7,)), pltpu.SemaphoreType.DMA((7,)), pltpu.SemaphoreType.DMA],
    )(v)


class _Exchange:
    N_SEMS = 13

    def __init__(self, kind, arrays):
        self.kind, self.n = kind, len(arrays)
        self.streams = 3 if kind == "gather" else 6
        lead = N_CHIPS if kind == "gather" else 7
        tail = [a.shape if kind == "gather" else a.shape[1:] for a in arrays]
        self.out_shape = [SDS((lead,) + tuple(t), a.dtype) for t, a in zip(tail, arrays)]
        self.scratch = [pltpu.SemaphoreType.DMA((self.n, self.N_SEMS)), pltpu.SemaphoreType.DMA((self.n, self.N_SEMS))]

    def _ctx(self, ins, outs, send_sems, recv_sems):
        x, y, c = _mesh_pos()
        myq = 2 * x + y
        sibling = (x, y, 1 - c)
        chips = _other_chips(x, y)
        gather = self.kind == "gather"
        ns = self.streams

        def rows(a, h):
            hr = outs[a].shape[1] // 2
            return pl.ds(h * hr, hr)

        def copy(a, ks, kr, src, dst, to):
            return pltpu.make_async_remote_copy(src_ref=src, dst_ref=dst, send_sem=send_sems.at[a, ks],
                                                recv_sem=recv_sems.at[a, kr], device_id=to, device_id_type=MESH)

        def place(a, t, h):
            if gather:
                px, py = chips[t]
                return outs[a].at[2 * px + py, rows(a, h)]
            return outs[a].at[t, rows(a, h)]

        def first(a, t):
            if gather:
                return copy(a, t, t, ins[a].at[rows(a, c)], outs[a].at[myq, rows(a, c)], (*chips[t], c))
            j, e = t // 2, t % 2
            px, py = chips[j]
            return copy(a, t, 2 * j + c, ins[a].at[2 * px + py, rows(a, e)], outs[a].at[2 * j + c, rows(a, e)], (px, py, e))

        def landed(a, t):
            return copy(a, t, t, place(a, t, c), place(a, t, c), sibling)

        def relay(a, t):
            return copy(a, ns + t, ns + t, place(a, t, c), place(a, t, c), sibling)

        def relayed(a, t):
            return copy(a, ns + t, ns + t, place(a, t, 1 - c), place(a, t, 1 - c), sibling)

        def own(a):
            return copy(a, 2 * ns, 2 * ns, ins[a].at[myq], outs[a].at[2 * ns // 2], sibling)

        return first, landed, relay, relayed, own

    def start(self, ins, outs, send_sems, recv_sems):
        first, _, _, _, own = self._ctx(ins, outs, send_sems, recv_sems)
        for a in range(self.n):
            for t in range(self.streams):
                first(a, t).start()
            if self.kind == "reduce":
                own(a).start()

    def relay(self, ins, outs, send_sems, recv_sems):
        _, landed, relay, _, _ = self._ctx(ins, outs, send_sems, recv_sems)
        for a in range(self.n):
            for t in range(self.streams):
                landed(a, t).wait_recv()
                relay(a, t).start()

    def finish(self, ins, outs, send_sems, recv_sems):
        first, _, relay, relayed, own = self._ctx(ins, outs, send_sems, recv_sems)
        for a in range(self.n):
            for t in range(self.streams):
                relayed(a, t).wait_recv()
            if self.kind == "reduce":
                own(a).wait_recv()
        for a in range(self.n):
            for t in range(self.streams):
                first(a, t).wait_send()
                relay(a, t).wait_send()
            if self.kind == "reduce":
                own(a).wait_send()

    def alone(self, arrays, name):
        n = self.n

        def body(*refs):
            args = (refs[:n], refs[n:2 * n], refs[2 * n], refs[2 * n + 1])
            self.start(*args)
            self.relay(*args)
            self.finish(*args)

        any_spec = pl.BlockSpec(memory_space=pl.ANY)
        return pl.pallas_call(body, name=name, out_shape=self.out_shape, in_specs=[any_spec] * n,
                              out_specs=[any_spec] * n, scratch_shapes=self.scratch)(*arrays)


RELAY_AT = (5, 8)


def _call(body, *args, carry=None, name, grid, in_specs, out_specs, out_shape, scratch_shapes=(), sem):
    if carry is None:
        outs = pl.pallas_call(body, name=name, grid=grid, in_specs=in_specs, out_specs=out_specs, out_shape=out_shape,
                              scratch_shapes=list(scratch_shapes), compiler_params=_cparams(*sem))(*args)
        return outs, None
    ex, arrays = carry
    n_in, n_out, n_scr, n = len(in_specs), len(out_specs), len(scratch_shapes), ex.n
    total = 1
    for g in grid:
        total *= g
    relay_step = (total * RELAY_AT[0]) // RELAY_AT[1]

    def hosted(*refs):
        ins, cins = refs[:n_in], refs[n_in:n_in + n]
        outs, couts = refs[n_in + n:n_in + n + n_out], refs[n_in + n + n_out:n_in + 2 * n + n_out]
        scr = refs[n_in + 2 * n + n_out:n_in + 2 * n + n_out + n_scr]
        sems = refs[n_in + 2 * n + n_out + n_scr:]
        step = 0
        for ax, g in enumerate(grid):
            step = step * g + pl.program_id(ax)

        @pl.when(step == 0)
        def _():
            ex.start(cins, couts, *sems)

        body(*ins, *outs, *scr)

        @pl.when(step == relay_step)
        def _():
            ex.relay(cins, couts, *sems)

        @pl.when(step == total - 1)
        def _():
            ex.finish(cins, couts, *sems)

    any_spec = pl.BlockSpec(memory_space=pl.ANY)
    res = pl.pallas_call(
        hosted, name=name, grid=grid, in_specs=list(in_specs) + [any_spec] * n,
        out_specs=list(out_specs) + [any_spec] * n, out_shape=list(out_shape) + ex.out_shape,
        scratch_shapes=list(scratch_shapes) + ex.scratch, compiler_params=_cparams(*(("arbitrary",) * len(grid))),
    )(*args, *arrays)
    return res[:n_out], res[n_out:]


def _divisor_block(rows, cands):
    for rb in cands:
        if rows % rb == 0:
            return rb
    raise ValueError(rows)


def _sum_devices(g):
    nd, m, n = g.shape

    def body(g_ref, o_ref):
        acc = g_ref[0]
        for d in range(1, nd):
            acc = acc + g_ref[d]
        o_ref[...] = acc

    return pl.pallas_call(body, name="small_grad_sum", out_shape=SDS((m, n), F32))(g)


def _ada_forward(c_all, ada_w, ada_b_cols):
    nl, d, ncol = ada_w.shape
    nb = c_all.shape[0]
    tn = 512 if ncol % 512 == 0 else ncol

    def body(c_ref, w_ref, b_ref, o_ref):
        cv = c_ref[...]
        ca = cv * _sigmoid(cv)
        o_ref[0] = _dot(ca, w_ref[0]) + b_ref[0]

    return pl.pallas_call(
        body, name="ada_forward", grid=(nl, ncol // tn),
        in_specs=[pl.BlockSpec((nb, d), lambda l, j: (0, 0)),
                  pl.BlockSpec((1, d, tn), lambda l, j: (l, 0, j)),
                  pl.BlockSpec((1, 1, tn), lambda l, j: (l, 0, j))],
        out_specs=pl.BlockSpec((1, nb, tn), lambda l, j: (l, 0, j)),
        out_shape=SDS((nl, nb, ncol), F32), compiler_params=_cparams("parallel", "parallel"),
    )(c_all, ada_w, ada_b_cols)


def _ada_backward(c_all_t, dmod_cols):
    d, nb = c_all_t.shape
    nl, _, ncol = dmod_cols.shape
    tn = 512 if ncol % 512 == 0 else ncol

    def body(c_ref, g_ref, o_ref):
        cv = c_ref[...]
        ca = cv * _sigmoid(cv)
        o_ref[0] = _dot(ca, g_ref[0])

    return pl.pallas_call(
        body, name="ada_backward", grid=(nl, ncol // tn),
        in_specs=[pl.BlockSpec((d, nb), lambda l, j: (0, 0)),
                  pl.BlockSpec((1, nb, tn), lambda l, j: (l, 0, j))],
        out_specs=pl.BlockSpec((1, d, tn), lambda l, j: (l, 0, j)),
        out_shape=SDS((nl, d, ncol), F32), compiler_params=_cparams("parallel", "parallel"),
    )(c_all_t, dmod_cols)


def _conv_a_fwd(x, vec, w1, b1):
    s, d = x.shape
    nq, _, hq = w1.shape
    tm = _row_block(s)

    def body(x_ref, vec_ref, w1_ref, b1_ref, h_ref, a_ref, g_ref, glu_ref):
        vec = vec_ref[...]
        _, _, n = _rms(x_ref[...], vec[0:1])
        hb = (n * (1.0 + vec[1:2]) + vec[2:3]).astype(_LOWP)
        h_ref[...] = hb
        u = [_dot(hb, w1_ref[q]) + b1_ref[:, q * hq:(q + 1) * hq] for q in range(nq)]
        for k in range(nq // 2):
            av, gv = u[k], u[nq // 2 + k]
            cols = slice(k * hq, (k + 1) * hq)
            a_ref[:, cols] = av.astype(_LOWP)
            g_ref[:, cols] = gv.astype(_LOWP)
            glu_ref[:, cols] = av * _sigmoid(gv)

    return pl.pallas_call(
        body, name="conv_a_fwd", grid=(s // tm,),
        in_specs=[_rows_spec(tm, d), _const_spec(vec.shape), _const_spec(w1.shape), _const_spec(b1.shape)],
        out_specs=[_rows_spec(tm, d)] * 4,
        out_shape=[SDS((s, d), _LOWP)] * 3 + [SDS((s, d), F32)],
        compiler_params=_cparams("parallel"),
    )(x, vec, w1, b1)


def _conv_b_fwd(glu, w8, vec, w2, x, carry=None):
    s, d = x.shape
    tm = _row_block(s)
    hb = tm // CONV_HALO

    def body(glu_ref, halo_ref, w8_ref, vec_ref, w2_ref, x_ref, cw_ref, s_ref, y_ref, xo_ref, ext_ref):
        i = pl.program_id(0)
        vec = vec_ref[...]
        ext_ref[0:CONV_HALO, :] = jnp.where(i > 0, halo_ref[...], 0.0)
        ext_ref[CONV_HALO:, :] = glu_ref[...]

        def tile(r0, c0):
            lanes = slice(c0, c0 + LANES)
            win = ext_ref[pl.ds(r0, TILE_ROWS + CONV_HALO), lanes]
            acc = jnp.zeros((TILE_ROWS // SUBLANES, SUBLANES, LANES), F32)
            for b in range(SUBLANES):
                sh = _shift_up(win, b)
                for a in range(CONV_HALO // SUBLANES + 1):
                    k = SUBLANES * a + b - (CONV_HALO - CONV_WIDTH + 1)
                    if 0 <= k < CONV_WIDTH:
                        acc = acc + w8_ref[k, :, lanes][None] * _tiles3(sh[SUBLANES * a:SUBLANES * a + TILE_ROWS])
            cw_ref[pl.ds(r0, TILE_ROWS), lanes] = acc.reshape(TILE_ROWS, LANES)

        _for_tiles(tm, 0, d, tile)
        cw = cw_ref[...] + vec[0:1]
        cw_ref[...] = cw
        cc = cw - jnp.mean(cw, axis=-1, keepdims=True)
        ch = cc * lax.rsqrt(jnp.mean(cc * cc, axis=-1, keepdims=True) + EPS)
        lo = ch * vec[1:2] + vec[2:3]
        sv = (lo * _sigmoid(lo)).astype(_LOWP)
        s_ref[...] = sv
        y = _dot(sv, w2_ref[...]) + vec[3:4]
        y_ref[...] = y.astype(_LOWP)
        xo_ref[...] = x_ref[...] + (1.0 + vec[4:5]) * y

    return _call(
        body, glu, glu, w8, vec, w2, x, carry=carry, name="conv_b_fwd", grid=(s // tm,),
        in_specs=[_rows_spec(tm, d),
                  pl.BlockSpec((CONV_HALO, d), lambda i: (jnp.maximum(i * hb - 1, 0), 0)),
                  _const_spec(w8.shape), _const_spec(vec.shape), _const_spec(w2.shape), _rows_spec(tm, d)],
        out_specs=[_rows_spec(tm, d)] * 4,
        out_shape=[SDS((s, d), F32), SDS((s, d), _LOWP), SDS((s, d), _LOWP), SDS((s, d), F32)],
        scratch_shapes=[pltpu.VMEM((tm + CONV_HALO, d), F32)], sem=("parallel",))


def _ffn_fwd(x, vec, wg, wu, wd, name, carry=None):
    s, d = x.shape
    nq, _, fq = wg.shape
    tm = _row_block(s)

    def body(x_ref, vec_ref, wg_hbm, wu_hbm, wd_hbm, h_ref, gt_ref, up_ref, act_ref, y_ref, xo_ref,
             hb_ref, yacc_ref, wg_ref, wu_ref, wd_ref, w_sems):
        i, q = pl.program_id(0), pl.program_id(1)
        vec = vec_ref[...]

        @pl.when((i == 0) & (q == 0))
        def _():
            _load_resident((wg_hbm, wu_hbm, wd_hbm), (wg_ref, wu_ref, wd_ref), w_sems)

        @pl.when(q == 0)
        def _():
            _, _, n = _rms(x_ref[...], vec[0:1])
            hb = (n * (1.0 + vec[1:2]) + vec[2:3]).astype(_LOWP)
            hb_ref[...] = hb
            h_ref[...] = hb
            yacc_ref[...] = jnp.zeros_like(yacc_ref)

        hb = hb_ref[...]
        gt = _dot(hb, wg_ref[q])
        up = _dot(hb, wu_ref[q])
        act = (gt * _sigmoid(gt) * up).astype(_LOWP)
        gt_ref[0] = gt.astype(_LOWP)
        up_ref[0] = up.astype(_LOWP)
        act_ref[0] = act
        yacc_ref[...] += _dot(act, wd_ref[q])

        @pl.when(q == nq - 1)
        def _():
            y = yacc_ref[...]
            y_ref[...] = y.astype(_LOWP)
            xo_ref[...] = x_ref[...] + (1.0 + vec[3:4]) * y

    rows = pl.BlockSpec((tm, d), lambda i, q: (i, 0))
    hid = pl.BlockSpec((1, tm, fq), lambda i, q: (q, i, 0))
    any_spec = pl.BlockSpec(memory_space=pl.ANY)
    return _call(
        body, x, vec, wg, wu, wd, carry=carry, name=name, grid=(s // tm, nq),
        in_specs=[rows, pl.BlockSpec(vec.shape, lambda i, q: (0, 0)), any_spec, any_spec, any_spec],
        out_specs=[rows, hid, hid, hid, rows, rows],
        out_shape=[SDS((s, d), _LOWP)] + [SDS((nq, s, fq), _LOWP)] * 3 + [SDS((s, d), _LOWP), SDS((s, d), F32)],
        scratch_shapes=[pltpu.VMEM((tm, d), _LOWP), pltpu.VMEM((tm, d), F32)] + _resident_scratch((wg, wu, wd)),
        sem=("arbitrary", "arbitrary"))


def _pool_inv_count(t0, nrows, w):
    t = t0 + lax.broadcasted_iota(jnp.int32, (nrows, LANES), 0)
    return 1.0 / jnp.minimum(t + 1, w).astype(F32)


def _pool_fwd(x, vec, pw):
    s, d = x.shape
    ng, gd, _ = pw.shape
    tm = _row_block(s)
    hb = tm // POOL_HALO

    def body(x_ref, halo_ref, vec_ref, pw_ref, mixed_ref, yp_ref, xo_ref, hext_ref):
        i = pl.program_id(0)
        vec = vec_ref[...]

        def modulated(xv):
            _, _, n = _rms(xv, vec[0:1])
            return n * (1.0 + vec[1:2]) + vec[2:3]

        hext_ref[0:POOL_HALO, :] = jnp.where(i > 0, modulated(halo_ref[...]), 0.0)
        hext_ref[POOL_HALO:, :] = modulated(x_ref[...])

        for g in range(ng):
            def tile(r0, c0, g=g):
                lanes = slice(c0, c0 + LANES)
                win = hext_ref[pl.ds(r0, TILE_ROWS + POOL_HALO), lanes]
                acc = win
                for step in range(g + 1):
                    acc = acc + _shift_down(acc, 2 ** step)
                inv = _pool_inv_count(i * tm + r0, TILE_ROWS, POOL_WINDOWS[g])
                mixed = acc[POOL_HALO:] * inv - win[POOL_HALO:]
                mixed_ref[pl.ds(r0, TILE_ROWS), lanes] = mixed.astype(_LOWP)

            _for_tiles(tm, g * gd, gd, tile)

        for g in range(ng):
            cols = slice(g * gd, (g + 1) * gd)
            yp = _dot(mixed_ref[:, cols], pw_ref[g])
            yp_ref[:, cols] = yp.astype(_LOWP)
            xo_ref[:, cols] = x_ref[:, cols] + (1.0 + vec[3:4, cols]) * (yp * vec[4:5, cols])

    return pl.pallas_call(
        body, name="pool_fwd", grid=(s // tm,),
        in_specs=[_rows_spec(tm, d),
                  pl.BlockSpec((POOL_HALO, d), lambda i: (jnp.maximum(i * hb - 1, 0), 0)),
                  _const_spec(vec.shape), _const_spec(pw.shape)],
        out_specs=[_rows_spec(tm, d)] * 3,
        out_shape=[SDS((s, d), _LOWP), SDS((s, d), _LOWP), SDS((s, d), F32)],
        scratch_shapes=[pltpu.VMEM((tm + POOL_HALO, d), F32)],
        compiler_params=_cparams("parallel"),
    )(x, x, vec, pw)


def _final_fwd_bwd(x, tgt, gamma):
    s, d = x.shape
    tm = _row_block(s)
    nsteps = s // tm

    def body(x_ref, t_ref, g_ref, dx_ref, sums_ref):
        i = pl.program_id(0)
        gamma_v = g_ref[...]
        xh, r, out = _rms(x_ref[...], gamma_v)
        err = out - t_ref[...]
        dout = err * (1.0 / d)

        @pl.when(i == 0)
        def _():
            sums_ref[...] = jnp.zeros_like(sums_ref)

        sums_ref[0:1, :] += _colsum(dout * xh)
        sums_ref[1:2, :] += _colsum(0.5 * err * dout)
        dx_ref[...] = _rms_bwd(dout, xh, r, gamma_v)

        @pl.when(i == nsteps - 1)
        def _():
            sums_ref[1:2, :] = jnp.broadcast_to(jnp.sum(sums_ref[1:2, :], axis=-1, keepdims=True), (1, d))

    return pl.pallas_call(
        body, name="final_loss", grid=(nsteps,),
        in_specs=[_rows_spec(tm, d), _rows_spec(tm, d), _const_spec(gamma.shape)],
        out_specs=[_rows_spec(tm, d), _const_spec((2, d))],
        out_shape=[SDS((s, d), F32), SDS((2, d), F32)],
        compiler_params=_cparams("arbitrary"),
    )(x, tgt, gamma)


def _ffn_bwd(dxo, x, vec, gt, up, y, wg, wu, wd, name, carry=None):
    s, d = x.shape
    nq, _, fq = wg.shape
    tm = _row_block(s)

    def body(dxo_ref, x_ref, vec_ref, gt_ref, up_ref, y_ref, wg_hbm, wu_hbm, wd_hbm,
             dgt_ref, dup_ref, dy_ref, dx_ref, sums_ref, dyb_ref, dh_ref, wg_ref, wu_ref, wd_ref, w_sems):
        i, q = pl.program_id(0), pl.program_id(1)
        vec = vec_ref[...]

        @pl.when((i == 0) & (q == 0))
        def _():
            _load_resident((wg_hbm, wu_hbm, wd_hbm), (wg_ref, wu_ref, wd_ref), w_sems)
            sums_ref[...] = jnp.zeros_like(sums_ref)

        @pl.when(q == 0)
        def _():
            dxv = dxo_ref[...]
            dyb = (dxv * (1.0 + vec[3:4])).astype(_LOWP)
            dyb_ref[...] = dyb
            dy_ref[...] = dyb
            dh_ref[...] = jnp.zeros_like(dh_ref)
            sums_ref[3:4, :] += _colsum(dxv * y_ref[...].astype(F32))

        halves = [slice(k * (tm // 2), (k + 1) * (tm // 2)) for k in range(2)]
        dacts = [_dot_nt(dyb_ref[rs, :], wd_ref[q]) for rs in halves]
        dgus = []
        for rs, dact in zip(halves, dacts):
            gtv = gt_ref[0, rs, :].astype(F32)
            sg = _sigmoid(gtv)
            dgt = (dact * up_ref[0, rs, :].astype(F32) * (sg * (1.0 + gtv * (1.0 - sg)))).astype(_LOWP)
            dup = (dact * (gtv * sg)).astype(_LOWP)
            dgt_ref[0, rs, :] = dgt
            dup_ref[0, rs, :] = dup
            dgus.append((dgt, dup))
        for rs, (dgt, dup) in zip(halves, dgus):
            dh_ref[rs, :] += _dot_nt(dgt, wg_ref[q]) + _dot_nt(dup, wu_ref[q])

        @pl.when(q == nq - 1)
        def _():
            dh = dh_ref[...]
            xh, r, n = _rms(x_ref[...], vec[0:1])
            sums_ref[2:3, :] += _colsum(dh)
            sums_ref[1:2, :] += _colsum(dh * n)
            dn = dh * (1.0 + vec[1:2])
            sums_ref[0:1, :] += _colsum(dn * xh)
            dx_ref[...] = dxo_ref[...] + _rms_bwd(dn, xh, r, vec[0:1])

    rows = pl.BlockSpec((tm, d), lambda i, q: (i, 0))
    hid = pl.BlockSpec((1, tm, fq), lambda i, q: (q, i, 0))
    return _call(
        body, dxo, x, vec, gt, up, y, wg, wu, wd, carry=carry, name=name, grid=(s // tm, nq),
        in_specs=[rows, rows, pl.BlockSpec(vec.shape, lambda i, q: (0, 0)), hid, hid, rows] +
                 [pl.BlockSpec(memory_space=pl.ANY)] * 3,
        out_specs=[hid, hid, rows, rows, pl.BlockSpec((4, d), lambda i, q: (0, 0))],
        out_shape=[SDS((nq, s, fq), _LOWP)] * 2 + [SDS((s, d), _LOWP), SDS((s, d), F32), SDS((4, d), F32)],
        scratch_shapes=[pltpu.VMEM((tm, d), _LOWP), pltpu.VMEM((tm, d), F32)] + _resident_scratch((wg, wu, wd)),
        sem=("arbitrary", "arbitrary"))


def _wgrad(a, b, a_spec, b_spec, nq, ka, nb, s, name, stack=None, stack_blocks=1, block=0):
    tk = min(2048, s)
    nk = s // tk

    def body(a_ref, b_ref, *rest):
        o_ref, acc_ref = rest[-2:]
        k = pl.program_id(1)

        @pl.when(k == 0)
        def _():
            acc_ref[...] = jnp.zeros_like(acc_ref)

        av = a_ref[...].reshape(tk, ka)
        bv = b_ref[...].reshape(tk, nb)
        acc_ref[...] += _dot_tn(av, bv)

        @pl.when(k == nk - 1)
        def _():
            o_ref[0] = acc_ref[...].astype(o_ref.dtype)

    extra = [] if stack is None else [stack]
    return pl.pallas_call(
        body, name=name, grid=(nq, nk),
        in_specs=[a_spec(tk), b_spec(tk)] + [pl.BlockSpec(memory_space=pl.ANY)] * len(extra),
        out_specs=pl.BlockSpec((1, ka, nb), lambda q, k: (q, block, 0)),
        out_shape=SDS((nq, stack_blocks * ka, nb), _LOWP),
        scratch_shapes=[pltpu.VMEM((ka, nb), F32)],
        input_output_aliases={2: 0} if extra else {},
        compiler_params=_cparams("parallel", "arbitrary"),
    )(a, b, *extra)


def _shared_rows(ncols):
    return lambda tk: pl.BlockSpec((tk, ncols), lambda q, k: (k, 0))


def _column_block(ncols):
    return lambda tk: pl.BlockSpec((tk, ncols), lambda q, k: (k, q))


def _quarter_major(ncols):
    return lambda tk: pl.BlockSpec((1, tk, ncols), lambda q, k: (q, k, 0))


def _pool_bwd(dxo, x, yp, vec, pw):
    s, d = x.shape
    ng, gd, _ = pw.shape
    tm = _row_block(s)
    hb = tm // POOL_HALO
    nsteps = s // tm
    last_halo = s // POOL_HALO - 1

    def body(dxo_ref, halo_ref, x_ref, yp_ref, vec_ref, pw_ref, dyp_ref, dx_ref, sums_ref, eext_ref, dh_ref):
        i = pl.program_id(0)
        vec = vec_ref[...]

        @pl.when(i == 0)
        def _():
            sums_ref[...] = jnp.zeros_like(sums_ref)

        dxv = dxo_ref[...]
        ypv = yp_ref[...].astype(F32)
        gate1 = 1.0 + vec[3:4]
        sums_ref[3:4, :] += _colsum(dxv * (ypv * vec[4:5]))
        sums_ref[4:5, :] += _colsum(dxv * gate1 * ypv)
        dyp = (dxv * gate1 * vec[4:5]).astype(_LOWP)
        dyp_ref[...] = dyp
        dyp_halo = (jnp.where(i < nsteps - 1, halo_ref[...], 0.0) * gate1 * vec[4:5]).astype(_LOWP)
        for g in range(ng):
            cols = slice(g * gd, (g + 1) * gd)
            dm = _dot_nt(dyp[:, cols], pw_ref[g])
            dh_ref[:, cols] = dm
            inv = _pool_inv_count(i * tm, tm, POOL_WINDOWS[g])
            eext_ref[0:tm, cols] = dm * jnp.concatenate([inv] * (gd // LANES), axis=1)
            eext_ref[tm:, cols] = _dot_nt(dyp_halo[:, cols], pw_ref[g]) * (1.0 / POOL_WINDOWS[g])

        for g in range(ng):
            def tile(r0, c0, g=g):
                acc = eext_ref[pl.ds(r0, TILE_ROWS + POOL_HALO), c0:c0 + LANES]
                for step in range(g + 1):
                    acc = acc + _shift_up(acc, 2 ** step)
                here = (pl.ds(r0, TILE_ROWS), slice(c0, c0 + LANES))
                dh_ref[here] = acc[:TILE_ROWS] - dh_ref[here]

            _for_tiles(tm, g * gd, gd, tile)

        dh = dh_ref[...]
        xh, r, n = _rms(x_ref[...], vec[0:1])
        sums_ref[2:3, :] += _colsum(dh)
        sums_ref[1:2, :] += _colsum(dh * n)
        dn = dh * (1.0 + vec[1:2])
        sums_ref[0:1, :] += _colsum(dn * xh)
        dx_ref[...] = dxv + _rms_bwd(dn, xh, r, vec[0:1])

    return pl.pallas_call(
        body, name="pool_bwd", grid=(nsteps,),
        in_specs=[_rows_spec(tm, d),
                  pl.BlockSpec((POOL_HALO, d), lambda i: (jnp.minimum((i + 1) * hb, last_halo), 0)),
                  _rows_spec(tm, d), _rows_spec(tm, d), _const_spec(vec.shape), _const_spec(pw.shape)],
        out_specs=[_rows_spec(tm, d), _rows_spec(tm, d), _const_spec((5, d))],
        out_shape=[SDS((s, d), _LOWP), SDS((s, d), F32), SDS((5, d), F32)],
        scratch_shapes=[pltpu.VMEM((tm + POOL_HALO, d), F32), pltpu.VMEM((tm, d), F32)],
        compiler_params=_cparams("arbitrary"),
    )(dxo, dxo, x, yp, vec, pw)


def _conv_b_bwd(dxo, cw, y, vec, w2):
    s, d = dxo.shape
    tm = _row_block(s)

    def body(dxo_ref, cw_ref, y_ref, vec_ref, w2_ref, dy_ref, dcw_ref, sums_ref):
        i = pl.program_id(0)
        vec = vec_ref[...]

        @pl.when(i == 0)
        def _():
            sums_ref[...] = jnp.zeros_like(sums_ref)

        dxv = dxo_ref[...]
        dy = dxv * (1.0 + vec[4:5])
        sums_ref[0:1, :] += _colsum(dxv * y_ref[...].astype(F32))
        sums_ref[1:2, :] += _colsum(dy)
        dyb = dy.astype(_LOWP)
        dy_ref[...] = dyb
        ds = _dot_nt(dyb, w2_ref[...])
        cw = cw_ref[...]
        cc = cw - jnp.mean(cw, axis=-1, keepdims=True)
        rs = lax.rsqrt(jnp.mean(cc * cc, axis=-1, keepdims=True) + EPS)
        ch = cc * rs
        lo = ch * vec[1:2] + vec[2:3]
        dlo = ds * _dsilu(lo)
        sums_ref[2:3, :] += _colsum(dlo * ch)
        sums_ref[3:4, :] += _colsum(dlo)
        dch = dlo * vec[1:2]
        dcw = rs * (dch - jnp.mean(dch, axis=-1, keepdims=True) - ch * jnp.mean(dch * ch, axis=-1, keepdims=True))
        sums_ref[4:5, :] += _colsum(dcw)
        dcw_ref[...] = dcw

    return pl.pallas_call(
        body, name="conv_b_bwd", grid=(s // tm,),
        in_specs=[_rows_spec(tm, d), _rows_spec(tm, d), _rows_spec(tm, d), _const_spec(vec.shape), _const_spec(w2.shape)],
        out_specs=[_rows_spec(tm, d), _rows_spec(tm, d), _const_spec((5, d))],
        out_shape=[SDS((s, d), _LOWP), SDS((s, d), F32), SDS((5, d), F32)],
        compiler_params=_cparams("arbitrary"),
    )(dxo, cw, y, vec, w2)


def _conv_a_bwd(dcw, glu, a, g, x, dxo, vec, w8, w1, carry=None):
    s, d = x.shape
    nq, _, hq = w1.shape
    tm = _row_block(s)
    hb = tm // CONV_HALO
    nsteps = s // tm
    last_halo = s // CONV_HALO - 1
    lead = CONV_HALO - CONV_WIDTH + 1

    def body(dcw_ref, dhalo_ref, glu_ref, ghalo_ref, a_ref, g_ref, x_ref, dxo_ref, vec_ref, w8_ref, w1_ref,
             du_ref, dx_ref, sums_ref, db1_ref, dwdw_ref, dext_ref, gext_ref, dglu_ref, wacc_ref):
        i = pl.program_id(0)
        vec = vec_ref[...]

        @pl.when(i == 0)
        def _():
            sums_ref[...] = jnp.zeros_like(sums_ref)
            db1_ref[...] = jnp.zeros_like(db1_ref)
            wacc_ref[...] = jnp.zeros_like(wacc_ref)

        dext_ref[0:tm, :] = dcw_ref[...]
        dext_ref[tm:, :] = jnp.where(i < nsteps - 1, dhalo_ref[...], 0.0)
        gext_ref[0:CONV_HALO, :] = jnp.where(i > 0, ghalo_ref[...], 0.0)
        gext_ref[CONV_HALO:, :] = glu_ref[...]

        def tile(r0, c0):
            lanes = slice(c0, c0 + LANES)
            dwin = dext_ref[pl.ds(r0, TILE_ROWS + CONV_HALO), lanes]
            acc = jnp.zeros((TILE_ROWS // SUBLANES, SUBLANES, LANES), F32)
            for b in range(SUBLANES):
                sh = _shift_up(dwin, b)
                for a8 in range(CONV_HALO // SUBLANES):
                    k = CONV_WIDTH - 1 - (SUBLANES * a8 + b)
                    if 0 <= k < CONV_WIDTH:
                        acc = acc + w8_ref[k, :, lanes][None] * _tiles3(sh[SUBLANES * a8:SUBLANES * a8 + TILE_ROWS])
            dglu_ref[pl.ds(r0, TILE_ROWS), lanes] = acc.reshape(TILE_ROWS, LANES)

            dc3 = _tiles3(dwin[:TILE_ROWS])
            gwin = gext_ref[pl.ds(r0, TILE_ROWS + CONV_HALO), lanes]
            for b in range(SUBLANES):
                sh = _shift_up(gwin, b)
                for a8 in range(CONV_HALO // SUBLANES + 1):
                    k = SUBLANES * a8 + b - lead
                    if 0 <= k < CONV_WIDTH:
                        part = jnp.sum(dc3 * _tiles3(sh[SUBLANES * a8:SUBLANES * a8 + TILE_ROWS]), axis=0)
                        wacc_ref[k, :, lanes] += part

        _for_tiles(tm, 0, d, tile)

        dglu = dglu_ref[...]
        av, gv = a_ref[...].astype(F32), g_ref[...].astype(F32)
        sg = _sigmoid(gv)
        da = dglu * sg
        dg = dglu * av * sg * (1.0 - sg)
        du_ref[:, 0:d] = da.astype(_LOWP)
        du_ref[:, d:2 * d] = dg.astype(_LOWP)
        db1_ref[:, 0:d] += _colsum(da)
        db1_ref[:, d:2 * d] += _colsum(dg)
        dh = jnp.zeros((tm, d), F32)
        for q in range(nq):
            dh = dh + _dot_nt(du_ref[:, q * hq:(q + 1) * hq], w1_ref[q])
        xh, r, n = _rms(x_ref[...], vec[0:1])
        sums_ref[2:3, :] += _colsum(dh)
        sums_ref[1:2, :] += _colsum(dh * n)
        dn = dh * (1.0 + vec[1:2])
        sums_ref[0:1, :] += _colsum(dn * xh)
        dx_ref[...] = dxo_ref[...] + _rms_bwd(dn, xh, r, vec[0:1])

        @pl.when(i == nsteps - 1)
        def _():
            dwdw_ref[...] = jnp.sum(wacc_ref[...], axis=1)

    return _call(
        body, dcw, dcw, glu, glu, a, g, x, dxo, vec, w8, w1, carry=carry, name="conv_a_bwd", grid=(nsteps,),
        in_specs=[_rows_spec(tm, d),
                  pl.BlockSpec((CONV_HALO, d), lambda i: (jnp.minimum((i + 1) * hb, last_halo), 0)),
                  _rows_spec(tm, d),
                  pl.BlockSpec((CONV_HALO, d), lambda i: (jnp.maximum(i * hb - 1, 0), 0)),
                  _rows_spec(tm, d), _rows_spec(tm, d), _rows_spec(tm, d), _rows_spec(tm, d),
                  _const_spec(vec.shape), _const_spec(w8.shape), _const_spec(w1.shape)],
        out_specs=[_rows_spec(tm, 2 * d), _rows_spec(tm, d), _const_spec((3, d)), _const_spec((1, 2 * d)),
                   _const_spec((CONV_WIDTH, d))],
        out_shape=[SDS((s, 2 * d), _LOWP), SDS((s, d), F32), SDS((3, d), F32), SDS((1, 2 * d), F32),
                   SDS((CONV_WIDTH, d), F32)],
        scratch_shapes=[pltpu.VMEM((tm + CONV_HALO, d), F32), pltpu.VMEM((tm + CONV_HALO, d), F32),
                        pltpu.VMEM((tm, d), F32), pltpu.VMEM((CONV_WIDTH, SUBLANES, d), F32)],
        sem=("arbitrary",))


def _adamw(w, g, m, v, name):
    r, c = w.shape
    rb = r
    for cand in (512, 352, 256, 184, 128, 64, 32, 16, 8):
        if r % cand == 0 and cand * c * 4 <= ADAM_BLOCK_BYTES:
            rb = cand
            break

    def body(w_ref, g_ref, m_ref, v_ref, d_ref, mo_ref, vo_ref):
        gv = g_ref[...]
        mn = ADAM_B1 * m_ref[...] + (1.0 - ADAM_B1) * gv
        vn = ADAM_B2 * v_ref[...] + (1.0 - ADAM_B2) * (gv * gv)
        m_hat = mn / (1.0 - ADAM_B1 ** ADAM_STEP)
        v_hat = vn / (1.0 - ADAM_B2 ** ADAM_STEP)
        d_ref[...] = -ADAM_LR * (m_hat / (jnp.sqrt(v_hat) + ADAM_EPS) + ADAM_WD * w_ref[...])
        mo_ref[...] = mn
        vo_ref[...] = vn

    spec = pl.BlockSpec((rb, c), lambda i: (i, 0))
    return pl.pallas_call(
        body, name=name, grid=(r // rb,), in_specs=[spec] * 4, out_specs=[spec] * 3,
        out_shape=[SDS((r, c), F32)] * 3, compiler_params=_cparams("parallel"),
    )(w, g, m, v)


def _adamw_nd(w, g, m, v, name):
    shape = w.shape
    two_d = (-1, shape[-1])
    outs = _adamw(w.reshape(two_d), g.reshape(two_d), m.reshape(two_d), v.reshape(two_d), name)
    return [o.reshape(shape) for o in outs]


def _adamw_from_slots(w, m, v, segs, pos, name):
    r, c_ = w.shape
    rb = _divisor_block(r, (128, 64, 32, 16))
    for _, _, first, rows in segs:
        while first % rb or rows % rb:
            rb //= 2
    assert rb % 16 == 0, (name, rb)
    plan, start = [], 0
    for _, _, first, rows in segs:
        plan.append((start, rows // rb, first // rb))
        start += rows // rb
    assert start * rb == r, (name, start, rb, r)

    def body(pos_ref, w_ref, m_ref, v_ref, *rest):
        seg_refs, (g_ref, d_ref, mo_ref, vo_ref) = rest[:2 * len(segs)], rest[2 * len(segs):]
        i = pl.program_id(0)
        for k, (s0, nblk, b0) in enumerate(plan):
            @pl.when((i >= s0) & (i < s0 + nblk))
            def _(k=k):
                slots = seg_refs[2 * k][...].astype(F32)
                gv = seg_refs[2 * k + 1][0].astype(F32) + slots[6]
                for t in range(6):
                    gv = gv + slots[t]
                mn = ADAM_B1 * m_ref[...] + (1.0 - ADAM_B1) * gv
                vn = ADAM_B2 * v_ref[...] + (1.0 - ADAM_B2) * (gv * gv)
                m_hat = mn / (1.0 - ADAM_B1 ** ADAM_STEP)
                v_hat = vn / (1.0 - ADAM_B2 ** ADAM_STEP)
                g_ref[...] = gv
                d_ref[...] = -ADAM_LR * (m_hat / (jnp.sqrt(v_hat) + ADAM_EPS) + ADAM_WD * w_ref[...])
                mo_ref[...] = mn
                vo_ref[...] = vn

    spec = pl.BlockSpec((rb, c_), lambda i, pos: (i, 0))
    seg_specs, seg_args = [], []
    for (slots, partials, _, _), (s0, nblk, b0) in zip(segs, plan):
        def in_seg(i, s0=s0, nblk=nblk):
            return jnp.clip(i - s0, 0, nblk - 1)
        seg_specs.append(pl.BlockSpec((slots.shape[0], rb, c_), lambda i, pos, f=in_seg, b0=b0: (0, b0 + f(i), 0)))
        seg_specs.append(pl.BlockSpec((1, rb, c_), lambda i, pos, f=in_seg, b0=b0: (pos[1], b0 + f(i), 0)))
        seg_args += [slots, partials]
    grid_spec = pltpu.PrefetchScalarGridSpec(
        num_scalar_prefetch=1, grid=(r // rb,), in_specs=[spec] * 3 + seg_specs, out_specs=[spec] * 4)
    return pl.pallas_call(
        body, name=name, grid_spec=grid_spec, out_shape=[SDS((r, c_), F32)] * 4,
        compiler_params=_cparams("arbitrary"),
    )(pos, w, m, v, *seg_args)


def _rows_of(*vs):
    return jnp.concatenate([v.reshape(-1, v.shape[-1]) for v in vs], axis=0)


def kernel(x, c, ada_w, ada_b, norm_mix_g, norm_ffn_g, conv_w1, conv_b1, conv_wdw, conv_bdw, conv_ln_g, conv_ln_b, conv_w2, conv_b2, pool_w, pool_ls, ffn_w_gate, ffn_w_up, ffn_w_down, final_g, loss_target, m_ada_w, m_ada_b, m_norm_mix_g, m_norm_ffn_g, m_conv_w1, m_conv_b1, m_conv_wdw, m_conv_bdw, m_conv_ln_g, m_conv_ln_b, m_conv_w2, m_conv_b2, m_pool_w, m_pool_ls, m_ffn_w_gate, m_ffn_w_up, m_ffn_w_down, m_final_g, v_ada_w, v_ada_b, v_norm_mix_g, v_norm_ffn_g, v_conv_w1, v_conv_b1, v_conv_wdw, v_conv_bdw, v_conv_ln_g, v_conv_ln_b, v_conv_w2, v_conv_b2, v_pool_w, v_pool_ls, v_ffn_w_gate, v_ffn_w_up, v_ffn_w_down, v_final_g):
    s, d = x.shape[1], x.shape[2]
    depth = ada_w.shape[0]
    assert depth == 2 and conv_w1.shape[0] == 1 and pool_w.shape[0] == 1, "one conv layer then one pool layer"
    fq = ffn_w_gate.shape[2]
    n_ada = ada_b.shape[1] // d
    ada_cols = ada_w.shape[2]
    ng, gq, gd = pool_w.shape[1], pool_w.shape[2], pool_w.shape[3]
    cq = conv_wdw.shape[2]
    ax, ay, ac = _mesh_pos()
    myq = 2 * ax + ay
    me = 4 * ax + 2 * ay + ac
    x0 = x.reshape(s, d)
    tgt = loss_target.reshape(s, d)

    n_c = d // LANES
    small_in = jnp.concatenate(
        [c.reshape(n_c, LANES), conv_wdw[0].reshape(-1, LANES), pool_ls.reshape(-1, LANES)], axis=0)
    small_in_all = _all_gather8(small_in, "gather_c_and_small_shards").reshape(N_DEV, -1, LANES)
    c_all = small_in_all[:, :n_c].reshape(N_DEV, d)
    ada_b_cols = lax.dynamic_slice_in_dim(ada_b, myq * ada_cols, ada_cols, axis=1).reshape(depth, 1, ada_cols)
    mod_part = _ada_forward(c_all, ada_w, ada_b_cols)
    mod_all = _all_gather8(mod_part.reshape(-1, LANES), "gather_mod").reshape(N_DEV, depth, N_DEV, ada_cols)
    mod_mine = lax.dynamic_index_in_dim(mod_all[0::2], me, axis=2, keepdims=False)
    mod = jnp.transpose(mod_mine, (1, 0, 2)).reshape(depth, n_ada, d)

    def gathered(ex_out, shards):
        return [lax.dynamic_update_slice(g, sh[None], (myq, 0, 0)) for g, sh in zip(ex_out, shards)]

    pos = jnp.stack([ac, myq]).astype(jnp.int32)
    conv_shards = [conv_w1[0].astype(_LOWP), conv_w2[0].astype(_LOWP)]
    ffn0_shards = [ffn_w_gate[0].astype(_LOWP), ffn_w_up[0].astype(_LOWP), ffn_w_down[0].astype(_LOWP)]
    l1_shards = [pool_w[0].reshape(ng * gq, gd).astype(_LOWP), ffn_w_gate[1].astype(_LOWP), ffn_w_up[1].astype(_LOWP),
                 ffn_w_down[1].astype(_LOWP)]
    w1, w2_all = gathered(_Exchange("gather", conv_shards).alone(conv_shards, "gather_conv_weights"), conv_shards)
    w2 = w2_all.reshape(d, d)
    wdw_full_shards = small_in_all[0::2, n_c:].reshape(N_CHIPS, CONV_WIDTH + 1, cq)
    wdw = jnp.transpose(wdw_full_shards[:, :CONV_WIDTH], (1, 0, 2)).reshape(CONV_WIDTH, d)
    pls = wdw_full_shards[:, CONV_WIDTH].reshape(1, d)
    w8 = jnp.broadcast_to(wdw[:, None, :], (CONV_WIDTH, SUBLANES, d))

    vec_ca = _rows_of(norm_mix_g[0], mod[0, 1], mod[0, 0])
    h0, a0, g0, glu = _conv_a_fwd(x0, vec_ca, w1, conv_b1)
    vec_cb = _rows_of(conv_bdw, conv_ln_g, conv_ln_b, conv_b2, mod[0, 2])
    (cw, s0, y0, x1), ffn0_w = _conv_b_fwd(glu, w8, vec_cb, w2, x0, carry=(_Exchange("gather", ffn0_shards), ffn0_shards))
    wg0, wu0, wd0 = gathered(ffn0_w, ffn0_shards)
    vec_f0 = _rows_of(norm_ffn_g[0], mod[0, 4], mod[0, 3], mod[0, 5])
    (h2a, gta, upa, acta, y2a, x2), l1_w = _ffn_fwd(x1, vec_f0, wg0, wu0, wd0, "ffn0_fwd",
                                                   carry=(_Exchange("gather", l1_shards), l1_shards))
    pw_all, wg1, wu1, wd1 = gathered(l1_w, l1_shards)
    pw = jnp.transpose(pw_all.reshape(N_CHIPS, ng, gq, gd), (1, 0, 2, 3)).reshape(ng, N_CHIPS * gq, gd)
    vec_p = _rows_of(norm_mix_g[1], mod[1, 1], mod[1, 0], mod[1, 2], pls)
    mixed, yp, x3 = _pool_fwd(x2, vec_p, pw)
    vec_f1 = _rows_of(norm_ffn_g[1], mod[1, 4], mod[1, 3], mod[1, 5])
    (h2b, gtb, upb, actb, y2b, x4), _ = _ffn_fwd(x3, vec_f1, wg1, wu1, wd1, "ffn1_fwd")
    dx4, fin_sums = _final_fwd_bwd(x4, tgt, final_g.reshape(1, d))
    loss = lax.psum(fin_sums[1, 0], ("x", "y", "c"))

    def ffn_wgrads(h2, dgt, dup, act, dy, tag):
        gate_up = _wgrad(h2, dgt, _shared_rows(d), _quarter_major(fq), N_CHIPS, d, fq, s, "wgrad_gate" + tag, stack_blocks=2)
        gate_up = _wgrad(h2, dup, _shared_rows(d), _quarter_major(fq), N_CHIPS, d, fq, s, "wgrad_up" + tag,
                         stack=gate_up, stack_blocks=2, block=1)
        down = _wgrad(act, dy, _quarter_major(fq), _shared_rows(d), N_CHIPS, fq, d, s, "wgrad_down" + tag)
        return [gate_up, down]

    (dgtb, dupb, dyb, dx3, sums_f1), _ = _ffn_bwd(dx4, x3, vec_f1, gtb, upb, y2b, wg1, wu1, wd1, "ffn1_bwd")
    part_l1 = ffn_wgrads(h2b, dgtb, dupb, actb, dyb, "1")
    dyp, dx2, sums_p = _pool_bwd(dx3, x2, yp, vec_p, pw)
    dpw = _wgrad(mixed, dyp, _column_block(gd), _column_block(gd), ng, gd, gd, s, "wgrad_pool")
    dpw_q = jnp.transpose(dpw.reshape(ng, N_CHIPS, gq, gd), (1, 0, 2, 3)).reshape(N_CHIPS, ng * gq, gd)
    (dgta, dupa, dya, dx1, sums_f0), slots_l1 = _ffn_bwd(dx2, x1, vec_f0, gta, upa, y2a, wg0, wu0, wd0, "ffn0_bwd",
                                                        carry=(_Exchange("reduce", part_l1), part_l1))
    dy0, dcw, sums_cb = _conv_b_bwd(dx1, cw, y0, vec_cb, w2)
    dw2 = _wgrad(s0, dy0, _column_block(d // N_CHIPS), _shared_rows(d), N_CHIPS, d // N_CHIPS, d, s, "wgrad_conv2")
    part_l0 = ffn_wgrads(h2a, dgta, dupa, acta, dya, "0") + [dpw_q, dw2]
    (du, dx0, sums_ca, db1, dwdw), slots_l0 = _conv_a_bwd(dcw, glu, a0, g0, x0, dx1, vec_ca, w8, w1,
                                                          carry=(_Exchange("reduce", part_l0), part_l0))
    dw1 = _wgrad(h0, du, _shared_rows(d), _column_block(2 * d // N_CHIPS), N_CHIPS, d, 2 * d // N_CHIPS, s, "wgrad_conv1")
    part_conv = [dw1]
    slots_conv = _Exchange("reduce", part_conv).alone(part_conv, "grad_exchange_conv")

    dmod = jnp.stack([
        _rows_of(sums_ca[2], sums_ca[1], sums_cb[0], sums_f0[2], sums_f0[1], sums_f0[3]),
        _rows_of(sums_p[2], sums_p[1], sums_p[3], sums_f1[2], sums_f1[1], sums_f1[3])])
    small = _rows_of(dmod.reshape(-1, d), sums_ca[0], sums_p[0], sums_f0[0], sums_f1[0], db1.reshape(2, d),
                     sums_cb[4], sums_cb[2], sums_cb[3], sums_cb[1], fin_sums[0], sums_p[4], dwdw)
    n_small = small.shape[0]
    pad = (-n_small) % SUBLANES
    small = jnp.concatenate([small, jnp.zeros((pad, d), F32)], axis=0) if pad else small
    small_all = _all_gather8(small.reshape(-1, LANES), "gather_small_grads").reshape(N_DEV, -1, LANES)
    small_sum = _sum_devices(small_all).reshape(-1, d)
    n_dm = depth * n_ada
    g_ada_b = small_sum[0:n_dm].reshape(depth, n_ada * d)
    g_norm_mix = small_sum[n_dm:n_dm + 2]
    g_norm_ffn = small_sum[n_dm + 2:n_dm + 4]
    g_b1 = small_sum[n_dm + 4:n_dm + 6].reshape(1, 2 * d)
    g_bdw, g_lng, g_lnb, g_b2 = (small_sum[n_dm + 6 + k].reshape(1, d) for k in range(4))
    g_final = small_sum[n_dm + 10]
    g_pls = lax.dynamic_slice_in_dim(small_sum[n_dm + 11].reshape(1, d), myq * cq, cq, axis=1)
    g_wdw = lax.dynamic_slice_in_dim(small_sum[n_dm + 12:n_dm + 12 + CONV_WIDTH], myq * cq, cq, axis=1).reshape(conv_wdw.shape)
    dmod_all = small_all.reshape(N_DEV, -1, d)[:, 0:n_dm].reshape(N_DEV, depth, n_ada * d)
    dmod_cols = lax.dynamic_slice_in_dim(jnp.transpose(dmod_all, (1, 0, 2)), myq * ada_cols, ada_cols, axis=2)
    g_ada_w = _ada_backward(c_all.T, dmod_cols)

    def small_pack(*vs):
        return jnp.concatenate([v.reshape(-1) for v in vs]).reshape(-1, LANES)

    names = ("ada_b", "norm_mix_g", "norm_ffn_g", "conv_b1", "conv_bdw", "conv_ln_g", "conv_ln_b", "conv_b2", "final_g")
    small_w = (ada_b, norm_mix_g, norm_ffn_g, conv_b1, conv_bdw, conv_ln_g, conv_ln_b, conv_b2, final_g)
    small_g = (g_ada_b, g_norm_mix, g_norm_ffn, g_b1, g_bdw, g_lng, g_lnb, g_b2, g_final)
    small_m = (m_ada_b, m_norm_mix_g, m_norm_ffn_g, m_conv_b1, m_conv_bdw, m_conv_ln_g, m_conv_ln_b, m_conv_b2, m_final_g)
    small_v = (v_ada_b, v_norm_mix_g, v_norm_ffn_g, v_conv_b1, v_conv_bdw, v_conv_ln_g, v_conv_ln_b, v_conv_b2, v_final_g)
    sd, sm, sv = _adamw(small_pack(*small_w), small_pack(*small_g), small_pack(*small_m), small_pack(*small_v), "adamw_small")

    def unpack(flat2d):
        flat = flat2d.reshape(-1)
        out, off = {}, 0
        for nm, wv in zip(names, small_w):
            out[nm] = flat[off:off + wv.size].reshape(wv.shape)
            off += wv.size
        return out

    sd, sm, sv = unpack(sd), unpack(sm), unpack(sv)
    grads = {
        "ada_w": g_ada_w, "ada_b": g_ada_b, "norm_mix_g": g_norm_mix, "norm_ffn_g": g_norm_ffn,
        "conv_b1": g_b1, "conv_wdw": g_wdw, "conv_bdw": g_bdw, "conv_ln_g": g_lng,
        "conv_ln_b": g_lnb, "conv_b2": g_b2, "pool_ls": g_pls, "final_g": g_final,
    }
    plain = {
        "ada_w": (ada_w, m_ada_w, v_ada_w), "conv_wdw": (conv_wdw, m_conv_wdw, v_conv_wdw),
        "pool_ls": (pool_ls, m_pool_ls, v_pool_ls),
    }
    from_slots = {
        "conv_w1": (conv_w1, m_conv_w1, v_conv_w1, [(slots_conv[0], part_conv[0], 0, d)]),
        "conv_w2": (conv_w2, m_conv_w2, v_conv_w2, [(slots_l0[3], part_l0[3], 0, d // N_CHIPS)]),
        "pool_w": (pool_w, m_pool_w, v_pool_w, [(slots_l0[2], part_l0[2], 0, ng * gq)]),
        "ffn_w_gate": (ffn_w_gate, m_ffn_w_gate, v_ffn_w_gate,
                       [(slots_l0[0], part_l0[0], 0, d), (slots_l1[0], part_l1[0], 0, d)]),
        "ffn_w_up": (ffn_w_up, m_ffn_w_up, v_ffn_w_up,
                     [(slots_l0[0], part_l0[0], d, d), (slots_l1[0], part_l1[0], d, d)]),
        "ffn_w_down": (ffn_w_down, m_ffn_w_down, v_ffn_w_down,
                       [(slots_l0[1], part_l0[1], 0, fq), (slots_l1[1], part_l1[1], 0, fq)]),
    }
    order = ("ada_w", "ada_b", "norm_mix_g", "norm_ffn_g", "conv_w1", "conv_b1", "conv_wdw", "conv_bdw", "conv_ln_g",
             "conv_ln_b", "conv_w2", "conv_b2", "pool_w", "pool_ls", "ffn_w_gate", "ffn_w_up", "ffn_w_down", "final_g")
    delta, new_m, new_v = {}, {}, {}
    for nm in order:
        if nm in from_slots:
            wv, mv, vv, segs = from_slots[nm]
            two_d = (-1, wv.shape[-1])
            outs = _adamw_from_slots(wv.reshape(two_d), mv.reshape(two_d), vv.reshape(two_d), segs, pos, "adamw_" + nm)
            grads[nm], delta[nm], new_m[nm], new_v[nm] = (o.reshape(wv.shape) for o in outs)
        elif nm in plain:
            wv, mv, vv = plain[nm]
            grads[nm] = grads[nm].reshape(wv.shape)
            delta[nm], new_m[nm], new_v[nm] = _adamw_nd(wv, grads[nm], mv, vv, "adamw_" + nm)
        else:
            delta[nm], new_m[nm], new_v[nm] = sd[nm], sm[nm], sv[nm]
            grads[nm] = grads[nm].reshape(dict(zip(names, small_w))[nm].shape)

    return (loss, dx0.reshape(x.shape), *[grads[n] for n in order], *[delta[n] for n in order],
            *[new_m[n] for n in order], *[new_v[n] for n in order])
```

```python
import functools

import jax
import jax.numpy as jnp
from jax import lax
from jax.experimental import pallas as pl
from jax.experimental.pallas import tpu as pltpu

F32 = jnp.float32
_LOWP = jnp.bfloat16
EPS = 1e-6
CONV_WIDTH = 31
POOL_WINDOWS = (2, 4, 8, 16)
ADAM_LR = 0.001
ADAM_B1 = 0.9
ADAM_B2 = 0.999
ADAM_EPS = 1e-08
ADAM_WD = 0.01
ADAM_STEP = 10

N_CHIPS = 4
N_DEV = 8
LANES = 128
SUBLANES = 8
CONV_HALO = 32
POOL_HALO = 16
TILE_ROWS = 64
VMEM_LIMIT = 56 * 1024 * 1024
ADAM_BLOCK_BYTES = 1 << 20
MESH = pl.DeviceIdType.MESH
SDS = jax.ShapeDtypeStruct


def _cparams(*sem):
    return pltpu.CompilerParams(dimension_semantics=sem if sem else None, vmem_limit_bytes=VMEM_LIMIT)


def _row_block(s):
    for tm in (512, 256, 128, 64, 32, 16):
        if s % tm == 0:
            return tm
    raise ValueError(f"sequence length {s} must be a multiple of 16")


def _rows_spec(tm, ncols):
    return pl.BlockSpec((tm, ncols), lambda i, *_: (i, 0))


def _const_spec(shape):
    nd = len(shape)
    return pl.BlockSpec(shape, lambda *_: (0,) * nd)


def _dot(a, b):
    return lax.dot_general(a.astype(_LOWP), b.astype(_LOWP), (((1,), (0,)), ((), ())), preferred_element_type=F32)


def _dot_nt(a, b):
    return lax.dot_general(a.astype(_LOWP), b.astype(_LOWP), (((1,), (1,)), ((), ())), preferred_element_type=F32)


def _dot_tn(a, b):
    return lax.dot_general(a.astype(_LOWP), b.astype(_LOWP), (((0,), (0,)), ((), ())), preferred_element_type=F32)


def _rms(x, gamma):
    r = lax.rsqrt(jnp.mean(x * x, axis=-1, keepdims=True) + EPS)
    xh = x * r
    return xh, r, xh * gamma


def _rms_bwd(dn, xh, r, gamma):
    dxh = dn * gamma
    return r * (dxh - xh * jnp.mean(dxh * xh, axis=-1, keepdims=True))


def _colsum(v):
    return jnp.sum(v, axis=0, keepdims=True)


def _sigmoid(v):
    return jax.nn.sigmoid(v)


def _dsilu(v):
    sg = _sigmoid(v)
    return sg * (1.0 + v * (1.0 - sg))


def _for_tiles(nrows, col0, ncols, fn):
    def step(t, carry):
        r0 = pl.multiple_of(t * TILE_ROWS, TILE_ROWS)
        for col in range(col0, col0 + ncols, LANES):
            fn(r0, col)
        return carry

    lax.fori_loop(0, nrows // TILE_ROWS, step, 0)


def _shift_up(win, b):
    return pltpu.roll(win, win.shape[0] - b, 0) if b else win


def _shift_down(win, b):
    return pltpu.roll(win, b, 0) if b else win


def _tiles3(v):
    return v.reshape(v.shape[0] // SUBLANES, SUBLANES, v.shape[1])


def _resident_scratch(arrays):
    return [pltpu.VMEM(a.shape, a.dtype) for a in arrays] + [pltpu.SemaphoreType.DMA((len(arrays),))]


def _load_resident(hbm_refs, vmem_refs, sems):
    copies = [pltpu.make_async_copy(src, dst, sems.at[k]) for k, (src, dst) in enumerate(zip(hbm_refs, vmem_refs))]
    for cp in copies:
        cp.start()
    for cp in copies:
        cp.wait()


def _mesh_pos():
    return lax.axis_index("x"), lax.axis_index("y"), lax.axis_index("c")


def _other_chips(x, y):
    return [(1 - x, y), (x, 1 - y), (1 - x, 1 - y)]


def _all_gather8(v, name):
    m, n = v.shape

    def body(x_ref, out_ref, send_sems, recv_sems, local_sem):
        x, y, c = _mesh_pos()
        me, sibling = (x, y, c), (x, y, 1 - c)
        chips = _other_chips(x, y)

        def rows(px, py, pc):
            return out_ref.at[pl.ds((4 * px + 2 * py + pc) * m, m), :]

        def copy(k, block, to, src=None):
            return pltpu.make_async_remote_copy(
                src_ref=rows(*block) if src is None else src, dst_ref=rows(*block),
                send_sem=send_sems.at[k], recv_sem=recv_sems.at[k], device_id=to, device_id_type=MESH)

        mine = pltpu.make_async_copy(x_ref, rows(*me), local_sem)
        mine.start()
        first = [copy(0, me, sibling, src=x_ref)]
        first += [copy(1 + j, me, (*chip, c), src=x_ref) for j, chip in enumerate(chips)]
        for cp in first:
            cp.start()
        passed = [copy(4 + j, (*chip, c), sibling) for j, chip in enumerate(chips)]
        for j, chip in enumerate(chips):
            copy(1 + j, (*chip, c), me).wait_recv()
            passed[j].start()
        copy(0, sibling, me).wait_recv()
        for j, chip in enumerate(chips):
            copy(4 + j, (*chip, 1 - c), me).wait_recv()
        for cp in first + passed:
            cp.wait_send()
        mine.wait()

    return pl.pallas_call(
        body, name=name,
        out_shape=SDS((N_DEV * m, n), v.dtype),
        in_specs=[pl.BlockSpec(memory_space=pltpu.VMEM)],
        out_specs=pl.BlockSpec(memory_space=pltpu.VMEM),
        scratch_shapes=[pltpu.SemaphoreType.DMA((7,)), pltpu.SemaphoreType.DMA((7,)), pltpu.SemaphoreType.DMA],
    )(v)


class _Exchange:
    N_SEMS = 7

    def __init__(self, kind, arrays):
        self.kind, self.n = kind, len(arrays)
        self.streams = 3
        if kind == "gather":
            self.out_shape = [SDS((N_CHIPS,) + a.shape, a.dtype) for a in arrays]
        else:
            self.out_shape = [SDS((N_CHIPS, 2 * a.shape[1], a.shape[2]), a.dtype) for a in arrays]
        self.scratch = [pltpu.SemaphoreType.DMA((self.n, self.N_SEMS)), pltpu.SemaphoreType.DMA((self.n, self.N_SEMS))]

    def _ctx(self, ins, outs, send_sems, recv_sems):
        x, y, c = _mesh_pos()
        myq = 2 * x + y
        sibling = (x, y, 1 - c)
        chips = _other_chips(x, y)
        gather = self.kind == "gather"
        ns = self.streams

        def rows(a, h):
            hr = outs[a].shape[1] // 2
            return pl.ds(h * hr, hr)

        def copy(a, ks, kr, src, dst, to):
            return pltpu.make_async_remote_copy(src_ref=src, dst_ref=dst, send_sem=send_sems.at[a, ks],
                                                recv_sem=recv_sems.at[a, kr], device_id=to, device_id_type=MESH)

        def place(a, t, h):
            if gather:
                px, py = chips[t]
                return outs[a].at[2 * px + py, rows(a, h)]
            return outs[a].at[t, rows(a, h)]

        def first(a, t):
            if gather:
                return copy(a, t, t, ins[a].at[rows(a, c)], outs[a].at[myq, rows(a, c)], (*chips[t], c))
            px, py = chips[t]
            return copy(a, t, t, ins[a].at[2 * px + py], outs[a].at[t, rows(a, c)], (px, py, c))

        def landed(a, t):
            return copy(a, t, t, place(a, t, c), place(a, t, c), sibling)

        def relay(a, t):
            return copy(a, ns + t, ns + t, place(a, t, c), place(a, t, c), sibling)

        def relayed(a, t):
            return copy(a, ns + t, ns + t, place(a, t, 1 - c), place(a, t, 1 - c), sibling)

        def own(a):
            return copy(a, 2 * ns, 2 * ns, ins[a].at[myq], outs[a].at[ns, rows(a, c)], sibling)

        return first, landed, relay, relayed, own

    def start(self, ins, outs, send_sems, recv_sems):
        first, _, _, _, own = self._ctx(ins, outs, send_sems, recv_sems)
        for a in range(self.n):
            for t in range(self.streams):
                first(a, t).start()
            if self.kind == "reduce":
                own(a).start()

    def relay(self, ins, outs, send_sems, recv_sems):
        _, landed, relay, _, _ = self._ctx(ins, outs, send_sems, recv_sems)
        for a in range(self.n):
            for t in range(self.streams):
                landed(a, t).wait_recv()
                relay(a, t).start()

    def finish(self, ins, outs, send_sems, recv_sems):
        first, _, relay, relayed, own = self._ctx(ins, outs, send_sems, recv_sems)
        for a in range(self.n):
            for t in range(self.streams):
                relayed(a, t).wait_recv()
            if self.kind == "reduce":
                own(a).wait_recv()
        for a in range(self.n):
            for t in range(self.streams):
                first(a, t).wait_send()
                relay(a, t).wait_send()
            if self.kind == "reduce":
                own(a).wait_send()

    def alone(self, arrays, name):
        n = self.n

        def body(*refs):
            args = (refs[:n], refs[n:2 * n], refs[2 * n], refs[2 * n + 1])
            self.start(*args)
            self.relay(*args)
            self.finish(*args)

        any_spec = pl.BlockSpec(memory_space=pl.ANY)
        return pl.pallas_call(body, name=name, out_shape=self.out_shape, in_specs=[any_spec] * n,
                              out_specs=[any_spec] * n, scratch_shapes=self.scratch)(*arrays)


RELAY_AT = (5, 8)


def _call(body, *args, carry=None, prefetch=None, name, grid, in_specs, out_specs, out_shape, scratch_shapes=(), sem):
    def run(fn, in_specs_, out_specs_, out_shape_, scratch_, sem_, operands):
        if prefetch is None:
            return pl.pallas_call(fn, name=name, grid=grid, in_specs=in_specs_, out_specs=out_specs_, out_shape=out_shape_,
                                  scratch_shapes=scratch_, compiler_params=_cparams(*sem_))(*operands)
        grid_spec = pltpu.PrefetchScalarGridSpec(num_scalar_prefetch=1, grid=grid, in_specs=in_specs_,
                                                 out_specs=out_specs_, scratch_shapes=scratch_)
        return pl.pallas_call(fn, name=name, grid_spec=grid_spec, out_shape=out_shape_,
                              compiler_params=_cparams(*sem_))(prefetch, *operands)

    if carry is None:
        return run(body, list(in_specs), list(out_specs), list(out_shape), list(scratch_shapes), sem, args), None
    ex, arrays = carry
    n_in, n_out, n_scr, n = len(in_specs), len(out_specs), len(scratch_shapes), ex.n
    n_pre = 0 if prefetch is None else 1
    total = 1
    for g in grid:
        total *= g
    relay_step = (total * RELAY_AT[0]) // RELAY_AT[1]

    def hosted(*refs):
        pre, refs = refs[:n_pre], refs[n_pre:]
        ins, cins = refs[:n_in], refs[n_in:n_in + n]
        outs, couts = refs[n_in + n:n_in + n + n_out], refs[n_in + n + n_out:n_in + 2 * n + n_out]
        scr = refs[n_in + 2 * n + n_out:n_in + 2 * n + n_out + n_scr]
        sems = refs[n_in + 2 * n + n_out + n_scr:]
        step = 0
        for ax, g in enumerate(grid):
            step = step * g + pl.program_id(ax)

        @pl.when(step == 0)
        def _():
            ex.start(cins, couts, *sems)

        body(*pre, *ins, *outs, *scr)

        @pl.when(step == relay_step)
        def _():
            ex.relay(cins, couts, *sems)

        @pl.when(step == total - 1)
        def _():
            ex.finish(cins, couts, *sems)

    any_spec = pl.BlockSpec(memory_space=pl.ANY)
    res = run(hosted, list(in_specs) + [any_spec] * n, list(out_specs) + [any_spec] * n, list(out_shape) + ex.out_shape,
              list(scratch_shapes) + ex.scratch, ("arbitrary",) * len(grid), (*args, *arrays))
    return res[:n_out], res[n_out:]


class _SiblingSwap:
    def __init__(self, arrays):
        self.n = len(arrays)
        self.out_shape = [SDS((a.shape[0], a.shape[1] // 2, a.shape[2]), a.dtype) for a in arrays]
        self.scratch = [pltpu.SemaphoreType.DMA((self.n,)), pltpu.SemaphoreType.DMA((self.n,))]

    def _copies(self, ins, outs, send_sems, recv_sems):
        x, y, c = _mesh_pos()
        cps = []
        for a in range(self.n):
            hr = ins[a].shape[1] // 2
            cps.append(pltpu.make_async_remote_copy(
                src_ref=ins[a].at[:, pl.ds((1 - c) * hr, hr)], dst_ref=outs[a],
                send_sem=send_sems.at[a], recv_sem=recv_sems.at[a], device_id=(x, y, 1 - c), device_id_type=MESH))
        return cps

    def start(self, *refs):
        for cp in self._copies(*refs):
            cp.start()

    def relay(self, *refs):
        pass

    def finish(self, *refs):
        for cp in self._copies(*refs):
            cp.wait()

    def alone(self, arrays, name):
        return _Exchange.alone(self, arrays, name)


def _divisor_block(rows, cands):
    for rb in cands:
        if rows % rb == 0:
            return rb
    raise ValueError(rows)


def _add_sibling_half(p, rb1, pos, name):
    nq, r, n = p.shape
    h = r // 2
    rb = _divisor_block(h, (512, 352, 256, 176, 128, 64, 32, 16))
    nb = h // rb

    def body(pos_ref, p_ref, r_ref, o_ref):
        o_ref[...] = (p_ref[...].astype(F32) + r_ref[...].astype(F32)).astype(o_ref.dtype)

    grid_spec = pltpu.PrefetchScalarGridSpec(
        num_scalar_prefetch=1, grid=(nq, nb),
        in_specs=[pl.BlockSpec((1, rb, n), lambda q, i, pos: (q, pos[0] * nb + i, 0)),
                  pl.BlockSpec((1, rb, n), lambda q, i, pos: (q, i, 0))],
        out_specs=pl.BlockSpec((1, rb, n), lambda q, i, pos: (q, i, 0)))
    return pl.pallas_call(
        body, name=name, grid_spec=grid_spec, out_shape=SDS((nq, h, n), p.dtype),
        compiler_params=_cparams("parallel", "parallel"),
    )(pos, p, rb1)


def _sum_devices(g):
    nd, m, n = g.shape

    def body(g_ref, o_ref):
        acc = g_ref[0]
        for d in range(1, nd):
            acc = acc + g_ref[d]
        o_ref[...] = acc

    return pl.pallas_call(body, name="small_grad_sum", out_shape=SDS((m, n), F32))(g)


def _ada_forward(c_all, ada_w, ada_b_cols):
    nl, d, ncol = ada_w.shape
    nb = c_all.shape[0]
    tn = 512 if ncol % 512 == 0 else ncol

    def body(c_ref, w_ref, b_ref, o_ref):
        cv = c_ref[...]
        ca = cv * _sigmoid(cv)
        o_ref[0] = _dot(ca, w_ref[0]) + b_ref[0]

    return pl.pallas_call(
        body, name="ada_forward", grid=(nl, ncol // tn),
        in_specs=[pl.BlockSpec((nb, d), lambda l, j: (0, 0)),
                  pl.BlockSpec((1, d, tn), lambda l, j: (l, 0, j)),
                  pl.BlockSpec((1, 1, tn), lambda l, j: (l, 0, j))],
        out_specs=pl.BlockSpec((1, nb, tn), lambda l, j: (l, 0, j)),
        out_shape=SDS((nl, nb, ncol), F32), compiler_params=_cparams("parallel", "parallel"),
    )(c_all, ada_w, ada_b_cols)


def _ada_backward(c_all_t, dmod_cols):
    d, nb = c_all_t.shape
    nl, _, ncol = dmod_cols.shape
    tn = 512 if ncol % 512 == 0 else ncol

    def body(c_ref, g_ref, o_ref):
        cv = c_ref[...]
        ca = cv * _sigmoid(cv)
        o_ref[0] = _dot(ca, g_ref[0])

    return pl.pallas_call(
        body, name="ada_backward", grid=(nl, ncol // tn),
        in_specs=[pl.BlockSpec((d, nb), lambda l, j: (0, 0)),
                  pl.BlockSpec((1, nb, tn), lambda l, j: (l, 0, j))],
        out_specs=pl.BlockSpec((1, d, tn), lambda l, j: (l, 0, j)),
        out_shape=SDS((nl, d, ncol), F32), compiler_params=_cparams("parallel", "parallel"),
    )(c_all_t, dmod_cols)


def _conv_a_fwd(x, vec, w1, b1):
    s, d = x.shape
    nq, _, hq = w1.shape
    tm = _row_block(s)

    def body(x_ref, vec_ref, w1_ref, b1_ref, h_ref, a_ref, g_ref, glu_ref):
        vec = vec_ref[...]
        _, _, n = _rms(x_ref[...], vec[0:1])
        hb = (n * (1.0 + vec[1:2]) + vec[2:3]).astype(_LOWP)
        h_ref[...] = hb
        u = [_dot(hb, w1_ref[q]) + b1_ref[:, q * hq:(q + 1) * hq] for q in range(nq)]
        for k in range(nq // 2):
            av, gv = u[k], u[nq // 2 + k]
            cols = slice(k * hq, (k + 1) * hq)
            a_ref[:, cols] = av.astype(_LOWP)
            g_ref[:, cols] = gv.astype(_LOWP)
            glu_ref[:, cols] = av * _sigmoid(gv)

    return pl.pallas_call(
        body, name="conv_a_fwd", grid=(s // tm,),
        in_specs=[_rows_spec(tm, d), _const_spec(vec.shape), _const_spec(w1.shape), _const_spec(b1.shape)],
        out_specs=[_rows_spec(tm, d)] * 4,
        out_shape=[SDS((s, d), _LOWP)] * 3 + [SDS((s, d), F32)],
        compiler_params=_cparams("parallel"),
    )(x, vec, w1, b1)


def _conv_b_fwd(glu, w8, vec, w2, x, carry=None):
    s, d = x.shape
    tm = _row_block(s)
    hb = tm // CONV_HALO

    def body(glu_ref, halo_ref, w8_ref, vec_ref, w2_ref, x_ref, cw_ref, s_ref, y_ref, xo_ref, ext_ref):
        i = pl.program_id(0)
        vec = vec_ref[...]
        ext_ref[0:CONV_HALO, :] = jnp.where(i > 0, halo_ref[...], 0.0)
        ext_ref[CONV_HALO:, :] = glu_ref[...]

        def tile(r0, c0):
            lanes = slice(c0, c0 + LANES)
            win = ext_ref[pl.ds(r0, TILE_ROWS + CONV_HALO), lanes]
            acc = jnp.zeros((TILE_ROWS // SUBLANES, SUBLANES, LANES), F32)
            for b in range(SUBLANES):
                sh = _shift_up(win, b)
                for a in range(CONV_HALO // SUBLANES + 1):
                    k = SUBLANES * a + b - (CONV_HALO - CONV_WIDTH + 1)
                    if 0 <= k < CONV_WIDTH:
                        acc = acc + w8_ref[k, :, lanes][None] * _tiles3(sh[SUBLANES * a:SUBLANES * a + TILE_ROWS])
            cw_ref[pl.ds(r0, TILE_ROWS), lanes] = acc.reshape(TILE_ROWS, LANES)

        _for_tiles(tm, 0, d, tile)
        cw = cw_ref[...] + vec[0:1]
        cw_ref[...] = cw
        cc = cw - jnp.mean(cw, axis=-1, keepdims=True)
        ch = cc * lax.rsqrt(jnp.mean(cc * cc, axis=-1, keepdims=True) + EPS)
        lo = ch * vec[1:2] + vec[2:3]
        sv = (lo * _sigmoid(lo)).astype(_LOWP)
        s_ref[...] = sv
        y = _dot(sv, w2_ref[...]) + vec[3:4]
        y_ref[...] = y.astype(_LOWP)
        xo_ref[...] = x_ref[...] + (1.0 + vec[4:5]) * y

    return _call(
        body, glu, glu, w8, vec, w2, x, carry=carry, name="conv_b_fwd", grid=(s // tm,),
        in_specs=[_rows_spec(tm, d),
                  pl.BlockSpec((CONV_HALO, d), lambda i: (jnp.maximum(i * hb - 1, 0), 0)),
                  _const_spec(w8.shape), _const_spec(vec.shape), _const_spec(w2.shape), _rows_spec(tm, d)],
        out_specs=[_rows_spec(tm, d)] * 4,
        out_shape=[SDS((s, d), F32), SDS((s, d), _LOWP), SDS((s, d), _LOWP), SDS((s, d), F32)],
        scratch_shapes=[pltpu.VMEM((tm + CONV_HALO, d), F32)], sem=("parallel",))


def _ffn_fwd(x, vec, wg, wu, wd, name, carry=None):
    s, d = x.shape
    nq, _, fq = wg.shape
    tm = _row_block(s)

    def body(x_ref, vec_ref, wg_hbm, wu_hbm, wd_hbm, h_ref, gt_ref, up_ref, act_ref, y_ref, xo_ref,
             hb_ref, yacc_ref, wg_ref, wu_ref, wd_ref, w_sems):
        i, q = pl.program_id(0), pl.program_id(1)
        vec = vec_ref[...]

        @pl.when((i == 0) & (q == 0))
        def _():
            _load_resident((wg_hbm, wu_hbm, wd_hbm), (wg_ref, wu_ref, wd_ref), w_sems)

        @pl.when(q == 0)
        def _():
            _, _, n = _rms(x_ref[...], vec[0:1])
            hb = (n * (1.0 + vec[1:2]) + vec[2:3]).astype(_LOWP)
            hb_ref[...] = hb
            h_ref[...] = hb
            yacc_ref[...] = jnp.zeros_like(yacc_ref)

        hb = hb_ref[...]
        gt = _dot(hb, wg_ref[q])
        up = _dot(hb, wu_ref[q])
        act = (gt * _sigmoid(gt) * up).astype(_LOWP)
        gt_ref[0] = gt.astype(_LOWP)
        up_ref[0] = up.astype(_LOWP)
        act_ref[0] = act
        yacc_ref[...] += _dot(act, wd_ref[q])

        @pl.when(q == nq - 1)
        def _():
            y = yacc_ref[...]
            y_ref[...] = y.astype(_LOWP)
            xo_ref[...] = x_ref[...] + (1.0 + vec[3:4]) * y

    rows = pl.BlockSpec((tm, d), lambda i, q: (i, 0))
    hid = pl.BlockSpec((1, tm, fq), lambda i, q: (q, i, 0))
    any_spec = pl.BlockSpec(memory_space=pl.ANY)
    return _call(
        body, x, vec, wg, wu, wd, carry=carry, name=name, grid=(s // tm, nq),
        in_specs=[rows, pl.BlockSpec(vec.shape, lambda i, q: (0, 0)), any_spec, any_spec, any_spec],
        out_specs=[rows, hid, hid, hid, rows, rows],
        out_shape=[SDS((s, d), _LOWP)] + [SDS((nq, s, fq), _LOWP)] * 3 + [SDS((s, d), _LOWP), SDS((s, d), F32)],
        scratch_shapes=[pltpu.VMEM((tm, d), _LOWP), pltpu.VMEM((tm, d), F32)] + _resident_scratch((wg, wu, wd)),
        sem=("arbitrary", "arbitrary"))


def _pool_inv_count(t0, nrows, w):
    t = t0 + lax.broadcasted_iota(jnp.int32, (nrows, LANES), 0)
    return 1.0 / jnp.minimum(t + 1, w).astype(F32)


def _pool_fwd(x, vec, pw):
    s, d = x.shape
    ng, gd, _ = pw.shape
    tm = _row_block(s)
    hb = tm // POOL_HALO

    def body(x_ref, halo_ref, vec_ref, pw_ref, mixed_ref, yp_ref, xo_ref, hext_ref):
        i = pl.program_id(0)
        vec = vec_ref[...]

        def modulated(xv):
            _, _, n = _rms(xv, vec[0:1])
            return n * (1.0 + vec[1:2]) + vec[2:3]

        hext_ref[0:POOL_HALO, :] = jnp.where(i > 0, modulated(halo_ref[...]), 0.0)
        hext_ref[POOL_HALO:, :] = modulated(x_ref[...])

        for g in range(ng):
            def tile(r0, c0, g=g):
                lanes = slice(c0, c0 + LANES)
                win = hext_ref[pl.ds(r0, TILE_ROWS + POOL_HALO), lanes]
                acc = win
                for step in range(g + 1):
                    acc = acc + _shift_down(acc, 2 ** step)
                inv = _pool_inv_count(i * tm + r0, TILE_ROWS, POOL_WINDOWS[g])
                mixed = acc[POOL_HALO:] * inv - win[POOL_HALO:]
                mixed_ref[pl.ds(r0, TILE_ROWS), lanes] = mixed.astype(_LOWP)

            _for_tiles(tm, g * gd, gd, tile)

        for g in range(ng):
            cols = slice(g * gd, (g + 1) * gd)
            yp = _dot(mixed_ref[:, cols], pw_ref[g])
            yp_ref[:, cols] = yp.astype(_LOWP)
            xo_ref[:, cols] = x_ref[:, cols] + (1.0 + vec[3:4, cols]) * (yp * vec[4:5, cols])

    return pl.pallas_call(
        body, name="pool_fwd", grid=(s // tm,),
        in_specs=[_rows_spec(tm, d),
                  pl.BlockSpec((POOL_HALO, d), lambda i: (jnp.maximum(i * hb - 1, 0), 0)),
                  _const_spec(vec.shape), _const_spec(pw.shape)],
        out_specs=[_rows_spec(tm, d)] * 3,
        out_shape=[SDS((s, d), _LOWP), SDS((s, d), _LOWP), SDS((s, d), F32)],
        scratch_shapes=[pltpu.VMEM((tm + POOL_HALO, d), F32)],
        compiler_params=_cparams("parallel"),
    )(x, x, vec, pw)


def _final_fwd_bwd(x, tgt, gamma):
    s, d = x.shape
    tm = _row_block(s)
    nsteps = s // tm

    def body(x_ref, t_ref, g_ref, dx_ref, sums_ref):
        i = pl.program_id(0)
        gamma_v = g_ref[...]
        xh, r, out = _rms(x_ref[...], gamma_v)
        err = out - t_ref[...]
        dout = err * (1.0 / d)

        @pl.when(i == 0)
        def _():
            sums_ref[...] = jnp.zeros_like(sums_ref)

        sums_ref[0:1, :] += _colsum(dout * xh)
        sums_ref[1:2, :] += _colsum(0.5 * err * dout)
        dx_ref[...] = _rms_bwd(dout, xh, r, gamma_v)

        @pl.when(i == nsteps - 1)
        def _():
            sums_ref[1:2, :] = jnp.broadcast_to(jnp.sum(sums_ref[1:2, :], axis=-1, keepdims=True), (1, d))

    return pl.pallas_call(
        body, name="final_loss", grid=(nsteps,),
        in_specs=[_rows_spec(tm, d), _rows_spec(tm, d), _const_spec(gamma.shape)],
        out_specs=[_rows_spec(tm, d), _const_spec((2, d))],
        out_shape=[SDS((s, d), F32), SDS((2, d), F32)],
        compiler_params=_cparams("arbitrary"),
    )(x, tgt, gamma)


def _ffn_bwd(dxo, x, vec, gt, up, y, wg, wu, wd, name, carry=None):
    s, d = x.shape
    nq, _, fq = wg.shape
    tm = _row_block(s)

    def body(dxo_ref, x_ref, vec_ref, gt_ref, up_ref, y_ref, wg_hbm, wu_hbm, wd_hbm,
             dgt_ref, dup_ref, dy_ref, dx_ref, sums_ref, dyb_ref, dh_ref, wg_ref, wu_ref, wd_ref, w_sems):
        i, q = pl.program_id(0), pl.program_id(1)
        vec = vec_ref[...]

        @pl.when((i == 0) & (q == 0))
        def _():
            _load_resident((wg_hbm, wu_hbm, wd_hbm), (wg_ref, wu_ref, wd_ref), w_sems)
            sums_ref[...] = jnp.zeros_like(sums_ref)

        @pl.when(q == 0)
        def _():
            dxv = dxo_ref[...]
            dyb = (dxv * (1.0 + vec[3:4])).astype(_LOWP)
            dyb_ref[...] = dyb
            dy_ref[...] = dyb
            dh_ref[...] = jnp.zeros_like(dh_ref)
            sums_ref[3:4, :] += _colsum(dxv * y_ref[...].astype(F32))

        halves = [slice(k * (tm // 2), (k + 1) * (tm // 2)) for k in range(2)]
        dacts = [_dot_nt(dyb_ref[rs, :], wd_ref[q]) for rs in halves]
        dgus = []
        for rs, dact in zip(halves, dacts):
            gtv = gt_ref[0, rs, :].astype(F32)
            sg = _sigmoid(gtv)
            dgt = (dact * up_ref[0, rs, :].astype(F32) * (sg * (1.0 + gtv * (1.0 - sg)))).astype(_LOWP)
            dup = (dact * (gtv * sg)).astype(_LOWP)
            dgt_ref[0, rs, :] = dgt
            dup_ref[0, rs, :] = dup
            dgus.append((dgt, dup))
        for rs, (dgt, dup) in zip(halves, dgus):
            dh_ref[rs, :] += _dot_nt(dgt, wg_ref[q]) + _dot_nt(dup, wu_ref[q])

        @pl.when(q == nq - 1)
        def _():
            dh = dh_ref[...]
            xh, r, n = _rms(x_ref[...], vec[0:1])
            sums_ref[2:3, :] += _colsum(dh)
            sums_ref[1:2, :] += _colsum(dh * n)
            dn = dh * (1.0 + vec[1:2])
            sums_ref[0:1, :] += _colsum(dn * xh)
            dx_ref[...] = dxo_ref[...] + _rms_bwd(dn, xh, r, vec[0:1])

    rows = pl.BlockSpec((tm, d), lambda i, q: (i, 0))
    hid = pl.BlockSpec((1, tm, fq), lambda i, q: (q, i, 0))
    return _call(
        body, dxo, x, vec, gt, up, y, wg, wu, wd, carry=carry, name=name, grid=(s // tm, nq),
        in_specs=[rows, rows, pl.BlockSpec(vec.shape, lambda i, q: (0, 0)), hid, hid, rows] +
                 [pl.BlockSpec(memory_space=pl.ANY)] * 3,
        out_specs=[hid, hid, rows, rows, pl.BlockSpec((4, d), lambda i, q: (0, 0))],
        out_shape=[SDS((nq, s, fq), _LOWP)] * 2 + [SDS((s, d), _LOWP), SDS((s, d), F32), SDS((4, d), F32)],
        scratch_shapes=[pltpu.VMEM((tm, d), _LOWP), pltpu.VMEM((tm, d), F32)] + _resident_scratch((wg, wu, wd)),
        sem=("arbitrary", "arbitrary"))


def _wgrad(a, b, a_spec, b_spec, nq, ka, nb, s, name, stack=None, stack_blocks=1, block=0):
    tk = min(2048, s)
    nk = s // tk

    def body(a_ref, b_ref, *rest):
        o_ref, acc_ref = rest[-2:]
        k = pl.program_id(1)

        @pl.when(k == 0)
        def _():
            acc_ref[...] = jnp.zeros_like(acc_ref)

        av = a_ref[...].reshape(tk, ka)
        bv = b_ref[...].reshape(tk, nb)
        acc_ref[...] += _dot_tn(av, bv)

        @pl.when(k == nk - 1)
        def _():
            o_ref[0] = acc_ref[...].astype(o_ref.dtype)

    extra = [] if stack is None else [stack]
    return pl.pallas_call(
        body, name=name, grid=(nq, nk),
        in_specs=[a_spec(tk), b_spec(tk)] + [pl.BlockSpec(memory_space=pl.ANY)] * len(extra),
        out_specs=pl.BlockSpec((1, ka, nb), lambda q, k: (q, block, 0)),
        out_shape=SDS((nq, stack_blocks * ka, nb), _LOWP),
        scratch_shapes=[pltpu.VMEM((ka, nb), F32)],
        input_output_aliases={2: 0} if extra else {},
        compiler_params=_cparams("parallel", "arbitrary"),
    )(a, b, *extra)


def _shared_rows(ncols):
    return lambda tk: pl.BlockSpec((tk, ncols), lambda q, k: (k, 0))


def _column_block(ncols):
    return lambda tk: pl.BlockSpec((tk, ncols), lambda q, k: (k, q))


def _quarter_major(ncols):
    return lambda tk: pl.BlockSpec((1, tk, ncols), lambda q, k: (q, k, 0))


def _pool_bwd(dxo, x, yp, vec, pw, carry=None):
    s, d = x.shape
    ng, gd, _ = pw.shape
    tm = _row_block(s)
    hb = tm // POOL_HALO
    nsteps = s // tm
    last_halo = s // POOL_HALO - 1

    def body(dxo_ref, halo_ref, x_ref, yp_ref, vec_ref, pw_ref, dyp_ref, dx_ref, sums_ref, eext_ref, dh_ref):
        i = pl.program_id(0)
        vec = vec_ref[...]

        @pl.when(i == 0)
        def _():
            sums_ref[...] = jnp.zeros_like(sums_ref)

        dxv = dxo_ref[...]
        ypv = yp_ref[...].astype(F32)
        gate1 = 1.0 + vec[3:4]
        sums_ref[3:4, :] += _colsum(dxv * (ypv * vec[4:5]))
        sums_ref[4:5, :] += _colsum(dxv * gate1 * ypv)
        dyp = (dxv * gate1 * vec[4:5]).astype(_LOWP)
        dyp_ref[...] = dyp
        dyp_halo = (jnp.where(i < nsteps - 1, halo_ref[...], 0.0) * gate1 * vec[4:5]).astype(_LOWP)
        for g in range(ng):
            cols = slice(g * gd, (g + 1) * gd)
            dm = _dot_nt(dyp[:, cols], pw_ref[g])
            dh_ref[:, cols] = dm
            inv = _pool_inv_count(i * tm, tm, POOL_WINDOWS[g])
            eext_ref[0:tm, cols] = dm * jnp.concatenate([inv] * (gd // LANES), axis=1)
            eext_ref[tm:, cols] = _dot_nt(dyp_halo[:, cols], pw_ref[g]) * (1.0 / POOL_WINDOWS[g])

        for g in range(ng):
            def tile(r0, c0, g=g):
                acc = eext_ref[pl.ds(r0, TILE_ROWS + POOL_HALO), c0:c0 + LANES]
                for step in range(g + 1):
                    acc = acc + _shift_up(acc, 2 ** step)
                here = (pl.ds(r0, TILE_ROWS), slice(c0, c0 + LANES))
                dh_ref[here] = acc[:TILE_ROWS] - dh_ref[here]

            _for_tiles(tm, g * gd, gd, tile)

        dh = dh_ref[...]
        xh, r, n = _rms(x_ref[...], vec[0:1])
        sums_ref[2:3, :] += _colsum(dh)
        sums_ref[1:2, :] += _colsum(dh * n)
        dn = dh * (1.0 + vec[1:2])
        sums_ref[0:1, :] += _colsum(dn * xh)
        dx_ref[...] = dxv + _rms_bwd(dn, xh, r, vec[0:1])

    return _call(
        body, dxo, dxo, x, yp, vec, pw, carry=carry, name="pool_bwd", grid=(nsteps,),
        in_specs=[_rows_spec(tm, d),
                  pl.BlockSpec((POOL_HALO, d), lambda i: (jnp.minimum((i + 1) * hb, last_halo), 0)),
                  _rows_spec(tm, d), _rows_spec(tm, d), _const_spec(vec.shape), _const_spec(pw.shape)],
        out_specs=[_rows_spec(tm, d), _rows_spec(tm, d), _const_spec((5, d))],
        out_shape=[SDS((s, d), _LOWP), SDS((s, d), F32), SDS((5, d), F32)],
        scratch_shapes=[pltpu.VMEM((tm + POOL_HALO, d), F32), pltpu.VMEM((tm, d), F32)], sem=("arbitrary",))


def _conv_b_bwd(dxo, cw, y, vec, w2, carry=None):
    s, d = dxo.shape
    tm = _row_block(s)

    def body(dxo_ref, cw_ref, y_ref, vec_ref, w2_ref, dy_ref, dcw_ref, sums_ref):
        i = pl.program_id(0)
        vec = vec_ref[...]

        @pl.when(i == 0)
        def _():
            sums_ref[...] = jnp.zeros_like(sums_ref)

        dxv = dxo_ref[...]
        dy = dxv * (1.0 + vec[4:5])
        sums_ref[0:1, :] += _colsum(dxv * y_ref[...].astype(F32))
        sums_ref[1:2, :] += _colsum(dy)
        dyb = dy.astype(_LOWP)
        dy_ref[...] = dyb
        ds = _dot_nt(dyb, w2_ref[...])
        cw = cw_ref[...]
        cc = cw - jnp.mean(cw, axis=-1, keepdims=True)
        rs = lax.rsqrt(jnp.mean(cc * cc, axis=-1, keepdims=True) + EPS)
        ch = cc * rs
        lo = ch * vec[1:2] + vec[2:3]
        dlo = ds * _dsilu(lo)
        sums_ref[2:3, :] += _colsum(dlo * ch)
        sums_ref[3:4, :] += _colsum(dlo)
        dch = dlo * vec[1:2]
        dcw = rs * (dch - jnp.mean(dch, axis=-1, keepdims=True) - ch * jnp.mean(dch * ch, axis=-1, keepdims=True))
        sums_ref[4:5, :] += _colsum(dcw)
        dcw_ref[...] = dcw

    return _call(
        body, dxo, cw, y, vec, w2, carry=carry, name="conv_b_bwd", grid=(s // tm,),
        in_specs=[_rows_spec(tm, d), _rows_spec(tm, d), _rows_spec(tm, d), _const_spec(vec.shape), _const_spec(w2.shape)],
        out_specs=[_rows_spec(tm, d), _rows_spec(tm, d), _const_spec((5, d))],
        out_shape=[SDS((s, d), _LOWP), SDS((s, d), F32), SDS((5, d), F32)], sem=("arbitrary",))


def _conv_a_bwd(dcw, glu, a, g, x, dxo, vec, w8, w1, carry=None):
    s, d = x.shape
    nq, _, hq = w1.shape
    tm = _row_block(s)
    hb = tm // CONV_HALO
    nsteps = s // tm
    last_halo = s // CONV_HALO - 1
    lead = CONV_HALO - CONV_WIDTH + 1

    def body(dcw_ref, dhalo_ref, glu_ref, ghalo_ref, a_ref, g_ref, x_ref, dxo_ref, vec_ref, w8_ref, w1_ref,
             du_ref, dx_ref, sums_ref, db1_ref, dwdw_ref, dext_ref, gext_ref, dglu_ref, wacc_ref):
        i = pl.program_id(0)
        vec = vec_ref[...]

        @pl.when(i == 0)
        def _():
            sums_ref[...] = jnp.zeros_like(sums_ref)
            db1_ref[...] = jnp.zeros_like(db1_ref)
            wacc_ref[...] = jnp.zeros_like(wacc_ref)

        dext_ref[0:tm, :] = dcw_ref[...]
        dext_ref[tm:, :] = jnp.where(i < nsteps - 1, dhalo_ref[...], 0.0)
        gext_ref[0:CONV_HALO, :] = jnp.where(i > 0, ghalo_ref[...], 0.0)
        gext_ref[CONV_HALO:, :] = glu_ref[...]

        def tile(r0, c0):
            lanes = slice(c0, c0 + LANES)
            dwin = dext_ref[pl.ds(r0, TILE_ROWS + CONV_HALO), lanes]
            acc = jnp.zeros((TILE_ROWS // SUBLANES, SUBLANES, LANES), F32)
            for b in range(SUBLANES):
                sh = _shift_up(dwin, b)
                for a8 in range(CONV_HALO // SUBLANES):
                    k = CONV_WIDTH - 1 - (SUBLANES * a8 + b)
                    if 0 <= k < CONV_WIDTH:
                        acc = acc + w8_ref[k, :, lanes][None] * _tiles3(sh[SUBLANES * a8:SUBLANES * a8 + TILE_ROWS])
            dglu_ref[pl.ds(r0, TILE_ROWS), lanes] = acc.reshape(TILE_ROWS, LANES)

            dc3 = _tiles3(dwin[:TILE_ROWS])
            gwin = gext_ref[pl.ds(r0, TILE_ROWS + CONV_HALO), lanes]
            for b in range(SUBLANES):
                sh = _shift_up(gwin, b)
                for a8 in range(CONV_HALO // SUBLANES + 1):
                    k = SUBLANES * a8 + b - lead
                    if 0 <= k < CONV_WIDTH:
                        part = jnp.sum(dc3 * _tiles3(sh[SUBLANES * a8:SUBLANES * a8 + TILE_ROWS]), axis=0)
                        wacc_ref[k, :, lanes] += part

        _for_tiles(tm, 0, d, tile)

        dglu = dglu_ref[...]
        av, gv = a_ref[...].astype(F32), g_ref[...].astype(F32)
        sg = _sigmoid(gv)
        da = dglu * sg
        dg = dglu * av * sg * (1.0 - sg)
        du_ref[:, 0:d] = da.astype(_LOWP)
        du_ref[:, d:2 * d] = dg.astype(_LOWP)
        db1_ref[:, 0:d] += _colsum(da)
        db1_ref[:, d:2 * d] += _colsum(dg)
        dh = jnp.zeros((tm, d), F32)
        for q in range(nq):
            dh = dh + _dot_nt(du_ref[:, q * hq:(q + 1) * hq], w1_ref[q])
        xh, r, n = _rms(x_ref[...], vec[0:1])
        sums_ref[2:3, :] += _colsum(dh)
        sums_ref[1:2, :] += _colsum(dh * n)
        dn = dh * (1.0 + vec[1:2])
        sums_ref[0:1, :] += _colsum(dn * xh)
        dx_ref[...] = dxo_ref[...] + _rms_bwd(dn, xh, r, vec[0:1])

        @pl.when(i == nsteps - 1)
        def _():
            dwdw_ref[...] = jnp.sum(wacc_ref[...], axis=1)

    return _call(
        body, dcw, dcw, glu, glu, a, g, x, dxo, vec, w8, w1, carry=carry, name="conv_a_bwd", grid=(nsteps,),
        in_specs=[_rows_spec(tm, d),
                  pl.BlockSpec((CONV_HALO, d), lambda i: (jnp.minimum((i + 1) * hb, last_halo), 0)),
                  _rows_spec(tm, d),
                  pl.BlockSpec((CONV_HALO, d), lambda i: (jnp.maximum(i * hb - 1, 0), 0)),
                  _rows_spec(tm, d), _rows_spec(tm, d), _rows_spec(tm, d), _rows_spec(tm, d),
                  _const_spec(vec.shape), _const_spec(w8.shape), _const_spec(w1.shape)],
        out_specs=[_rows_spec(tm, 2 * d), _rows_spec(tm, d), _const_spec((3, d)), _const_spec((1, 2 * d)),
                   _const_spec((CONV_WIDTH, d))],
        out_shape=[SDS((s, 2 * d), _LOWP), SDS((s, d), F32), SDS((3, d), F32), SDS((1, 2 * d), F32),
                   SDS((CONV_WIDTH, d), F32)],
        scratch_shapes=[pltpu.VMEM((tm + CONV_HALO, d), F32), pltpu.VMEM((tm + CONV_HALO, d), F32),
                        pltpu.VMEM((tm, d), F32), pltpu.VMEM((CONV_WIDTH, SUBLANES, d), F32)],
        sem=("arbitrary",))


def _adamw(w, g, m, v, name):
    r, c = w.shape
    rb = r
    for cand in (512, 352, 256, 184, 128, 64, 32, 16, 8):
        if r % cand == 0 and cand * c * 4 <= ADAM_BLOCK_BYTES:
            rb = cand
            break

    def body(w_ref, g_ref, m_ref, v_ref, d_ref, mo_ref, vo_ref):
        gv = g_ref[...]
        mn = ADAM_B1 * m_ref[...] + (1.0 - ADAM_B1) * gv
        vn = ADAM_B2 * v_ref[...] + (1.0 - ADAM_B2) * (gv * gv)
        m_hat = mn / (1.0 - ADAM_B1 ** ADAM_STEP)
        v_hat = vn / (1.0 - ADAM_B2 ** ADAM_STEP)
        d_ref[...] = -ADAM_LR * (m_hat / (jnp.sqrt(v_hat) + ADAM_EPS) + ADAM_WD * w_ref[...])
        mo_ref[...] = mn
        vo_ref[...] = vn

    spec = pl.BlockSpec((rb, c), lambda i: (i, 0))
    return pl.pallas_call(
        body, name=name, grid=(r // rb,), in_specs=[spec] * 4, out_specs=[spec] * 3,
        out_shape=[SDS((r, c), F32)] * 3, compiler_params=_cparams("parallel"),
    )(w, g, m, v)


def _adamw_nd(w, g, m, v, name):
    shape = w.shape
    two_d = (-1, shape[-1])
    outs = _adamw(w.reshape(two_d), g.reshape(two_d), m.reshape(two_d), v.reshape(two_d), name)
    return [o.reshape(shape) for o in outs]


def _adamw_from_slots(w, m, v, segs, pos, name, carry=None):
    r, c_ = w.shape
    rb = _divisor_block(r, (256, 176, 128, 64, 32, 16))
    for slots, _, first, rows in segs:
        half = slots.shape[1] // 2
        while first % rb or rows % rb or half % rb:
            rb //= 2
    assert rb % 16 == 0, (name, rb)
    plan, start = [], 0
    for slots, _, first, rows in segs:
        plan.append((start, rows // rb, first // rb, slots.shape[1] // 2 // rb))
        start += rows // rb
    assert start * rb == r, (name, start, rb, r)

    def body(pos_ref, w_ref, m_ref, v_ref, *rest):
        seg_refs, (g_ref, d_ref, mo_ref, vo_ref) = rest[:2 * len(segs)], rest[2 * len(segs):]
        i = pl.program_id(0)
        for k, (s0, nblk, b0, nbh) in enumerate(plan):
            @pl.when((i >= s0) & (i < s0 + nblk))
            def _(k=k, s0=s0, b0=b0, nbh=nbh):
                slots = seg_refs[2 * k][...].astype(F32)
                blk = b0 + i - s0
                in_my_half = (blk >= pos_ref[0] * nbh) & (blk < (pos_ref[0] + 1) * nbh)
                own = jnp.where(in_my_half, seg_refs[2 * k + 1][0].astype(F32), slots[3])
                gv = ((own + slots[0]) + slots[1]) + slots[2]
                mn = ADAM_B1 * m_ref[...] + (1.0 - ADAM_B1) * gv
                vn = ADAM_B2 * v_ref[...] + (1.0 - ADAM_B2) * (gv * gv)
                m_hat = mn / (1.0 - ADAM_B1 ** ADAM_STEP)
                v_hat = vn / (1.0 - ADAM_B2 ** ADAM_STEP)
                g_ref[...] = gv
                d_ref[...] = -ADAM_LR * (m_hat / (jnp.sqrt(v_hat) + ADAM_EPS) + ADAM_WD * w_ref[...])
                mo_ref[...] = mn
                vo_ref[...] = vn

    spec = pl.BlockSpec((rb, c_), lambda i, pos: (i, 0))
    seg_specs, seg_args = [], []
    for (slots, mine, _, _), (s0, nblk, b0, nbh) in zip(segs, plan):
        def in_seg(i, s0=s0, nblk=nblk):
            return jnp.clip(i - s0, 0, nblk - 1)
        seg_specs.append(pl.BlockSpec((N_CHIPS, rb, c_), lambda i, pos, f=in_seg, b0=b0: (0, b0 + f(i), 0)))
        seg_specs.append(pl.BlockSpec(
            (1, rb, c_), lambda i, pos, f=in_seg, b0=b0, nbh=nbh: (pos[1], jnp.clip(b0 + f(i) - pos[0] * nbh, 0, nbh - 1), 0)))
        seg_args += [slots, mine]
    return _call(body, w, m, v, *seg_args, carry=carry, prefetch=pos, name=name, grid=(r // rb,),
                 in_specs=[spec] * 3 + seg_specs, out_specs=[spec] * 4, out_shape=[SDS((r, c_), F32)] * 4,
                 sem=("arbitrary",))


def _rows_of(*vs):
    return jnp.concatenate([v.reshape(-1, v.shape[-1]) for v in vs], axis=0)


def kernel(x, c, ada_w, ada_b, norm_mix_g, norm_ffn_g, conv_w1, conv_b1, conv_wdw, conv_bdw, conv_ln_g, conv_ln_b, conv_w2, conv_b2, pool_w, pool_ls, ffn_w_gate, ffn_w_up, ffn_w_down, final_g, loss_target, m_ada_w, m_ada_b, m_norm_mix_g, m_norm_ffn_g, m_conv_w1, m_conv_b1, m_conv_wdw, m_conv_bdw, m_conv_ln_g, m_conv_ln_b, m_conv_w2, m_conv_b2, m_pool_w, m_pool_ls, m_ffn_w_gate, m_ffn_w_up, m_ffn_w_down, m_final_g, v_ada_w, v_ada_b, v_norm_mix_g, v_norm_ffn_g, v_conv_w1, v_conv_b1, v_conv_wdw, v_conv_bdw, v_conv_ln_g, v_conv_ln_b, v_conv_w2, v_conv_b2, v_pool_w, v_pool_ls, v_ffn_w_gate, v_ffn_w_up, v_ffn_w_down, v_final_g):
    s, d = x.shape[1], x.shape[2]
    depth = ada_w.shape[0]
    assert depth == 2 and conv_w1.shape[0] == 1 and pool_w.shape[0] == 1, "one conv layer then one pool layer"
    fq = ffn_w_gate.shape[2]
    n_ada = ada_b.shape[1] // d
    ada_cols = ada_w.shape[2]
    ng, gq, gd = pool_w.shape[1], pool_w.shape[2], pool_w.shape[3]
    cq = conv_wdw.shape[2]
    ax, ay, ac = _mesh_pos()
    myq = 2 * ax + ay
    me = 4 * ax + 2 * ay + ac
    x0 = x.reshape(s, d)
    tgt = loss_target.reshape(s, d)

    n_c = d // LANES
    small_in = jnp.concatenate(
        [c.reshape(n_c, LANES), conv_wdw[0].reshape(-1, LANES), pool_ls.reshape(-1, LANES)], axis=0)
    small_in_all = _all_gather8(small_in, "gather_c_and_small_shards").reshape(N_DEV, -1, LANES)
    c_all = small_in_all[:, :n_c].reshape(N_DEV, d)
    ada_b_cols = lax.dynamic_slice_in_dim(ada_b, myq * ada_cols, ada_cols, axis=1).reshape(depth, 1, ada_cols)
    mod_part = _ada_forward(c_all, ada_w, ada_b_cols)
    mod_all = _all_gather8(mod_part.reshape(-1, LANES), "gather_mod").reshape(N_DEV, depth, N_DEV, ada_cols)
    mod_mine = lax.dynamic_index_in_dim(mod_all[0::2], me, axis=2, keepdims=False)
    mod = jnp.transpose(mod_mine, (1, 0, 2)).reshape(depth, n_ada, d)

    def gathered(ex_out, shards):
        return [lax.dynamic_update_slice(g, sh[None], (myq, 0, 0)) for g, sh in zip(ex_out, shards)]

    pos = jnp.stack([ac, myq]).astype(jnp.int32)
    conv_shards = [conv_w1[0].astype(_LOWP), conv_w2[0].astype(_LOWP)]
    ffn0_shards = [ffn_w_gate[0].astype(_LOWP), ffn_w_up[0].astype(_LOWP), ffn_w_down[0].astype(_LOWP)]
    l1_shards = [pool_w[0].reshape(ng * gq, gd).astype(_LOWP), ffn_w_gate[1].astype(_LOWP), ffn_w_up[1].astype(_LOWP),
                 ffn_w_down[1].astype(_LOWP)]
    w1, w2_all = gathered(_Exchange("gather", conv_shards).alone(conv_shards, "gather_conv_weights"), conv_shards)
    w2 = w2_all.reshape(d, d)
    wdw_full_shards = small_in_all[0::2, n_c:].reshape(N_CHIPS, CONV_WIDTH + 1, cq)
    wdw = jnp.transpose(wdw_full_shards[:, :CONV_WIDTH], (1, 0, 2)).reshape(CONV_WIDTH, d)
    pls = wdw_full_shards[:, CONV_WIDTH].reshape(1, d)
    w8 = jnp.broadcast_to(wdw[:, None, :], (CONV_WIDTH, SUBLANES, d))

    vec_ca = _rows_of(norm_mix_g[0], mod[0, 1], mod[0, 0])
    h0, a0, g0, glu = _conv_a_fwd(x0, vec_ca, w1, conv_b1)
    vec_cb = _rows_of(conv_bdw, conv_ln_g, conv_ln_b, conv_b2, mod[0, 2])
    (cw, s0, y0, x1), ffn0_w = _conv_b_fwd(glu, w8, vec_cb, w2, x0, carry=(_Exchange("gather", ffn0_shards), ffn0_shards))
    wg0, wu0, wd0 = gathered(ffn0_w, ffn0_shards)
    vec_f0 = _rows_of(norm_ffn_g[0], mod[0, 4], mod[0, 3], mod[0, 5])
    (h2a, gta, upa, acta, y2a, x2), l1_w = _ffn_fwd(x1, vec_f0, wg0, wu0, wd0, "ffn0_fwd",
                                                   carry=(_Exchange("gather", l1_shards), l1_shards))
    pw_all, wg1, wu1, wd1 = gathered(l1_w, l1_shards)
    pw = jnp.transpose(pw_all.reshape(N_CHIPS, ng, gq, gd), (1, 0, 2, 3)).reshape(ng, N_CHIPS * gq, gd)
    vec_p = _rows_of(norm_mix_g[1], mod[1, 1], mod[1, 0], mod[1, 2], pls)
    mixed, yp, x3 = _pool_fwd(x2, vec_p, pw)
    vec_f1 = _rows_of(norm_ffn_g[1], mod[1, 4], mod[1, 3], mod[1, 5])
    (h2b, gtb, upb, actb, y2b, x4), _ = _ffn_fwd(x3, vec_f1, wg1, wu1, wd1, "ffn1_fwd")
    dx4, fin_sums = _final_fwd_bwd(x4, tgt, final_g.reshape(1, d))
    loss = lax.psum(fin_sums[1, 0], ("x", "y", "c"))

    def add_halves(partials, swapped, tag):
        return [_add_sibling_half(p, r, pos, f"grad_add_{tag}{k}") for k, (p, r) in enumerate(zip(partials, swapped))]

    def ffn_wgrads(h2, dgt, dup, act, dy, tag):
        gate_up = _wgrad(h2, dgt, _shared_rows(d), _quarter_major(fq), N_CHIPS, d, fq, s, "wgrad_gate" + tag, stack_blocks=2)
        gate_up = _wgrad(h2, dup, _shared_rows(d), _quarter_major(fq), N_CHIPS, d, fq, s, "wgrad_up" + tag,
                         stack=gate_up, stack_blocks=2, block=1)
        down = _wgrad(act, dy, _quarter_major(fq), _shared_rows(d), N_CHIPS, fq, d, s, "wgrad_down" + tag)
        return [gate_up, down]

    (dgtb, dupb, dyb, dx3, sums_f1), _ = _ffn_bwd(dx4, x3, vec_f1, gtb, upb, y2b, wg1, wu1, wd1, "ffn1_bwd")
    part_l1 = ffn_wgrads(h2b, dgtb, dupb, actb, dyb, "1")
    (dyp, dx2, sums_p), swapped_l1 = _pool_bwd(dx3, x2, yp, vec_p, pw, carry=(_SiblingSwap(part_l1), part_l1))
    cs_l1 = add_halves(part_l1, swapped_l1, "l1")
    dpw = _wgrad(mixed, dyp, _column_block(gd), _column_block(gd), ng, gd, gd, s, "wgrad_pool")
    dpw_q = jnp.transpose(dpw.reshape(ng, N_CHIPS, gq, gd), (1, 0, 2, 3)).reshape(N_CHIPS, ng * gq, gd)
    (dgta, dupa, dya, dx1, sums_f0), slots_l1 = _ffn_bwd(dx2, x1, vec_f0, gta, upa, y2a, wg0, wu0, wd0, "ffn0_bwd",
                                                        carry=(_Exchange("reduce", cs_l1), cs_l1))
    part_l0 = ffn_wgrads(h2a, dgta, dupa, acta, dya, "0") + [dpw_q]
    (dy0, dcw, sums_cb), swapped_l0 = _conv_b_bwd(dx1, cw, y0, vec_cb, w2, carry=(_SiblingSwap(part_l0), part_l0))
    dw2 = _wgrad(s0, dy0, _column_block(d // N_CHIPS), _shared_rows(d), N_CHIPS, d // N_CHIPS, d, s, "wgrad_conv2")
    cs_l0 = add_halves(part_l0, swapped_l0, "l0") + add_halves([dw2], _SiblingSwap([dw2]).alone([dw2], "grad_swap_w2"), "w2")
    (du, dx0, sums_ca, db1, dwdw), slots_l0 = _conv_a_bwd(dcw, glu, a0, g0, x0, dx1, vec_ca, w8, w1,
                                                          carry=(_Exchange("reduce", cs_l0), cs_l0))
    dw1 = _wgrad(h0, du, _shared_rows(d), _column_block(2 * d // N_CHIPS), N_CHIPS, d, 2 * d // N_CHIPS, s, "wgrad_conv1")
    cs_conv = add_halves([dw1], _SiblingSwap([dw1]).alone([dw1], "grad_swap_w1"), "w1")

    dmod = jnp.stack([
        _rows_of(sums_ca[2], sums_ca[1], sums_cb[0], sums_f0[2], sums_f0[1], sums_f0[3]),
        _rows_of(sums_p[2], sums_p[1], sums_p[3], sums_f1[2], sums_f1[1], sums_f1[3])])
    small = _rows_of(dmod.reshape(-1, d), sums_ca[0], sums_p[0], sums_f0[0], sums_f1[0], db1.reshape(2, d),
                     sums_cb[4], sums_cb[2], sums_cb[3], sums_cb[1], fin_sums[0], sums_p[4], dwdw)
    n_small = small.shape[0]
    pad = (-n_small) % SUBLANES
    small = jnp.concatenate([small, jnp.zeros((pad, d), F32)], axis=0) if pad else small
    small_all = _all_gather8(small.reshape(-1, LANES), "gather_small_grads").reshape(N_DEV, -1, LANES)
    small_sum = _sum_devices(small_all).reshape(-1, d)
    n_dm = depth * n_ada
    g_ada_b = small_sum[0:n_dm].reshape(depth, n_ada * d)
    g_norm_mix = small_sum[n_dm:n_dm + 2]
    g_norm_ffn = small_sum[n_dm + 2:n_dm + 4]
    g_b1 = small_sum[n_dm + 4:n_dm + 6].reshape(1, 2 * d)
    g_bdw, g_lng, g_lnb, g_b2 = (small_sum[n_dm + 6 + k].reshape(1, d) for k in range(4))
    g_final = small_sum[n_dm + 10]
    g_pls = lax.dynamic_slice_in_dim(small_sum[n_dm + 11].reshape(1, d), myq * cq, cq, axis=1)
    g_wdw = lax.dynamic_slice_in_dim(small_sum[n_dm + 12:n_dm + 12 + CONV_WIDTH], myq * cq, cq, axis=1).reshape(conv_wdw.shape)
    dmod_all = small_all.reshape(N_DEV, -1, d)[:, 0:n_dm].reshape(N_DEV, depth, n_ada * d)
    dmod_cols = lax.dynamic_slice_in_dim(jnp.transpose(dmod_all, (1, 0, 2)), myq * ada_cols, ada_cols, axis=2)
    g_ada_w = _ada_backward(c_all.T, dmod_cols)

    def small_pack(*vs):
        return jnp.concatenate([v.reshape(-1) for v in vs]).reshape(-1, LANES)

    names = ("ada_b", "norm_mix_g", "norm_ffn_g", "conv_b1", "conv_bdw", "conv_ln_g", "conv_ln_b", "conv_b2", "final_g")
    small_w = (ada_b, norm_mix_g, norm_ffn_g, conv_b1, conv_bdw, conv_ln_g, conv_ln_b, conv_b2, final_g)
    small_g = (g_ada_b, g_norm_mix, g_norm_ffn, g_b1, g_bdw, g_lng, g_lnb, g_b2, g_final)
    small_m = (m_ada_b, m_norm_mix_g, m_norm_ffn_g, m_conv_b1, m_conv_bdw, m_conv_ln_g, m_conv_ln_b, m_conv_b2, m_final_g)
    small_v = (v_ada_b, v_norm_mix_g, v_norm_ffn_g, v_conv_b1, v_conv_bdw, v_conv_ln_g, v_conv_ln_b, v_conv_b2, v_final_g)
    sd, sm, sv = _adamw(small_pack(*small_w), small_pack(*small_g), small_pack(*small_m), small_pack(*small_v), "adamw_small")

    def unpack(flat2d):
        flat = flat2d.reshape(-1)
        out, off = {}, 0
        for nm, wv in zip(names, small_w):
            out[nm] = flat[off:off + wv.size].reshape(wv.shape)
            off += wv.size
        return out

    sd, sm, sv = unpack(sd), unpack(sm), unpack(sv)
    grads = {
        "ada_w": g_ada_w, "ada_b": g_ada_b, "norm_mix_g": g_norm_mix, "norm_ffn_g": g_norm_ffn,
        "conv_b1": g_b1, "conv_wdw": g_wdw, "conv_bdw": g_bdw, "conv_ln_g": g_lng,
        "conv_ln_b": g_lnb, "conv_b2": g_b2, "pool_ls": g_pls, "final_g": g_final,
    }
    plain = {
        "ada_w": (ada_w, m_ada_w, v_ada_w), "conv_wdw": (conv_wdw, m_conv_wdw, v_conv_wdw),
        "pool_ls": (pool_ls, m_pool_ls, v_pool_ls),
    }
    from_slots = {
        "ffn_w_gate": (ffn_w_gate, m_ffn_w_gate, v_ffn_w_gate,
                       lambda: [(slots_l0[0], cs_l0[0], 0, d), (slots_l1[0], cs_l1[0], 0, d)]),
        "ffn_w_up": (ffn_w_up, m_ffn_w_up, v_ffn_w_up,
                     lambda: [(slots_l0[0], cs_l0[0], d, d), (slots_l1[0], cs_l1[0], d, d)]),
        "ffn_w_down": (ffn_w_down, m_ffn_w_down, v_ffn_w_down,
                       lambda: [(slots_l0[1], cs_l0[1], 0, fq), (slots_l1[1], cs_l1[1], 0, fq)]),
        "pool_w": (pool_w, m_pool_w, v_pool_w, lambda: [(slots_l0[2], cs_l0[2], 0, ng * gq)]),
        "conv_w2": (conv_w2, m_conv_w2, v_conv_w2, lambda: [(slots_l0[3], cs_l0[3], 0, d // N_CHIPS)]),
        "conv_w1": (conv_w1, m_conv_w1, v_conv_w1, lambda: [(slots_conv[0], cs_conv[0], 0, d)]),
    }
    order = ("ada_w", "ada_b", "norm_mix_g", "norm_ffn_g", "conv_w1", "conv_b1", "conv_wdw", "conv_bdw", "conv_ln_g",
             "conv_ln_b", "conv_w2", "conv_b2", "pool_w", "pool_ls", "ffn_w_gate", "ffn_w_up", "ffn_w_down", "final_g")
    delta, new_m, new_v = {}, {}, {}
    for nm, (wv, mv, vv, segs) in from_slots.items():
        two_d = (-1, wv.shape[-1])
        carry = (_Exchange("reduce", cs_conv), cs_conv) if nm == "ffn_w_gate" else None
        outs, brought = _adamw_from_slots(wv.reshape(two_d), mv.reshape(two_d), vv.reshape(two_d), segs(), pos,
                                          "adamw_" + nm, carry=carry)
        if carry is not None:
            slots_conv = brought
        grads[nm], delta[nm], new_m[nm], new_v[nm] = (o.reshape(wv.shape) for o in outs)
    for nm in order:
        if nm in from_slots:
            continue
        elif nm in plain:
            wv, mv, vv = plain[nm]
            grads[nm] = grads[nm].reshape(wv.shape)
            delta[nm], new_m[nm], new_v[nm] = _adamw_nd(wv, grads[nm], mv, vv, "adamw_" + nm)
        else:
            delta[nm], new_m[nm], new_v[nm] = sd[nm], sm[nm], sv[nm]
            grads[nm] = grads[nm].reshape(dict(zip(names, small_w))[nm].shape)

    return (loss, dx0.reshape(x.shape), *[grads[n] for n in order], *[delta[n] for n in order],
            *[new_m[n] for n in order], *[new_v[n] for n in order])
```

```python
import functools

import jax
import jax.numpy as jnp
from jax import lax
from jax.experimental import pallas as pl
from jax.experimental.pallas import tpu as pltpu

F32 = jnp.float32
_LOWP = jnp.bfloat16
EPS = 1e-6
CONV_WIDTH = 31
POOL_WINDOWS = (2, 4, 8, 16)
ADAM_LR = 0.001
ADAM_B1 = 0.9
ADAM_B2 = 0.999
ADAM_EPS = 1e-08
ADAM_WD = 0.01
ADAM_STEP = 10

N_CHIPS = 4
N_DEV = 8
LANES = 128
SUBLANES = 8
CONV_HALO = 32
POOL_HALO = 16
TILE_ROWS = 64
VMEM_LIMIT = 56 * 1024 * 1024
ADAM_BLOCK_BYTES = 1 << 20
WGRAD_ROWS = 4096
MESH = pl.DeviceIdType.MESH
SDS = jax.ShapeDtypeStruct


def _cparams(*sem):
    return pltpu.CompilerParams(dimension_semantics=sem if sem else None, vmem_limit_bytes=VMEM_LIMIT)


def _row_block(s):
    for tm in (512, 256, 128, 64, 32, 16):
        if s % tm == 0:
            return tm
    raise ValueError(f"sequence length {s} must be a multiple of 16")


def _rows_spec(tm, ncols):
    return pl.BlockSpec((tm, ncols), lambda i, *_: (i, 0))


def _const_spec(shape):
    nd = len(shape)
    return pl.BlockSpec(shape, lambda *_: (0,) * nd)


def _dot(a, b):
    return lax.dot_general(a.astype(_LOWP), b.astype(_LOWP), (((1,), (0,)), ((), ())), preferred_element_type=F32)


def _dot_nt(a, b):
    return lax.dot_general(a.astype(_LOWP), b.astype(_LOWP), (((1,), (1,)), ((), ())), preferred_element_type=F32)


def _dot_tn(a, b):
    return lax.dot_general(a.astype(_LOWP), b.astype(_LOWP), (((0,), (0,)), ((), ())), preferred_element_type=F32)


def _rms(x, gamma):
    r = lax.rsqrt(jnp.mean(x * x, axis=-1, keepdims=True) + EPS)
    xh = x * r
    return xh, r, xh * gamma


def _rms_bwd(dn, xh, r, gamma):
    dxh = dn * gamma
    return r * (dxh - xh * jnp.mean(dxh * xh, axis=-1, keepdims=True))


def _colsum(v):
    return jnp.sum(v, axis=0, keepdims=True)


def _sigmoid(v):
    return jax.nn.sigmoid(v)


def _dsilu(v):
    sg = _sigmoid(v)
    return sg * (1.0 + v * (1.0 - sg))


def _for_tiles(nrows, col0, ncols, fn):
    def step(t, carry):
        r0 = pl.multiple_of(t * TILE_ROWS, TILE_ROWS)
        for col in range(col0, col0 + ncols, LANES):
            fn(r0, col)
        return carry

    lax.fori_loop(0, nrows // TILE_ROWS, step, 0)


def _shift_up(win, b):
    return pltpu.roll(win, win.shape[0] - b, 0) if b else win


def _shift_down(win, b):
    return pltpu.roll(win, b, 0) if b else win


def _tiles3(v):
    return v.reshape(v.shape[0] // SUBLANES, SUBLANES, v.shape[1])


def _resident_scratch(arrays):
    return [pltpu.VMEM(a.shape, a.dtype) for a in arrays] + [pltpu.SemaphoreType.DMA((len(arrays),))]


def _load_resident(hbm_refs, vmem_refs, sems):
    copies = [pltpu.make_async_copy(src, dst, sems.at[k]) for k, (src, dst) in enumerate(zip(hbm_refs, vmem_refs))]
    for cp in copies:
        cp.start()
    for cp in copies:
        cp.wait()


def _mesh_pos():
    return lax.axis_index("x"), lax.axis_index("y"), lax.axis_index("c")


def _other_chips(x, y):
    return [(1 - x, y), (x, 1 - y), (1 - x, 1 - y)]


def _all_gather8(v, name):
    m, n = v.shape

    def body(x_ref, out_ref, send_sems, recv_sems, local_sem):
        x, y, c = _mesh_pos()
        me, sibling = (x, y, c), (x, y, 1 - c)
        chips = _other_chips(x, y)

        def rows(px, py, pc):
            return out_ref.at[pl.ds((4 * px + 2 * py + pc) * m, m), :]

        def copy(k, block, to, src=None):
            return pltpu.make_async_remote_copy(
                src_ref=rows(*block) if src is None else src, dst_ref=rows(*block),
                send_sem=send_sems.at[k], recv_sem=recv_sems.at[k], device_id=to, device_id_type=MESH)

        mine = pltpu.make_async_copy(x_ref, rows(*me), local_sem)
        mine.start()
        first = [copy(0, me, sibling, src=x_ref)]
        first += [copy(1 + j, me, (*chip, c), src=x_ref) for j, chip in enumerate(chips)]
        for cp in first:
            cp.start()
        passed = [copy(4 + j, (*chip, c), sibling) for j, chip in enumerate(chips)]
        for j, chip in enumerate(chips):
            copy(1 + j, (*chip, c), me).wait_recv()
            passed[j].start()
        copy(0, sibling, me).wait_recv()
        for j, chip in enumerate(chips):
            copy(4 + j, (*chip, 1 - c), me).wait_recv()
        for cp in first + passed:
            cp.wait_send()
        mine.wait()

    return pl.pallas_call(
        body, name=name,
        out_shape=SDS((N_DEV * m, n), v.dtype),
        in_specs=[pl.BlockSpec(memory_space=pltpu.VMEM)],
        out_specs=pl.BlockSpec(memory_space=pltpu.VMEM),
        scratch_shapes=[pltpu.SemaphoreType.DMA((7,)), pltpu.SemaphoreType.DMA((7,)), pltpu.SemaphoreType.DMA],
    )(v)


class _Exchange:
    N_SEMS = 7

    def __init__(self, kind, arrays):
        self.kind, self.n = kind, len(arrays)
        self.streams = 3
        if kind == "gather":
            self.out_shape = [SDS((N_CHIPS,) + a.shape, a.dtype) for a in arrays]
        else:
            self.out_shape = [SDS((N_CHIPS, 2 * a.shape[1], a.shape[2]), a.dtype) for a in arrays]
        self.scratch = [pltpu.SemaphoreType.DMA((self.n, self.N_SEMS)), pltpu.SemaphoreType.DMA((self.n, self.N_SEMS))]

    def _ctx(self, ins, outs, send_sems, recv_sems):
        x, y, c = _mesh_pos()
        myq = 2 * x + y
        sibling = (x, y, 1 - c)
        chips = _other_chips(x, y)
        gather = self.kind == "gather"
        ns = self.streams

        def rows(a, h):
            hr = outs[a].shape[1] // 2
            return pl.ds(h * hr, hr)

        def copy(a, ks, kr, src, dst, to):
            return pltpu.make_async_remote_copy(src_ref=src, dst_ref=dst, send_sem=send_sems.at[a, ks],
                                                recv_sem=recv_sems.at[a, kr], device_id=to, device_id_type=MESH)

        def place(a, t, h):
            if gather:
                px, py = chips[t]
                return outs[a].at[2 * px + py, rows(a, h)]
            return outs[a].at[t, rows(a, h)]

        def first(a, t):
            if gather:
                return copy(a, t, t, ins[a].at[rows(a, c)], outs[a].at[myq, rows(a, c)], (*chips[t], c))
            px, py = chips[t]
            return copy(a, t, t, ins[a].at[2 * px + py], outs[a].at[t, rows(a, c)], (px, py, c))

        def landed(a, t):
            return copy(a, t, t, place(a, t, c), place(a, t, c), sibling)

        def relay(a, t):
            return copy(a, ns + t, ns + t, place(a, t, c), place(a, t, c), sibling)

        def relayed(a, t):
            return copy(a, ns + t, ns + t, place(a, t, 1 - c), place(a, t, 1 - c), sibling)

        def own(a):
            return copy(a, 2 * ns, 2 * ns, ins[a].at[myq], outs[a].at[ns, rows(a, c)], sibling)

        return first, landed, relay, relayed, own

    def start(self, ins, outs, send_sems, recv_sems):
        first, _, _, _, own = self._ctx(ins, outs, send_sems, recv_sems)
        for a in range(self.n):
            for t in range(self.streams):
                first(a, t).start()
            if self.kind == "reduce":
                own(a).start()

    def relay(self, ins, outs, send_sems, recv_sems):
        _, landed, relay, _, _ = self._ctx(ins, outs, send_sems, recv_sems)
        for a in range(self.n):
            for t in range(self.streams):
                landed(a, t).wait_recv()
                relay(a, t).start()

    def finish(self, ins, outs, send_sems, recv_sems):
        first, _, relay, relayed, own = self._ctx(ins, outs, send_sems, recv_sems)
        for a in range(self.n):
            for t in range(self.streams):
                relayed(a, t).wait_recv()
            if self.kind == "reduce":
                own(a).wait_recv()
        for a in range(self.n):
            for t in range(self.streams):
                first(a, t).wait_send()
                relay(a, t).wait_send()
            if self.kind == "reduce":
                own(a).wait_send()

    def alone(self, arrays, name):
        n = self.n

        def body(*refs):
            args = (refs[:n], refs[n:2 * n], refs[2 * n], refs[2 * n + 1])
            self.start(*args)
            self.relay(*args)
            self.finish(*args)

        any_spec = pl.BlockSpec(memory_space=pl.ANY)
        return pl.pallas_call(body, name=name, out_shape=self.out_shape, in_specs=[any_spec] * n,
                              out_specs=[any_spec] * n, scratch_shapes=self.scratch)(*arrays)


RELAY_AT = (5, 8)


def _call(body, *args, carry=None, prefetch=None, name, grid, in_specs, out_specs, out_shape, scratch_shapes=(), sem):
    def run(fn, in_specs_, out_specs_, out_shape_, scratch_, sem_, operands):
        if prefetch is None:
            return pl.pallas_call(fn, name=name, grid=grid, in_specs=in_specs_, out_specs=out_specs_, out_shape=out_shape_,
                                  scratch_shapes=scratch_, compiler_params=_cparams(*sem_))(*operands)
        grid_spec = pltpu.PrefetchScalarGridSpec(num_scalar_prefetch=1, grid=grid, in_specs=in_specs_,
                                                 out_specs=out_specs_, scratch_shapes=scratch_)
        return pl.pallas_call(fn, name=name, grid_spec=grid_spec, out_shape=out_shape_,
                              compiler_params=_cparams(*sem_))(prefetch, *operands)

    if carry is None:
        return run(body, list(in_specs), list(out_specs), list(out_shape), list(scratch_shapes), sem, args), None
    ex, arrays = carry
    n_in, n_out, n_scr, n = len(in_specs), len(out_specs), len(scratch_shapes), ex.n
    n_pre = 0 if prefetch is None else 1
    total = 1
    for g in grid:
        total *= g
    relay_step = (total * RELAY_AT[0]) // RELAY_AT[1]

    def hosted(*refs):
        pre, refs = refs[:n_pre], refs[n_pre:]
        ins, cins = refs[:n_in], refs[n_in:n_in + n]
        outs, couts = refs[n_in + n:n_in + n + n_out], refs[n_in + n + n_out:n_in + 2 * n + n_out]
        scr = refs[n_in + 2 * n + n_out:n_in + 2 * n + n_out + n_scr]
        sems = refs[n_in + 2 * n + n_out + n_scr:]
        step = 0
        for ax, g in enumerate(grid):
            step = step * g + pl.program_id(ax)

        @pl.when(step == 0)
        def _():
            ex.start(cins, couts, *sems)

        body(*pre, *ins, *outs, *scr)

        @pl.when(step == relay_step)
        def _():
            ex.relay(cins, couts, *sems)

        @pl.when(step == total - 1)
        def _():
            ex.finish(cins, couts, *sems)

    any_spec = pl.BlockSpec(memory_space=pl.ANY)
    res = run(hosted, list(in_specs) + [any_spec] * n, list(out_specs) + [any_spec] * n, list(out_shape) + ex.out_shape,
              list(scratch_shapes) + ex.scratch, ("arbitrary",) * len(grid), (*args, *arrays))
    return res[:n_out], res[n_out:]


class _SiblingSwap:
    def __init__(self, arrays):
        self.n = len(arrays)
        self.out_shape = [SDS((a.shape[0], a.shape[1] // 2, a.shape[2]), a.dtype) for a in arrays]
        self.scratch = [pltpu.SemaphoreType.DMA((self.n,)), pltpu.SemaphoreType.DMA((self.n,))]

    def _copies(self, ins, outs, send_sems, recv_sems):
        x, y, c = _mesh_pos()
        cps = []
        for a in range(self.n):
            hr = ins[a].shape[1] // 2
            cps.append(pltpu.make_async_remote_copy(
                src_ref=ins[a].at[:, pl.ds((1 - c) * hr, hr)], dst_ref=outs[a],
                send_sem=send_sems.at[a], recv_sem=recv_sems.at[a], device_id=(x, y, 1 - c), device_id_type=MESH))
        return cps

    def start(self, *refs):
        for cp in self._copies(*refs):
            cp.start()

    def relay(self, *refs):
        pass

    def finish(self, *refs):
        for cp in self._copies(*refs):
            cp.wait()

    def alone(self, arrays, name):
        return _Exchange.alone(self, arrays, name)


def _divisor_block(rows, cands):
    for rb in cands:
        if rows % rb == 0:
            return rb
    raise ValueError(rows)


def _add_sibling_half(p, rb1, pos, name):
    nq, r, n = p.shape
    h = r // 2
    rb = _divisor_block(h, (512, 352, 256, 176, 128, 64, 32, 16))
    nb = h // rb

    def body(pos_ref, p_ref, r_ref, o_ref):
        o_ref[...] = (p_ref[...].astype(F32) + r_ref[...].astype(F32)).astype(o_ref.dtype)

    grid_spec = pltpu.PrefetchScalarGridSpec(
        num_scalar_prefetch=1, grid=(nq, nb),
        in_specs=[pl.BlockSpec((1, rb, n), lambda q, i, pos: (q, pos[0] * nb + i, 0)),
                  pl.BlockSpec((1, rb, n), lambda q, i, pos: (q, i, 0))],
        out_specs=pl.BlockSpec((1, rb, n), lambda q, i, pos: (q, i, 0)))
    return pl.pallas_call(
        body, name=name, grid_spec=grid_spec, out_shape=SDS((nq, h, n), p.dtype),
        compiler_params=_cparams("parallel", "parallel"),
    )(pos, p, rb1)


def _sum_devices(g):
    nd, m, n = g.shape

    def body(g_ref, o_ref):
        acc = g_ref[0]
        for d in range(1, nd):
            acc = acc + g_ref[d]
        o_ref[...] = acc

    return pl.pallas_call(body, name="small_grad_sum", out_shape=SDS((m, n), F32))(g)


def _ada_forward(c_all, ada_w, ada_b_cols, carry=None):
    nl, d, ncol = ada_w.shape
    nb = c_all.shape[0]
    tn = 512 if ncol % 512 == 0 else ncol

    def body(c_ref, w_ref, b_ref, o_ref):
        cv = c_ref[...]
        ca = cv * _sigmoid(cv)
        o_ref[0] = _dot(ca, w_ref[0]) + b_ref[0]

    return _call(
        body, c_all, ada_w, ada_b_cols, carry=carry, name="ada_forward", grid=(nl, ncol // tn),
        in_specs=[pl.BlockSpec((nb, d), lambda l, j: (0, 0)),
                  pl.BlockSpec((1, d, tn), lambda l, j: (l, 0, j)),
                  pl.BlockSpec((1, 1, tn), lambda l, j: (l, 0, j))],
        out_specs=[pl.BlockSpec((1, nb, tn), lambda l, j: (l, 0, j))],
        out_shape=[SDS((nl, nb, ncol), F32)], sem=("parallel", "parallel"))


def _ada_backward(c_all_t, dmod_cols):
    d, nb = c_all_t.shape
    nl, _, ncol = dmod_cols.shape
    tn = 512 if ncol % 512 == 0 else ncol

    def body(c_ref, g_ref, o_ref):
        cv = c_ref[...]
        ca = cv * _sigmoid(cv)
        o_ref[0] = _dot(ca, g_ref[0])

    return pl.pallas_call(
        body, name="ada_backward", grid=(nl, ncol // tn),
        in_specs=[pl.BlockSpec((d, nb), lambda l, j: (0, 0)),
                  pl.BlockSpec((1, nb, tn), lambda l, j: (l, 0, j))],
        out_specs=pl.BlockSpec((1, d, tn), lambda l, j: (l, 0, j)),
        out_shape=SDS((nl, d, ncol), F32), compiler_params=_cparams("parallel", "parallel"),
    )(c_all_t, dmod_cols)


def _conv_a_fwd(x, vec, w1, b1):
    s, d = x.shape
    nq, _, hq = w1.shape
    tm = _row_block(s)

    def body(x_ref, vec_ref, w1_ref, b1_ref, h_ref, a_ref, g_ref, glu_ref):
        vec = vec_ref[...]
        _, _, n = _rms(x_ref[...], vec[0:1])
        hb = (n * (1.0 + vec[1:2]) + vec[2:3]).astype(_LOWP)
        h_ref[...] = hb
        u = [_dot(hb, w1_ref[q]) + b1_ref[:, q * hq:(q + 1) * hq] for q in range(nq)]
        for k in range(nq // 2):
            av, gv = u[k], u[nq // 2 + k]
            cols = slice(k * hq, (k + 1) * hq)
            a_ref[:, cols] = av.astype(_LOWP)
            g_ref[:, cols] = gv.astype(_LOWP)
            glu_ref[:, cols] = av * _sigmoid(gv)

    return pl.pallas_call(
        body, name="conv_a_fwd", grid=(s // tm,),
        in_specs=[_rows_spec(tm, d), _const_spec(vec.shape), _const_spec(w1.shape), _const_spec(b1.shape)],
        out_specs=[_rows_spec(tm, d)] * 4,
        out_shape=[SDS((s, d), _LOWP)] * 3 + [SDS((s, d), F32)],
        compiler_params=_cparams("parallel"),
    )(x, vec, w1, b1)


def _conv_b_fwd(glu, w8, vec, w2, x, carry=None):
    s, d = x.shape
    tm = _row_block(s)
    hb = tm // CONV_HALO

    def body(glu_ref, halo_ref, w8_ref, vec_ref, w2_ref, x_ref, cw_ref, s_ref, y_ref, xo_ref, ext_ref):
        i = pl.program_id(0)
        vec = vec_ref[...]
        ext_ref[0:CONV_HALO, :] = jnp.where(i > 0, halo_ref[...], 0.0)
        ext_ref[CONV_HALO:, :] = glu_ref[...]

        def tile(r0, c0):
            lanes = slice(c0, c0 + LANES)
            win = ext_ref[pl.ds(r0, TILE_ROWS + CONV_HALO), lanes]
            acc = jnp.zeros((TILE_ROWS // SUBLANES, SUBLANES, LANES), F32)
            for b in range(SUBLANES):
                sh = _shift_up(win, b)
                for a in range(CONV_HALO // SUBLANES + 1):
                    k = SUBLANES * a + b - (CONV_HALO - CONV_WIDTH + 1)
                    if 0 <= k < CONV_WIDTH:
                        acc = acc + w8_ref[k, :, lanes][None] * _tiles3(sh[SUBLANES * a:SUBLANES * a + TILE_ROWS])
            cw_ref[pl.ds(r0, TILE_ROWS), lanes] = acc.reshape(TILE_ROWS, LANES)

        _for_tiles(tm, 0, d, tile)
        cw = cw_ref[...] + vec[0:1]
        cw_ref[...] = cw
        cc = cw - jnp.mean(cw, axis=-1, keepdims=True)
        ch = cc * lax.rsqrt(jnp.mean(cc * cc, axis=-1, keepdims=True) + EPS)
        lo = ch * vec[1:2] + vec[2:3]
        sv = (lo * _sigmoid(lo)).astype(_LOWP)
        s_ref[...] = sv
        y = _dot(sv, w2_ref[...]) + vec[3:4]
        y_ref[...] = y.astype(_LOWP)
        xo_ref[...] = x_ref[...] + (1.0 + vec[4:5]) * y

    return _call(
        body, glu, glu, w8, vec, w2, x, carry=carry, name="conv_b_fwd", grid=(s // tm,),
        in_specs=[_rows_spec(tm, d),
                  pl.BlockSpec((CONV_HALO, d), lambda i: (jnp.maximum(i * hb - 1, 0), 0)),
                  _const_spec(w8.shape), _const_spec(vec.shape), _const_spec(w2.shape), _rows_spec(tm, d)],
        out_specs=[_rows_spec(tm, d)] * 4,
        out_shape=[SDS((s, d), F32), SDS((s, d), _LOWP), SDS((s, d), _LOWP), SDS((s, d), F32)],
        scratch_shapes=[pltpu.VMEM((tm + CONV_HALO, d), F32)], sem=("parallel",))


def _ffn_fwd(x, vec, wg, wu, wd, name, carry=None, head=None):
    s, d = x.shape
    nq, _, fq = wg.shape
    tm = _row_block(s)
    nsteps = s // tm
    n_head = 0 if head is None else 2

    def body(x_ref, vec_ref, wg_hbm, wu_hbm, wd_hbm, *rest):
        head_refs, rest = rest[:n_head], rest[n_head:]
        h_ref, gt_ref, up_ref, act_ref, y_ref, xo_ref = rest[:6]
        sums_ref = rest[6] if head is not None else None
        hb_ref, yacc_ref, wg_ref, wu_ref, wd_ref, w_sems = rest[6 + n_head // 2:]
        i, q = pl.program_id(0), pl.program_id(1)
        vec = vec_ref[...]

        @pl.when((i == 0) & (q == 0))
        def _():
            _load_resident((wg_hbm, wu_hbm, wd_hbm), (wg_ref, wu_ref, wd_ref), w_sems)
            if head is not None:
                sums_ref[...] = jnp.zeros_like(sums_ref)

        @pl.when(q == 0)
        def _():
            _, _, n = _rms(x_ref[...], vec[0:1])
            hb = (n * (1.0 + vec[1:2]) + vec[2:3]).astype(_LOWP)
            hb_ref[...] = hb
            h_ref[...] = hb
            yacc_ref[...] = jnp.zeros_like(yacc_ref)

        hb = hb_ref[...]
        gt = _dot(hb, wg_ref[q])
        up = _dot(hb, wu_ref[q])
        act = (gt * _sigmoid(gt) * up).astype(_LOWP)
        gt_ref[0] = gt.astype(_LOWP)
        up_ref[0] = up.astype(_LOWP)
        act_ref[0] = act
        yacc_ref[...] += _dot(act, wd_ref[q])

        @pl.when(q == nq - 1)
        def _():
            y = yacc_ref[...]
            y_ref[...] = y.astype(_LOWP)
            xo = x_ref[...] + (1.0 + vec[3:4]) * y
            if head is None:
                xo_ref[...] = xo
            else:
                t_ref, g_ref = head_refs
                gamma_v = g_ref[...]
                xh, r, out = _rms(xo, gamma_v)
                err = out - t_ref[...]
                dout = err * (1.0 / d)
                sums_ref[0:1, :] += _colsum(dout * xh)
                sums_ref[1:2, :] += _colsum(0.5 * err * dout)
                xo_ref[...] = _rms_bwd(dout, xh, r, gamma_v)

        if head is not None:
            @pl.when((i == nsteps - 1) & (q == nq - 1))
            def _():
                sums_ref[1:2, :] = jnp.broadcast_to(jnp.sum(sums_ref[1:2, :], axis=-1, keepdims=True), (1, d))

    rows = pl.BlockSpec((tm, d), lambda i, q: (i, 0))
    hid = pl.BlockSpec((1, tm, fq), lambda i, q: (q, i, 0))
    any_spec = pl.BlockSpec(memory_space=pl.ANY)
    head_args = [] if head is None else [head[0], head[1]]
    head_specs = [] if head is None else [rows, pl.BlockSpec((1, d), lambda i, q: (0, 0))]
    sums_specs = [] if head is None else [pl.BlockSpec((2, d), lambda i, q: (0, 0))]
    sums_shape = [] if head is None else [SDS((2, d), F32)]
    return _call(
        body, x, vec, wg, wu, wd, *head_args, carry=carry, name=name, grid=(nsteps, nq),
        in_specs=[rows, pl.BlockSpec(vec.shape, lambda i, q: (0, 0)), any_spec, any_spec, any_spec] + head_specs,
        out_specs=[rows, hid, hid, hid, rows, rows] + sums_specs,
        out_shape=[SDS((s, d), _LOWP)] + [SDS((nq, s, fq), _LOWP)] * 3 + [SDS((s, d), _LOWP), SDS((s, d), F32)] + sums_shape,
        scratch_shapes=[pltpu.VMEM((tm, d), _LOWP), pltpu.VMEM((tm, d), F32)] + _resident_scratch((wg, wu, wd)),
        sem=("arbitrary", "arbitrary"))


def _pool_inv_count(t0, nrows, w):
    t = t0 + lax.broadcasted_iota(jnp.int32, (nrows, LANES), 0)
    return 1.0 / jnp.minimum(t + 1, w).astype(F32)


def _pool_fwd(x, vec, pw):
    s, d = x.shape
    ng, gd, _ = pw.shape
    tm = _row_block(s)
    hb = tm // POOL_HALO

    def body(x_ref, halo_ref, vec_ref, pw_ref, mixed_ref, yp_ref, xo_ref, hext_ref):
        i = pl.program_id(0)
        vec = vec_ref[...]

        def modulated(xv):
            _, _, n = _rms(xv, vec[0:1])
            return n * (1.0 + vec[1:2]) + vec[2:3]

        hext_ref[0:POOL_HALO, :] = jnp.where(i > 0, modulated(halo_ref[...]), 0.0)
        hext_ref[POOL_HALO:, :] = modulated(x_ref[...])

        for g in range(ng):
            def tile(r0, c0, g=g):
                lanes = slice(c0, c0 + LANES)
                win = hext_ref[pl.ds(r0, TILE_ROWS + POOL_HALO), lanes]
                acc = win
                for step in range(g + 1):
                    acc = acc + _shift_down(acc, 2 ** step)
                inv = _pool_inv_count(i * tm + r0, TILE_ROWS, POOL_WINDOWS[g])
                mixed = acc[POOL_HALO:] * inv - win[POOL_HALO:]
                mixed_ref[pl.ds(r0, TILE_ROWS), lanes] = mixed.astype(_LOWP)

            _for_tiles(tm, g * gd, gd, tile)

        for g in range(ng):
            cols = slice(g * gd, (g + 1) * gd)
            yp = _dot(mixed_ref[:, cols], pw_ref[g])
            yp_ref[:, cols] = yp.astype(_LOWP)
            xo_ref[:, cols] = x_ref[:, cols] + (1.0 + vec[3:4, cols]) * (yp * vec[4:5, cols])

    return pl.pallas_call(
        body, name="pool_fwd", grid=(s // tm,),
        in_specs=[_rows_spec(tm, d),
                  pl.BlockSpec((POOL_HALO, d), lambda i: (jnp.maximum(i * hb - 1, 0), 0)),
                  _const_spec(vec.shape), _const_spec(pw.shape)],
        out_specs=[_rows_spec(tm, d)] * 3,
        out_shape=[SDS((s, d), _LOWP), SDS((s, d), _LOWP), SDS((s, d), F32)],
        scratch_shapes=[pltpu.VMEM((tm + POOL_HALO, d), F32)],
        compiler_params=_cparams("parallel"),
    )(x, x, vec, pw)


def _ffn_bwd(dxo, x, vec, gt, up, y, wg, wu, wd, name, carry=None):
    s, d = x.shape
    nq, _, fq = wg.shape
    tm = _row_block(s)

    def body(dxo_ref, x_ref, vec_ref, gt_ref, up_ref, y_ref, wg_hbm, wu_hbm, wd_hbm,
             dgt_ref, dup_ref, dy_ref, dx_ref, sums_ref, dyb_ref, dh_ref, wg_ref, wu_ref, wd_ref, w_sems):
        i, q = pl.program_id(0), pl.program_id(1)
        vec = vec_ref[...]

        @pl.when((i == 0) & (q == 0))
        def _():
            _load_resident((wg_hbm, wu_hbm, wd_hbm), (wg_ref, wu_ref, wd_ref), w_sems)
            sums_ref[...] = jnp.zeros_like(sums_ref)

        @pl.when(q == 0)
        def _():
            dxv = dxo_ref[...]
            dyb = (dxv * (1.0 + vec[3:4])).astype(_LOWP)
            dyb_ref[...] = dyb
            dy_ref[...] = dyb
            dh_ref[...] = jnp.zeros_like(dh_ref)
            sums_ref[3:4, :] += _colsum(dxv * y_ref[...].astype(F32))

        halves = [slice(k * (tm // 2), (k + 1) * (tm // 2)) for k in range(2)]
        dacts = [_dot_nt(dyb_ref[rs, :], wd_ref[q]) for rs in halves]
        dgus = []
        for rs, dact in zip(halves, dacts):
            gtv = gt_ref[0, rs, :].astype(F32)
            sg = _sigmoid(gtv)
            dgt = (dact * up_ref[0, rs, :].astype(F32) * (sg * (1.0 + gtv * (1.0 - sg)))).astype(_LOWP)
            dup = (dact * (gtv * sg)).astype(_LOWP)
            dgt_ref[0, rs, :] = dgt
            dup_ref[0, rs, :] = dup
            dgus.append((dgt, dup))
        for rs, (dgt, dup) in zip(halves, dgus):
            dh_ref[rs, :] += _dot_nt(dgt, wg_ref[q]) + _dot_nt(dup, wu_ref[q])

        @pl.when(q == nq - 1)
        def _():
            dh = dh_ref[...]
            xh, r, n = _rms(x_ref[...], vec[0:1])
            sums_ref[2:3, :] += _colsum(dh)
            sums_ref[1:2, :] += _colsum(dh * n)
            dn = dh * (1.0 + vec[1:2])
            sums_ref[0:1, :] += _colsum(dn * xh)
            dx_ref[...] = dxo_ref[...] + _rms_bwd(dn, xh, r, vec[0:1])

    rows = pl.BlockSpec((tm, d), lambda i, q: (i, 0))
    hid = pl.BlockSpec((1, tm, fq), lambda i, q: (q, i, 0))
    return _call(
        body, dxo, x, vec, gt, up, y, wg, wu, wd, carry=carry, name=name, grid=(s // tm, nq),
        in_specs=[rows, rows, pl.BlockSpec(vec.shape, lambda i, q: (0, 0)), hid, hid, rows] +
                 [pl.BlockSpec(memory_space=pl.ANY)] * 3,
        out_specs=[hid, hid, rows, rows, pl.BlockSpec((4, d), lambda i, q: (0, 0))],
        out_shape=[SDS((nq, s, fq), _LOWP)] * 2 + [SDS((s, d), _LOWP), SDS((s, d), F32), SDS((4, d), F32)],
        scratch_shapes=[pltpu.VMEM((tm, d), _LOWP), pltpu.VMEM((tm, d), F32)] + _resident_scratch((wg, wu, wd)),
        sem=("arbitrary", "arbitrary"))


def _wgrad(a, b, a_spec, b_spec, nq, ka, nb, s, name, stack=None, stack_blocks=1, block=0):
    tk = min(WGRAD_ROWS, s)
    nk = s // tk

    def body(a_ref, b_ref, *rest):
        o_ref, acc_ref = rest[-2:]
        k = pl.program_id(1)

        @pl.when(k == 0)
        def _():
            acc_ref[...] = jnp.zeros_like(acc_ref)

        av = a_ref[...].reshape(tk, ka)
        bv = b_ref[...].reshape(tk, nb)
        acc_ref[...] += _dot_tn(av, bv)

        @pl.when(k == nk - 1)
        def _():
            o_ref[0] = acc_ref[...].astype(o_ref.dtype)

    extra = [] if stack is None else [stack]
    return pl.pallas_call(
        body, name=name, grid=(nq, nk),
        in_specs=[a_spec(tk), b_spec(tk)] + [pl.BlockSpec(memory_space=pl.ANY)] * len(extra),
        out_specs=pl.BlockSpec((1, ka, nb), lambda q, k: (q, block, 0)),
        out_shape=SDS((nq, stack_blocks * ka, nb), _LOWP),
        scratch_shapes=[pltpu.VMEM((ka, nb), F32)],
        input_output_aliases={2: 0} if extra else {},
        compiler_params=_cparams("parallel", "arbitrary"),
    )(a, b, *extra)


def _shared_rows(ncols):
    return lambda tk: pl.BlockSpec((tk, ncols), lambda q, k: (k, 0))


def _column_block(ncols):
    return lambda tk: pl.BlockSpec((tk, ncols), lambda q, k: (k, q))


def _quarter_major(ncols):
    return lambda tk: pl.BlockSpec((1, tk, ncols), lambda q, k: (q, k, 0))


def _pool_bwd(dxo, x, yp, vec, pw, carry=None):
    s, d = x.shape
    ng, gd, _ = pw.shape
    tm = _row_block(s)
    hb = tm // POOL_HALO
    nsteps = s // tm
    last_halo = s // POOL_HALO - 1

    def body(dxo_ref, halo_ref, x_ref, yp_ref, vec_ref, pw_ref, dyp_ref, dx_ref, sums_ref, eext_ref, dh_ref):
        i = pl.program_id(0)
        vec = vec_ref[...]

        @pl.when(i == 0)
        def _():
            sums_ref[...] = jnp.zeros_like(sums_ref)

        dxv = dxo_ref[...]
        ypv = yp_ref[...].astype(F32)
        gate1 = 1.0 + vec[3:4]
        sums_ref[3:4, :] += _colsum(dxv * (ypv * vec[4:5]))
        sums_ref[4:5, :] += _colsum(dxv * gate1 * ypv)
        dyp = (dxv * gate1 * vec[4:5]).astype(_LOWP)
        dyp_ref[...] = dyp
        dyp_halo = (jnp.where(i < nsteps - 1, halo_ref[...], 0.0) * gate1 * vec[4:5]).astype(_LOWP)
        for g in range(ng):
            cols = slice(g * gd, (g + 1) * gd)
            dm = _dot_nt(dyp[:, cols], pw_ref[g])
            dh_ref[:, cols] = dm
            inv = _pool_inv_count(i * tm, tm, POOL_WINDOWS[g])
            eext_ref[0:tm, cols] = dm * jnp.concatenate([inv] * (gd // LANES), axis=1)
            eext_ref[tm:, cols] = _dot_nt(dyp_halo[:, cols], pw_ref[g]) * (1.0 / POOL_WINDOWS[g])

        for g in range(ng):
            def tile(r0, c0, g=g):
                acc = eext_ref[pl.ds(r0, TILE_ROWS + POOL_HALO), c0:c0 + LANES]
                for step in range(g + 1):
                    acc = acc + _shift_up(acc, 2 ** step)
                here = (pl.ds(r0, TILE_ROWS), slice(c0, c0 + LANES))
                dh_ref[here] = acc[:TILE_ROWS] - dh_ref[here]

            _for_tiles(tm, g * gd, gd, tile)

        dh = dh_ref[...]
        xh, r, n = _rms(x_ref[...], vec[0:1])
        sums_ref[2:3, :] += _colsum(dh)
        sums_ref[1:2, :] += _colsum(dh * n)
        dn = dh * (1.0 + vec[1:2])
        sums_ref[0:1, :] += _colsum(dn * xh)
        dx_ref[...] = dxv + _rms_bwd(dn, xh, r, vec[0:1])

    return _call(
        body, dxo, dxo, x, yp, vec, pw, carry=carry, name="pool_bwd", grid=(nsteps,),
        in_specs=[_rows_spec(tm, d),
                  pl.BlockSpec((POOL_HALO, d), lambda i: (jnp.minimum((i + 1) * hb, last_halo), 0)),
                  _rows_spec(tm, d), _rows_spec(tm, d), _const_spec(vec.shape), _const_spec(pw.shape)],
        out_specs=[_rows_spec(tm, d), _rows_spec(tm, d), _const_spec((5, d))],
        out_shape=[SDS((s, d), _LOWP), SDS((s, d), F32), SDS((5, d), F32)],
        scratch_shapes=[pltpu.VMEM((tm + POOL_HALO, d), F32), pltpu.VMEM((tm, d), F32)], sem=("arbitrary",))


def _conv_b_bwd(dxo, cw, y, vec, w2, carry=None):
    s, d = dxo.shape
    tm = _row_block(s)

    def body(dxo_ref, cw_ref, y_ref, vec_ref, w2_ref, dy_ref, dcw_ref, sums_ref):
        i = pl.program_id(0)
        vec = vec_ref[...]

        @pl.when(i == 0)
        def _():
            sums_ref[...] = jnp.zeros_like(sums_ref)

        dxv = dxo_ref[...]
        dy = dxv * (1.0 + vec[4:5])
        sums_ref[0:1, :] += _colsum(dxv * y_ref[...].astype(F32))
        sums_ref[1:2, :] += _colsum(dy)
        dyb = dy.astype(_LOWP)
        dy_ref[...] = dyb
        ds = _dot_nt(dyb, w2_ref[...])
        cw = cw_ref[...]
        cc = cw - jnp.mean(cw, axis=-1, keepdims=True)
        rs = lax.rsqrt(jnp.mean(cc * cc, axis=-1, keepdims=True) + EPS)
        ch = cc * rs
        lo = ch * vec[1:2] + vec[2:3]
        dlo = ds * _dsilu(lo)
        sums_ref[2:3, :] += _colsum(dlo * ch)
        sums_ref[3:4, :] += _colsum(dlo)
        dch = dlo * vec[1:2]
        dcw = rs * (dch - jnp.mean(dch, axis=-1, keepdims=True) - ch * jnp.mean(dch * ch, axis=-1, keepdims=True))
        sums_ref[4:5, :] += _colsum(dcw)
        dcw_ref[...] = dcw

    return _call(
        body, dxo, cw, y, vec, w2, carry=carry, name="conv_b_bwd", grid=(s // tm,),
        in_specs=[_rows_spec(tm, d), _rows_spec(tm, d), _rows_spec(tm, d), _const_spec(vec.shape), _const_spec(w2.shape)],
        out_specs=[_rows_spec(tm, d), _rows_spec(tm, d), _const_spec((5, d))],
        out_shape=[SDS((s, d), _LOWP), SDS((s, d), F32), SDS((5, d), F32)], sem=("arbitrary",))


def _conv_a_bwd(dcw, glu, a, g, x, dxo, vec, w8, w1, carry=None):
    s, d = x.shape
    nq, _, hq = w1.shape
    tm = _row_block(s)
    hb = tm // CONV_HALO
    nsteps = s // tm
    last_halo = s // CONV_HALO - 1
    lead = CONV_HALO - CONV_WIDTH + 1

    def body(dcw_ref, dhalo_ref, glu_ref, ghalo_ref, a_ref, g_ref, x_ref, dxo_ref, vec_ref, w8_ref, w1_ref,
             du_ref, dx_ref, sums_ref, db1_ref, dwdw_ref, dext_ref, gext_ref, dglu_ref, wacc_ref):
        i = pl.program_id(0)
        vec = vec_ref[...]

        @pl.when(i == 0)
        def _():
            sums_ref[...] = jnp.zeros_like(sums_ref)
            db1_ref[...] = jnp.zeros_like(db1_ref)
            wacc_ref[...] = jnp.zeros_like(wacc_ref)

        dext_ref[0:tm, :] = dcw_ref[...]
        dext_ref[tm:, :] = jnp.where(i < nsteps - 1, dhalo_ref[...], 0.0)
        gext_ref[0:CONV_HALO, :] = jnp.where(i > 0, ghalo_ref[...], 0.0)
        gext_ref[CONV_HALO:, :] = glu_ref[...]

        def tile(r0, c0):
            lanes = slice(c0, c0 + LANES)
            dwin = dext_ref[pl.ds(r0, TILE_ROWS + CONV_HALO), lanes]
            acc = jnp.zeros((TILE_ROWS // SUBLANES, SUBLANES, LANES), F32)
            for b in range(SUBLANES):
                sh = _shift_up(dwin, b)
                for a8 in range(CONV_HALO // SUBLANES):
                    k = CONV_WIDTH - 1 - (SUBLANES * a8 + b)
                    if 0 <= k < CONV_WIDTH:
                        acc = acc + w8_ref[k, :, lanes][None] * _tiles3(sh[SUBLANES * a8:SUBLANES * a8 + TILE_ROWS])
            dglu_ref[pl.ds(r0, TILE_ROWS), lanes] = acc.reshape(TILE_ROWS, LANES)

            dc3 = _tiles3(dwin[:TILE_ROWS])
            gwin = gext_ref[pl.ds(r0, TILE_ROWS + CONV_HALO), lanes]
            for b in range(SUBLANES):
                sh = _shift_up(gwin, b)
                for a8 in range(CONV_HALO // SUBLANES + 1):
                    k = SUBLANES * a8 + b - lead
                    if 0 <= k < CONV_WIDTH:
                        part = jnp.sum(dc3 * _tiles3(sh[SUBLANES * a8:SUBLANES * a8 + TILE_ROWS]), axis=0)
                        wacc_ref[k, :, lanes] += part

        _for_tiles(tm, 0, d, tile)

        dglu = dglu_ref[...]
        av, gv = a_ref[...].astype(F32), g_ref[...].astype(F32)
        sg = _sigmoid(gv)
        da = dglu * sg
        dg = dglu * av * sg * (1.0 - sg)
        du_ref[:, 0:d] = da.astype(_LOWP)
        du_ref[:, d:2 * d] = dg.astype(_LOWP)
        db1_ref[:, 0:d] += _colsum(da)
        db1_ref[:, d:2 * d] += _colsum(dg)
        dh = jnp.zeros((tm, d), F32)
        for q in range(nq):
            dh = dh + _dot_nt(du_ref[:, q * hq:(q + 1) * hq], w1_ref[q])
        xh, r, n = _rms(x_ref[...], vec[0:1])
        sums_ref[2:3, :] += _colsum(dh)
        sums_ref[1:2, :] += _colsum(dh * n)
        dn = dh * (1.0 + vec[1:2])
        sums_ref[0:1, :] += _colsum(dn * xh)
        dx_ref[...] = dxo_ref[...] + _rms_bwd(dn, xh, r, vec[0:1])

        @pl.when(i == nsteps - 1)
        def _():
            dwdw_ref[...] = jnp.sum(wacc_ref[...], axis=1)

    return _call(
        body, dcw, dcw, glu, glu, a, g, x, dxo, vec, w8, w1, carry=carry, name="conv_a_bwd", grid=(nsteps,),
        in_specs=[_rows_spec(tm, d),
                  pl.BlockSpec((CONV_HALO, d), lambda i: (jnp.minimum((i + 1) * hb, last_halo), 0)),
                  _rows_spec(tm, d),
                  pl.BlockSpec((CONV_HALO, d), lambda i: (jnp.maximum(i * hb - 1, 0), 0)),
                  _rows_spec(tm, d), _rows_spec(tm, d), _rows_spec(tm, d), _rows_spec(tm, d),
                  _const_spec(vec.shape), _const_spec(w8.shape), _const_spec(w1.shape)],
        out_specs=[_rows_spec(tm, 2 * d), _rows_spec(tm, d), _const_spec((3, d)), _const_spec((1, 2 * d)),
                   _const_spec((CONV_WIDTH, d))],
        out_shape=[SDS((s, 2 * d), _LOWP), SDS((s, d), F32), SDS((3, d), F32), SDS((1, 2 * d), F32),
                   SDS((CONV_WIDTH, d), F32)],
        scratch_shapes=[pltpu.VMEM((tm + CONV_HALO, d), F32), pltpu.VMEM((tm + CONV_HALO, d), F32),
                        pltpu.VMEM((tm, d), F32), pltpu.VMEM((CONV_WIDTH, SUBLANES, d), F32)],
        sem=("arbitrary",))


def _adamw(w, g, m, v, name):
    r, c = w.shape
    rb = r
    for cand in (512, 352, 256, 184, 128, 64, 32, 16, 8):
        if r % cand == 0 and cand * c * 4 <= ADAM_BLOCK_BYTES:
            rb = cand
            break

    def body(w_ref, g_ref, m_ref, v_ref, d_ref, mo_ref, vo_ref):
        gv = g_ref[...]
        mn = ADAM_B1 * m_ref[...] + (1.0 - ADAM_B1) * gv
        vn = ADAM_B2 * v_ref[...] + (1.0 - ADAM_B2) * (gv * gv)
        m_hat = mn / (1.0 - ADAM_B1 ** ADAM_STEP)
        v_hat = vn / (1.0 - ADAM_B2 ** ADAM_STEP)
        d_ref[...] = -ADAM_LR * (m_hat / (jnp.sqrt(v_hat) + ADAM_EPS) + ADAM_WD * w_ref[...])
        mo_ref[...] = mn
        vo_ref[...] = vn

    spec = pl.BlockSpec((rb, c), lambda i: (i, 0))
    return pl.pallas_call(
        body, name=name, grid=(r // rb,), in_specs=[spec] * 4, out_specs=[spec] * 3,
        out_shape=[SDS((r, c), F32)] * 3, compiler_params=_cparams("parallel"),
    )(w, g, m, v)


def _adamw_nd(w, g, m, v, name):
    shape = w.shape
    two_d = (-1, shape[-1])
    outs = _adamw(w.reshape(two_d), g.reshape(two_d), m.reshape(two_d), v.reshape(two_d), name)
    return [o.reshape(shape) for o in outs]


def _adamw_from_slots(w, m, v, segs, pos, name, carry=None):
    r, c_ = w.shape
    rb = _divisor_block(r, (256, 176, 128, 64, 32, 16))
    for slots, _, first, rows in segs:
        half = slots.shape[1] // 2
        while first % rb or rows % rb or half % rb:
            rb //= 2
    assert rb % 16 == 0, (name, rb)
    plan, start = [], 0
    for slots, _, first, rows in segs:
        plan.append((start, rows // rb, first // rb, slots.shape[1] // 2 // rb))
        start += rows // rb
    assert start * rb == r, (name, start, rb, r)

    def body(pos_ref, w_ref, m_ref, v_ref, *rest):
        seg_refs, (g_ref, d_ref, mo_ref, vo_ref) = rest[:2 * len(segs)], rest[2 * len(segs):]
        i = pl.program_id(0)
        for k, (s0, nblk, b0, nbh) in enumerate(plan):
            @pl.when((i >= s0) & (i < s0 + nblk))
            def _(k=k, s0=s0, b0=b0, nbh=nbh):
                slots = seg_refs[2 * k][...].astype(F32)
                blk = b0 + i - s0
                in_my_half = (blk >= pos_ref[0] * nbh) & (blk < (pos_ref[0] + 1) * nbh)
                own = jnp.where(in_my_half, seg_refs[2 * k + 1][0].astype(F32), slots[3])
                gv = ((own + slots[0]) + slots[1]) + slots[2]
                mn = ADAM_B1 * m_ref[...] + (1.0 - ADAM_B1) * gv
                vn = ADAM_B2 * v_ref[...] + (1.0 - ADAM_B2) * (gv * gv)
                m_hat = mn / (1.0 - ADAM_B1 ** ADAM_STEP)
                v_hat = vn / (1.0 - ADAM_B2 ** ADAM_STEP)
                g_ref[...] = gv
                d_ref[...] = -ADAM_LR * (m_hat / (jnp.sqrt(v_hat) + ADAM_EPS) + ADAM_WD * w_ref[...])
                mo_ref[...] = mn
                vo_ref[...] = vn

    spec = pl.BlockSpec((rb, c_), lambda i, pos: (i, 0))
    seg_specs, seg_args = [], []
    for (slots, mine, _, _), (s0, nblk, b0, nbh) in zip(segs, plan):
        def in_seg(i, s0=s0, nblk=nblk):
            return jnp.clip(i - s0, 0, nblk - 1)
        seg_specs.append(pl.BlockSpec((N_CHIPS, rb, c_), lambda i, pos, f=in_seg, b0=b0: (0, b0 + f(i), 0)))
        seg_specs.append(pl.BlockSpec(
            (1, rb, c_), lambda i, pos, f=in_seg, b0=b0, nbh=nbh: (pos[1], jnp.clip(b0 + f(i) - pos[0] * nbh, 0, nbh - 1), 0)))
        seg_args += [slots, mine]
    return _call(body, w, m, v, *seg_args, carry=carry, prefetch=pos, name=name, grid=(r // rb,),
                 in_specs=[spec] * 3 + seg_specs, out_specs=[spec] * 4, out_shape=[SDS((r, c_), F32)] * 4,
                 sem=("arbitrary",))


def _rows_of(*vs):
    return jnp.concatenate([v.reshape(-1, v.shape[-1]) for v in vs], axis=0)


def kernel(x, c, ada_w, ada_b, norm_mix_g, norm_ffn_g, conv_w1, conv_b1, conv_wdw, conv_bdw, conv_ln_g, conv_ln_b, conv_w2, conv_b2, pool_w, pool_ls, ffn_w_gate, ffn_w_up, ffn_w_down, final_g, loss_target, m_ada_w, m_ada_b, m_norm_mix_g, m_norm_ffn_g, m_conv_w1, m_conv_b1, m_conv_wdw, m_conv_bdw, m_conv_ln_g, m_conv_ln_b, m_conv_w2, m_conv_b2, m_pool_w, m_pool_ls, m_ffn_w_gate, m_ffn_w_up, m_ffn_w_down, m_final_g, v_ada_w, v_ada_b, v_norm_mix_g, v_norm_ffn_g, v_conv_w1, v_conv_b1, v_conv_wdw, v_conv_bdw, v_conv_ln_g, v_conv_ln_b, v_conv_w2, v_conv_b2, v_pool_w, v_pool_ls, v_ffn_w_gate, v_ffn_w_up, v_ffn_w_down, v_final_g):
    s, d = x.shape[1], x.shape[2]
    depth = ada_w.shape[0]
    assert depth == 2 and conv_w1.shape[0] == 1 and pool_w.shape[0] == 1, "one conv layer then one pool layer"
    fq = ffn_w_gate.shape[2]
    n_ada = ada_b.shape[1] // d
    ada_cols = ada_w.shape[2]
    ng, gq, gd = pool_w.shape[1], pool_w.shape[2], pool_w.shape[3]
    cq = conv_wdw.shape[2]
    ax, ay, ac = _mesh_pos()
    myq = 2 * ax + ay
    me = 4 * ax + 2 * ay + ac
    x0 = x.reshape(s, d)
    tgt = loss_target.reshape(s, d)

    n_c = d // LANES
    small_in = jnp.concatenate(
        [c.reshape(n_c, LANES), conv_wdw[0].reshape(-1, LANES), pool_ls.reshape(-1, LANES)], axis=0)
    small_in_all = _all_gather8(small_in, "gather_c_and_small_shards").reshape(N_DEV, -1, LANES)
    c_all = small_in_all[:, :n_c].reshape(N_DEV, d)
    ada_b_cols = lax.dynamic_slice_in_dim(ada_b, myq * ada_cols, ada_cols, axis=1).reshape(depth, 1, ada_cols)
    conv_shards = [conv_w1[0].astype(_LOWP), conv_w2[0].astype(_LOWP)]
    (mod_part,), conv_w = _ada_forward(c_all, ada_w, ada_b_cols, carry=(_Exchange("gather", conv_shards), conv_shards))
    mod_all = _all_gather8(mod_part.reshape(-1, LANES), "gather_mod").reshape(N_DEV, depth, N_DEV, ada_cols)
    mod_mine = lax.dynamic_index_in_dim(mod_all[0::2], me, axis=2, keepdims=False)
    mod = jnp.transpose(mod_mine, (1, 0, 2)).reshape(depth, n_ada, d)

    def gathered(ex_out, shards):
        return [lax.dynamic_update_slice(g, sh[None], (myq, 0, 0)) for g, sh in zip(ex_out, shards)]

    pos = jnp.stack([ac, myq]).astype(jnp.int32)
    ffn0_shards = [ffn_w_gate[0].astype(_LOWP), ffn_w_up[0].astype(_LOWP), ffn_w_down[0].astype(_LOWP)]
    l1_shards = [pool_w[0].reshape(ng * gq, gd).astype(_LOWP), ffn_w_gate[1].astype(_LOWP), ffn_w_up[1].astype(_LOWP),
                 ffn_w_down[1].astype(_LOWP)]
    w1, w2_all = gathered(conv_w, conv_shards)
    w2 = w2_all.reshape(d, d)
    wdw_full_shards = small_in_all[0::2, n_c:].reshape(N_CHIPS, CONV_WIDTH + 1, cq)
    wdw = jnp.transpose(wdw_full_shards[:, :CONV_WIDTH], (1, 0, 2)).reshape(CONV_WIDTH, d)
    pls = wdw_full_shards[:, CONV_WIDTH].reshape(1, d)
    w8 = jnp.broadcast_to(wdw[:, None, :], (CONV_WIDTH, SUBLANES, d))

    vec_ca = _rows_of(norm_mix_g[0], mod[0, 1], mod[0, 0])
    h0, a0, g0, glu = _conv_a_fwd(x0, vec_ca, w1, conv_b1)
    vec_cb = _rows_of(conv_bdw, conv_ln_g, conv_ln_b, conv_b2, mod[0, 2])
    (cw, s0, y0, x1), ffn0_w = _conv_b_fwd(glu, w8, vec_cb, w2, x0, carry=(_Exchange("gather", ffn0_shards), ffn0_shards))
    wg0, wu0, wd0 = gathered(ffn0_w, ffn0_shards)
    vec_f0 = _rows_of(norm_ffn_g[0], mod[0, 4], mod[0, 3], mod[0, 5])
    (h2a, gta, upa, acta, y2a, x2), l1_w = _ffn_fwd(x1, vec_f0, wg0, wu0, wd0, "ffn0_fwd",
                                                   carry=(_Exchange("gather", l1_shards), l1_shards))
    pw_all, wg1, wu1, wd1 = gathered(l1_w, l1_shards)
    pw = jnp.transpose(pw_all.reshape(N_CHIPS, ng, gq, gd), (1, 0, 2, 3)).reshape(ng, N_CHIPS * gq, gd)
    vec_p = _rows_of(norm_mix_g[1], mod[1, 1], mod[1, 0], mod[1, 2], pls)
    mixed, yp, x3 = _pool_fwd(x2, vec_p, pw)
    vec_f1 = _rows_of(norm_ffn_g[1], mod[1, 4], mod[1, 3], mod[1, 5])
    (h2b, gtb, upb, actb, y2b, dx4, fin_sums), _ = _ffn_fwd(x3, vec_f1, wg1, wu1, wd1, "ffn1_fwd_loss",
                                                           head=(tgt, final_g.reshape(1, d)))
    loss = lax.psum(fin_sums[1, 0], ("x", "y", "c"))

    def add_halves(partials, swapped, tag):
        return [_add_sibling_half(p, r, pos, f"grad_add_{tag}{k}") for k, (p, r) in enumerate(zip(partials, swapped))]

    def ffn_wgrads(h2, dgt, dup, act, dy, tag):
        gate_up = _wgrad(h2, dgt, _shared_rows(d), _quarter_major(fq), N_CHIPS, d, fq, s, "wgrad_gate" + tag, stack_blocks=2)
        gate_up = _wgrad(h2, dup, _shared_rows(d), _quarter_major(fq), N_CHIPS, d, fq, s, "wgrad_up" + tag,
                         stack=gate_up, stack_blocks=2, block=1)
        down = _wgrad(act, dy, _quarter_major(fq), _shared_rows(d), N_CHIPS, fq, d, s, "wgrad_down" + tag)
        return [gate_up, down]

    (dgtb, dupb, dyb, dx3, sums_f1), _ = _ffn_bwd(dx4, x3, vec_f1, gtb, upb, y2b, wg1, wu1, wd1, "ffn1_bwd")
    part_l1 = ffn_wgrads(h2b, dgtb, dupb, actb, dyb, "1")
    (dyp, dx2, sums_p), swapped_l1 = _pool_bwd(dx3, x2, yp, vec_p, pw, carry=(_SiblingSwap(part_l1), part_l1))
    cs_l1 = add_halves(part_l1, swapped_l1, "l1")
    dpw = _wgrad(mixed, dyp, _column_block(gd), _column_block(gd), ng, gd, gd, s, "wgrad_pool")
    dpw_q = jnp.transpose(dpw.reshape(ng, N_CHIPS, gq, gd), (1, 0, 2, 3)).reshape(N_CHIPS, ng * gq, gd)
    (dgta, dupa, dya, dx1, sums_f0), slots_l1 = _ffn_bwd(dx2, x1, vec_f0, gta, upa, y2a, wg0, wu0, wd0, "ffn0_bwd",
                                                        carry=(_Exchange("reduce", cs_l1), cs_l1))
    part_l0 = ffn_wgrads(h2a, dgta, dupa, acta, dya, "0") + [dpw_q]
    (dy0, dcw, sums_cb), swapped_l0 = _conv_b_bwd(dx1, cw, y0, vec_cb, w2, carry=(_SiblingSwap(part_l0), part_l0))
    dw2 = _wgrad(s0, dy0, _column_block(d // N_CHIPS), _shared_rows(d), N_CHIPS, d // N_CHIPS, d, s, "wgrad_conv2")
    cs_l0 = add_halves(part_l0, swapped_l0, "l0") + add_halves([dw2], _SiblingSwap([dw2]).alone([dw2], "grad_swap_w2"), "w2")
    (du, dx0, sums_ca, db1, dwdw), slots_l0 = _conv_a_bwd(dcw, glu, a0, g0, x0, dx1, vec_ca, w8, w1,
                                                          carry=(_Exchange("reduce", cs_l0), cs_l0))
    dw1 = _wgrad(h0, du, _shared_rows(d), _column_block(2 * d // N_CHIPS), N_CHIPS, d, 2 * d // N_CHIPS, s, "wgrad_conv1")
    cs_conv = add_halves([dw1], _SiblingSwap([dw1]).alone([dw1], "grad_swap_w1"), "w1")

    dmod = jnp.stack([
        _rows_of(sums_ca[2], sums_ca[1], sums_cb[0], sums_f0[2], sums_f0[1], sums_f0[3]),
        _rows_of(sums_p[2], sums_p[1], sums_p[3], sums_f1[2], sums_f1[1], sums_f1[3])])
    small = _rows_of(dmod.reshape(-1, d), sums_ca[0], sums_p[0], sums_f0[0], sums_f1[0], db1.reshape(2, d),
                     sums_cb[4], sums_cb[2], sums_cb[3], sums_cb[1], fin_sums[0], sums_p[4], dwdw)
    n_small = small.shape[0]
    pad = (-n_small) % SUBLANES
    small = jnp.concatenate([small, jnp.zeros((pad, d), F32)], axis=0) if pad else small
    small_all = _all_gather8(small.reshape(-1, LANES), "gather_small_grads").reshape(N_DEV, -1, LANES)
    small_sum = _sum_devices(small_all).reshape(-1, d)
    n_dm = depth * n_ada
    g_ada_b = small_sum[0:n_dm].reshape(depth, n_ada * d)
    g_norm_mix = small_sum[n_dm:n_dm + 2]
    g_norm_ffn = small_sum[n_dm + 2:n_dm + 4]
    g_b1 = small_sum[n_dm + 4:n_dm + 6].reshape(1, 2 * d)
    g_bdw, g_lng, g_lnb, g_b2 = (small_sum[n_dm + 6 + k].reshape(1, d) for k in range(4))
    g_final = small_sum[n_dm + 10]
    g_pls = lax.dynamic_slice_in_dim(small_sum[n_dm + 11].reshape(1, d), myq * cq, cq, axis=1)
    g_wdw = lax.dynamic_slice_in_dim(small_sum[n_dm + 12:n_dm + 12 + CONV_WIDTH], myq * cq, cq, axis=1).reshape(conv_wdw.shape)
    dmod_all = small_all.reshape(N_DEV, -1, d)[:, 0:n_dm].reshape(N_DEV, depth, n_ada * d)
    dmod_cols = lax.dynamic_slice_in_dim(jnp.transpose(dmod_all, (1, 0, 2)), myq * ada_cols, ada_cols, axis=2)
    g_ada_w = _ada_backward(c_all.T, dmod_cols)

    def small_pack(*vs):
        return jnp.concatenate([v.reshape(-1) for v in vs]).reshape(-1, LANES)

    names = ("ada_b", "norm_mix_g", "norm_ffn_g", "conv_b1", "conv_bdw", "conv_ln_g", "conv_ln_b", "conv_b2", "final_g")
    small_w = (ada_b, norm_mix_g, norm_ffn_g, conv_b1, conv_bdw, conv_ln_g, conv_ln_b, conv_b2, final_g)
    small_g = (g_ada_b, g_norm_mix, g_norm_ffn, g_b1, g_bdw, g_lng, g_lnb, g_b2, g_final)
    small_m = (m_ada_b, m_norm_mix_g, m_norm_ffn_g, m_conv_b1, m_conv_bdw, m_conv_ln_g, m_conv_ln_b, m_conv_b2, m_final_g)
    small_v = (v_ada_b, v_norm_mix_g, v_norm_ffn_g, v_conv_b1, v_conv_bdw, v_conv_ln_g, v_conv_ln_b, v_conv_b2, v_final_g)
    sd, sm, sv = _adamw(small_pack(*small_w), small_pack(*small_g), small_pack(*small_m), small_pack(*small_v), "adamw_small")

    def unpack(flat2d):
        flat = flat2d.reshape(-1)
        out, off = {}, 0
        for nm, wv in zip(names, small_w):
            out[nm] = flat[off:off + wv.size].reshape(wv.shape)
            off += wv.size
        return out

    sd, sm, sv = unpack(sd), unpack(sm), unpack(sv)
    grads = {
        "ada_w": g_ada_w, "ada_b": g_ada_b, "norm_mix_g": g_norm_mix, "norm_ffn_g": g_norm_ffn,
        "conv_b1": g_b1, "conv_wdw": g_wdw, "conv_bdw": g_bdw, "conv_ln_g": g_lng,
        "conv_ln_b": g_lnb, "conv_b2": g_b2, "pool_ls": g_pls, "final_g": g_final,
    }
    plain = {
        "ada_w": (ada_w, m_ada_w, v_ada_w), "conv_wdw": (conv_wdw, m_conv_wdw, v_conv_wdw),
        "pool_ls": (pool_ls, m_pool_ls, v_pool_ls),
    }
    from_slots = {
        "ffn_w_gate": (ffn_w_gate, m_ffn_w_gate, v_ffn_w_gate,
                       lambda: [(slots_l0[0], cs_l0[0], 0, d), (slots_l1[0], cs_l1[0], 0, d)]),
        "ffn_w_up": (ffn_w_up, m_ffn_w_up, v_ffn_w_up,
                     lambda: [(slots_l0[0], cs_l0[0], d, d), (slots_l1[0], cs_l1[0], d, d)]),
        "ffn_w_down": (ffn_w_down, m_ffn_w_down, v_ffn_w_down,
                       lambda: [(slots_l0[1], cs_l0[1], 0, fq), (slots_l1[1], cs_l1[1], 0, fq)]),
        "pool_w": (pool_w, m_pool_w, v_pool_w, lambda: [(slots_l0[2], cs_l0[2], 0, ng * gq)]),
        "conv_w2": (conv_w2, m_conv_w2, v_conv_w2, lambda: [(slots_l0[3], cs_l0[3], 0, d // N_CHIPS)]),
        "conv_w1": (conv_w1, m_conv_w1, v_conv_w1, lambda: [(slots_conv[0], cs_conv[0], 0, d)]),
    }
    order = ("ada_w", "ada_b", "norm_mix_g", "norm_ffn_g", "conv_w1", "conv_b1", "conv_wdw", "conv_bdw", "conv_ln_g",
             "conv_ln_b", "conv_w2", "conv_b2", "pool_w", "pool_ls", "ffn_w_gate", "ffn_w_up", "ffn_w_down", "final_g")
    delta, new_m, new_v = {}, {}, {}
    for nm, (wv, mv, vv, segs) in from_slots.items():
        two_d = (-1, wv.shape[-1])
        carry = (_Exchange("reduce", cs_conv), cs_conv) if nm == "ffn_w_gate" else None
        outs, brought = _adamw_from_slots(wv.reshape(two_d), mv.reshape(two_d), vv.reshape(two_d), segs(), pos,
                                          "adamw_" + nm, carry=carry)
        if carry is not None:
            slots_conv = brought
        grads[nm], delta[nm], new_m[nm], new_v[nm] = (o.reshape(wv.shape) for o in outs)
    for nm in order:
        if nm in from_slots:
            continue
        elif nm in plain:
            wv, mv, vv = plain[nm]
            grads[nm] = grads[nm].reshape(wv.shape)
            delta[nm], new_m[nm], new_v[nm] = _adamw_nd(wv, grads[nm], mv, vv, "adamw_" + nm)
        else:
            delta[nm], new_m[nm], new_v[nm] = sd[nm], sm[nm], sv[nm]
            grads[nm] = grads[nm].reshape(dict(zip(names, small_w))[nm].shape)

    return (loss, dx0.reshape(x.shape), *[grads[n] for n in order], *[delta[n] for n in order],
            *[new_m[n] for n in order], *[new_v[n] for n in order])
```

```python
import functools

import jax
import jax.numpy as jnp
from jax import lax
from jax.experimental import pallas as pl
from jax.experimental.pallas import tpu as pltpu

F32 = jnp.float32
_LOWP = jnp.bfloat16
EPS = 1e-6
CONV_WIDTH = 31
POOL_WINDOWS = (2, 4, 8, 16)
ADAM_LR = 0.001
ADAM_B1 = 0.9
ADAM_B2 = 0.999
ADAM_EPS = 1e-08
ADAM_WD = 0.01
ADAM_STEP = 10

N_CHIPS = 4
N_DEV = 8
LANES = 128
SUBLANES = 8
CONV_HALO = 32
POOL_HALO = 16
TILE_ROWS = 64
VMEM_LIMIT = 56 * 1024 * 1024
ADAM_BLOCK_BYTES = 1 << 20
WGRAD_ROWS = 4096
MESH = pl.DeviceIdType.MESH
SDS = jax.ShapeDtypeStruct


def _cparams(*sem):
    return pltpu.CompilerParams(dimension_semantics=sem if sem else None, vmem_limit_bytes=VMEM_LIMIT)


def _row_block(s):
    for tm in (512, 256, 128, 64, 32, 16):
        if s % tm == 0:
            return tm
    raise ValueError(f"sequence length {s} must be a multiple of 16")


def _rows_spec(tm, ncols):
    return pl.BlockSpec((tm, ncols), lambda i, *_: (i, 0))


def _const_spec(shape):
    nd = len(shape)
    return pl.BlockSpec(shape, lambda *_: (0,) * nd)


def _dot(a, b):
    return lax.dot_general(a.astype(_LOWP), b.astype(_LOWP), (((1,), (0,)), ((), ())), preferred_element_type=F32)


def _dot_nt(a, b):
    return lax.dot_general(a.astype(_LOWP), b.astype(_LOWP), (((1,), (1,)), ((), ())), preferred_element_type=F32)


def _dot_tn(a, b):
    return lax.dot_general(a.astype(_LOWP), b.astype(_LOWP), (((0,), (0,)), ((), ())), preferred_element_type=F32)


def _rms(x, gamma):
    r = lax.rsqrt(jnp.mean(x * x, axis=-1, keepdims=True) + EPS)
    xh = x * r
    return xh, r, xh * gamma


def _rms_bwd(dn, xh, r, gamma):
    dxh = dn * gamma
    return r * (dxh - xh * jnp.mean(dxh * xh, axis=-1, keepdims=True))


def _colsum(v):
    return jnp.sum(v, axis=0, keepdims=True)


def _sigmoid(v):
    return jax.nn.sigmoid(v)


def _dsilu(v):
    sg = _sigmoid(v)
    return sg * (1.0 + v * (1.0 - sg))


def _for_tiles(nrows, col0, ncols, fn):
    def step(t, carry):
        r0 = pl.multiple_of(t * TILE_ROWS, TILE_ROWS)
        for col in range(col0, col0 + ncols, LANES):
            fn(r0, col)
        return carry

    lax.fori_loop(0, nrows // TILE_ROWS, step, 0)


def _shift_up(win, b):
    return pltpu.roll(win, win.shape[0] - b, 0) if b else win


def _shift_down(win, b):
    return pltpu.roll(win, b, 0) if b else win


def _tiles3(v):
    return v.reshape(v.shape[0] // SUBLANES, SUBLANES, v.shape[1])


def _resident_scratch(arrays):
    return [pltpu.VMEM(a.shape, a.dtype) for a in arrays] + [pltpu.SemaphoreType.DMA((len(arrays),))]


def _load_resident(hbm_refs, vmem_refs, sems):
    copies = [pltpu.make_async_copy(src, dst, sems.at[k]) for k, (src, dst) in enumerate(zip(hbm_refs, vmem_refs))]
    for cp in copies:
        cp.start()
    for cp in copies:
        cp.wait()


def _mesh_pos():
    return lax.axis_index("x"), lax.axis_index("y"), lax.axis_index("c")


def _other_chips(x, y):
    return [(1 - x, y), (x, 1 - y), (1 - x, 1 - y)]


def _all_gather8(v, name):
    m, n = v.shape

    def body(x_ref, out_ref, send_sems, recv_sems, local_sem):
        x, y, c = _mesh_pos()
        me, sibling = (x, y, c), (x, y, 1 - c)
        chips = _other_chips(x, y)

        def rows(px, py, pc):
            return out_ref.at[pl.ds((4 * px + 2 * py + pc) * m, m), :]

        def copy(k, block, to, src=None):
            return pltpu.make_async_remote_copy(
                src_ref=rows(*block) if src is None else src, dst_ref=rows(*block),
                send_sem=send_sems.at[k], recv_sem=recv_sems.at[k], device_id=to, device_id_type=MESH)

        mine = pltpu.make_async_copy(x_ref, rows(*me), local_sem)
        mine.start()
        first = [copy(0, me, sibling, src=x_ref)]
        first += [copy(1 + j, me, (*chip, c), src=x_ref) for j, chip in enumerate(chips)]
        for cp in first:
            cp.start()
        passed = [copy(4 + j, (*chip, c), sibling) for j, chip in enumerate(chips)]
        for j, chip in enumerate(chips):
            copy(1 + j, (*chip, c), me).wait_recv()
            passed[j].start()
        copy(0, sibling, me).wait_recv()
        for j, chip in enumerate(chips):
            copy(4 + j, (*chip, 1 - c), me).wait_recv()
        for cp in first + passed:
            cp.wait_send()
        mine.wait()

    return pl.pallas_call(
        body, name=name,
        out_shape=SDS((N_DEV * m, n), v.dtype),
        in_specs=[pl.BlockSpec(memory_space=pltpu.VMEM)],
        out_specs=pl.BlockSpec(memory_space=pltpu.VMEM),
        scratch_shapes=[pltpu.SemaphoreType.DMA((7,)), pltpu.SemaphoreType.DMA((7,)), pltpu.SemaphoreType.DMA],
    )(v)


class _Exchange:
    N_SEMS = 7

    def __init__(self, kind, arrays):
        self.kind, self.n = kind, len(arrays)
        self.streams = 3
        if kind == "gather":
            self.out_shape = [SDS((N_CHIPS,) + a.shape, a.dtype) for a in arrays]
        else:
            self.out_shape = [SDS((N_CHIPS, 2 * a.shape[1], a.shape[2]), a.dtype) for a in arrays]
        self.scratch = [pltpu.SemaphoreType.DMA((self.n, self.N_SEMS)), pltpu.SemaphoreType.DMA((self.n, self.N_SEMS))]

    def _ctx(self, ins, outs, send_sems, recv_sems):
        x, y, c = _mesh_pos()
        myq = 2 * x + y
        sibling = (x, y, 1 - c)
        chips = _other_chips(x, y)
        gather = self.kind == "gather"
        ns = self.streams

        def rows(a, h):
            hr = outs[a].shape[1] // 2
            return pl.ds(h * hr, hr)

        def copy(a, ks, kr, src, dst, to):
            return pltpu.make_async_remote_copy(src_ref=src, dst_ref=dst, send_sem=send_sems.at[a, ks],
                                                recv_sem=recv_sems.at[a, kr], device_id=to, device_id_type=MESH)

        def place(a, t, h):
            if gather:
                px, py = chips[t]
                return outs[a].at[2 * px + py, rows(a, h)]
            return outs[a].at[t, rows(a, h)]

        def first(a, t):
            if gather:
                return copy(a, t, t, ins[a].at[rows(a, c)], outs[a].at[myq, rows(a, c)], (*chips[t], c))
            px, py = chips[t]
            return copy(a, t, t, ins[a].at[2 * px + py], outs[a].at[t, rows(a, c)], (px, py, c))

        def landed(a, t):
            return copy(a, t, t, place(a, t, c), place(a, t, c), sibling)

        def relay(a, t):
            return copy(a, ns + t, ns + t, place(a, t, c), place(a, t, c), sibling)

        def relayed(a, t):
            return copy(a, ns + t, ns + t, place(a, t, 1 - c), place(a, t, 1 - c), sibling)

        def own(a):
            return copy(a, 2 * ns, 2 * ns, ins[a].at[myq], outs[a].at[ns, rows(a, c)], sibling)

        return first, landed, relay, relayed, own

    def start(self, ins, outs, send_sems, recv_sems):
        first, _, _, _, own = self._ctx(ins, outs, send_sems, recv_sems)
        for a in range(self.n):
            for t in range(self.streams):
                first(a, t).start()
            if self.kind == "reduce":
                own(a).start()

    def relay(self, ins, outs, send_sems, recv_sems):
        _, landed, relay, _, _ = self._ctx(ins, outs, send_sems, recv_sems)
        for a in range(self.n):
            for t in range(self.streams):
                landed(a, t).wait_recv()
                relay(a, t).start()

    def finish(self, ins, outs, send_sems, recv_sems):
        first, _, relay, relayed, own = self._ctx(ins, outs, send_sems, recv_sems)
        for a in range(self.n):
            for t in range(self.streams):
                relayed(a, t).wait_recv()
            if self.kind == "reduce":
                own(a).wait_recv()
        for a in range(self.n):
            for t in range(self.streams):
                first(a, t).wait_send()
                relay(a, t).wait_send()
            if self.kind == "reduce":
                own(a).wait_send()

    def alone(self, arrays, name):
        n = self.n

        def body(*refs):
            args = (refs[:n], refs[n:2 * n], refs[2 * n], refs[2 * n + 1])
            self.start(*args)
            self.relay(*args)
            self.finish(*args)

        any_spec = pl.BlockSpec(memory_space=pl.ANY)
        return pl.pallas_call(body, name=name, out_shape=self.out_shape, in_specs=[any_spec] * n,
                              out_specs=[any_spec] * n, scratch_shapes=self.scratch)(*arrays)


RELAY_AT = (5, 8)


def _call(body, *args, carry=None, prefetch=None, name, grid, in_specs, out_specs, out_shape, scratch_shapes=(), sem):
    def run(fn, in_specs_, out_specs_, out_shape_, scratch_, sem_, operands):
        if prefetch is None:
            return pl.pallas_call(fn, name=name, grid=grid, in_specs=in_specs_, out_specs=out_specs_, out_shape=out_shape_,
                                  scratch_shapes=scratch_, compiler_params=_cparams(*sem_))(*operands)
        grid_spec = pltpu.PrefetchScalarGridSpec(num_scalar_prefetch=1, grid=grid, in_specs=in_specs_,
                                                 out_specs=out_specs_, scratch_shapes=scratch_)
        return pl.pallas_call(fn, name=name, grid_spec=grid_spec, out_shape=out_shape_,
                              compiler_params=_cparams(*sem_))(prefetch, *operands)

    if carry is None:
        return run(body, list(in_specs), list(out_specs), list(out_shape), list(scratch_shapes), sem, args), None
    ex, arrays = carry
    n_in, n_out, n_scr, n = len(in_specs), len(out_specs), len(scratch_shapes), ex.n
    n_pre = 0 if prefetch is None else 1
    total = 1
    for g in grid:
        total *= g
    relay_step = (total * RELAY_AT[0]) // RELAY_AT[1]

    def hosted(*refs):
        pre, refs = refs[:n_pre], refs[n_pre:]
        ins, cins = refs[:n_in], refs[n_in:n_in + n]
        outs, couts = refs[n_in + n:n_in + n + n_out], refs[n_in + n + n_out:n_in + 2 * n + n_out]
        scr = refs[n_in + 2 * n + n_out:n_in + 2 * n + n_out + n_scr]
        sems = refs[n_in + 2 * n + n_out + n_scr:]
        step = 0
        for ax, g in enumerate(grid):
            step = step * g + pl.program_id(ax)

        @pl.when(step == 0)
        def _():
            ex.start(cins, couts, *sems)

        body(*pre, *ins, *outs, *scr)

        @pl.when(step == relay_step)
        def _():
            ex.relay(cins, couts, *sems)

        @pl.when(step == total - 1)
        def _():
            ex.finish(cins, couts, *sems)

    any_spec = pl.BlockSpec(memory_space=pl.ANY)
    res = run(hosted, list(in_specs) + [any_spec] * n, list(out_specs) + [any_spec] * n, list(out_shape) + ex.out_shape,
              list(scratch_shapes) + ex.scratch, ("arbitrary",) * len(grid), (*args, *arrays))
    return res[:n_out], res[n_out:]


class _SiblingSwap:
    def __init__(self, arrays):
        self.n = len(arrays)
        self.out_shape = [SDS((a.shape[0], a.shape[1] // 2, a.shape[2]), a.dtype) for a in arrays]
        self.scratch = [pltpu.SemaphoreType.DMA((self.n,)), pltpu.SemaphoreType.DMA((self.n,))]

    def _copies(self, ins, outs, send_sems, recv_sems):
        x, y, c = _mesh_pos()
        cps = []
        for a in range(self.n):
            hr = ins[a].shape[1] // 2
            cps.append(pltpu.make_async_remote_copy(
                src_ref=ins[a].at[:, pl.ds((1 - c) * hr, hr)], dst_ref=outs[a],
                send_sem=send_sems.at[a], recv_sem=recv_sems.at[a], device_id=(x, y, 1 - c), device_id_type=MESH))
        return cps

    def start(self, *refs):
        for cp in self._copies(*refs):
            cp.start()

    def relay(self, *refs):
        pass

    def finish(self, *refs):
        for cp in self._copies(*refs):
            cp.wait()

    def alone(self, arrays, name):
        return _Exchange.alone(self, arrays, name)


def _divisor_block(rows, cands):
    for rb in cands:
        if rows % rb == 0:
            return rb
    raise ValueError(rows)


def _add_sibling_half(p, rb1, pos, name):
    nq, r, n = p.shape
    h = r // 2
    rb = _divisor_block(h, (512, 352, 256, 176, 128, 64, 32, 16))
    nb = h // rb

    def body(pos_ref, p_ref, r_ref, o_ref):
        o_ref[...] = (p_ref[...].astype(F32) + r_ref[...].astype(F32)).astype(o_ref.dtype)

    grid_spec = pltpu.PrefetchScalarGridSpec(
        num_scalar_prefetch=1, grid=(nq, nb),
        in_specs=[pl.BlockSpec((1, rb, n), lambda q, i, pos: (q, pos[0] * nb + i, 0)),
                  pl.BlockSpec((1, rb, n), lambda q, i, pos: (q, i, 0))],
        out_specs=pl.BlockSpec((1, rb, n), lambda q, i, pos: (q, i, 0)))
    return pl.pallas_call(
        body, name=name, grid_spec=grid_spec, out_shape=SDS((nq, h, n), p.dtype),
        compiler_params=_cparams("parallel", "parallel"),
    )(pos, p, rb1)


def _sum_devices(g):
    nd, m, n = g.shape

    def body(g_ref, o_ref):
        acc = g_ref[0]
        for d in range(1, nd):
            acc = acc + g_ref[d]
        o_ref[...] = acc

    return pl.pallas_call(body, name="small_grad_sum", out_shape=SDS((m, n), F32))(g)


def _ada_forward(c_all, ada_w, ada_b_cols, carry=None):
    nl, d, ncol = ada_w.shape
    nb = c_all.shape[0]
    tn = 512 if ncol % 512 == 0 else ncol

    def body(c_ref, w_ref, b_ref, o_ref):
        cv = c_ref[...]
        ca = cv * _sigmoid(cv)
        o_ref[0] = _dot(ca, w_ref[0]) + b_ref[0]

    return _call(
        body, c_all, ada_w, ada_b_cols, carry=carry, name="ada_forward", grid=(nl, ncol // tn),
        in_specs=[pl.BlockSpec((nb, d), lambda l, j: (0, 0)),
                  pl.BlockSpec((1, d, tn), lambda l, j: (l, 0, j)),
                  pl.BlockSpec((1, 1, tn), lambda l, j: (l, 0, j))],
        out_specs=[pl.BlockSpec((1, nb, tn), lambda l, j: (l, 0, j))],
        out_shape=[SDS((nl, nb, ncol), F32)], sem=("parallel", "parallel"))


def _ada_backward(c_all_t, dmod_cols):
    d, nb = c_all_t.shape
    nl, _, ncol = dmod_cols.shape
    tn = 512 if ncol % 512 == 0 else ncol

    def body(c_ref, g_ref, o_ref):
        cv = c_ref[...]
        ca = cv * _sigmoid(cv)
        o_ref[0] = _dot(ca, g_ref[0])

    return pl.pallas_call(
        body, name="ada_backward", grid=(nl, ncol // tn),
        in_specs=[pl.BlockSpec((d, nb), lambda l, j: (0, 0)),
                  pl.BlockSpec((1, nb, tn), lambda l, j: (l, 0, j))],
        out_specs=pl.BlockSpec((1, d, tn), lambda l, j: (l, 0, j)),
        out_shape=SDS((nl, d, ncol), F32), compiler_params=_cparams("parallel", "parallel"),
    )(c_all_t, dmod_cols)


def _conv_a_fwd(x, vec, w1, b1):
    s, d = x.shape
    nq, _, hq = w1.shape
    tm = _row_block(s)

    def body(x_ref, vec_ref, w1_ref, b1_ref, h_ref, a_ref, g_ref, glu_ref):
        vec = vec_ref[...]
        _, _, n = _rms(x_ref[...], vec[0:1])
        hb = (n * (1.0 + vec[1:2]) + vec[2:3]).astype(_LOWP)
        h_ref[...] = hb
        u = [_dot(hb, w1_ref[q]) + b1_ref[:, q * hq:(q + 1) * hq] for q in range(nq)]
        for k in range(nq // 2):
            av, gv = u[k], u[nq // 2 + k]
            cols = slice(k * hq, (k + 1) * hq)
            a_ref[:, cols] = av.astype(_LOWP)
            g_ref[:, cols] = gv.astype(_LOWP)
            glu_ref[:, cols] = av * _sigmoid(gv)

    return pl.pallas_call(
        body, name="conv_a_fwd", grid=(s // tm,),
        in_specs=[_rows_spec(tm, d), _const_spec(vec.shape), _const_spec(w1.shape), _const_spec(b1.shape)],
        out_specs=[_rows_spec(tm, d)] * 4,
        out_shape=[SDS((s, d), _LOWP)] * 3 + [SDS((s, d), F32)],
        compiler_params=_cparams("parallel"),
    )(x, vec, w1, b1)


def _conv_b_fwd(glu, w8, vec, w2, x, carry=None):
    s, d = x.shape
    tm = _row_block(s)
    hb = tm // CONV_HALO

    def body(glu_ref, halo_ref, w8_ref, vec_ref, w2_ref, x_ref, cw_ref, s_ref, y_ref, xo_ref, ext_ref):
        i = pl.program_id(0)
        vec = vec_ref[...]
        ext_ref[0:CONV_HALO, :] = jnp.where(i > 0, halo_ref[...], 0.0)
        ext_ref[CONV_HALO:, :] = glu_ref[...]

        def tile(r0, c0):
            lanes = slice(c0, c0 + LANES)
            win = ext_ref[pl.ds(r0, TILE_ROWS + CONV_HALO), lanes]
            acc = jnp.zeros((TILE_ROWS // SUBLANES, SUBLANES, LANES), F32)
            for b in range(SUBLANES):
                sh = _shift_up(win, b)
                for a in range(CONV_HALO // SUBLANES + 1):
                    k = SUBLANES * a + b - (CONV_HALO - CONV_WIDTH + 1)
                    if 0 <= k < CONV_WIDTH:
                        acc = acc + w8_ref[k, :, lanes][None] * _tiles3(sh[SUBLANES * a:SUBLANES * a + TILE_ROWS])
            cw_ref[pl.ds(r0, TILE_ROWS), lanes] = acc.reshape(TILE_ROWS, LANES)

        _for_tiles(tm, 0, d, tile)
        cw = cw_ref[...] + vec[0:1]
        cw_ref[...] = cw
        cc = cw - jnp.mean(cw, axis=-1, keepdims=True)
        ch = cc * lax.rsqrt(jnp.mean(cc * cc, axis=-1, keepdims=True) + EPS)
        lo = ch * vec[1:2] + vec[2:3]
        sv = (lo * _sigmoid(lo)).astype(_LOWP)
        s_ref[...] = sv
        y = _dot(sv, w2_ref[...]) + vec[3:4]
        y_ref[...] = y.astype(_LOWP)
        xo_ref[...] = x_ref[...] + (1.0 + vec[4:5]) * y

    return _call(
        body, glu, glu, w8, vec, w2, x, carry=carry, name="conv_b_fwd", grid=(s // tm,),
        in_specs=[_rows_spec(tm, d),
                  pl.BlockSpec((CONV_HALO, d), lambda i: (jnp.maximum(i * hb - 1, 0), 0)),
                  _const_spec(w8.shape), _const_spec(vec.shape), _const_spec(w2.shape), _rows_spec(tm, d)],
        out_specs=[_rows_spec(tm, d)] * 4,
        out_shape=[SDS((s, d), F32), SDS((s, d), _LOWP), SDS((s, d), _LOWP), SDS((s, d), F32)],
        scratch_shapes=[pltpu.VMEM((tm + CONV_HALO, d), F32)], sem=("parallel",))


def _ffn_fwd(x, vec, wg, wu, wd, name, carry=None, head=None):
    s, d = x.shape
    nq, fq, _ = wg.shape
    tm = _row_block(s)
    nsteps = s // tm
    n_head = 0 if head is None else 2

    def body(x_ref, vec_ref, wg_hbm, wu_hbm, wd_hbm, *rest):
        head_refs, rest = rest[:n_head], rest[n_head:]
        h_ref, gt_ref, up_ref, act_ref, y_ref, xo_ref = rest[:6]
        sums_ref = rest[6] if head is not None else None
        hb_ref, yacc_ref, wg_ref, wu_ref, wd_ref, w_sems = rest[6 + n_head // 2:]
        i, q = pl.program_id(0), pl.program_id(1)
        vec = vec_ref[...]

        @pl.when((i == 0) & (q == 0))
        def _():
            _load_resident((wg_hbm, wu_hbm, wd_hbm), (wg_ref, wu_ref, wd_ref), w_sems)
            if head is not None:
                sums_ref[...] = jnp.zeros_like(sums_ref)

        @pl.when(q == 0)
        def _():
            _, _, n = _rms(x_ref[...], vec[0:1])
            hb = (n * (1.0 + vec[1:2]) + vec[2:3]).astype(_LOWP)
            hb_ref[...] = hb
            h_ref[...] = hb
            yacc_ref[...] = jnp.zeros_like(yacc_ref)

        hb = hb_ref[...]
        gt = _dot_nt(hb, wg_ref[q])
        up = _dot_nt(hb, wu_ref[q])
        act = (gt * _sigmoid(gt) * up).astype(_LOWP)
        gt_ref[0] = gt.astype(_LOWP)
        up_ref[0] = up.astype(_LOWP)
        act_ref[0] = act
        yacc_ref[...] += _dot(act, wd_ref[q])

        @pl.when(q == nq - 1)
        def _():
            y = yacc_ref[...]
            y_ref[...] = y.astype(_LOWP)
            xo = x_ref[...] + (1.0 + vec[3:4]) * y
            if head is None:
                xo_ref[...] = xo
            else:
                t_ref, g_ref = head_refs
                gamma_v = g_ref[...]
                xh, r, out = _rms(xo, gamma_v)
                err = out - t_ref[...]
                dout = err * (1.0 / d)
                sums_ref[0:1, :] += _colsum(dout * xh)
                sums_ref[1:2, :] += _colsum(0.5 * err * dout)
                xo_ref[...] = _rms_bwd(dout, xh, r, gamma_v)

        if head is not None:
            @pl.when((i == nsteps - 1) & (q == nq - 1))
            def _():
                sums_ref[1:2, :] = jnp.broadcast_to(jnp.sum(sums_ref[1:2, :], axis=-1, keepdims=True), (1, d))

    rows = pl.BlockSpec((tm, d), lambda i, q: (i, 0))
    hid = pl.BlockSpec((1, tm, fq), lambda i, q: (q, i, 0))
    any_spec = pl.BlockSpec(memory_space=pl.ANY)
    head_args = [] if head is None else [head[0], head[1]]
    head_specs = [] if head is None else [rows, pl.BlockSpec((1, d), lambda i, q: (0, 0))]
    sums_specs = [] if head is None else [pl.BlockSpec((2, d), lambda i, q: (0, 0))]
    sums_shape = [] if head is None else [SDS((2, d), F32)]
    return _call(
        body, x, vec, wg, wu, wd, *head_args, carry=carry, name=name, grid=(nsteps, nq),
        in_specs=[rows, pl.BlockSpec(vec.shape, lambda i, q: (0, 0)), any_spec, any_spec, any_spec] + head_specs,
        out_specs=[rows, hid, hid, hid, rows, rows] + sums_specs,
        out_shape=[SDS((s, d), _LOWP)] + [SDS((nq, s, fq), _LOWP)] * 3 + [SDS((s, d), _LOWP), SDS((s, d), F32)] + sums_shape,
        scratch_shapes=[pltpu.VMEM((tm, d), _LOWP), pltpu.VMEM((tm, d), F32)] + _resident_scratch((wg, wu, wd)),
        sem=("arbitrary", "arbitrary"))


def _pool_inv_count(t0, nrows, w):
    t = t0 + lax.broadcasted_iota(jnp.int32, (nrows, LANES), 0)
    return 1.0 / jnp.minimum(t + 1, w).astype(F32)


def _pool_fwd(x, vec, pw):
    s, d = x.shape
    ng, gd, _ = pw.shape
    tm = _row_block(s)
    hb = tm // POOL_HALO

    def body(x_ref, halo_ref, vec_ref, pw_ref, mixed_ref, yp_ref, xo_ref, hext_ref):
        i = pl.program_id(0)
        vec = vec_ref[...]

        def modulated(xv):
            _, _, n = _rms(xv, vec[0:1])
            return n * (1.0 + vec[1:2]) + vec[2:3]

        hext_ref[0:POOL_HALO, :] = jnp.where(i > 0, modulated(halo_ref[...]), 0.0)
        hext_ref[POOL_HALO:, :] = modulated(x_ref[...])

        for g in range(ng):
            def tile(r0, c0, g=g):
                lanes = slice(c0, c0 + LANES)
                win = hext_ref[pl.ds(r0, TILE_ROWS + POOL_HALO), lanes]
                acc = win
                for step in range(g + 1):
                    acc = acc + _shift_down(acc, 2 ** step)
                inv = _pool_inv_count(i * tm + r0, TILE_ROWS, POOL_WINDOWS[g])
                mixed = acc[POOL_HALO:] * inv - win[POOL_HALO:]
                mixed_ref[pl.ds(r0, TILE_ROWS), lanes] = mixed.astype(_LOWP)

            _for_tiles(tm, g * gd, gd, tile)

        for g in range(ng):
            cols = slice(g * gd, (g + 1) * gd)
            yp = _dot(mixed_ref[:, cols], pw_ref[g])
            yp_ref[:, cols] = yp.astype(_LOWP)
            xo_ref[:, cols] = x_ref[:, cols] + (1.0 + vec[3:4, cols]) * (yp * vec[4:5, cols])

    return pl.pallas_call(
        body, name="pool_fwd", grid=(s // tm,),
        in_specs=[_rows_spec(tm, d),
                  pl.BlockSpec((POOL_HALO, d), lambda i: (jnp.maximum(i * hb - 1, 0), 0)),
                  _const_spec(vec.shape), _const_spec(pw.shape)],
        out_specs=[_rows_spec(tm, d)] * 3,
        out_shape=[SDS((s, d), _LOWP), SDS((s, d), _LOWP), SDS((s, d), F32)],
        scratch_shapes=[pltpu.VMEM((tm + POOL_HALO, d), F32)],
        compiler_params=_cparams("parallel"),
    )(x, x, vec, pw)


def _ffn_bwd(dxo, x, vec, gt, up, y, wg, wu, wd, name, carry=None):
    s, d = x.shape
    nq, fq, _ = wg.shape
    tm = _row_block(s)

    def body(dxo_ref, x_ref, vec_ref, gt_ref, up_ref, y_ref, wg_hbm, wu_hbm, wd_hbm,
             dgt_ref, dup_ref, dy_ref, dx_ref, sums_ref, dyb_ref, dh_ref, wg_ref, wu_ref, wd_ref, w_sems):
        i, q = pl.program_id(0), pl.program_id(1)
        vec = vec_ref[...]

        @pl.when((i == 0) & (q == 0))
        def _():
            _load_resident((wg_hbm, wu_hbm, wd_hbm), (wg_ref, wu_ref, wd_ref), w_sems)
            sums_ref[...] = jnp.zeros_like(sums_ref)

        @pl.when(q == 0)
        def _():
            dxv = dxo_ref[...]
            dyb = (dxv * (1.0 + vec[3:4])).astype(_LOWP)
            dyb_ref[...] = dyb
            dy_ref[...] = dyb
            dh_ref[...] = jnp.zeros_like(dh_ref)
            sums_ref[3:4, :] += _colsum(dxv * y_ref[...].astype(F32))

        halves = [slice(k * (tm // 2), (k + 1) * (tm // 2)) for k in range(2)]
        dacts = [_dot_nt(dyb_ref[rs, :], wd_ref[q]) for rs in halves]
        dgus = []
        for rs, dact in zip(halves, dacts):
            gtv = gt_ref[0, rs, :].astype(F32)
            sg = _sigmoid(gtv)
            dgt = (dact * up_ref[0, rs, :].astype(F32) * (sg * (1.0 + gtv * (1.0 - sg)))).astype(_LOWP)
            dup = (dact * (gtv * sg)).astype(_LOWP)
            dgt_ref[0, rs, :] = dgt
            dup_ref[0, rs, :] = dup
            dgus.append((dgt, dup))
        for rs, (dgt, dup) in zip(halves, dgus):
            dh_ref[rs, :] += _dot(dgt, wg_ref[q]) + _dot(dup, wu_ref[q])

        @pl.when(q == nq - 1)
        def _():
            dh = dh_ref[...]
            xh, r, n = _rms(x_ref[...], vec[0:1])
            sums_ref[2:3, :] += _colsum(dh)
            sums_ref[1:2, :] += _colsum(dh * n)
            dn = dh * (1.0 + vec[1:2])
            sums_ref[0:1, :] += _colsum(dn * xh)
            dx_ref[...] = dxo_ref[...] + _rms_bwd(dn, xh, r, vec[0:1])

    rows = pl.BlockSpec((tm, d), lambda i, q: (i, 0))
    hid = pl.BlockSpec((1, tm, fq), lambda i, q: (q, i, 0))
    return _call(
        body, dxo, x, vec, gt, up, y, wg, wu, wd, carry=carry, name=name, grid=(s // tm, nq),
        in_specs=[rows, rows, pl.BlockSpec(vec.shape, lambda i, q: (0, 0)), hid, hid, rows] +
                 [pl.BlockSpec(memory_space=pl.ANY)] * 3,
        out_specs=[hid, hid, rows, rows, pl.BlockSpec((4, d), lambda i, q: (0, 0))],
        out_shape=[SDS((nq, s, fq), _LOWP)] * 2 + [SDS((s, d), _LOWP), SDS((s, d), F32), SDS((4, d), F32)],
        scratch_shapes=[pltpu.VMEM((tm, d), _LOWP), pltpu.VMEM((tm, d), F32)] + _resident_scratch((wg, wu, wd)),
        sem=("arbitrary", "arbitrary"))


def _wgrad(a, b, a_spec, b_spec, nq, ka, nb, s, name, stack=None, stack_blocks=1, block=0):
    tk = min(WGRAD_ROWS, s)
    nk = s // tk

    def body(a_ref, b_ref, *rest):
        o_ref, acc_ref = rest[-2:]
        k = pl.program_id(1)

        @pl.when(k == 0)
        def _():
            acc_ref[...] = jnp.zeros_like(acc_ref)

        av = a_ref[...].reshape(tk, ka)
        bv = b_ref[...].reshape(tk, nb)
        acc_ref[...] += _dot_tn(av, bv)

        @pl.when(k == nk - 1)
        def _():
            o_ref[0] = acc_ref[...].astype(o_ref.dtype)

    extra = [] if stack is None else [stack]
    return pl.pallas_call(
        body, name=name, grid=(nq, nk),
        in_specs=[a_spec(tk), b_spec(tk)] + [pl.BlockSpec(memory_space=pl.ANY)] * len(extra),
        out_specs=pl.BlockSpec((1, ka, nb), lambda q, k: (q, block, 0)),
        out_shape=SDS((nq, stack_blocks * ka, nb), _LOWP),
        scratch_shapes=[pltpu.VMEM((ka, nb), F32)],
        input_output_aliases={2: 0} if extra else {},
        compiler_params=_cparams("parallel", "arbitrary"),
    )(a, b, *extra)


def _shared_rows(ncols):
    return lambda tk: pl.BlockSpec((tk, ncols), lambda q, k: (k, 0))


def _column_block(ncols):
    return lambda tk: pl.BlockSpec((tk, ncols), lambda q, k: (k, q))


def _quarter_major(ncols):
    return lambda tk: pl.BlockSpec((1, tk, ncols), lambda q, k: (q, k, 0))


def _pool_bwd(dxo, x, yp, vec, pw, carry=None):
    s, d = x.shape
    ng, gd, _ = pw.shape
    tm = _row_block(s)
    hb = tm // POOL_HALO
    nsteps = s // tm
    last_halo = s // POOL_HALO - 1

    def body(dxo_ref, halo_ref, x_ref, yp_ref, vec_ref, pw_ref, dyp_ref, dx_ref, sums_ref, eext_ref, dh_ref):
        i = pl.program_id(0)
        vec = vec_ref[...]

        @pl.when(i == 0)
        def _():
            sums_ref[...] = jnp.zeros_like(sums_ref)

        dxv = dxo_ref[...]
        ypv = yp_ref[...].astype(F32)
        gate1 = 1.0 + vec[3:4]
        sums_ref[3:4, :] += _colsum(dxv * (ypv * vec[4:5]))
        sums_ref[4:5, :] += _colsum(dxv * gate1 * ypv)
        dyp = (dxv * gate1 * vec[4:5]).astype(_LOWP)
        dyp_ref[...] = dyp
        dyp_halo = (jnp.where(i < nsteps - 1, halo_ref[...], 0.0) * gate1 * vec[4:5]).astype(_LOWP)
        for g in range(ng):
            cols = slice(g * gd, (g + 1) * gd)
            dm = _dot_nt(dyp[:, cols], pw_ref[g])
            dh_ref[:, cols] = dm
            inv = _pool_inv_count(i * tm, tm, POOL_WINDOWS[g])
            eext_ref[0:tm, cols] = dm * jnp.concatenate([inv] * (gd // LANES), axis=1)
            eext_ref[tm:, cols] = _dot_nt(dyp_halo[:, cols], pw_ref[g]) * (1.0 / POOL_WINDOWS[g])

        for g in range(ng):
            def tile(r0, c0, g=g):
                acc = eext_ref[pl.ds(r0, TILE_ROWS + POOL_HALO), c0:c0 + LANES]
                for step in range(g + 1):
                    acc = acc + _shift_up(acc, 2 ** step)
                here = (pl.ds(r0, TILE_ROWS), slice(c0, c0 + LANES))
                dh_ref[here] = acc[:TILE_ROWS] - dh_ref[here]

            _for_tiles(tm, g * gd, gd, tile)

        dh = dh_ref[...]
        xh, r, n = _rms(x_ref[...], vec[0:1])
        sums_ref[2:3, :] += _colsum(dh)
        sums_ref[1:2, :] += _colsum(dh * n)
        dn = dh * (1.0 + vec[1:2])
        sums_ref[0:1, :] += _colsum(dn * xh)
        dx_ref[...] = dxv + _rms_bwd(dn, xh, r, vec[0:1])

    return _call(
        body, dxo, dxo, x, yp, vec, pw, carry=carry, name="pool_bwd", grid=(nsteps,),
        in_specs=[_rows_spec(tm, d),
                  pl.BlockSpec((POOL_HALO, d), lambda i: (jnp.minimum((i + 1) * hb, last_halo), 0)),
                  _rows_spec(tm, d), _rows_spec(tm, d), _const_spec(vec.shape), _const_spec(pw.shape)],
        out_specs=[_rows_spec(tm, d), _rows_spec(tm, d), _const_spec((5, d))],
        out_shape=[SDS((s, d), _LOWP), SDS((s, d), F32), SDS((5, d), F32)],
        scratch_shapes=[pltpu.VMEM((tm + POOL_HALO, d), F32), pltpu.VMEM((tm, d), F32)], sem=("arbitrary",))


def _conv_b_bwd(dxo, cw, y, vec, w2, carry=None):
    s, d = dxo.shape
    tm = _row_block(s)

    def body(dxo_ref, cw_ref, y_ref, vec_ref, w2_ref, dy_ref, dcw_ref, sums_ref):
        i = pl.program_id(0)
        vec = vec_ref[...]

        @pl.when(i == 0)
        def _():
            sums_ref[...] = jnp.zeros_like(sums_ref)

        dxv = dxo_ref[...]
        dy = dxv * (1.0 + vec[4:5])
        sums_ref[0:1, :] += _colsum(dxv * y_ref[...].astype(F32))
        sums_ref[1:2, :] += _colsum(dy)
        dyb = dy.astype(_LOWP)
        dy_ref[...] = dyb
        ds = _dot_nt(dyb, w2_ref[...])
        cw = cw_ref[...]
        cc = cw - jnp.mean(cw, axis=-1, keepdims=True)
        rs = lax.rsqrt(jnp.mean(cc * cc, axis=-1, keepdims=True) + EPS)
        ch = cc * rs
        lo = ch * vec[1:2] + vec[2:3]
        dlo = ds * _dsilu(lo)
        sums_ref[2:3, :] += _colsum(dlo * ch)
        sums_ref[3:4, :] += _colsum(dlo)
        dch = dlo * vec[1:2]
        dcw = rs * (dch - jnp.mean(dch, axis=-1, keepdims=True) - ch * jnp.mean(dch * ch, axis=-1, keepdims=True))
        sums_ref[4:5, :] += _colsum(dcw)
        dcw_ref[...] = dcw

    return _call(
        body, dxo, cw, y, vec, w2, carry=carry, name="conv_b_bwd", grid=(s // tm,),
        in_specs=[_rows_spec(tm, d), _rows_spec(tm, d), _rows_spec(tm, d), _const_spec(vec.shape), _const_spec(w2.shape)],
        out_specs=[_rows_spec(tm, d), _rows_spec(tm, d), _const_spec((5, d))],
        out_shape=[SDS((s, d), _LOWP), SDS((s, d), F32), SDS((5, d), F32)], sem=("arbitrary",))


def _conv_a_bwd(dcw, glu, a, g, x, dxo, vec, w8, w1, carry=None):
    s, d = x.shape
    nq, _, hq = w1.shape
    tm = _row_block(s)
    hb = tm // CONV_HALO
    nsteps = s // tm
    last_halo = s // CONV_HALO - 1
    lead = CONV_HALO - CONV_WIDTH + 1

    def body(dcw_ref, dhalo_ref, glu_ref, ghalo_ref, a_ref, g_ref, x_ref, dxo_ref, vec_ref, w8_ref, w1_ref,
             du_ref, dx_ref, sums_ref, db1_ref, dwdw_ref, dext_ref, gext_ref, dglu_ref, wacc_ref):
        i = pl.program_id(0)
        vec = vec_ref[...]

        @pl.when(i == 0)
        def _():
            sums_ref[...] = jnp.zeros_like(sums_ref)
            db1_ref[...] = jnp.zeros_like(db1_ref)
            wacc_ref[...] = jnp.zeros_like(wacc_ref)

        dext_ref[0:tm, :] = dcw_ref[...]
        dext_ref[tm:, :] = jnp.where(i < nsteps - 1, dhalo_ref[...], 0.0)
        gext_ref[0:CONV_HALO, :] = jnp.where(i > 0, ghalo_ref[...], 0.0)
        gext_ref[CONV_HALO:, :] = glu_ref[...]

        def tile(r0, c0):
            lanes = slice(c0, c0 + LANES)
            dwin = dext_ref[pl.ds(r0, TILE_ROWS + CONV_HALO), lanes]
            acc = jnp.zeros((TILE_ROWS // SUBLANES, SUBLANES, LANES), F32)
            for b in range(SUBLANES):
                sh = _shift_up(dwin, b)
                for a8 in range(CONV_HALO // SUBLANES):
                    k = CONV_WIDTH - 1 - (SUBLANES * a8 + b)
                    if 0 <= k < CONV_WIDTH:
                        acc = acc + w8_ref[k, :, lanes][None] * _tiles3(sh[SUBLANES * a8:SUBLANES * a8 + TILE_ROWS])
            dglu_ref[pl.ds(r0, TILE_ROWS), lanes] = acc.reshape(TILE_ROWS, LANES)

            dc3 = _tiles3(dwin[:TILE_ROWS])
            gwin = gext_ref[pl.ds(r0, TILE_ROWS + CONV_HALO), lanes]
            for b in range(SUBLANES):
                sh = _shift_up(gwin, b)
                for a8 in range(CONV_HALO // SUBLANES + 1):
                    k = SUBLANES * a8 + b - lead
                    if 0 <= k < CONV_WIDTH:
                        part = jnp.sum(dc3 * _tiles3(sh[SUBLANES * a8:SUBLANES * a8 + TILE_ROWS]), axis=0)
                        wacc_ref[k, :, lanes] += part

        _for_tiles(tm, 0, d, tile)

        dglu = dglu_ref[...]
        av, gv = a_ref[...].astype(F32), g_ref[...].astype(F32)
        sg = _sigmoid(gv)
        da = dglu * sg
        dg = dglu * av * sg * (1.0 - sg)
        du_ref[:, 0:d] = da.astype(_LOWP)
        du_ref[:, d:2 * d] = dg.astype(_LOWP)
        db1_ref[:, 0:d] += _colsum(da)
        db1_ref[:, d:2 * d] += _colsum(dg)
        dh = jnp.zeros((tm, d), F32)
        for q in range(nq):
            dh = dh + _dot_nt(du_ref[:, q * hq:(q + 1) * hq], w1_ref[q])
        xh, r, n = _rms(x_ref[...], vec[0:1])
        sums_ref[2:3, :] += _colsum(dh)
        sums_ref[1:2, :] += _colsum(dh * n)
        dn = dh * (1.0 + vec[1:2])
        sums_ref[0:1, :] += _colsum(dn * xh)
        dx_ref[...] = dxo_ref[...] + _rms_bwd(dn, xh, r, vec[0:1])

        @pl.when(i == nsteps - 1)
        def _():
            dwdw_ref[...] = jnp.sum(wacc_ref[...], axis=1)

    return _call(
        body, dcw, dcw, glu, glu, a, g, x, dxo, vec, w8, w1, carry=carry, name="conv_a_bwd", grid=(nsteps,),
        in_specs=[_rows_spec(tm, d),
                  pl.BlockSpec((CONV_HALO, d), lambda i: (jnp.minimum((i + 1) * hb, last_halo), 0)),
                  _rows_spec(tm, d),
                  pl.BlockSpec((CONV_HALO, d), lambda i: (jnp.maximum(i * hb - 1, 0), 0)),
                  _rows_spec(tm, d), _rows_spec(tm, d), _rows_spec(tm, d), _rows_spec(tm, d),
                  _const_spec(vec.shape), _const_spec(w8.shape), _const_spec(w1.shape)],
        out_specs=[_rows_spec(tm, 2 * d), _rows_spec(tm, d), _const_spec((3, d)), _const_spec((1, 2 * d)),
                   _const_spec((CONV_WIDTH, d))],
        out_shape=[SDS((s, 2 * d), _LOWP), SDS((s, d), F32), SDS((3, d), F32), SDS((1, 2 * d), F32),
                   SDS((CONV_WIDTH, d), F32)],
        scratch_shapes=[pltpu.VMEM((tm + CONV_HALO, d), F32), pltpu.VMEM((tm + CONV_HALO, d), F32),
                        pltpu.VMEM((tm, d), F32), pltpu.VMEM((CONV_WIDTH, SUBLANES, d), F32)],
        sem=("arbitrary",))


def _adamw(w, g, m, v, name):
    r, c = w.shape
    rb = r
    for cand in (512, 352, 256, 184, 128, 64, 32, 16, 8):
        if r % cand == 0 and cand * c * 4 <= ADAM_BLOCK_BYTES:
            rb = cand
            break

    def body(w_ref, g_ref, m_ref, v_ref, d_ref, mo_ref, vo_ref):
        gv = g_ref[...]
        mn = ADAM_B1 * m_ref[...] + (1.0 - ADAM_B1) * gv
        vn = ADAM_B2 * v_ref[...] + (1.0 - ADAM_B2) * (gv * gv)
        m_hat = mn / (1.0 - ADAM_B1 ** ADAM_STEP)
        v_hat = vn / (1.0 - ADAM_B2 ** ADAM_STEP)
        d_ref[...] = -ADAM_LR * (m_hat / (jnp.sqrt(v_hat) + ADAM_EPS) + ADAM_WD * w_ref[...])
        mo_ref[...] = mn
        vo_ref[...] = vn

    spec = pl.BlockSpec((rb, c), lambda i: (i, 0))
    return pl.pallas_call(
        body, name=name, grid=(r // rb,), in_specs=[spec] * 4, out_specs=[spec] * 3,
        out_shape=[SDS((r, c), F32)] * 3, compiler_params=_cparams("parallel"),
    )(w, g, m, v)


def _adamw_nd(w, g, m, v, name):
    shape = w.shape
    two_d = (-1, shape[-1])
    outs = _adamw(w.reshape(two_d), g.reshape(two_d), m.reshape(two_d), v.reshape(two_d), name)
    return [o.reshape(shape) for o in outs]


def _adamw_from_slots(w, m, v, segs, pos, name, carry=None):
    r, c_ = w.shape
    rb = _divisor_block(r, (256, 176, 128, 64, 32, 16))
    for slots, _, first, rows in segs:
        half = slots.shape[1] // 2
        while first % rb or rows % rb or half % rb:
            rb //= 2
    assert rb % 16 == 0, (name, rb)
    plan, start = [], 0
    for slots, _, first, rows in segs:
        plan.append((start, rows // rb, first // rb, slots.shape[1] // 2 // rb))
        start += rows // rb
    assert start * rb == r, (name, start, rb, r)

    def body(pos_ref, w_ref, m_ref, v_ref, *rest):
        seg_refs, (g_ref, d_ref, mo_ref, vo_ref) = rest[:2 * len(segs)], rest[2 * len(segs):]
        i = pl.program_id(0)
        for k, (s0, nblk, b0, nbh) in enumerate(plan):
            @pl.when((i >= s0) & (i < s0 + nblk))
            def _(k=k, s0=s0, b0=b0, nbh=nbh):
                slots = seg_refs[2 * k][...].astype(F32)
                blk = b0 + i - s0
                in_my_half = (blk >= pos_ref[0] * nbh) & (blk < (pos_ref[0] + 1) * nbh)
                own = jnp.where(in_my_half, seg_refs[2 * k + 1][0].astype(F32), slots[3])
                gv = ((own + slots[0]) + slots[1]) + slots[2]
                mn = ADAM_B1 * m_ref[...] + (1.0 - ADAM_B1) * gv
                vn = ADAM_B2 * v_ref[...] + (1.0 - ADAM_B2) * (gv * gv)
                m_hat = mn / (1.0 - ADAM_B1 ** ADAM_STEP)
                v_hat = vn / (1.0 - ADAM_B2 ** ADAM_STEP)
                g_ref[...] = gv
                d_ref[...] = -ADAM_LR * (m_hat / (jnp.sqrt(v_hat) + ADAM_EPS) + ADAM_WD * w_ref[...])
                mo_ref[...] = mn
                vo_ref[...] = vn

    spec = pl.BlockSpec((rb, c_), lambda i, pos: (i, 0))
    seg_specs, seg_args = [], []
    for (slots, mine, _, _), (s0, nblk, b0, nbh) in zip(segs, plan):
        def in_seg(i, s0=s0, nblk=nblk):
            return jnp.clip(i - s0, 0, nblk - 1)
        seg_specs.append(pl.BlockSpec((N_CHIPS, rb, c_), lambda i, pos, f=in_seg, b0=b0: (0, b0 + f(i), 0)))
        seg_specs.append(pl.BlockSpec(
            (1, rb, c_), lambda i, pos, f=in_seg, b0=b0, nbh=nbh: (pos[1], jnp.clip(b0 + f(i) - pos[0] * nbh, 0, nbh - 1), 0)))
        seg_args += [slots, mine]
    return _call(body, w, m, v, *seg_args, carry=carry, prefetch=pos, name=name, grid=(r // rb,),
                 in_specs=[spec] * 3 + seg_specs, out_specs=[spec] * 4, out_shape=[SDS((r, c_), F32)] * 4,
                 sem=("arbitrary",))


def _rows_of(*vs):
    return jnp.concatenate([v.reshape(-1, v.shape[-1]) for v in vs], axis=0)


def kernel(x, c, ada_w, ada_b, norm_mix_g, norm_ffn_g, conv_w1, conv_b1, conv_wdw, conv_bdw, conv_ln_g, conv_ln_b, conv_w2, conv_b2, pool_w, pool_ls, ffn_w_gate, ffn_w_up, ffn_w_down, final_g, loss_target, m_ada_w, m_ada_b, m_norm_mix_g, m_norm_ffn_g, m_conv_w1, m_conv_b1, m_conv_wdw, m_conv_bdw, m_conv_ln_g, m_conv_ln_b, m_conv_w2, m_conv_b2, m_pool_w, m_pool_ls, m_ffn_w_gate, m_ffn_w_up, m_ffn_w_down, m_final_g, v_ada_w, v_ada_b, v_norm_mix_g, v_norm_ffn_g, v_conv_w1, v_conv_b1, v_conv_wdw, v_conv_bdw, v_conv_ln_g, v_conv_ln_b, v_conv_w2, v_conv_b2, v_pool_w, v_pool_ls, v_ffn_w_gate, v_ffn_w_up, v_ffn_w_down, v_final_g):
    s, d = x.shape[1], x.shape[2]
    depth = ada_w.shape[0]
    assert depth == 2 and conv_w1.shape[0] == 1 and pool_w.shape[0] == 1, "one conv layer then one pool layer"
    fq = ffn_w_gate.shape[2]
    n_ada = ada_b.shape[1] // d
    ada_cols = ada_w.shape[2]
    ng, gq, gd = pool_w.shape[1], pool_w.shape[2], pool_w.shape[3]
    cq = conv_wdw.shape[2]
    ax, ay, ac = _mesh_pos()
    myq = 2 * ax + ay
    me = 4 * ax + 2 * ay + ac
    x0 = x.reshape(s, d)
    tgt = loss_target.reshape(s, d)

    n_c = d // LANES
    small_in = jnp.concatenate(
        [c.reshape(n_c, LANES), conv_wdw[0].reshape(-1, LANES), pool_ls.reshape(-1, LANES)], axis=0)
    small_in_all = _all_gather8(small_in, "gather_c_and_small_shards").reshape(N_DEV, -1, LANES)
    c_all = small_in_all[:, :n_c].reshape(N_DEV, d)
    ada_b_cols = lax.dynamic_slice_in_dim(ada_b, myq * ada_cols, ada_cols, axis=1).reshape(depth, 1, ada_cols)
    conv_shards = [conv_w1[0].astype(_LOWP), conv_w2[0].astype(_LOWP)]
    (mod_part,), conv_w = _ada_forward(c_all, ada_w, ada_b_cols, carry=(_Exchange("gather", conv_shards), conv_shards))
    mod_all = _all_gather8(mod_part.reshape(-1, LANES), "gather_mod").reshape(N_DEV, depth, N_DEV, ada_cols)
    mod_mine = lax.dynamic_index_in_dim(mod_all[0::2], me, axis=2, keepdims=False)
    mod = jnp.transpose(mod_mine, (1, 0, 2)).reshape(depth, n_ada, d)

    def gathered(ex_out, shards):
        return [lax.dynamic_update_slice(g, sh[None], (myq, 0, 0)) for g, sh in zip(ex_out, shards)]

    pos = jnp.stack([ac, myq]).astype(jnp.int32)
    gate_t, up_t = jnp.swapaxes(ffn_w_gate, 1, 2), jnp.swapaxes(ffn_w_up, 1, 2)
    ffn0_shards = [gate_t[0].astype(_LOWP), up_t[0].astype(_LOWP), ffn_w_down[0].astype(_LOWP)]
    l1_shards = [pool_w[0].reshape(ng * gq, gd).astype(_LOWP), gate_t[1].astype(_LOWP), up_t[1].astype(_LOWP),
                 ffn_w_down[1].astype(_LOWP)]
    w1, w2_all = gathered(conv_w, conv_shards)
    w2 = w2_all.reshape(d, d)
    wdw_full_shards = small_in_all[0::2, n_c:].reshape(N_CHIPS, CONV_WIDTH + 1, cq)
    wdw = jnp.transpose(wdw_full_shards[:, :CONV_WIDTH], (1, 0, 2)).reshape(CONV_WIDTH, d)
    pls = wdw_full_shards[:, CONV_WIDTH].reshape(1, d)
    w8 = jnp.broadcast_to(wdw[:, None, :], (CONV_WIDTH, SUBLANES, d))

    vec_ca = _rows_of(norm_mix_g[0], mod[0, 1], mod[0, 0])
    h0, a0, g0, glu = _conv_a_fwd(x0, vec_ca, w1, conv_b1)
    vec_cb = _rows_of(conv_bdw, conv_ln_g, conv_ln_b, conv_b2, mod[0, 2])
    (cw, s0, y0, x1), ffn0_w = _conv_b_fwd(glu, w8, vec_cb, w2, x0, carry=(_Exchange("gather", ffn0_shards), ffn0_shards))
    wg0, wu0, wd0 = gathered(ffn0_w, ffn0_shards)
    vec_f0 = _rows_of(norm_ffn_g[0], mod[0, 4], mod[0, 3], mod[0, 5])
    (h2a, gta, upa, acta, y2a, x2), l1_w = _ffn_fwd(x1, vec_f0, wg0, wu0, wd0, "ffn0_fwd",
                                                   carry=(_Exchange("gather", l1_shards), l1_shards))
    pw_all, wg1, wu1, wd1 = gathered(l1_w, l1_shards)
    pw = jnp.transpose(pw_all.reshape(N_CHIPS, ng, gq, gd), (1, 0, 2, 3)).reshape(ng, N_CHIPS * gq, gd)
    vec_p = _rows_of(norm_mix_g[1], mod[1, 1], mod[1, 0], mod[1, 2], pls)
    mixed, yp, x3 = _pool_fwd(x2, vec_p, pw)
    vec_f1 = _rows_of(norm_ffn_g[1], mod[1, 4], mod[1, 3], mod[1, 5])
    (h2b, gtb, upb, actb, y2b, dx4, fin_sums), _ = _ffn_fwd(x3, vec_f1, wg1, wu1, wd1, "ffn1_fwd_loss",
                                                           head=(tgt, final_g.reshape(1, d)))
    loss = lax.psum(fin_sums[1, 0], ("x", "y", "c"))

    def add_halves(partials, swapped, tag):
        return [_add_sibling_half(p, r, pos, f"grad_add_{tag}{k}") for k, (p, r) in enumerate(zip(partials, swapped))]

    def ffn_wgrads(h2, dgt, dup, act, dy, tag):
        gate_up = _wgrad(dgt, h2, _quarter_major(fq), _shared_rows(d), N_CHIPS, fq, d, s, "wgrad_gate" + tag, stack_blocks=2)
        gate_up = _wgrad(dup, h2, _quarter_major(fq), _shared_rows(d), N_CHIPS, fq, d, s, "wgrad_up" + tag,
                         stack=gate_up, stack_blocks=2, block=1)
        down = _wgrad(act, dy, _quarter_major(fq), _shared_rows(d), N_CHIPS, fq, d, s, "wgrad_down" + tag)
        return [gate_up, down]

    (dgtb, dupb, dyb, dx3, sums_f1), _ = _ffn_bwd(dx4, x3, vec_f1, gtb, upb, y2b, wg1, wu1, wd1, "ffn1_bwd")
    part_l1 = ffn_wgrads(h2b, dgtb, dupb, actb, dyb, "1")
    (dyp, dx2, sums_p), swapped_l1 = _pool_bwd(dx3, x2, yp, vec_p, pw, carry=(_SiblingSwap(part_l1), part_l1))
    cs_l1 = add_halves(part_l1, swapped_l1, "l1")
    dpw = _wgrad(mixed, dyp, _column_block(gd), _column_block(gd), ng, gd, gd, s, "wgrad_pool")
    dpw_q = jnp.transpose(dpw.reshape(ng, N_CHIPS, gq, gd), (1, 0, 2, 3)).reshape(N_CHIPS, ng * gq, gd)
    (dgta, dupa, dya, dx1, sums_f0), slots_l1 = _ffn_bwd(dx2, x1, vec_f0, gta, upa, y2a, wg0, wu0, wd0, "ffn0_bwd",
                                                        carry=(_Exchange("reduce", cs_l1), cs_l1))
    part_l0 = ffn_wgrads(h2a, dgta, dupa, acta, dya, "0") + [dpw_q]
    (dy0, dcw, sums_cb), swapped_l0 = _conv_b_bwd(dx1, cw, y0, vec_cb, w2, carry=(_SiblingSwap(part_l0), part_l0))
    dw2 = _wgrad(s0, dy0, _column_block(d // N_CHIPS), _shared_rows(d), N_CHIPS, d // N_CHIPS, d, s, "wgrad_conv2")
    cs_l0 = add_halves(part_l0, swapped_l0, "l0") + add_halves([dw2], _SiblingSwap([dw2]).alone([dw2], "grad_swap_w2"), "w2")
    (du, dx0, sums_ca, db1, dwdw), slots_l0 = _conv_a_bwd(dcw, glu, a0, g0, x0, dx1, vec_ca, w8, w1,
                                                          carry=(_Exchange("reduce", cs_l0), cs_l0))
    dw1 = _wgrad(h0, du, _shared_rows(d), _column_block(2 * d // N_CHIPS), N_CHIPS, d, 2 * d // N_CHIPS, s, "wgrad_conv1")
    cs_conv = add_halves([dw1], _SiblingSwap([dw1]).alone([dw1], "grad_swap_w1"), "w1")

    dmod = jnp.stack([
        _rows_of(sums_ca[2], sums_ca[1], sums_cb[0], sums_f0[2], sums_f0[1], sums_f0[3]),
        _rows_of(sums_p[2], sums_p[1], sums_p[3], sums_f1[2], sums_f1[1], sums_f1[3])])
    small = _rows_of(dmod.reshape(-1, d), sums_ca[0], sums_p[0], sums_f0[0], sums_f1[0], db1.reshape(2, d),
                     sums_cb[4], sums_cb[2], sums_cb[3], sums_cb[1], fin_sums[0], sums_p[4], dwdw)
    n_small = small.shape[0]
    pad = (-n_small) % SUBLANES
    small = jnp.concatenate([small, jnp.zeros((pad, d), F32)], axis=0) if pad else small
    small_all = _all_gather8(small.reshape(-1, LANES), "gather_small_grads").reshape(N_DEV, -1, LANES)
    small_sum = _sum_devices(small_all).reshape(-1, d)
    n_dm = depth * n_ada
    g_ada_b = small_sum[0:n_dm].reshape(depth, n_ada * d)
    g_norm_mix = small_sum[n_dm:n_dm + 2]
    g_norm_ffn = small_sum[n_dm + 2:n_dm + 4]
    g_b1 = small_sum[n_dm + 4:n_dm + 6].reshape(1, 2 * d)
    g_bdw, g_lng, g_lnb, g_b2 = (small_sum[n_dm + 6 + k].reshape(1, d) for k in range(4))
    g_final = small_sum[n_dm + 10]
    g_pls = lax.dynamic_slice_in_dim(small_sum[n_dm + 11].reshape(1, d), myq * cq, cq, axis=1)
    g_wdw = lax.dynamic_slice_in_dim(small_sum[n_dm + 12:n_dm + 12 + CONV_WIDTH], myq * cq, cq, axis=1).reshape(conv_wdw.shape)
    dmod_all = small_all.reshape(N_DEV, -1, d)[:, 0:n_dm].reshape(N_DEV, depth, n_ada * d)
    dmod_cols = lax.dynamic_slice_in_dim(jnp.transpose(dmod_all, (1, 0, 2)), myq * ada_cols, ada_cols, axis=2)
    g_ada_w = _ada_backward(c_all.T, dmod_cols)

    def small_pack(*vs):
        return jnp.concatenate([v.reshape(-1) for v in vs]).reshape(-1, LANES)

    names = ("ada_b", "norm_mix_g", "norm_ffn_g", "conv_b1", "conv_bdw", "conv_ln_g", "conv_ln_b", "conv_b2", "final_g")
    small_w = (ada_b, norm_mix_g, norm_ffn_g, conv_b1, conv_bdw, conv_ln_g, conv_ln_b, conv_b2, final_g)
    small_g = (g_ada_b, g_norm_mix, g_norm_ffn, g_b1, g_bdw, g_lng, g_lnb, g_b2, g_final)
    small_m = (m_ada_b, m_norm_mix_g, m_norm_ffn_g, m_conv_b1, m_conv_bdw, m_conv_ln_g, m_conv_ln_b, m_conv_b2, m_final_g)
    small_v = (v_ada_b, v_norm_mix_g, v_norm_ffn_g, v_conv_b1, v_conv_bdw, v_conv_ln_g, v_conv_ln_b, v_conv_b2, v_final_g)
    sd, sm, sv = _adamw(small_pack(*small_w), small_pack(*small_g), small_pack(*small_m), small_pack(*small_v), "adamw_small")

    def unpack(flat2d):
        flat = flat2d.reshape(-1)
        out, off = {}, 0
        for nm, wv in zip(names, small_w):
            out[nm] = flat[off:off + wv.size].reshape(wv.shape)
            off += wv.size
        return out

    sd, sm, sv = unpack(sd), unpack(sm), unpack(sv)
    grads = {
        "ada_w": g_ada_w, "ada_b": g_ada_b, "norm_mix_g": g_norm_mix, "norm_ffn_g": g_norm_ffn,
        "conv_b1": g_b1, "conv_wdw": g_wdw, "conv_bdw": g_bdw, "conv_ln_g": g_lng,
        "conv_ln_b": g_lnb, "conv_b2": g_b2, "pool_ls": g_pls, "final_g": g_final,
    }
    plain = {
        "ada_w": (ada_w, m_ada_w, v_ada_w), "conv_wdw": (conv_wdw, m_conv_wdw, v_conv_wdw),
        "pool_ls": (pool_ls, m_pool_ls, v_pool_ls),
    }
    from_slots = {
        "ffn_w_gate": (ffn_w_gate, m_ffn_w_gate, v_ffn_w_gate,
                       lambda: [(slots_l0[0], cs_l0[0], 0, fq), (slots_l1[0], cs_l1[0], 0, fq)]),
        "ffn_w_up": (ffn_w_up, m_ffn_w_up, v_ffn_w_up,
                     lambda: [(slots_l0[0], cs_l0[0], fq, fq), (slots_l1[0], cs_l1[0], fq, fq)]),
        "ffn_w_down": (ffn_w_down, m_ffn_w_down, v_ffn_w_down,
                       lambda: [(slots_l0[1], cs_l0[1], 0, fq), (slots_l1[1], cs_l1[1], 0, fq)]),
        "pool_w": (pool_w, m_pool_w, v_pool_w, lambda: [(slots_l0[2], cs_l0[2], 0, ng * gq)]),
        "conv_w2": (conv_w2, m_conv_w2, v_conv_w2, lambda: [(slots_l0[3], cs_l0[3], 0, d // N_CHIPS)]),
        "conv_w1": (conv_w1, m_conv_w1, v_conv_w1, lambda: [(slots_conv[0], cs_conv[0], 0, d)]),
    }
    order = ("ada_w", "ada_b", "norm_mix_g", "norm_ffn_g", "conv_w1", "conv_b1", "conv_wdw", "conv_bdw", "conv_ln_g",
             "conv_ln_b", "conv_w2", "conv_b2", "pool_w", "pool_ls", "ffn_w_gate", "ffn_w_up", "ffn_w_down", "final_g")
    delta, new_m, new_v = {}, {}, {}
    transposed = ("ffn_w_gate", "ffn_w_up")
    for nm, (wv, mv, vv, segs) in from_slots.items():
        if nm in transposed:
            wv, mv, vv = (jnp.swapaxes(t, 1, 2) for t in (wv, mv, vv))
        two_d = (-1, wv.shape[-1])
        carry = (_Exchange("reduce", cs_conv), cs_conv) if nm == "ffn_w_gate" else None
        outs, brought = _adamw_from_slots(wv.reshape(two_d), mv.reshape(two_d), vv.reshape(two_d), segs(), pos,
                                          "adamw_" + nm, carry=carry)
        if carry is not None:
            slots_conv = brought
        outs = [o.reshape(wv.shape) for o in outs]
        if nm in transposed:
            outs = [jnp.swapaxes(o, 1, 2) for o in outs]
        grads[nm], delta[nm], new_m[nm], new_v[nm] = outs
    for nm in order:
        if nm in from_slots:
            continue
        elif nm in plain:
            wv, mv, vv = plain[nm]
            grads[nm] = grads[nm].reshape(wv.shape)
            delta[nm], new_m[nm], new_v[nm] = _adamw_nd(wv, grads[nm], mv, vv, "adamw_" + nm)
        else:
            delta[nm], new_m[nm], new_v[nm] = sd[nm], sm[nm], sv[nm]
            grads[nm] = grads[nm].reshape(dict(zip(names, small_w))[nm].shape)

    return (loss, dx0.reshape(x.shape), *[grads[n] for n in order], *[delta[n] for n in order],
            *[new_m[n] for n in order], *[new_v[n] for n in order])
```

```python
import functools

import jax
import jax.numpy as jnp
from jax import lax
from jax.experimental import pallas as pl
from jax.experimental.pallas import tpu as pltpu

F32 = jnp.float32
_LOWP = jnp.bfloat16
EPS = 1e-6
CONV_WIDTH = 31
POOL_WINDOWS = (2, 4, 8, 16)
ADAM_LR = 0.001
ADAM_B1 = 0.9
ADAM_B2 = 0.999
ADAM_EPS = 1e-08
ADAM_WD = 0.01
ADAM_STEP = 10

N_CHIPS = 4
N_DEV = 8
LANES = 128
SUBLANES = 8
CONV_HALO = 32
POOL_HALO = 16
TILE_ROWS = 64
VMEM_LIMIT = 56 * 1024 * 1024
ADAM_BLOCK_BYTES = 1 << 20
WGRAD_ROWS = 4096
FFN_FWD_QUARTERS_PER_STEP = 2
MESH = pl.DeviceIdType.MESH
SDS = jax.ShapeDtypeStruct


def _cparams(*sem):
    return pltpu.CompilerParams(dimension_semantics=sem if sem else None, vmem_limit_bytes=VMEM_LIMIT)


def _row_block(s):
    for tm in (512, 256, 128, 64, 32, 16):
        if s % tm == 0:
            return tm
    raise ValueError(f"sequence length {s} must be a multiple of 16")


def _rows_spec(tm, ncols):
    return pl.BlockSpec((tm, ncols), lambda i, *_: (i, 0))


def _const_spec(shape):
    nd = len(shape)
    return pl.BlockSpec(shape, lambda *_: (0,) * nd)


def _dot(a, b):
    return lax.dot_general(a.astype(_LOWP), b.astype(_LOWP), (((1,), (0,)), ((), ())), preferred_element_type=F32)


def _dot_nt(a, b):
    return lax.dot_general(a.astype(_LOWP), b.astype(_LOWP), (((1,), (1,)), ((), ())), preferred_element_type=F32)


def _dot_tn(a, b):
    return lax.dot_general(a.astype(_LOWP), b.astype(_LOWP), (((0,), (0,)), ((), ())), preferred_element_type=F32)


def _rms(x, gamma):
    r = lax.rsqrt(jnp.mean(x * x, axis=-1, keepdims=True) + EPS)
    xh = x * r
    return xh, r, xh * gamma


def _rms_bwd(dn, xh, r, gamma):
    dxh = dn * gamma
    return r * (dxh - xh * jnp.mean(dxh * xh, axis=-1, keepdims=True))


def _colsum(v):
    return jnp.sum(v, axis=0, keepdims=True)


def _sigmoid(v):
    return jax.nn.sigmoid(v)


def _dsilu(v):
    sg = _sigmoid(v)
    return sg * (1.0 + v * (1.0 - sg))


def _for_tiles(nrows, col0, ncols, fn):
    def step(t, carry):
        r0 = pl.multiple_of(t * TILE_ROWS, TILE_ROWS)
        for col in range(col0, col0 + ncols, LANES):
            fn(r0, col)
        return carry

    lax.fori_loop(0, nrows // TILE_ROWS, step, 0)


def _shift_up(win, b):
    return pltpu.roll(win, win.shape[0] - b, 0) if b else win


def _shift_down(win, b):
    return pltpu.roll(win, b, 0) if b else win


def _tiles3(v):
    return v.reshape(v.shape[0] // SUBLANES, SUBLANES, v.shape[1])


def _resident_scratch(arrays):
    return [pltpu.VMEM(a.shape, a.dtype) for a in arrays] + [pltpu.SemaphoreType.DMA((len(arrays),))]


def _load_resident(hbm_refs, vmem_refs, sems):
    copies = [pltpu.make_async_copy(src, dst, sems.at[k]) for k, (src, dst) in enumerate(zip(hbm_refs, vmem_refs))]
    for cp in copies:
        cp.start()
    for cp in copies:
        cp.wait()


def _mesh_pos():
    return lax.axis_index("x"), lax.axis_index("y"), lax.axis_index("c")


def _other_chips(x, y):
    return [(1 - x, y), (x, 1 - y), (1 - x, 1 - y)]


def _all_gather8(v, name):
    m, n = v.shape

    def body(x_ref, out_ref, send_sems, recv_sems, local_sem):
        x, y, c = _mesh_pos()
        me, sibling = (x, y, c), (x, y, 1 - c)
        chips = _other_chips(x, y)

        def rows(px, py, pc):
            return out_ref.at[pl.ds((4 * px + 2 * py + pc) * m, m), :]

        def copy(k, block, to, src=None):
            return pltpu.make_async_remote_copy(
                src_ref=rows(*block) if src is None else src, dst_ref=rows(*block),
                send_sem=send_sems.at[k], recv_sem=recv_sems.at[k], device_id=to, device_id_type=MESH)

        mine = pltpu.make_async_copy(x_ref, rows(*me), local_sem)
        mine.start()
        first = [copy(0, me, sibling, src=x_ref)]
        first += [copy(1 + j, me, (*chip, c), src=x_ref) for j, chip in enumerate(chips)]
        for cp in first:
            cp.start()
        passed = [copy(4 + j, (*chip, c), sibling) for j, chip in enumerate(chips)]
        for j, chip in enumerate(chips):
            copy(1 + j, (*chip, c), me).wait_recv()
            passed[j].start()
        copy(0, sibling, me).wait_recv()
        for j, chip in enumerate(chips):
            copy(4 + j, (*chip, 1 - c), me).wait_recv()
        for cp in first + passed:
            cp.wait_send()
        mine.wait()

    return pl.pallas_call(
        body, name=name,
        out_shape=SDS((N_DEV * m, n), v.dtype),
        in_specs=[pl.BlockSpec(memory_space=pltpu.VMEM)],
        out_specs=pl.BlockSpec(memory_space=pltpu.VMEM),
        scratch_shapes=[pltpu.SemaphoreType.DMA((7,)), pltpu.SemaphoreType.DMA((7,)), pltpu.SemaphoreType.DMA],
    )(v)


class _Exchange:
    N_SEMS = 7

    def __init__(self, kind, arrays):
        self.kind, self.n = kind, len(arrays)
        self.streams = 3
        if kind == "gather":
            self.out_shape = [SDS((N_CHIPS,) + a.shape, a.dtype) for a in arrays]
        else:
            self.out_shape = [SDS((N_CHIPS, 2 * a.shape[1], a.shape[2]), a.dtype) for a in arrays]
        self.scratch = [pltpu.SemaphoreType.DMA((self.n, self.N_SEMS)), pltpu.SemaphoreType.DMA((self.n, self.N_SEMS))]

    def _ctx(self, ins, outs, send_sems, recv_sems):
        x, y, c = _mesh_pos()
        myq = 2 * x + y
        sibling = (x, y, 1 - c)
        chips = _other_chips(x, y)
        gather = self.kind == "gather"
        ns = self.streams

        def rows(a, h):
            hr = outs[a].shape[1] // 2
            return pl.ds(h * hr, hr)

        def copy(a, ks, kr, src, dst, to):
            return pltpu.make_async_remote_copy(src_ref=src, dst_ref=dst, send_sem=send_sems.at[a, ks],
                                                recv_sem=recv_sems.at[a, kr], device_id=to, device_id_type=MESH)

        def place(a, t, h):
            if gather:
                px, py = chips[t]
                return outs[a].at[2 * px + py, rows(a, h)]
            return outs[a].at[t, rows(a, h)]

        def first(a, t):
            if gather:
                return copy(a, t, t, ins[a].at[rows(a, c)], outs[a].at[myq, rows(a, c)], (*chips[t], c))
            px, py = chips[t]
            return copy(a, t, t, ins[a].at[2 * px + py], outs[a].at[t, rows(a, c)], (px, py, c))

        def landed(a, t):
            return copy(a, t, t, place(a, t, c), place(a, t, c), sibling)

        def relay(a, t):
            return copy(a, ns + t, ns + t, place(a, t, c), place(a, t, c), sibling)

        def relayed(a, t):
            return copy(a, ns + t, ns + t, place(a, t, 1 - c), place(a, t, 1 - c), sibling)

        def own(a):
            return copy(a, 2 * ns, 2 * ns, ins[a].at[myq], outs[a].at[ns, rows(a, c)], sibling)

        return first, landed, relay, relayed, own

    def start(self, ins, outs, send_sems, recv_sems):
        first, _, _, _, own = self._ctx(ins, outs, send_sems, recv_sems)
        for a in range(self.n):
            for t in range(self.streams):
                first(a, t).start()
            if self.kind == "reduce":
                own(a).start()

    def relay(self, ins, outs, send_sems, recv_sems):
        _, landed, relay, _, _ = self._ctx(ins, outs, send_sems, recv_sems)
        for a in range(self.n):
            for t in range(self.streams):
                landed(a, t).wait_recv()
                relay(a, t).start()

    def finish(self, ins, outs, send_sems, recv_sems):
        first, _, relay, relayed, own = self._ctx(ins, outs, send_sems, recv_sems)
        for a in range(self.n):
            for t in range(self.streams):
                relayed(a, t).wait_recv()
            if self.kind == "reduce":
                own(a).wait_recv()
        for a in range(self.n):
            for t in range(self.streams):
                first(a, t).wait_send()
                relay(a, t).wait_send()
            if self.kind == "reduce":
                own(a).wait_send()

    def alone(self, arrays, name):
        n = self.n

        def body(*refs):
            args = (refs[:n], refs[n:2 * n], refs[2 * n], refs[2 * n + 1])
            self.start(*args)
            self.relay(*args)
            self.finish(*args)

        any_spec = pl.BlockSpec(memory_space=pl.ANY)
        return pl.pallas_call(body, name=name, out_shape=self.out_shape, in_specs=[any_spec] * n,
                              out_specs=[any_spec] * n, scratch_shapes=self.scratch)(*arrays)


RELAY_AT = (5, 8)


def _call(body, *args, carry=None, prefetch=None, name, grid, in_specs, out_specs, out_shape, scratch_shapes=(), sem):
    def run(fn, in_specs_, out_specs_, out_shape_, scratch_, sem_, operands):
        if prefetch is None:
            return pl.pallas_call(fn, name=name, grid=grid, in_specs=in_specs_, out_specs=out_specs_, out_shape=out_shape_,
                                  scratch_shapes=scratch_, compiler_params=_cparams(*sem_))(*operands)
        grid_spec = pltpu.PrefetchScalarGridSpec(num_scalar_prefetch=1, grid=grid, in_specs=in_specs_,
                                                 out_specs=out_specs_, scratch_shapes=scratch_)
        return pl.pallas_call(fn, name=name, grid_spec=grid_spec, out_shape=out_shape_,
                              compiler_params=_cparams(*sem_))(prefetch, *operands)

    if carry is None:
        return run(body, list(in_specs), list(out_specs), list(out_shape), list(scratch_shapes), sem, args), None
    ex, arrays = carry
    n_in, n_out, n_scr, n = len(in_specs), len(out_specs), len(scratch_shapes), ex.n
    n_pre = 0 if prefetch is None else 1
    total = 1
    for g in grid:
        total *= g
    relay_step = (total * RELAY_AT[0]) // RELAY_AT[1]

    def hosted(*refs):
        pre, refs = refs[:n_pre], refs[n_pre:]
        ins, cins = refs[:n_in], refs[n_in:n_in + n]
        outs, couts = refs[n_in + n:n_in + n + n_out], refs[n_in + n + n_out:n_in + 2 * n + n_out]
        scr = refs[n_in + 2 * n + n_out:n_in + 2 * n + n_out + n_scr]
        sems = refs[n_in + 2 * n + n_out + n_scr:]
        step = 0
        for ax, g in enumerate(grid):
            step = step * g + pl.program_id(ax)

        @pl.when(step == 0)
        def _():
            ex.start(cins, couts, *sems)

        body(*pre, *ins, *outs, *scr)

        @pl.when(step == relay_step)
        def _():
            ex.relay(cins, couts, *sems)

        @pl.when(step == total - 1)
        def _():
            ex.finish(cins, couts, *sems)

    any_spec = pl.BlockSpec(memory_space=pl.ANY)
    res = run(hosted, list(in_specs) + [any_spec] * n, list(out_specs) + [any_spec] * n, list(out_shape) + ex.out_shape,
              list(scratch_shapes) + ex.scratch, ("arbitrary",) * len(grid), (*args, *arrays))
    return res[:n_out], res[n_out:]


class _SiblingSwap:
    def __init__(self, arrays):
        self.n = len(arrays)
        self.out_shape = [SDS((a.shape[0], a.shape[1] // 2, a.shape[2]), a.dtype) for a in arrays]
        self.scratch = [pltpu.SemaphoreType.DMA((self.n,)), pltpu.SemaphoreType.DMA((self.n,))]

    def _copies(self, ins, outs, send_sems, recv_sems):
        x, y, c = _mesh_pos()
        cps = []
        for a in range(self.n):
            hr = ins[a].shape[1] // 2
            cps.append(pltpu.make_async_remote_copy(
                src_ref=ins[a].at[:, pl.ds((1 - c) * hr, hr)], dst_ref=outs[a],
                send_sem=send_sems.at[a], recv_sem=recv_sems.at[a], device_id=(x, y, 1 - c), device_id_type=MESH))
        return cps

    def start(self, *refs):
        for cp in self._copies(*refs):
            cp.start()

    def relay(self, *refs):
        pass

    def finish(self, *refs):
        for cp in self._copies(*refs):
            cp.wait()

    def alone(self, arrays, name):
        return _Exchange.alone(self, arrays, name)


def _divisor_block(rows, cands):
    for rb in cands:
        if rows % rb == 0:
            return rb
    raise ValueError(rows)


def _add_sibling_half(p, rb1, pos, name):
    nq, r, n = p.shape
    h = r // 2
    rb = _divisor_block(h, (512, 352, 256, 176, 128, 64, 32, 16))
    nb = h // rb

    def body(pos_ref, p_ref, r_ref, o_ref):
        o_ref[...] = (p_ref[...].astype(F32) + r_ref[...].astype(F32)).astype(o_ref.dtype)

    grid_spec = pltpu.PrefetchScalarGridSpec(
        num_scalar_prefetch=1, grid=(nq, nb),
        in_specs=[pl.BlockSpec((1, rb, n), lambda q, i, pos: (q, pos[0] * nb + i, 0)),
                  pl.BlockSpec((1, rb, n), lambda q, i, pos: (q, i, 0))],
        out_specs=pl.BlockSpec((1, rb, n), lambda q, i, pos: (q, i, 0)))
    return pl.pallas_call(
        body, name=name, grid_spec=grid_spec, out_shape=SDS((nq, h, n), p.dtype),
        compiler_params=_cparams("parallel", "parallel"),
    )(pos, p, rb1)


def _sum_devices(g):
    nd, m, n = g.shape

    def body(g_ref, o_ref):
        acc = g_ref[0]
        for d in range(1, nd):
            acc = acc + g_ref[d]
        o_ref[...] = acc

    return pl.pallas_call(body, name="small_grad_sum", out_shape=SDS((m, n), F32))(g)


def _ada_forward(c_all, ada_w, ada_b_cols, carry=None):
    nl, d, ncol = ada_w.shape
    nb = c_all.shape[0]
    tn = 512 if ncol % 512 == 0 else ncol

    def body(c_ref, w_ref, b_ref, o_ref):
        cv = c_ref[...]
        ca = cv * _sigmoid(cv)
        o_ref[0] = _dot(ca, w_ref[0]) + b_ref[0]

    return _call(
        body, c_all, ada_w, ada_b_cols, carry=carry, name="ada_forward", grid=(nl, ncol // tn),
        in_specs=[pl.BlockSpec((nb, d), lambda l, j: (0, 0)),
                  pl.BlockSpec((1, d, tn), lambda l, j: (l, 0, j)),
                  pl.BlockSpec((1, 1, tn), lambda l, j: (l, 0, j))],
        out_specs=[pl.BlockSpec((1, nb, tn), lambda l, j: (l, 0, j))],
        out_shape=[SDS((nl, nb, ncol), F32)], sem=("parallel", "parallel"))


def _ada_backward(c_all_t, dmod_cols):
    d, nb = c_all_t.shape
    nl, _, ncol = dmod_cols.shape
    tn = 512 if ncol % 512 == 0 else ncol

    def body(c_ref, g_ref, o_ref):
        cv = c_ref[...]
        ca = cv * _sigmoid(cv)
        o_ref[0] = _dot(ca, g_ref[0])

    return pl.pallas_call(
        body, name="ada_backward", grid=(nl, ncol // tn),
        in_specs=[pl.BlockSpec((d, nb), lambda l, j: (0, 0)),
                  pl.BlockSpec((1, nb, tn), lambda l, j: (l, 0, j))],
        out_specs=pl.BlockSpec((1, d, tn), lambda l, j: (l, 0, j)),
        out_shape=SDS((nl, d, ncol), F32), compiler_params=_cparams("parallel", "parallel"),
    )(c_all_t, dmod_cols)


def _conv_a_fwd(x, vec, w1, b1):
    s, d = x.shape
    nq, _, hq = w1.shape
    tm = _row_block(s)

    def body(x_ref, vec_ref, w1_ref, b1_ref, h_ref, a_ref, g_ref, glu_ref):
        vec = vec_ref[...]
        _, _, n = _rms(x_ref[...], vec[0:1])
        hb = (n * (1.0 + vec[1:2]) + vec[2:3]).astype(_LOWP)
        h_ref[...] = hb
        u = [_dot(hb, w1_ref[q]) + b1_ref[:, q * hq:(q + 1) * hq] for q in range(nq)]
        for k in range(nq // 2):
            av, gv = u[k], u[nq // 2 + k]
            cols = slice(k * hq, (k + 1) * hq)
            a_ref[:, cols] = av.astype(_LOWP)
            g_ref[:, cols] = gv.astype(_LOWP)
            glu_ref[:, cols] = av * _sigmoid(gv)

    return pl.pallas_call(
        body, name="conv_a_fwd", grid=(s // tm,),
        in_specs=[_rows_spec(tm, d), _const_spec(vec.shape), _const_spec(w1.shape), _const_spec(b1.shape)],
        out_specs=[_rows_spec(tm, d)] * 4,
        out_shape=[SDS((s, d), _LOWP)] * 3 + [SDS((s, d), F32)],
        compiler_params=_cparams("parallel"),
    )(x, vec, w1, b1)


def _conv_b_fwd(glu, w8, vec, w2, x, carry=None):
    s, d = x.shape
    tm = _row_block(s)
    hb = tm // CONV_HALO

    def body(glu_ref, halo_ref, w8_ref, vec_ref, w2_ref, x_ref, cw_ref, s_ref, y_ref, xo_ref, ext_ref):
        i = pl.program_id(0)
        vec = vec_ref[...]
        ext_ref[0:CONV_HALO, :] = jnp.where(i > 0, halo_ref[...], 0.0)
        ext_ref[CONV_HALO:, :] = glu_ref[...]

        def tile(r0, c0):
            lanes = slice(c0, c0 + LANES)
            win = ext_ref[pl.ds(r0, TILE_ROWS + CONV_HALO), lanes]
            acc = jnp.zeros((TILE_ROWS // SUBLANES, SUBLANES, LANES), F32)
            for b in range(SUBLANES):
                sh = _shift_up(win, b)
                for a in range(CONV_HALO // SUBLANES + 1):
                    k = SUBLANES * a + b - (CONV_HALO - CONV_WIDTH + 1)
                    if 0 <= k < CONV_WIDTH:
                        acc = acc + w8_ref[k, :, lanes][None] * _tiles3(sh[SUBLANES * a:SUBLANES * a + TILE_ROWS])
            cw_ref[pl.ds(r0, TILE_ROWS), lanes] = acc.reshape(TILE_ROWS, LANES)

        _for_tiles(tm, 0, d, tile)
        cw = cw_ref[...] + vec[0:1]
        cw_ref[...] = cw
        cc = cw - jnp.mean(cw, axis=-1, keepdims=True)
        ch = cc * lax.rsqrt(jnp.mean(cc * cc, axis=-1, keepdims=True) + EPS)
        lo = ch * vec[1:2] + vec[2:3]
        sv = (lo * _sigmoid(lo)).astype(_LOWP)
        s_ref[...] = sv
        y = _dot(sv, w2_ref[...]) + vec[3:4]
        y_ref[...] = y.astype(_LOWP)
        xo_ref[...] = x_ref[...] + (1.0 + vec[4:5]) * y

    return _call(
        body, glu, glu, w8, vec, w2, x, carry=carry, name="conv_b_fwd", grid=(s // tm,),
        in_specs=[_rows_spec(tm, d),
                  pl.BlockSpec((CONV_HALO, d), lambda i: (jnp.maximum(i * hb - 1, 0), 0)),
                  _const_spec(w8.shape), _const_spec(vec.shape), _const_spec(w2.shape), _rows_spec(tm, d)],
        out_specs=[_rows_spec(tm, d)] * 4,
        out_shape=[SDS((s, d), F32), SDS((s, d), _LOWP), SDS((s, d), _LOWP), SDS((s, d), F32)],
        scratch_shapes=[pltpu.VMEM((tm + CONV_HALO, d), F32)], sem=("parallel",))


def _ffn_fwd(x, vec, wg, wu, wd, name, carry=None, head=None):
    s, d = x.shape
    nq, fq, _ = wg.shape
    tm = _row_block(s)
    nsteps = s // tm
    n_head = 0 if head is None else 2
    qps = FFN_FWD_QUARTERS_PER_STEP

    def body(x_ref, vec_ref, wg_hbm, wu_hbm, wd_hbm, *rest):
        head_refs, rest = rest[:n_head], rest[n_head:]
        h_ref, gt_ref, up_ref, act_ref, y_ref, xo_ref = rest[:6]
        sums_ref = rest[6] if head is not None else None
        hb_ref, yacc_ref, wg_ref, wu_ref, wd_ref, w_sems = rest[6 + n_head // 2:]
        i, p = pl.program_id(0), pl.program_id(1)
        vec = vec_ref[...]

        @pl.when((i == 0) & (p == 0))
        def _():
            _load_resident((wg_hbm, wu_hbm, wd_hbm), (wg_ref, wu_ref, wd_ref), w_sems)
            if head is not None:
                sums_ref[...] = jnp.zeros_like(sums_ref)

        @pl.when(p == 0)
        def _():
            _, _, n = _rms(x_ref[...], vec[0:1])
            hb = (n * (1.0 + vec[1:2]) + vec[2:3]).astype(_LOWP)
            hb_ref[...] = hb
            h_ref[...] = hb
            yacc_ref[...] = jnp.zeros_like(yacc_ref)

        hb = hb_ref[...]
        for k in range(qps):
            q = p * qps + k
            gt = _dot_nt(hb, wg_ref[q])
            up = _dot_nt(hb, wu_ref[q])
            act = (gt * _sigmoid(gt) * up).astype(_LOWP)
            gt_ref[k] = gt.astype(_LOWP)
            up_ref[k] = up.astype(_LOWP)
            act_ref[k] = act
            yacc_ref[...] += _dot(act, wd_ref[q])

        @pl.when(p == nq // qps - 1)
        def _():
            y = yacc_ref[...]
            y_ref[...] = y.astype(_LOWP)
            xo = x_ref[...] + (1.0 + vec[3:4]) * y
            if head is None:
                xo_ref[...] = xo
            else:
                t_ref, g_ref = head_refs
                gamma_v = g_ref[...]
                xh, r, out = _rms(xo, gamma_v)
                err = out - t_ref[...]
                dout = err * (1.0 / d)
                sums_ref[0:1, :] += _colsum(dout * xh)
                sums_ref[1:2, :] += _colsum(0.5 * err * dout)
                xo_ref[...] = _rms_bwd(dout, xh, r, gamma_v)

        if head is not None:
            @pl.when((i == nsteps - 1) & (p == nq // qps - 1))
            def _():
                sums_ref[1:2, :] = jnp.broadcast_to(jnp.sum(sums_ref[1:2, :], axis=-1, keepdims=True), (1, d))

    rows = pl.BlockSpec((tm, d), lambda i, q: (i, 0))
    hid = pl.BlockSpec((qps, tm, fq), lambda i, q: (q, i, 0))
    any_spec = pl.BlockSpec(memory_space=pl.ANY)
    head_args = [] if head is None else [head[0], head[1]]
    head_specs = [] if head is None else [rows, pl.BlockSpec((1, d), lambda i, q: (0, 0))]
    sums_specs = [] if head is None else [pl.BlockSpec((2, d), lambda i, q: (0, 0))]
    sums_shape = [] if head is None else [SDS((2, d), F32)]
    return _call(
        body, x, vec, wg, wu, wd, *head_args, carry=carry, name=name, grid=(nsteps, nq // qps),
        in_specs=[rows, pl.BlockSpec(vec.shape, lambda i, q: (0, 0)), any_spec, any_spec, any_spec] + head_specs,
        out_specs=[rows, hid, hid, hid, rows, rows] + sums_specs,
        out_shape=[SDS((s, d), _LOWP)] + [SDS((nq, s, fq), _LOWP)] * 3 + [SDS((s, d), _LOWP), SDS((s, d), F32)] + sums_shape,
        scratch_shapes=[pltpu.VMEM((tm, d), _LOWP), pltpu.VMEM((tm, d), F32)] + _resident_scratch((wg, wu, wd)),
        sem=("arbitrary", "arbitrary"))


def _pool_inv_count(t0, nrows, w):
    t = t0 + lax.broadcasted_iota(jnp.int32, (nrows, LANES), 0)
    return 1.0 / jnp.minimum(t + 1, w).astype(F32)


def _pool_fwd(x, vec, pw):
    s, d = x.shape
    ng, gd, _ = pw.shape
    tm = _row_block(s)
    hb = tm // POOL_HALO

    def body(x_ref, halo_ref, vec_ref, pw_ref, mixed_ref, yp_ref, xo_ref, hext_ref):
        i = pl.program_id(0)
        vec = vec_ref[...]

        def modulated(xv):
            _, _, n = _rms(xv, vec[0:1])
            return n * (1.0 + vec[1:2]) + vec[2:3]

        hext_ref[0:POOL_HALO, :] = jnp.where(i > 0, modulated(halo_ref[...]), 0.0)
        hext_ref[POOL_HALO:, :] = modulated(x_ref[...])

        for g in range(ng):
            def tile(r0, c0, g=g):
                lanes = slice(c0, c0 + LANES)
                win = hext_ref[pl.ds(r0, TILE_ROWS + POOL_HALO), lanes]
                acc = win
                for step in range(g + 1):
                    acc = acc + _shift_down(acc, 2 ** step)
                inv = _pool_inv_count(i * tm + r0, TILE_ROWS, POOL_WINDOWS[g])
                mixed = acc[POOL_HALO:] * inv - win[POOL_HALO:]
                mixed_ref[pl.ds(r0, TILE_ROWS), lanes] = mixed.astype(_LOWP)

            _for_tiles(tm, g * gd, gd, tile)

        for g in range(ng):
            cols = slice(g * gd, (g + 1) * gd)
            yp = _dot(mixed_ref[:, cols], pw_ref[g])
            yp_ref[:, cols] = yp.astype(_LOWP)
            xo_ref[:, cols] = x_ref[:, cols] + (1.0 + vec[3:4, cols]) * (yp * vec[4:5, cols])

    return pl.pallas_call(
        body, name="pool_fwd", grid=(s // tm,),
        in_specs=[_rows_spec(tm, d),
                  pl.BlockSpec((POOL_HALO, d), lambda i: (jnp.maximum(i * hb - 1, 0), 0)),
                  _const_spec(vec.shape), _const_spec(pw.shape)],
        out_specs=[_rows_spec(tm, d)] * 3,
        out_shape=[SDS((s, d), _LOWP), SDS((s, d), _LOWP), SDS((s, d), F32)],
        scratch_shapes=[pltpu.VMEM((tm + POOL_HALO, d), F32)],
        compiler_params=_cparams("parallel"),
    )(x, x, vec, pw)


def _ffn_bwd(dxo, x, vec, gt, up, y, wg, wu, wd, name, carry=None):
    s, d = x.shape
    nq, fq, _ = wg.shape
    tm = _row_block(s)

    def body(dxo_ref, x_ref, vec_ref, gt_ref, up_ref, y_ref, wg_hbm, wu_hbm, wd_hbm,
             dgt_ref, dup_ref, dy_ref, dx_ref, sums_ref, dyb_ref, dh_ref, wg_ref, wu_ref, wd_ref, w_sems):
        i, q = pl.program_id(0), pl.program_id(1)
        vec = vec_ref[...]

        @pl.when((i == 0) & (q == 0))
        def _():
            _load_resident((wg_hbm, wu_hbm, wd_hbm), (wg_ref, wu_ref, wd_ref), w_sems)
            sums_ref[...] = jnp.zeros_like(sums_ref)

        @pl.when(q == 0)
        def _():
            dxv = dxo_ref[...]
            dyb = (dxv * (1.0 + vec[3:4])).astype(_LOWP)
            dyb_ref[...] = dyb
            dy_ref[...] = dyb
            dh_ref[...] = jnp.zeros_like(dh_ref)
            sums_ref[3:4, :] += _colsum(dxv * y_ref[...].astype(F32))

        halves = [slice(k * (tm // 2), (k + 1) * (tm // 2)) for k in range(2)]
        dacts = [_dot_nt(dyb_ref[rs, :], wd_ref[q]) for rs in halves]
        dgus = []
        for rs, dact in zip(halves, dacts):
            gtv = gt_ref[0, rs, :].astype(F32)
            sg = _sigmoid(gtv)
            dgt = (dact * up_ref[0, rs, :].astype(F32) * (sg * (1.0 + gtv * (1.0 - sg)))).astype(_LOWP)
            dup = (dact * (gtv * sg)).astype(_LOWP)
            dgt_ref[0, rs, :] = dgt
            dup_ref[0, rs, :] = dup
            dgus.append((dgt, dup))
        for rs, (dgt, dup) in zip(halves, dgus):
            dh_ref[rs, :] += _dot(dgt, wg_ref[q]) + _dot(dup, wu_ref[q])

        @pl.when(q == nq - 1)
        def _():
            dh = dh_ref[...]
            xh, r, n = _rms(x_ref[...], vec[0:1])
            sums_ref[2:3, :] += _colsum(dh)
            sums_ref[1:2, :] += _colsum(dh * n)
            dn = dh * (1.0 + vec[1:2])
            sums_ref[0:1, :] += _colsum(dn * xh)
            dx_ref[...] = dxo_ref[...] + _rms_bwd(dn, xh, r, vec[0:1])

    rows = pl.BlockSpec((tm, d), lambda i, q: (i, 0))
    hid = pl.BlockSpec((1, tm, fq), lambda i, q: (q, i, 0))
    return _call(
        body, dxo, x, vec, gt, up, y, wg, wu, wd, carry=carry, name=name, grid=(s // tm, nq),
        in_specs=[rows, rows, pl.BlockSpec(vec.shape, lambda i, q: (0, 0)), hid, hid, rows] +
                 [pl.BlockSpec(memory_space=pl.ANY)] * 3,
        out_specs=[hid, hid, rows, rows, pl.BlockSpec((4, d), lambda i, q: (0, 0))],
        out_shape=[SDS((nq, s, fq), _LOWP)] * 2 + [SDS((s, d), _LOWP), SDS((s, d), F32), SDS((4, d), F32)],
        scratch_shapes=[pltpu.VMEM((tm, d), _LOWP), pltpu.VMEM((tm, d), F32)] + _resident_scratch((wg, wu, wd)),
        sem=("arbitrary", "arbitrary"))


def _wgrad(a, b, a_spec, b_spec, nq, ka, nb, s, name, stack=None, stack_blocks=1, block=0):
    tk = min(WGRAD_ROWS, s)
    nk = s // tk

    def body(a_ref, b_ref, *rest):
        o_ref, acc_ref = rest[-2:]
        k = pl.program_id(1)

        @pl.when(k == 0)
        def _():
            acc_ref[...] = jnp.zeros_like(acc_ref)

        av = a_ref[...].reshape(tk, ka)
        bv = b_ref[...].reshape(tk, nb)
        acc_ref[...] += _dot_tn(av, bv)

        @pl.when(k == nk - 1)
        def _():
            o_ref[0] = acc_ref[...].astype(o_ref.dtype)

    extra = [] if stack is None else [stack]
    return pl.pallas_call(
        body, name=name, grid=(nq, nk),
        in_specs=[a_spec(tk), b_spec(tk)] + [pl.BlockSpec(memory_space=pl.ANY)] * len(extra),
        out_specs=pl.BlockSpec((1, ka, nb), lambda q, k: (q, block, 0)),
        out_shape=SDS((nq, stack_blocks * ka, nb), _LOWP),
        scratch_shapes=[pltpu.VMEM((ka, nb), F32)],
        input_output_aliases={2: 0} if extra else {},
        compiler_params=_cparams("parallel", "arbitrary"),
    )(a, b, *extra)


def _shared_rows(ncols):
    return lambda tk: pl.BlockSpec((tk, ncols), lambda q, k: (k, 0))


def _column_block(ncols):
    return lambda tk: pl.BlockSpec((tk, ncols), lambda q, k: (k, q))


def _quarter_major(ncols):
    return lambda tk: pl.BlockSpec((1, tk, ncols), lambda q, k: (q, k, 0))


def _pool_bwd(dxo, x, yp, vec, pw, carry=None):
    s, d = x.shape
    ng, gd, _ = pw.shape
    tm = _row_block(s)
    hb = tm // POOL_HALO
    nsteps = s // tm
    last_halo = s // POOL_HALO - 1

    def body(dxo_ref, halo_ref, x_ref, yp_ref, vec_ref, pw_ref, dyp_ref, dx_ref, sums_ref, eext_ref, dh_ref):
        i = pl.program_id(0)
        vec = vec_ref[...]

        @pl.when(i == 0)
        def _():
            sums_ref[...] = jnp.zeros_like(sums_ref)

        dxv = dxo_ref[...]
        ypv = yp_ref[...].astype(F32)
        gate1 = 1.0 + vec[3:4]
        sums_ref[3:4, :] += _colsum(dxv * (ypv * vec[4:5]))
        sums_ref[4:5, :] += _colsum(dxv * gate1 * ypv)
        dyp = (dxv * gate1 * vec[4:5]).astype(_LOWP)
        dyp_ref[...] = dyp
        dyp_halo = (jnp.where(i < nsteps - 1, halo_ref[...], 0.0) * gate1 * vec[4:5]).astype(_LOWP)
        for g in range(ng):
            cols = slice(g * gd, (g + 1) * gd)
            dm = _dot_nt(dyp[:, cols], pw_ref[g])
            dh_ref[:, cols] = dm
            inv = _pool_inv_count(i * tm, tm, POOL_WINDOWS[g])
            eext_ref[0:tm, cols] = dm * jnp.concatenate([inv] * (gd // LANES), axis=1)
            eext_ref[tm:, cols] = _dot_nt(dyp_halo[:, cols], pw_ref[g]) * (1.0 / POOL_WINDOWS[g])

        for g in range(ng):
            def tile(r0, c0, g=g):
                acc = eext_ref[pl.ds(r0, TILE_ROWS + POOL_HALO), c0:c0 + LANES]
                for step in range(g + 1):
                    acc = acc + _shift_up(acc, 2 ** step)
                here = (pl.ds(r0, TILE_ROWS), slice(c0, c0 + LANES))
                dh_ref[here] = acc[:TILE_ROWS] - dh_ref[here]

            _for_tiles(tm, g * gd, gd, tile)

        dh = dh_ref[...]
        xh, r, n = _rms(x_ref[...], vec[0:1])
        sums_ref[2:3, :] += _colsum(dh)
        sums_ref[1:2, :] += _colsum(dh * n)
        dn = dh * (1.0 + vec[1:2])
        sums_ref[0:1, :] += _colsum(dn * xh)
        dx_ref[...] = dxv + _rms_bwd(dn, xh, r, vec[0:1])

    return _call(
        body, dxo, dxo, x, yp, vec, pw, carry=carry, name="pool_bwd", grid=(nsteps,),
        in_specs=[_rows_spec(tm, d),
                  pl.BlockSpec((POOL_HALO, d), lambda i: (jnp.minimum((i + 1) * hb, last_halo), 0)),
                  _rows_spec(tm, d), _rows_spec(tm, d), _const_spec(vec.shape), _const_spec(pw.shape)],
        out_specs=[_rows_spec(tm, d), _rows_spec(tm, d), _const_spec((5, d))],
        out_shape=[SDS((s, d), _LOWP), SDS((s, d), F32), SDS((5, d), F32)],
        scratch_shapes=[pltpu.VMEM((tm + POOL_HALO, d), F32), pltpu.VMEM((tm, d), F32)], sem=("arbitrary",))


def _conv_b_bwd(dxo, cw, y, vec, w2, carry=None):
    s, d = dxo.shape
    tm = _row_block(s)

    def body(dxo_ref, cw_ref, y_ref, vec_ref, w2_ref, dy_ref, dcw_ref, sums_ref):
        i = pl.program_id(0)
        vec = vec_ref[...]

        @pl.when(i == 0)
        def _():
            sums_ref[...] = jnp.zeros_like(sums_ref)

        dxv = dxo_ref[...]
        dy = dxv * (1.0 + vec[4:5])
        sums_ref[0:1, :] += _colsum(dxv * y_ref[...].astype(F32))
        sums_ref[1:2, :] += _colsum(dy)
        dyb = dy.astype(_LOWP)
        dy_ref[...] = dyb
        ds = _dot_nt(dyb, w2_ref[...])
        cw = cw_ref[...]
        cc = cw - jnp.mean(cw, axis=-1, keepdims=True)
        rs = lax.rsqrt(jnp.mean(cc * cc, axis=-1, keepdims=True) + EPS)
        ch = cc * rs
        lo = ch * vec[1:2] + vec[2:3]
        dlo = ds * _dsilu(lo)
        sums_ref[2:3, :] += _colsum(dlo * ch)
        sums_ref[3:4, :] += _colsum(dlo)
        dch = dlo * vec[1:2]
        dcw = rs * (dch - jnp.mean(dch, axis=-1, keepdims=True) - ch * jnp.mean(dch * ch, axis=-1, keepdims=True))
        sums_ref[4:5, :] += _colsum(dcw)
        dcw_ref[...] = dcw

    return _call(
        body, dxo, cw, y, vec, w2, carry=carry, name="conv_b_bwd", grid=(s // tm,),
        in_specs=[_rows_spec(tm, d), _rows_spec(tm, d), _rows_spec(tm, d), _const_spec(vec.shape), _const_spec(w2.shape)],
        out_specs=[_rows_spec(tm, d), _rows_spec(tm, d), _const_spec((5, d))],
        out_shape=[SDS((s, d), _LOWP), SDS((s, d), F32), SDS((5, d), F32)], sem=("arbitrary",))


def _conv_a_bwd(dcw, glu, a, g, x, dxo, vec, w8, w1, carry=None):
    s, d = x.shape
    nq, _, hq = w1.shape
    tm = _row_block(s)
    hb = tm // CONV_HALO
    nsteps = s // tm
    last_halo = s // CONV_HALO - 1
    lead = CONV_HALO - CONV_WIDTH + 1

    def body(dcw_ref, dhalo_ref, glu_ref, ghalo_ref, a_ref, g_ref, x_ref, dxo_ref, vec_ref, w8_ref, w1_ref,
             du_ref, dx_ref, sums_ref, db1_ref, dwdw_ref, dext_ref, gext_ref, dglu_ref, wacc_ref):
        i = pl.program_id(0)
        vec = vec_ref[...]

        @pl.when(i == 0)
        def _():
            sums_ref[...] = jnp.zeros_like(sums_ref)
            db1_ref[...] = jnp.zeros_like(db1_ref)
            wacc_ref[...] = jnp.zeros_like(wacc_ref)

        dext_ref[0:tm, :] = dcw_ref[...]
        dext_ref[tm:, :] = jnp.where(i < nsteps - 1, dhalo_ref[...], 0.0)
        gext_ref[0:CONV_HALO, :] = jnp.where(i > 0, ghalo_ref[...], 0.0)
        gext_ref[CONV_HALO:, :] = glu_ref[...]

        def tile(r0, c0):
            lanes = slice(c0, c0 + LANES)
            dwin = dext_ref[pl.ds(r0, TILE_ROWS + CONV_HALO), lanes]
            acc = jnp.zeros((TILE_ROWS // SUBLANES, SUBLANES, LANES), F32)
            for b in range(SUBLANES):
                sh = _shift_up(dwin, b)
                for a8 in range(CONV_HALO // SUBLANES):
                    k = CONV_WIDTH - 1 - (SUBLANES * a8 + b)
                    if 0 <= k < CONV_WIDTH:
                        acc = acc + w8_ref[k, :, lanes][None] * _tiles3(sh[SUBLANES * a8:SUBLANES * a8 + TILE_ROWS])
            dglu_ref[pl.ds(r0, TILE_ROWS), lanes] = acc.reshape(TILE_ROWS, LANES)

            dc3 = _tiles3(dwin[:TILE_ROWS])
            gwin = gext_ref[pl.ds(r0, TILE_ROWS + CONV_HALO), lanes]
            for b in range(SUBLANES):
                sh = _shift_up(gwin, b)
                for a8 in range(CONV_HALO // SUBLANES + 1):
                    k = SUBLANES * a8 + b - lead
                    if 0 <= k < CONV_WIDTH:
                        part = jnp.sum(dc3 * _tiles3(sh[SUBLANES * a8:SUBLANES * a8 + TILE_ROWS]), axis=0)
                        wacc_ref[k, :, lanes] += part

        _for_tiles(tm, 0, d, tile)

        dglu = dglu_ref[...]
        av, gv = a_ref[...].astype(F32), g_ref[...].astype(F32)
        sg = _sigmoid(gv)
        da = dglu * sg
        dg = dglu * av * sg * (1.0 - sg)
        du_ref[:, 0:d] = da.astype(_LOWP)
        du_ref[:, d:2 * d] = dg.astype(_LOWP)
        db1_ref[:, 0:d] += _colsum(da)
        db1_ref[:, d:2 * d] += _colsum(dg)
        dh = jnp.zeros((tm, d), F32)
        for q in range(nq):
            dh = dh + _dot_nt(du_ref[:, q * hq:(q + 1) * hq], w1_ref[q])
        xh, r, n = _rms(x_ref[...], vec[0:1])
        sums_ref[2:3, :] += _colsum(dh)
        sums_ref[1:2, :] += _colsum(dh * n)
        dn = dh * (1.0 + vec[1:2])
        sums_ref[0:1, :] += _colsum(dn * xh)
        dx_ref[...] = dxo_ref[...] + _rms_bwd(dn, xh, r, vec[0:1])

        @pl.when(i == nsteps - 1)
        def _():
            dwdw_ref[...] = jnp.sum(wacc_ref[...], axis=1)

    return _call(
        body, dcw, dcw, glu, glu, a, g, x, dxo, vec, w8, w1, carry=carry, name="conv_a_bwd", grid=(nsteps,),
        in_specs=[_rows_spec(tm, d),
                  pl.BlockSpec((CONV_HALO, d), lambda i: (jnp.minimum((i + 1) * hb, last_halo), 0)),
                  _rows_spec(tm, d),
                  pl.BlockSpec((CONV_HALO, d), lambda i: (jnp.maximum(i * hb - 1, 0), 0)),
                  _rows_spec(tm, d), _rows_spec(tm, d), _rows_spec(tm, d), _rows_spec(tm, d),
                  _const_spec(vec.shape), _const_spec(w8.shape), _const_spec(w1.shape)],
        out_specs=[_rows_spec(tm, 2 * d), _rows_spec(tm, d), _const_spec((3, d)), _const_spec((1, 2 * d)),
                   _const_spec((CONV_WIDTH, d))],
        out_shape=[SDS((s, 2 * d), _LOWP), SDS((s, d), F32), SDS((3, d), F32), SDS((1, 2 * d), F32),
                   SDS((CONV_WIDTH, d), F32)],
        scratch_shapes=[pltpu.VMEM((tm + CONV_HALO, d), F32), pltpu.VMEM((tm + CONV_HALO, d), F32),
                        pltpu.VMEM((tm, d), F32), pltpu.VMEM((CONV_WIDTH, SUBLANES, d), F32)],
        sem=("arbitrary",))


def _adamw(w, g, m, v, name):
    r, c = w.shape
    rb = r
    for cand in (512, 352, 256, 184, 128, 64, 32, 16, 8):
        if r % cand == 0 and cand * c * 4 <= ADAM_BLOCK_BYTES:
            rb = cand
            break

    def body(w_ref, g_ref, m_ref, v_ref, d_ref, mo_ref, vo_ref):
        gv = g_ref[...]
        mn = ADAM_B1 * m_ref[...] + (1.0 - ADAM_B1) * gv
        vn = ADAM_B2 * v_ref[...] + (1.0 - ADAM_B2) * (gv * gv)
        m_hat = mn / (1.0 - ADAM_B1 ** ADAM_STEP)
        v_hat = vn / (1.0 - ADAM_B2 ** ADAM_STEP)
        d_ref[...] = -ADAM_LR * (m_hat / (jnp.sqrt(v_hat) + ADAM_EPS) + ADAM_WD * w_ref[...])
        mo_ref[...] = mn
        vo_ref[...] = vn

    spec = pl.BlockSpec((rb, c), lambda i: (i, 0))
    return pl.pallas_call(
        body, name=name, grid=(r // rb,), in_specs=[spec] * 4, out_specs=[spec] * 3,
        out_shape=[SDS((r, c), F32)] * 3, compiler_params=_cparams("parallel"),
    )(w, g, m, v)


def _adamw_nd(w, g, m, v, name):
    shape = w.shape
    two_d = (-1, shape[-1])
    outs = _adamw(w.reshape(two_d), g.reshape(two_d), m.reshape(two_d), v.reshape(two_d), name)
    return [o.reshape(shape) for o in outs]


def _adamw_from_slots(w, m, v, segs, pos, name, carry=None):
    r, c_ = w.shape
    rb = _divisor_block(r, (256, 176, 128, 64, 32, 16))
    for slots, _, first, rows in segs:
        half = slots.shape[1] // 2
        while first % rb or rows % rb or half % rb:
            rb //= 2
    assert rb % 16 == 0, (name, rb)
    plan, start = [], 0
    for slots, _, first, rows in segs:
        plan.append((start, rows // rb, first // rb, slots.shape[1] // 2 // rb))
        start += rows // rb
    assert start * rb == r, (name, start, rb, r)

    def body(pos_ref, w_ref, m_ref, v_ref, *rest):
        seg_refs, (g_ref, d_ref, mo_ref, vo_ref) = rest[:2 * len(segs)], rest[2 * len(segs):]
        i = pl.program_id(0)
        for k, (s0, nblk, b0, nbh) in enumerate(plan):
            @pl.when((i >= s0) & (i < s0 + nblk))
            def _(k=k, s0=s0, b0=b0, nbh=nbh):
                slots = seg_refs[2 * k][...].astype(F32)
                blk = b0 + i - s0
                in_my_half = (blk >= pos_ref[0] * nbh) & (blk < (pos_ref[0] + 1) * nbh)
                own = jnp.where(in_my_half, seg_refs[2 * k + 1][0].astype(F32), slots[3])
                gv = ((own + slots[0]) + slots[1]) + slots[2]
                mn = ADAM_B1 * m_ref[...] + (1.0 - ADAM_B1) * gv
                vn = ADAM_B2 * v_ref[...] + (1.0 - ADAM_B2) * (gv * gv)
                m_hat = mn / (1.0 - ADAM_B1 ** ADAM_STEP)
                v_hat = vn / (1.0 - ADAM_B2 ** ADAM_STEP)
                g_ref[...] = gv
                d_ref[...] = -ADAM_LR * (m_hat / (jnp.sqrt(v_hat) + ADAM_EPS) + ADAM_WD * w_ref[...])
                mo_ref[...] = mn
                vo_ref[...] = vn

    spec = pl.BlockSpec((rb, c_), lambda i, pos: (i, 0))
    seg_specs, seg_args = [], []
    for (slots, mine, _, _), (s0, nblk, b0, nbh) in zip(segs, plan):
        def in_seg(i, s0=s0, nblk=nblk):
            return jnp.clip(i - s0, 0, nblk - 1)
        seg_specs.append(pl.BlockSpec((N_CHIPS, rb, c_), lambda i, pos, f=in_seg, b0=b0: (0, b0 + f(i), 0)))
        seg_specs.append(pl.BlockSpec(
            (1, rb, c_), lambda i, pos, f=in_seg, b0=b0, nbh=nbh: (pos[1], jnp.clip(b0 + f(i) - pos[0] * nbh, 0, nbh - 1), 0)))
        seg_args += [slots, mine]
    return _call(body, w, m, v, *seg_args, carry=carry, prefetch=pos, name=name, grid=(r // rb,),
                 in_specs=[spec] * 3 + seg_specs, out_specs=[spec] * 4, out_shape=[SDS((r, c_), F32)] * 4,
                 sem=("arbitrary",))


def _rows_of(*vs):
    return jnp.concatenate([v.reshape(-1, v.shape[-1]) for v in vs], axis=0)


def kernel(x, c, ada_w, ada_b, norm_mix_g, norm_ffn_g, conv_w1, conv_b1, conv_wdw, conv_bdw, conv_ln_g, conv_ln_b, conv_w2, conv_b2, pool_w, pool_ls, ffn_w_gate, ffn_w_up, ffn_w_down, final_g, loss_target, m_ada_w, m_ada_b, m_norm_mix_g, m_norm_ffn_g, m_conv_w1, m_conv_b1, m_conv_wdw, m_conv_bdw, m_conv_ln_g, m_conv_ln_b, m_conv_w2, m_conv_b2, m_pool_w, m_pool_ls, m_ffn_w_gate, m_ffn_w_up, m_ffn_w_down, m_final_g, v_ada_w, v_ada_b, v_norm_mix_g, v_norm_ffn_g, v_conv_w1, v_conv_b1, v_conv_wdw, v_conv_bdw, v_conv_ln_g, v_conv_ln_b, v_conv_w2, v_conv_b2, v_pool_w, v_pool_ls, v_ffn_w_gate, v_ffn_w_up, v_ffn_w_down, v_final_g):
    s, d = x.shape[1], x.shape[2]
    depth = ada_w.shape[0]
    assert depth == 2 and conv_w1.shape[0] == 1 and pool_w.shape[0] == 1, "one conv layer then one pool layer"
    fq = ffn_w_gate.shape[2]
    n_ada = ada_b.shape[1] // d
    ada_cols = ada_w.shape[2]
    ng, gq, gd = pool_w.shape[1], pool_w.shape[2], pool_w.shape[3]
    cq = conv_wdw.shape[2]
    ax, ay, ac = _mesh_pos()
    myq = 2 * ax + ay
    me = 4 * ax + 2 * ay + ac
    x0 = x.reshape(s, d)
    tgt = loss_target.reshape(s, d)

    n_c = d // LANES
    small_in = jnp.concatenate(
        [c.reshape(n_c, LANES), conv_wdw[0].reshape(-1, LANES), pool_ls.reshape(-1, LANES)], axis=0)
    small_in_all = _all_gather8(small_in, "gather_c_and_small_shards").reshape(N_DEV, -1, LANES)
    c_all = small_in_all[:, :n_c].reshape(N_DEV, d)
    ada_b_cols = lax.dynamic_slice_in_dim(ada_b, myq * ada_cols, ada_cols, axis=1).reshape(depth, 1, ada_cols)
    conv_shards = [conv_w1[0].astype(_LOWP), conv_w2[0].astype(_LOWP)]
    (mod_part,), conv_w = _ada_forward(c_all, ada_w, ada_b_cols, carry=(_Exchange("gather", conv_shards), conv_shards))
    mod_all = _all_gather8(mod_part.reshape(-1, LANES), "gather_mod").reshape(N_DEV, depth, N_DEV, ada_cols)
    mod_mine = lax.dynamic_index_in_dim(mod_all[0::2], me, axis=2, keepdims=False)
    mod = jnp.transpose(mod_mine, (1, 0, 2)).reshape(depth, n_ada, d)

    def gathered(ex_out, shards):
        return [lax.dynamic_update_slice(g, sh[None], (myq, 0, 0)) for g, sh in zip(ex_out, shards)]

    pos = jnp.stack([ac, myq]).astype(jnp.int32)
    gate_t, up_t = jnp.swapaxes(ffn_w_gate, 1, 2), jnp.swapaxes(ffn_w_up, 1, 2)
    ffn0_shards = [gate_t[0].astype(_LOWP), up_t[0].astype(_LOWP), ffn_w_down[0].astype(_LOWP)]
    l1_shards = [pool_w[0].reshape(ng * gq, gd).astype(_LOWP), gate_t[1].astype(_LOWP), up_t[1].astype(_LOWP),
                 ffn_w_down[1].astype(_LOWP)]
    w1, w2_all = gathered(conv_w, conv_shards)
    w2 = w2_all.reshape(d, d)
    wdw_full_shards = small_in_all[0::2, n_c:].reshape(N_CHIPS, CONV_WIDTH + 1, cq)
    wdw = jnp.transpose(wdw_full_shards[:, :CONV_WIDTH], (1, 0, 2)).reshape(CONV_WIDTH, d)
    pls = wdw_full_shards[:, CONV_WIDTH].reshape(1, d)
    w8 = jnp.broadcast_to(wdw[:, None, :], (CONV_WIDTH, SUBLANES, d))

    vec_ca = _rows_of(norm_mix_g[0], mod[0, 1], mod[0, 0])
    h0, a0, g0, glu = _conv_a_fwd(x0, vec_ca, w1, conv_b1)
    vec_cb = _rows_of(conv_bdw, conv_ln_g, conv_ln_b, conv_b2, mod[0, 2])
    (cw, s0, y0, x1), ffn0_w = _conv_b_fwd(glu, w8, vec_cb, w2, x0, carry=(_Exchange("gather", ffn0_shards), ffn0_shards))
    wg0, wu0, wd0 = gathered(ffn0_w, ffn0_shards)
    vec_f0 = _rows_of(norm_ffn_g[0], mod[0, 4], mod[0, 3], mod[0, 5])
    (h2a, gta, upa, acta, y2a, x2), l1_w = _ffn_fwd(x1, vec_f0, wg0, wu0, wd0, "ffn0_fwd",
                                                   carry=(_Exchange("gather", l1_shards), l1_shards))
    pw_all, wg1, wu1, wd1 = gathered(l1_w, l1_shards)
    pw = jnp.transpose(pw_all.reshape(N_CHIPS, ng, gq, gd), (1, 0, 2, 3)).reshape(ng, N_CHIPS * gq, gd)
    vec_p = _rows_of(norm_mix_g[1], mod[1, 1], mod[1, 0], mod[1, 2], pls)
    mixed, yp, x3 = _pool_fwd(x2, vec_p, pw)
    vec_f1 = _rows_of(norm_ffn_g[1], mod[1, 4], mod[1, 3], mod[1, 5])
    (h2b, gtb, upb, actb, y2b, dx4, fin_sums), _ = _ffn_fwd(x3, vec_f1, wg1, wu1, wd1, "ffn1_fwd_loss",
                                                           head=(tgt, final_g.reshape(1, d)))

    def add_halves(partials, swapped, tag):
        return [_add_sibling_half(p, r, pos, f"grad_add_{tag}{k}") for k, (p, r) in enumerate(zip(partials, swapped))]

    def ffn_wgrads(h2, dgt, dup, act, dy, tag):
        gate_up = _wgrad(dgt, h2, _quarter_major(fq), _shared_rows(d), N_CHIPS, fq, d, s, "wgrad_gate" + tag, stack_blocks=2)
        gate_up = _wgrad(dup, h2, _quarter_major(fq), _shared_rows(d), N_CHIPS, fq, d, s, "wgrad_up" + tag,
                         stack=gate_up, stack_blocks=2, block=1)
        down = _wgrad(act, dy, _quarter_major(fq), _shared_rows(d), N_CHIPS, fq, d, s, "wgrad_down" + tag)
        return [gate_up, down]

    (dgtb, dupb, dyb, dx3, sums_f1), _ = _ffn_bwd(dx4, x3, vec_f1, gtb, upb, y2b, wg1, wu1, wd1, "ffn1_bwd")
    part_l1 = ffn_wgrads(h2b, dgtb, dupb, actb, dyb, "1")
    (dyp, dx2, sums_p), swapped_l1 = _pool_bwd(dx3, x2, yp, vec_p, pw, carry=(_SiblingSwap(part_l1), part_l1))
    cs_l1 = add_halves(part_l1, swapped_l1, "l1")
    dpw = _wgrad(mixed, dyp, _column_block(gd), _column_block(gd), ng, gd, gd, s, "wgrad_pool")
    dpw_q = jnp.transpose(dpw.reshape(ng, N_CHIPS, gq, gd), (1, 0, 2, 3)).reshape(N_CHIPS, ng * gq, gd)
    (dgta, dupa, dya, dx1, sums_f0), slots_l1 = _ffn_bwd(dx2, x1, vec_f0, gta, upa, y2a, wg0, wu0, wd0, "ffn0_bwd",
                                                        carry=(_Exchange("reduce", cs_l1), cs_l1))
    part_l0 = ffn_wgrads(h2a, dgta, dupa, acta, dya, "0") + [dpw_q]
    (dy0, dcw, sums_cb), swapped_l0 = _conv_b_bwd(dx1, cw, y0, vec_cb, w2, carry=(_SiblingSwap(part_l0), part_l0))
    dw2 = _wgrad(s0, dy0, _column_block(d // N_CHIPS), _shared_rows(d), N_CHIPS, d // N_CHIPS, d, s, "wgrad_conv2")
    cs_l0 = add_halves(part_l0, swapped_l0, "l0") + add_halves([dw2], _SiblingSwap([dw2]).alone([dw2], "grad_swap_w2"), "w2")
    (du, dx0, sums_ca, db1, dwdw), slots_l0 = _conv_a_bwd(dcw, glu, a0, g0, x0, dx1, vec_ca, w8, w1,
                                                          carry=(_Exchange("reduce", cs_l0), cs_l0))
    dw1 = _wgrad(h0, du, _shared_rows(d), _column_block(2 * d // N_CHIPS), N_CHIPS, d, 2 * d // N_CHIPS, s, "wgrad_conv1")
    cs_conv = add_halves([dw1], _SiblingSwap([dw1]).alone([dw1], "grad_swap_w1"), "w1")

    dmod = jnp.stack([
        _rows_of(sums_ca[2], sums_ca[1], sums_cb[0], sums_f0[2], sums_f0[1], sums_f0[3]),
        _rows_of(sums_p[2], sums_p[1], sums_p[3], sums_f1[2], sums_f1[1], sums_f1[3])])
    small = _rows_of(dmod.reshape(-1, d), sums_ca[0], sums_p[0], sums_f0[0], sums_f1[0], db1.reshape(2, d),
                     sums_cb[4], sums_cb[2], sums_cb[3], sums_cb[1], fin_sums[0], sums_p[4], dwdw, fin_sums[1])
    n_small = small.shape[0]
    pad = (-n_small) % SUBLANES
    small = jnp.concatenate([small, jnp.zeros((pad, d), F32)], axis=0) if pad else small
    small_all = _all_gather8(small.reshape(-1, LANES), "gather_small_grads").reshape(N_DEV, -1, LANES)
    small_sum = _sum_devices(small_all).reshape(-1, d)
    n_dm = depth * n_ada
    g_ada_b = small_sum[0:n_dm].reshape(depth, n_ada * d)
    g_norm_mix = small_sum[n_dm:n_dm + 2]
    g_norm_ffn = small_sum[n_dm + 2:n_dm + 4]
    g_b1 = small_sum[n_dm + 4:n_dm + 6].reshape(1, 2 * d)
    g_bdw, g_lng, g_lnb, g_b2 = (small_sum[n_dm + 6 + k].reshape(1, d) for k in range(4))
    g_final = small_sum[n_dm + 10]
    g_pls = lax.dynamic_slice_in_dim(small_sum[n_dm + 11].reshape(1, d), myq * cq, cq, axis=1)
    g_wdw = lax.dynamic_slice_in_dim(small_sum[n_dm + 12:n_dm + 12 + CONV_WIDTH], myq * cq, cq, axis=1).reshape(conv_wdw.shape)
    loss = small_sum[n_dm + 12 + CONV_WIDTH, 0]
    dmod_all = small_all.reshape(N_DEV, -1, d)[:, 0:n_dm].reshape(N_DEV, depth, n_ada * d)
    dmod_cols = lax.dynamic_slice_in_dim(jnp.transpose(dmod_all, (1, 0, 2)), myq * ada_cols, ada_cols, axis=2)
    g_ada_w = _ada_backward(c_all.T, dmod_cols)

    def small_pack(*vs):
        return jnp.concatenate([v.reshape(-1) for v in vs]).reshape(-1, LANES)

    names = ("ada_b", "norm_mix_g", "norm_ffn_g", "conv_b1", "conv_bdw", "conv_ln_g", "conv_ln_b", "conv_b2", "final_g")
    small_w = (ada_b, norm_mix_g, norm_ffn_g, conv_b1, conv_bdw, conv_ln_g, conv_ln_b, conv_b2, final_g)
    small_g = (g_ada_b, g_norm_mix, g_norm_ffn, g_b1, g_bdw, g_lng, g_lnb, g_b2, g_final)
    small_m = (m_ada_b, m_norm_mix_g, m_norm_ffn_g, m_conv_b1, m_conv_bdw, m_conv_ln_g, m_conv_ln_b, m_conv_b2, m_final_g)
    small_v = (v_ada_b, v_norm_mix_g, v_norm_ffn_g, v_conv_b1, v_conv_bdw, v_conv_ln_g, v_conv_ln_b, v_conv_b2, v_final_g)
    sd, sm, sv = _adamw(small_pack(*small_w), small_pack(*small_g), small_pack(*small_m), small_pack(*small_v), "adamw_small")

    def unpack(flat2d):
        flat = flat2d.reshape(-1)
        out, off = {}, 0
        for nm, wv in zip(names, small_w):
            out[nm] = flat[off:off + wv.size].reshape(wv.shape)
            off += wv.size
        return out

    sd, sm, sv = unpack(sd), unpack(sm), unpack(sv)
    grads = {
        "ada_w": g_ada_w, "ada_b": g_ada_b, "norm_mix_g": g_norm_mix, "norm_ffn_g": g_norm_ffn,
        "conv_b1": g_b1, "conv_wdw": g_wdw, "conv_bdw": g_bdw, "conv_ln_g": g_lng,
        "conv_ln_b": g_lnb, "conv_b2": g_b2, "pool_ls": g_pls, "final_g": g_final,
    }
    plain = {
        "ada_w": (ada_w, m_ada_w, v_ada_w), "conv_wdw": (conv_wdw, m_conv_wdw, v_conv_wdw),
        "pool_ls": (pool_ls, m_pool_ls, v_pool_ls),
    }
    from_slots = {
        "ffn_w_gate": (ffn_w_gate, m_ffn_w_gate, v_ffn_w_gate,
                       lambda: [(slots_l0[0], cs_l0[0], 0, fq), (slots_l1[0], cs_l1[0], 0, fq)]),
        "ffn_w_up": (ffn_w_up, m_ffn_w_up, v_ffn_w_up,
                     lambda: [(slots_l0[0], cs_l0[0], fq, fq), (slots_l1[0], cs_l1[0], fq, fq)]),
        "ffn_w_down": (ffn_w_down, m_ffn_w_down, v_ffn_w_down,
                       lambda: [(slots_l0[1], cs_l0[1], 0, fq), (slots_l1[1], cs_l1[1], 0, fq)]),
        "pool_w": (pool_w, m_pool_w, v_pool_w, lambda: [(slots_l0[2], cs_l0[2], 0, ng * gq)]),
        "conv_w2": (conv_w2, m_conv_w2, v_conv_w2, lambda: [(slots_l0[3], cs_l0[3], 0, d // N_CHIPS)]),
        "conv_w1": (conv_w1, m_conv_w1, v_conv_w1, lambda: [(slots_conv[0], cs_conv[0], 0, d)]),
    }
    order = ("ada_w", "ada_b", "norm_mix_g", "norm_ffn_g", "conv_w1", "conv_b1", "conv_wdw", "conv_bdw", "conv_ln_g",
             "conv_ln_b", "conv_w2", "conv_b2", "pool_w", "pool_ls", "ffn_w_gate", "ffn_w_up", "ffn_w_down", "final_g")
    delta, new_m, new_v = {}, {}, {}
    transposed = ("ffn_w_gate", "ffn_w_up")
    for nm, (wv, mv, vv, segs) in from_slots.items():
        if nm in transposed:
            wv, mv, vv = (jnp.swapaxes(t, 1, 2) for t in (wv, mv, vv))
        two_d = (-1, wv.shape[-1])
        carry = (_Exchange("reduce", cs_conv), cs_conv) if nm == "ffn_w_gate" else None
        outs, brought = _adamw_from_slots(wv.reshape(two_d), mv.reshape(two_d), vv.reshape(two_d), segs(), pos,
                                          "adamw_" + nm, carry=carry)
        if carry is not None:
            slots_conv = brought
        outs = [o.reshape(wv.shape) for o in outs]
        if nm in transposed:
            outs = [jnp.swapaxes(o, 1, 2) for o in outs]
        grads[nm], delta[nm], new_m[nm], new_v[nm] = outs
    for nm in order:
        if nm in from_slots:
            continue
        elif nm in plain:
            wv, mv, vv = plain[nm]
            grads[nm] = grads[nm].reshape(wv.shape)
            delta[nm], new_m[nm], new_v[nm] = _adamw_nd(wv, grads[nm], mv, vv, "adamw_" + nm)
        else:
            delta[nm], new_m[nm], new_v[nm] = sd[nm], sm[nm], sv[nm]
            grads[nm] = grads[nm].reshape(dict(zip(names, small_w))[nm].shape)

    return (loss, dx0.reshape(x.shape), *[grads[n] for n in order], *[delta[n] for n in order],
            *[new_m[n] for n in order], *[new_v[n] for n in order])
```

```python
import functools

import jax
import jax.numpy as jnp
from jax import lax
from jax.experimental import pallas as pl
from jax.experimental.pallas import tpu as pltpu

F32 = jnp.float32
_LOWP = jnp.bfloat16
EPS = 1e-6
CONV_WIDTH = 31
POOL_WINDOWS = (2, 4, 8, 16)
ADAM_LR = 0.001
ADAM_B1 = 0.9
ADAM_B2 = 0.999
ADAM_EPS = 1e-08
ADAM_WD = 0.01
ADAM_STEP = 10

N_CHIPS = 4
N_DEV = 8
LANES = 128
SUBLANES = 8
CONV_HALO = 32
POOL_HALO = 16
TILE_ROWS = 64
VMEM_LIMIT = 56 * 1024 * 1024
ADAM_BLOCK_BYTES = 1 << 20
WGRAD_ROWS = 4096
FFN_FWD_QUARTERS_PER_STEP = 4
FFN_BWD_QUARTERS_PER_STEP = 2
MESH = pl.DeviceIdType.MESH
SDS = jax.ShapeDtypeStruct


def _cparams(*sem):
    return pltpu.CompilerParams(dimension_semantics=sem if sem else None, vmem_limit_bytes=VMEM_LIMIT)


def _row_block(s):
    for tm in (512, 256, 128, 64, 32, 16):
        if s % tm == 0:
            return tm
    raise ValueError(f"sequence length {s} must be a multiple of 16")


def _rows_spec(tm, ncols):
    return pl.BlockSpec((tm, ncols), lambda i, *_: (i, 0))


def _const_spec(shape):
    nd = len(shape)
    return pl.BlockSpec(shape, lambda *_: (0,) * nd)


def _dot(a, b):
    return lax.dot_general(a.astype(_LOWP), b.astype(_LOWP), (((1,), (0,)), ((), ())), preferred_element_type=F32)


def _dot_nt(a, b):
    return lax.dot_general(a.astype(_LOWP), b.astype(_LOWP), (((1,), (1,)), ((), ())), preferred_element_type=F32)


def _dot_tn(a, b):
    return lax.dot_general(a.astype(_LOWP), b.astype(_LOWP), (((0,), (0,)), ((), ())), preferred_element_type=F32)


def _rms(x, gamma):
    r = lax.rsqrt(jnp.mean(x * x, axis=-1, keepdims=True) + EPS)
    xh = x * r
    return xh, r, xh * gamma


def _rms_bwd(dn, xh, r, gamma):
    dxh = dn * gamma
    return r * (dxh - xh * jnp.mean(dxh * xh, axis=-1, keepdims=True))


def _colsum(v):
    return jnp.sum(v, axis=0, keepdims=True)


def _sigmoid(v):
    return jax.nn.sigmoid(v)


def _dsilu(v):
    sg = _sigmoid(v)
    return sg * (1.0 + v * (1.0 - sg))


def _for_tiles(nrows, col0, ncols, fn):
    def step(t, carry):
        r0 = pl.multiple_of(t * TILE_ROWS, TILE_ROWS)
        for col in range(col0, col0 + ncols, LANES):
            fn(r0, col)
        return carry

    lax.fori_loop(0, nrows // TILE_ROWS, step, 0)


def _shift_up(win, b):
    return pltpu.roll(win, win.shape[0] - b, 0) if b else win


def _shift_down(win, b):
    return pltpu.roll(win, b, 0) if b else win


def _tiles3(v):
    return v.reshape(v.shape[0] // SUBLANES, SUBLANES, v.shape[1])


def _resident_scratch(arrays):
    return [pltpu.VMEM(a.shape, a.dtype) for a in arrays] + [pltpu.SemaphoreType.DMA((len(arrays),))]


def _load_resident(hbm_refs, vmem_refs, sems):
    copies = [pltpu.make_async_copy(src, dst, sems.at[k]) for k, (src, dst) in enumerate(zip(hbm_refs, vmem_refs))]
    for cp in copies:
        cp.start()
    for cp in copies:
        cp.wait()


def _mesh_pos():
    return lax.axis_index("x"), lax.axis_index("y"), lax.axis_index("c")


def _other_chips(x, y):
    return [(1 - x, y), (x, 1 - y), (1 - x, 1 - y)]


def _all_gather8(v, name):
    m, n = v.shape

    def body(x_ref, out_ref, send_sems, recv_sems, local_sem):
        x, y, c = _mesh_pos()
        me, sibling = (x, y, c), (x, y, 1 - c)
        chips = _other_chips(x, y)

        def rows(px, py, pc):
            return out_ref.at[pl.ds((4 * px + 2 * py + pc) * m, m), :]

        def copy(k, block, to, src=None):
            return pltpu.make_async_remote_copy(
                src_ref=rows(*block) if src is None else src, dst_ref=rows(*block),
                send_sem=send_sems.at[k], recv_sem=recv_sems.at[k], device_id=to, device_id_type=MESH)

        mine = pltpu.make_async_copy(x_ref, rows(*me), local_sem)
        mine.start()
        first = [copy(0, me, sibling, src=x_ref)]
        first += [copy(1 + j, me, (*chip, c), src=x_ref) for j, chip in enumerate(chips)]
        for cp in first:
            cp.start()
        passed = [copy(4 + j, (*chip, c), sibling) for j, chip in enumerate(chips)]
        for j, chip in enumerate(chips):
            copy(1 + j, (*chip, c), me).wait_recv()
            passed[j].start()
        copy(0, sibling, me).wait_recv()
        for j, chip in enumerate(chips):
            copy(4 + j, (*chip, 1 - c), me).wait_recv()
        for cp in first + passed:
            cp.wait_send()
        mine.wait()

    return pl.pallas_call(
        body, name=name,
        out_shape=SDS((N_DEV * m, n), v.dtype),
        in_specs=[pl.BlockSpec(memory_space=pltpu.VMEM)],
        out_specs=pl.BlockSpec(memory_space=pltpu.VMEM),
        scratch_shapes=[pltpu.SemaphoreType.DMA((7,)), pltpu.SemaphoreType.DMA((7,)), pltpu.SemaphoreType.DMA],
    )(v)


class _Exchange:
    N_SEMS = 7

    def __init__(self, kind, arrays):
        self.kind, self.n = kind, len(arrays)
        self.streams = 3
        if kind == "gather":
            self.out_shape = [SDS((N_CHIPS,) + a.shape, a.dtype) for a in arrays]
        else:
            self.out_shape = [SDS((N_CHIPS, 2 * a.shape[1], a.shape[2]), a.dtype) for a in arrays]
        self.scratch = [pltpu.SemaphoreType.DMA((self.n, self.N_SEMS)), pltpu.SemaphoreType.DMA((self.n, self.N_SEMS))]

    def _ctx(self, ins, outs, send_sems, recv_sems):
        x, y, c = _mesh_pos()
        myq = 2 * x + y
        sibling = (x, y, 1 - c)
        chips = _other_chips(x, y)
        gather = self.kind == "gather"
        ns = self.streams

        def rows(a, h):
            hr = outs[a].shape[1] // 2
            return pl.ds(h * hr, hr)

        def copy(a, ks, kr, src, dst, to):
            return pltpu.make_async_remote_copy(src_ref=src, dst_ref=dst, send_sem=send_sems.at[a, ks],
                                                recv_sem=recv_sems.at[a, kr], device_id=to, device_id_type=MESH)

        def place(a, t, h):
            if gather:
                px, py = chips[t]
                return outs[a].at[2 * px + py, rows(a, h)]
            return outs[a].at[t, rows(a, h)]

        def first(a, t):
            if gather:
                return copy(a, t, t, ins[a].at[rows(a, c)], outs[a].at[myq, rows(a, c)], (*chips[t], c))
            px, py = chips[t]
            return copy(a, t, t, ins[a].at[2 * px + py], outs[a].at[t, rows(a, c)], (px, py, c))

        def landed(a, t):
            return copy(a, t, t, place(a, t, c), place(a, t, c), sibling)

        def relay(a, t):
            return copy(a, ns + t, ns + t, place(a, t, c), place(a, t, c), sibling)

        def relayed(a, t):
            return copy(a, ns + t, ns + t, place(a, t, 1 - c), place(a, t, 1 - c), sibling)

        def own(a):
            return copy(a, 2 * ns, 2 * ns, ins[a].at[myq], outs[a].at[ns, rows(a, c)], sibling)

        return first, landed, relay, relayed, own

    def start(self, ins, outs, send_sems, recv_sems):
        first, _, _, _, own = self._ctx(ins, outs, send_sems, recv_sems)
        for a in range(self.n):
            for t in range(self.streams):
                first(a, t).start()
            if self.kind == "reduce":
                own(a).start()

    def relay(self, ins, outs, send_sems, recv_sems):
        _, landed, relay, _, _ = self._ctx(ins, outs, send_sems, recv_sems)
        for a in range(self.n):
            for t in range(self.streams):
                landed(a, t).wait_recv()
                relay(a, t).start()

    def finish(self, ins, outs, send_sems, recv_sems):
        first, _, relay, relayed, own = self._ctx(ins, outs, send_sems, recv_sems)
        for a in range(self.n):
            for t in range(self.streams):
                relayed(a, t).wait_recv()
            if self.kind == "reduce":
                own(a).wait_recv()
        for a in range(self.n):
            for t in range(self.streams):
                first(a, t).wait_send()
                relay(a, t).wait_send()
            if self.kind == "reduce":
                own(a).wait_send()

    def alone(self, arrays, name):
        n = self.n

        def body(*refs):
            args = (refs[:n], refs[n:2 * n], refs[2 * n], refs[2 * n + 1])
            self.start(*args)
            self.relay(*args)
            self.finish(*args)

        any_spec = pl.BlockSpec(memory_space=pl.ANY)
        return pl.pallas_call(body, name=name, out_shape=self.out_shape, in_specs=[any_spec] * n,
                              out_specs=[any_spec] * n, scratch_shapes=self.scratch)(*arrays)


RELAY_AT = (5, 8)


def _call(body, *args, carry=None, prefetch=None, name, grid, in_specs, out_specs, out_shape, scratch_shapes=(), sem):
    def run(fn, in_specs_, out_specs_, out_shape_, scratch_, sem_, operands):
        if prefetch is None:
            return pl.pallas_call(fn, name=name, grid=grid, in_specs=in_specs_, out_specs=out_specs_, out_shape=out_shape_,
                                  scratch_shapes=scratch_, compiler_params=_cparams(*sem_))(*operands)
        grid_spec = pltpu.PrefetchScalarGridSpec(num_scalar_prefetch=1, grid=grid, in_specs=in_specs_,
                                                 out_specs=out_specs_, scratch_shapes=scratch_)
        return pl.pallas_call(fn, name=name, grid_spec=grid_spec, out_shape=out_shape_,
                              compiler_params=_cparams(*sem_))(prefetch, *operands)

    if carry is None:
        return run(body, list(in_specs), list(out_specs), list(out_shape), list(scratch_shapes), sem, args), None
    ex, arrays = carry
    n_in, n_out, n_scr, n = len(in_specs), len(out_specs), len(scratch_shapes), ex.n
    n_pre = 0 if prefetch is None else 1
    total = 1
    for g in grid:
        total *= g
    relay_step = (total * RELAY_AT[0]) // RELAY_AT[1]

    def hosted(*refs):
        pre, refs = refs[:n_pre], refs[n_pre:]
        ins, cins = refs[:n_in], refs[n_in:n_in + n]
        outs, couts = refs[n_in + n:n_in + n + n_out], refs[n_in + n + n_out:n_in + 2 * n + n_out]
        scr = refs[n_in + 2 * n + n_out:n_in + 2 * n + n_out + n_scr]
        sems = refs[n_in + 2 * n + n_out + n_scr:]
        step = 0
        for ax, g in enumerate(grid):
            step = step * g + pl.program_id(ax)

        @pl.when(step == 0)
        def _():
            ex.start(cins, couts, *sems)

        body(*pre, *ins, *outs, *scr)

        @pl.when(step == relay_step)
        def _():
            ex.relay(cins, couts, *sems)

        @pl.when(step == total - 1)
        def _():
            ex.finish(cins, couts, *sems)

    any_spec = pl.BlockSpec(memory_space=pl.ANY)
    res = run(hosted, list(in_specs) + [any_spec] * n, list(out_specs) + [any_spec] * n, list(out_shape) + ex.out_shape,
              list(scratch_shapes) + ex.scratch, ("arbitrary",) * len(grid), (*args, *arrays))
    return res[:n_out], res[n_out:]


class _SiblingSwap:
    def __init__(self, arrays):
        self.n = len(arrays)
        self.out_shape = [SDS((a.shape[0], a.shape[1] // 2, a.shape[2]), a.dtype) for a in arrays]
        self.scratch = [pltpu.SemaphoreType.DMA((self.n,)), pltpu.SemaphoreType.DMA((self.n,))]

    def _copies(self, ins, outs, send_sems, recv_sems):
        x, y, c = _mesh_pos()
        cps = []
        for a in range(self.n):
            hr = ins[a].shape[1] // 2
            cps.append(pltpu.make_async_remote_copy(
                src_ref=ins[a].at[:, pl.ds((1 - c) * hr, hr)], dst_ref=outs[a],
                send_sem=send_sems.at[a], recv_sem=recv_sems.at[a], device_id=(x, y, 1 - c), device_id_type=MESH))
        return cps

    def start(self, *refs):
        for cp in self._copies(*refs):
            cp.start()

    def relay(self, *refs):
        pass

    def finish(self, *refs):
        for cp in self._copies(*refs):
            cp.wait()

    def alone(self, arrays, name):
        return _Exchange.alone(self, arrays, name)


def _divisor_block(rows, cands):
    for rb in cands:
        if rows % rb == 0:
            return rb
    raise ValueError(rows)


def _add_sibling_half(p, rb1, pos, name):
    nq, r, n = p.shape
    h = r // 2
    rb = _divisor_block(h, (512, 352, 256, 176, 128, 64, 32, 16))
    nb = h // rb

    def body(pos_ref, p_ref, r_ref, o_ref):
        o_ref[...] = (p_ref[...].astype(F32) + r_ref[...].astype(F32)).astype(o_ref.dtype)

    grid_spec = pltpu.PrefetchScalarGridSpec(
        num_scalar_prefetch=1, grid=(nq, nb),
        in_specs=[pl.BlockSpec((1, rb, n), lambda q, i, pos: (q, pos[0] * nb + i, 0)),
                  pl.BlockSpec((1, rb, n), lambda q, i, pos: (q, i, 0))],
        out_specs=pl.BlockSpec((1, rb, n), lambda q, i, pos: (q, i, 0)))
    return pl.pallas_call(
        body, name=name, grid_spec=grid_spec, out_shape=SDS((nq, h, n), p.dtype),
        compiler_params=_cparams("parallel", "parallel"),
    )(pos, p, rb1)


def _sum_devices(g):
    nd, m, n = g.shape

    def body(g_ref, o_ref):
        acc = g_ref[0]
        for d in range(1, nd):
            acc = acc + g_ref[d]
        o_ref[...] = acc

    return pl.pallas_call(body, name="small_grad_sum", out_shape=SDS((m, n), F32))(g)


def _ada_forward(c_all, ada_w, ada_b_cols, carry=None):
    nl, d, ncol = ada_w.shape
    nb = c_all.shape[0]
    tn = 512 if ncol % 512 == 0 else ncol

    def body(c_ref, w_ref, b_ref, o_ref):
        cv = c_ref[...]
        ca = cv * _sigmoid(cv)
        o_ref[0] = _dot(ca, w_ref[0]) + b_ref[0]

    return _call(
        body, c_all, ada_w, ada_b_cols, carry=carry, name="ada_forward", grid=(nl, ncol // tn),
        in_specs=[pl.BlockSpec((nb, d), lambda l, j: (0, 0)),
                  pl.BlockSpec((1, d, tn), lambda l, j: (l, 0, j)),
                  pl.BlockSpec((1, 1, tn), lambda l, j: (l, 0, j))],
        out_specs=[pl.BlockSpec((1, nb, tn), lambda l, j: (l, 0, j))],
        out_shape=[SDS((nl, nb, ncol), F32)], sem=("parallel", "parallel"))


def _ada_backward(c_all_t, dmod_cols):
    d, nb = c_all_t.shape
    nl, _, ncol = dmod_cols.shape
    tn = 512 if ncol % 512 == 0 else ncol

    def body(c_ref, g_ref, o_ref):
        cv = c_ref[...]
        ca = cv * _sigmoid(cv)
        o_ref[0] = _dot(ca, g_ref[0])

    return pl.pallas_call(
        body, name="ada_backward", grid=(nl, ncol // tn),
        in_specs=[pl.BlockSpec((d, nb), lambda l, j: (0, 0)),
                  pl.BlockSpec((1, nb, tn), lambda l, j: (l, 0, j))],
        out_specs=pl.BlockSpec((1, d, tn), lambda l, j: (l, 0, j)),
        out_shape=SDS((nl, d, ncol), F32), compiler_params=_cparams("parallel", "parallel"),
    )(c_all_t, dmod_cols)


def _conv_a_fwd(x, vec, w1, b1):
    s, d = x.shape
    nq, _, hq = w1.shape
    tm = _row_block(s)

    def body(x_ref, vec_ref, w1_ref, b1_ref, h_ref, a_ref, g_ref, glu_ref):
        vec = vec_ref[...]
        _, _, n = _rms(x_ref[...], vec[0:1])
        hb = (n * (1.0 + vec[1:2]) + vec[2:3]).astype(_LOWP)
        h_ref[...] = hb
        u = [_dot(hb, w1_ref[q]) + b1_ref[:, q * hq:(q + 1) * hq] for q in range(nq)]
        for k in range(nq // 2):
            av, gv = u[k], u[nq // 2 + k]
            cols = slice(k * hq, (k + 1) * hq)
            a_ref[:, cols] = av.astype(_LOWP)
            g_ref[:, cols] = gv.astype(_LOWP)
            glu_ref[:, cols] = av * _sigmoid(gv)

    return pl.pallas_call(
        body, name="conv_a_fwd", grid=(s // tm,),
        in_specs=[_rows_spec(tm, d), _const_spec(vec.shape), _const_spec(w1.shape), _const_spec(b1.shape)],
        out_specs=[_rows_spec(tm, d)] * 4,
        out_shape=[SDS((s, d), _LOWP)] * 3 + [SDS((s, d), F32)],
        compiler_params=_cparams("parallel"),
    )(x, vec, w1, b1)


def _conv_b_fwd(glu, w8, vec, w2, x, carry=None):
    s, d = x.shape
    tm = _row_block(s)
    hb = tm // CONV_HALO

    def body(glu_ref, halo_ref, w8_ref, vec_ref, w2_ref, x_ref, cw_ref, s_ref, y_ref, xo_ref, ext_ref):
        i = pl.program_id(0)
        vec = vec_ref[...]
        ext_ref[0:CONV_HALO, :] = jnp.where(i > 0, halo_ref[...], 0.0)
        ext_ref[CONV_HALO:, :] = glu_ref[...]

        def tile(r0, c0):
            lanes = slice(c0, c0 + LANES)
            win = ext_ref[pl.ds(r0, TILE_ROWS + CONV_HALO), lanes]
            acc = jnp.zeros((TILE_ROWS // SUBLANES, SUBLANES, LANES), F32)
            for b in range(SUBLANES):
                sh = _shift_up(win, b)
                for a in range(CONV_HALO // SUBLANES + 1):
                    k = SUBLANES * a + b - (CONV_HALO - CONV_WIDTH + 1)
                    if 0 <= k < CONV_WIDTH:
                        acc = acc + w8_ref[k, :, lanes][None] * _tiles3(sh[SUBLANES * a:SUBLANES * a + TILE_ROWS])
            cw_ref[pl.ds(r0, TILE_ROWS), lanes] = acc.reshape(TILE_ROWS, LANES)

        _for_tiles(tm, 0, d, tile)
        cw = cw_ref[...] + vec[0:1]
        cw_ref[...] = cw
        cc = cw - jnp.mean(cw, axis=-1, keepdims=True)
        ch = cc * lax.rsqrt(jnp.mean(cc * cc, axis=-1, keepdims=True) + EPS)
        lo = ch * vec[1:2] + vec[2:3]
        sv = (lo * _sigmoid(lo)).astype(_LOWP)
        s_ref[...] = sv
        y = _dot(sv, w2_ref[...]) + vec[3:4]
        y_ref[...] = y.astype(_LOWP)
        xo_ref[...] = x_ref[...] + (1.0 + vec[4:5]) * y

    return _call(
        body, glu, glu, w8, vec, w2, x, carry=carry, name="conv_b_fwd", grid=(s // tm,),
        in_specs=[_rows_spec(tm, d),
                  pl.BlockSpec((CONV_HALO, d), lambda i: (jnp.maximum(i * hb - 1, 0), 0)),
                  _const_spec(w8.shape), _const_spec(vec.shape), _const_spec(w2.shape), _rows_spec(tm, d)],
        out_specs=[_rows_spec(tm, d)] * 4,
        out_shape=[SDS((s, d), F32), SDS((s, d), _LOWP), SDS((s, d), _LOWP), SDS((s, d), F32)],
        scratch_shapes=[pltpu.VMEM((tm + CONV_HALO, d), F32)], sem=("parallel",))


def _ffn_fwd(x, vec, wg, wu, wd, name, carry=None, head=None):
    s, d = x.shape
    nq, fq, _ = wg.shape
    tm = _row_block(s)
    nsteps = s // tm
    n_head = 0 if head is None else 2
    qps = FFN_FWD_QUARTERS_PER_STEP

    def body(x_ref, vec_ref, wg_hbm, wu_hbm, wd_hbm, *rest):
        head_refs, rest = rest[:n_head], rest[n_head:]
        h_ref, gt_ref, up_ref, act_ref, y_ref, xo_ref = rest[:6]
        sums_ref = rest[6] if head is not None else None
        hb_ref, yacc_ref, wg_ref, wu_ref, wd_ref, w_sems = rest[6 + n_head // 2:]
        i, p = pl.program_id(0), pl.program_id(1)
        vec = vec_ref[...]

        @pl.when((i == 0) & (p == 0))
        def _():
            _load_resident((wg_hbm, wu_hbm, wd_hbm), (wg_ref, wu_ref, wd_ref), w_sems)
            if head is not None:
                sums_ref[...] = jnp.zeros_like(sums_ref)

        @pl.when(p == 0)
        def _():
            _, _, n = _rms(x_ref[...], vec[0:1])
            hb = (n * (1.0 + vec[1:2]) + vec[2:3]).astype(_LOWP)
            hb_ref[...] = hb
            h_ref[...] = hb
            yacc_ref[...] = jnp.zeros_like(yacc_ref)

        hb = hb_ref[...]
        for k in range(qps):
            q = p * qps + k
            gt = _dot_nt(hb, wg_ref[q])
            up = _dot_nt(hb, wu_ref[q])
            act = (gt * _sigmoid(gt) * up).astype(_LOWP)
            gt_ref[k] = gt.astype(_LOWP)
            up_ref[k] = up.astype(_LOWP)
            act_ref[k] = act
            yacc_ref[...] += _dot(act, wd_ref[q])

        @pl.when(p == nq // qps - 1)
        def _():
            y = yacc_ref[...]
            y_ref[...] = y.astype(_LOWP)
            xo = x_ref[...] + (1.0 + vec[3:4]) * y
            if head is None:
                xo_ref[...] = xo
            else:
                t_ref, g_ref = head_refs
                gamma_v = g_ref[...]
                xh, r, out = _rms(xo, gamma_v)
                err = out - t_ref[...]
                dout = err * (1.0 / d)
                sums_ref[0:1, :] += _colsum(dout * xh)
                sums_ref[1:2, :] += _colsum(0.5 * err * dout)
                xo_ref[...] = _rms_bwd(dout, xh, r, gamma_v)

        if head is not None:
            @pl.when((i == nsteps - 1) & (p == nq // qps - 1))
            def _():
                sums_ref[1:2, :] = jnp.broadcast_to(jnp.sum(sums_ref[1:2, :], axis=-1, keepdims=True), (1, d))

    rows = pl.BlockSpec((tm, d), lambda i, q: (i, 0))
    hid = pl.BlockSpec((qps, tm, fq), lambda i, q: (q, i, 0))
    any_spec = pl.BlockSpec(memory_space=pl.ANY)
    head_args = [] if head is None else [head[0], head[1]]
    head_specs = [] if head is None else [rows, pl.BlockSpec((1, d), lambda i, q: (0, 0))]
    sums_specs = [] if head is None else [pl.BlockSpec((2, d), lambda i, q: (0, 0))]
    sums_shape = [] if head is None else [SDS((2, d), F32)]
    return _call(
        body, x, vec, wg, wu, wd, *head_args, carry=carry, name=name, grid=(nsteps, nq // qps),
        in_specs=[rows, pl.BlockSpec(vec.shape, lambda i, q: (0, 0)), any_spec, any_spec, any_spec] + head_specs,
        out_specs=[rows, hid, hid, hid, rows, rows] + sums_specs,
        out_shape=[SDS((s, d), _LOWP)] + [SDS((nq, s, fq), _LOWP)] * 3 + [SDS((s, d), _LOWP), SDS((s, d), F32)] + sums_shape,
        scratch_shapes=[pltpu.VMEM((tm, d), _LOWP), pltpu.VMEM((tm, d), F32)] + _resident_scratch((wg, wu, wd)),
        sem=("arbitrary", "arbitrary"))


def _pool_inv_count(t0, nrows, w):
    t = t0 + lax.broadcasted_iota(jnp.int32, (nrows, LANES), 0)
    return 1.0 / jnp.minimum(t + 1, w).astype(F32)


def _pool_fwd(x, vec, pw):
    s, d = x.shape
    ng, gd, _ = pw.shape
    tm = _row_block(s)
    hb = tm // POOL_HALO

    def body(x_ref, halo_ref, vec_ref, pw_ref, mixed_ref, yp_ref, xo_ref, hext_ref):
        i = pl.program_id(0)
        vec = vec_ref[...]

        def modulated(xv):
            _, _, n = _rms(xv, vec[0:1])
            return n * (1.0 + vec[1:2]) + vec[2:3]

        hext_ref[0:POOL_HALO, :] = jnp.where(i > 0, modulated(halo_ref[...]), 0.0)
        hext_ref[POOL_HALO:, :] = modulated(x_ref[...])

        for g in range(ng):
            def tile(r0, c0, g=g):
                lanes = slice(c0, c0 + LANES)
                win = hext_ref[pl.ds(r0, TILE_ROWS + POOL_HALO), lanes]
                acc = win
                for step in range(g + 1):
                    acc = acc + _shift_down(acc, 2 ** step)
                inv = _pool_inv_count(i * tm + r0, TILE_ROWS, POOL_WINDOWS[g])
                mixed = acc[POOL_HALO:] * inv - win[POOL_HALO:]
                mixed_ref[pl.ds(r0, TILE_ROWS), lanes] = mixed.astype(_LOWP)

            _for_tiles(tm, g * gd, gd, tile)

        for g in range(ng):
            cols = slice(g * gd, (g + 1) * gd)
            yp = _dot(mixed_ref[:, cols], pw_ref[g])
            yp_ref[:, cols] = yp.astype(_LOWP)
            xo_ref[:, cols] = x_ref[:, cols] + (1.0 + vec[3:4, cols]) * (yp * vec[4:5, cols])

    return pl.pallas_call(
        body, name="pool_fwd", grid=(s // tm,),
        in_specs=[_rows_spec(tm, d),
                  pl.BlockSpec((POOL_HALO, d), lambda i: (jnp.maximum(i * hb - 1, 0), 0)),
                  _const_spec(vec.shape), _const_spec(pw.shape)],
        out_specs=[_rows_spec(tm, d)] * 3,
        out_shape=[SDS((s, d), _LOWP), SDS((s, d), _LOWP), SDS((s, d), F32)],
        scratch_shapes=[pltpu.VMEM((tm + POOL_HALO, d), F32)],
        compiler_params=_cparams("parallel"),
    )(x, x, vec, pw)


def _ffn_bwd(dxo, x, vec, gt, up, y, wg, wu, wd, name, carry=None):
    s, d = x.shape
    nq, fq, _ = wg.shape
    tm = _row_block(s)
    qps = FFN_BWD_QUARTERS_PER_STEP

    def body(dxo_ref, x_ref, vec_ref, gt_ref, up_ref, y_ref, wg_hbm, wu_hbm, wd_hbm,
             dgt_ref, dup_ref, dy_ref, dx_ref, sums_ref, dyb_ref, dh_ref, wg_ref, wu_ref, wd_ref, w_sems):
        i, p = pl.program_id(0), pl.program_id(1)
        vec = vec_ref[...]

        @pl.when((i == 0) & (p == 0))
        def _():
            _load_resident((wg_hbm, wu_hbm, wd_hbm), (wg_ref, wu_ref, wd_ref), w_sems)
            sums_ref[...] = jnp.zeros_like(sums_ref)

        @pl.when(p == 0)
        def _():
            dxv = dxo_ref[...]
            dyb = (dxv * (1.0 + vec[3:4])).astype(_LOWP)
            dyb_ref[...] = dyb
            dy_ref[...] = dyb
            dh_ref[...] = jnp.zeros_like(dh_ref)
            sums_ref[3:4, :] += _colsum(dxv * y_ref[...].astype(F32))

        halves = [slice(k * (tm // 2), (k + 1) * (tm // 2)) for k in range(2)]
        for k in range(qps):
            q = p * qps + k
            dacts = [_dot_nt(dyb_ref[rs, :], wd_ref[q]) for rs in halves]
            dgus = []
            for rs, dact in zip(halves, dacts):
                gtv = gt_ref[k, rs, :].astype(F32)
                sg = _sigmoid(gtv)
                dgt = (dact * up_ref[k, rs, :].astype(F32) * (sg * (1.0 + gtv * (1.0 - sg)))).astype(_LOWP)
                dup = (dact * (gtv * sg)).astype(_LOWP)
                dgt_ref[k, rs, :] = dgt
                dup_ref[k, rs, :] = dup
                dgus.append((dgt, dup))
            for rs, (dgt, dup) in zip(halves, dgus):
                dh_ref[rs, :] += _dot(dgt, wg_ref[q]) + _dot(dup, wu_ref[q])

        @pl.when(p == nq // qps - 1)
        def _():
            dh = dh_ref[...]
            xh, r, n = _rms(x_ref[...], vec[0:1])
            sums_ref[2:3, :] += _colsum(dh)
            sums_ref[1:2, :] += _colsum(dh * n)
            dn = dh * (1.0 + vec[1:2])
            sums_ref[0:1, :] += _colsum(dn * xh)
            dx_ref[...] = dxo_ref[...] + _rms_bwd(dn, xh, r, vec[0:1])

    rows = pl.BlockSpec((tm, d), lambda i, q: (i, 0))
    hid = pl.BlockSpec((qps, tm, fq), lambda i, q: (q, i, 0))
    return _call(
        body, dxo, x, vec, gt, up, y, wg, wu, wd, carry=carry, name=name, grid=(s // tm, nq // qps),
        in_specs=[rows, rows, pl.BlockSpec(vec.shape, lambda i, q: (0, 0)), hid, hid, rows] +
                 [pl.BlockSpec(memory_space=pl.ANY)] * 3,
        out_specs=[hid, hid, rows, rows, pl.BlockSpec((4, d), lambda i, q: (0, 0))],
        out_shape=[SDS((nq, s, fq), _LOWP)] * 2 + [SDS((s, d), _LOWP), SDS((s, d), F32), SDS((4, d), F32)],
        scratch_shapes=[pltpu.VMEM((tm, d), _LOWP), pltpu.VMEM((tm, d), F32)] + _resident_scratch((wg, wu, wd)),
        sem=("arbitrary", "arbitrary"))


def _wgrad(a, b, a_spec, b_spec, nq, ka, nb, s, name, stack=None, stack_blocks=1, block=0):
    tk = min(WGRAD_ROWS, s)
    nk = s // tk

    def body(a_ref, b_ref, *rest):
        o_ref, acc_ref = rest[-2:]
        k = pl.program_id(1)

        @pl.when(k == 0)
        def _():
            acc_ref[...] = jnp.zeros_like(acc_ref)

        av = a_ref[...].reshape(tk, ka)
        bv = b_ref[...].reshape(tk, nb)
        acc_ref[...] += _dot_tn(av, bv)

        @pl.when(k == nk - 1)
        def _():
            o_ref[0] = acc_ref[...].astype(o_ref.dtype)

    extra = [] if stack is None else [stack]
    return pl.pallas_call(
        body, name=name, grid=(nq, nk),
        in_specs=[a_spec(tk), b_spec(tk)] + [pl.BlockSpec(memory_space=pl.ANY)] * len(extra),
        out_specs=pl.BlockSpec((1, ka, nb), lambda q, k: (q, block, 0)),
        out_shape=SDS((nq, stack_blocks * ka, nb), _LOWP),
        scratch_shapes=[pltpu.VMEM((ka, nb), F32)],
        input_output_aliases={2: 0} if extra else {},
        compiler_params=_cparams("parallel", "arbitrary"),
    )(a, b, *extra)


def _shared_rows(ncols):
    return lambda tk: pl.BlockSpec((tk, ncols), lambda q, k: (k, 0))


def _column_block(ncols):
    return lambda tk: pl.BlockSpec((tk, ncols), lambda q, k: (k, q))


def _quarter_major(ncols):
    return lambda tk: pl.BlockSpec((1, tk, ncols), lambda q, k: (q, k, 0))


def _pool_bwd(dxo, x, yp, vec, pw, carry=None):
    s, d = x.shape
    ng, gd, _ = pw.shape
    tm = _row_block(s)
    hb = tm // POOL_HALO
    nsteps = s // tm
    last_halo = s // POOL_HALO - 1

    def body(dxo_ref, halo_ref, x_ref, yp_ref, vec_ref, pw_ref, dyp_ref, dx_ref, sums_ref, eext_ref, dh_ref):
        i = pl.program_id(0)
        vec = vec_ref[...]

        @pl.when(i == 0)
        def _():
            sums_ref[...] = jnp.zeros_like(sums_ref)

        dxv = dxo_ref[...]
        ypv = yp_ref[...].astype(F32)
        gate1 = 1.0 + vec[3:4]
        sums_ref[3:4, :] += _colsum(dxv * (ypv * vec[4:5]))
        sums_ref[4:5, :] += _colsum(dxv * gate1 * ypv)
        dyp = (dxv * gate1 * vec[4:5]).astype(_LOWP)
        dyp_ref[...] = dyp
        dyp_halo = (jnp.where(i < nsteps - 1, halo_ref[...], 0.0) * gate1 * vec[4:5]).astype(_LOWP)
        for g in range(ng):
            cols = slice(g * gd, (g + 1) * gd)
            dm = _dot_nt(dyp[:, cols], pw_ref[g])
            dh_ref[:, cols] = dm
            inv = _pool_inv_count(i * tm, tm, POOL_WINDOWS[g])
            eext_ref[0:tm, cols] = dm * jnp.concatenate([inv] * (gd // LANES), axis=1)
            eext_ref[tm:, cols] = _dot_nt(dyp_halo[:, cols], pw_ref[g]) * (1.0 / POOL_WINDOWS[g])

        for g in range(ng):
            def tile(r0, c0, g=g):
                acc = eext_ref[pl.ds(r0, TILE_ROWS + POOL_HALO), c0:c0 + LANES]
                for step in range(g + 1):
                    acc = acc + _shift_up(acc, 2 ** step)
                here = (pl.ds(r0, TILE_ROWS), slice(c0, c0 + LANES))
                dh_ref[here] = acc[:TILE_ROWS] - dh_ref[here]

            _for_tiles(tm, g * gd, gd, tile)

        dh = dh_ref[...]
        xh, r, n = _rms(x_ref[...], vec[0:1])
        sums_ref[2:3, :] += _colsum(dh)
        sums_ref[1:2, :] += _colsum(dh * n)
        dn = dh * (1.0 + vec[1:2])
        sums_ref[0:1, :] += _colsum(dn * xh)
        dx_ref[...] = dxv + _rms_bwd(dn, xh, r, vec[0:1])

    return _call(
        body, dxo, dxo, x, yp, vec, pw, carry=carry, name="pool_bwd", grid=(nsteps,),
        in_specs=[_rows_spec(tm, d),
                  pl.BlockSpec((POOL_HALO, d), lambda i: (jnp.minimum((i + 1) * hb, last_halo), 0)),
                  _rows_spec(tm, d), _rows_spec(tm, d), _const_spec(vec.shape), _const_spec(pw.shape)],
        out_specs=[_rows_spec(tm, d), _rows_spec(tm, d), _const_spec((5, d))],
        out_shape=[SDS((s, d), _LOWP), SDS((s, d), F32), SDS((5, d), F32)],
        scratch_shapes=[pltpu.VMEM((tm + POOL_HALO, d), F32), pltpu.VMEM((tm, d), F32)], sem=("arbitrary",))


def _conv_b_bwd(dxo, cw, y, vec, w2, carry=None):
    s, d = dxo.shape
    tm = _row_block(s)

    def body(dxo_ref, cw_ref, y_ref, vec_ref, w2_ref, dy_ref, dcw_ref, sums_ref):
        i = pl.program_id(0)
        vec = vec_ref[...]

        @pl.when(i == 0)
        def _():
            sums_ref[...] = jnp.zeros_like(sums_ref)

        dxv = dxo_ref[...]
        dy = dxv * (1.0 + vec[4:5])
        sums_ref[0:1, :] += _colsum(dxv * y_ref[...].astype(F32))
        sums_ref[1:2, :] += _colsum(dy)
        dyb = dy.astype(_LOWP)
        dy_ref[...] = dyb
        ds = _dot_nt(dyb, w2_ref[...])
        cw = cw_ref[...]
        cc = cw - jnp.mean(cw, axis=-1, keepdims=True)
        rs = lax.rsqrt(jnp.mean(cc * cc, axis=-1, keepdims=True) + EPS)
        ch = cc * rs
        lo = ch * vec[1:2] + vec[2:3]
        dlo = ds * _dsilu(lo)
        sums_ref[2:3, :] += _colsum(dlo * ch)
        sums_ref[3:4, :] += _colsum(dlo)
        dch = dlo * vec[1:2]
        dcw = rs * (dch - jnp.mean(dch, axis=-1, keepdims=True) - ch * jnp.mean(dch * ch, axis=-1, keepdims=True))
        sums_ref[4:5, :] += _colsum(dcw)
        dcw_ref[...] = dcw

    return _call(
        body, dxo, cw, y, vec, w2, carry=carry, name="conv_b_bwd", grid=(s // tm,),
        in_specs=[_rows_spec(tm, d), _rows_spec(tm, d), _rows_spec(tm, d), _const_spec(vec.shape), _const_spec(w2.shape)],
        out_specs=[_rows_spec(tm, d), _rows_spec(tm, d), _const_spec((5, d))],
        out_shape=[SDS((s, d), _LOWP), SDS((s, d), F32), SDS((5, d), F32)], sem=("arbitrary",))


def _conv_a_bwd(dcw, glu, a, g, x, dxo, vec, w8, w1, carry=None):
    s, d = x.shape
    nq, _, hq = w1.shape
    tm = _row_block(s)
    hb = tm // CONV_HALO
    nsteps = s // tm
    last_halo = s // CONV_HALO - 1
    lead = CONV_HALO - CONV_WIDTH + 1

    def body(dcw_ref, dhalo_ref, glu_ref, ghalo_ref, a_ref, g_ref, x_ref, dxo_ref, vec_ref, w8_ref, w1_ref,
             du_ref, dx_ref, sums_ref, db1_ref, dwdw_ref, dext_ref, gext_ref, dglu_ref, wacc_ref):
        i = pl.program_id(0)
        vec = vec_ref[...]

        @pl.when(i == 0)
        def _():
            sums_ref[...] = jnp.zeros_like(sums_ref)
            db1_ref[...] = jnp.zeros_like(db1_ref)
            wacc_ref[...] = jnp.zeros_like(wacc_ref)

        dext_ref[0:tm, :] = dcw_ref[...]
        dext_ref[tm:, :] = jnp.where(i < nsteps - 1, dhalo_ref[...], 0.0)
        gext_ref[0:CONV_HALO, :] = jnp.where(i > 0, ghalo_ref[...], 0.0)
        gext_ref[CONV_HALO:, :] = glu_ref[...]

        def tile(r0, c0):
            lanes = slice(c0, c0 + LANES)
            dwin = dext_ref[pl.ds(r0, TILE_ROWS + CONV_HALO), lanes]
            acc = jnp.zeros((TILE_ROWS // SUBLANES, SUBLANES, LANES), F32)
            for b in range(SUBLANES):
                sh = _shift_up(dwin, b)
                for a8 in range(CONV_HALO // SUBLANES):
                    k = CONV_WIDTH - 1 - (SUBLANES * a8 + b)
                    if 0 <= k < CONV_WIDTH:
                        acc = acc + w8_ref[k, :, lanes][None] * _tiles3(sh[SUBLANES * a8:SUBLANES * a8 + TILE_ROWS])
            dglu_ref[pl.ds(r0, TILE_ROWS), lanes] = acc.reshape(TILE_ROWS, LANES)

            dc3 = _tiles3(dwin[:TILE_ROWS])
            gwin = gext_ref[pl.ds(r0, TILE_ROWS + CONV_HALO), lanes]
            for b in range(SUBLANES):
                sh = _shift_up(gwin, b)
                for a8 in range(CONV_HALO // SUBLANES + 1):
                    k = SUBLANES * a8 + b - lead
                    if 0 <= k < CONV_WIDTH:
                        part = jnp.sum(dc3 * _tiles3(sh[SUBLANES * a8:SUBLANES * a8 + TILE_ROWS]), axis=0)
                        wacc_ref[k, :, lanes] += part

        _for_tiles(tm, 0, d, tile)

        dglu = dglu_ref[...]
        av, gv = a_ref[...].astype(F32), g_ref[...].astype(F32)
        sg = _sigmoid(gv)
        da = dglu * sg
        dg = dglu * av * sg * (1.0 - sg)
        du_ref[:, 0:d] = da.astype(_LOWP)
        du_ref[:, d:2 * d] = dg.astype(_LOWP)
        db1_ref[:, 0:d] += _colsum(da)
        db1_ref[:, d:2 * d] += _colsum(dg)
        dh = jnp.zeros((tm, d), F32)
        for q in range(nq):
            dh = dh + _dot_nt(du_ref[:, q * hq:(q + 1) * hq], w1_ref[q])
        xh, r, n = _rms(x_ref[...], vec[0:1])
        sums_ref[2:3, :] += _colsum(dh)
        sums_ref[1:2, :] += _colsum(dh * n)
        dn = dh * (1.0 + vec[1:2])
        sums_ref[0:1, :] += _colsum(dn * xh)
        dx_ref[...] = dxo_ref[...] + _rms_bwd(dn, xh, r, vec[0:1])

        @pl.when(i == nsteps - 1)
        def _():
            dwdw_ref[...] = jnp.sum(wacc_ref[...], axis=1)

    return _call(
        body, dcw, dcw, glu, glu, a, g, x, dxo, vec, w8, w1, carry=carry, name="conv_a_bwd", grid=(nsteps,),
        in_specs=[_rows_spec(tm, d),
                  pl.BlockSpec((CONV_HALO, d), lambda i: (jnp.minimum((i + 1) * hb, last_halo), 0)),
                  _rows_spec(tm, d),
                  pl.BlockSpec((CONV_HALO, d), lambda i: (jnp.maximum(i * hb - 1, 0), 0)),
                  _rows_spec(tm, d), _rows_spec(tm, d), _rows_spec(tm, d), _rows_spec(tm, d),
                  _const_spec(vec.shape), _const_spec(w8.shape), _const_spec(w1.shape)],
        out_specs=[_rows_spec(tm, 2 * d), _rows_spec(tm, d), _const_spec((3, d)), _const_spec((1, 2 * d)),
                   _const_spec((CONV_WIDTH, d))],
        out_shape=[SDS((s, 2 * d), _LOWP), SDS((s, d), F32), SDS((3, d), F32), SDS((1, 2 * d), F32),
                   SDS((CONV_WIDTH, d), F32)],
        scratch_shapes=[pltpu.VMEM((tm + CONV_HALO, d), F32), pltpu.VMEM((tm + CONV_HALO, d), F32),
                        pltpu.VMEM((tm, d), F32), pltpu.VMEM((CONV_WIDTH, SUBLANES, d), F32)],
        sem=("arbitrary",))


def _adamw(w, g, m, v, name):
    r, c = w.shape
    rb = r
    for cand in (512, 352, 256, 184, 128, 64, 32, 16, 8):
        if r % cand == 0 and cand * c * 4 <= ADAM_BLOCK_BYTES:
            rb = cand
            break

    def body(w_ref, g_ref, m_ref, v_ref, d_ref, mo_ref, vo_ref):
        gv = g_ref[...]
        mn = ADAM_B1 * m_ref[...] + (1.0 - ADAM_B1) * gv
        vn = ADAM_B2 * v_ref[...] + (1.0 - ADAM_B2) * (gv * gv)
        m_hat = mn / (1.0 - ADAM_B1 ** ADAM_STEP)
        v_hat = vn / (1.0 - ADAM_B2 ** ADAM_STEP)
        d_ref[...] = -ADAM_LR * (m_hat / (jnp.sqrt(v_hat) + ADAM_EPS) + ADAM_WD * w_ref[...])
        mo_ref[...] = mn
        vo_ref[...] = vn

    spec = pl.BlockSpec((rb, c), lambda i: (i, 0))
    return pl.pallas_call(
        body, name=name, grid=(r // rb,), in_specs=[spec] * 4, out_specs=[spec] * 3,
        out_shape=[SDS((r, c), F32)] * 3, compiler_params=_cparams("parallel"),
    )(w, g, m, v)


def _adamw_nd(w, g, m, v, name):
    shape = w.shape
    two_d = (-1, shape[-1])
    outs = _adamw(w.reshape(two_d), g.reshape(two_d), m.reshape(two_d), v.reshape(two_d), name)
    return [o.reshape(shape) for o in outs]


def _adamw_from_slots(w, m, v, segs, pos, name, carry=None):
    r, c_ = w.shape
    rb = _divisor_block(r, (256, 176, 128, 64, 32, 16))
    for slots, _, first, rows in segs:
        half = slots.shape[1] // 2
        while first % rb or rows % rb or half % rb:
            rb //= 2
    assert rb % 16 == 0, (name, rb)
    plan, start = [], 0
    for slots, _, first, rows in segs:
        plan.append((start, rows // rb, first // rb, slots.shape[1] // 2 // rb))
        start += rows // rb
    assert start * rb == r, (name, start, rb, r)

    def body(pos_ref, w_ref, m_ref, v_ref, *rest):
        seg_refs, (g_ref, d_ref, mo_ref, vo_ref) = rest[:2 * len(segs)], rest[2 * len(segs):]
        i = pl.program_id(0)
        for k, (s0, nblk, b0, nbh) in enumerate(plan):
            @pl.when((i >= s0) & (i < s0 + nblk))
            def _(k=k, s0=s0, b0=b0, nbh=nbh):
                slots = seg_refs[2 * k][...].astype(F32)
                blk = b0 + i - s0
                in_my_half = (blk >= pos_ref[0] * nbh) & (blk < (pos_ref[0] + 1) * nbh)
                own = jnp.where(in_my_half, seg_refs[2 * k + 1][0].astype(F32), slots[3])
                gv = ((own + slots[0]) + slots[1]) + slots[2]
                mn = ADAM_B1 * m_ref[...] + (1.0 - ADAM_B1) * gv
                vn = ADAM_B2 * v_ref[...] + (1.0 - ADAM_B2) * (gv * gv)
                m_hat = mn / (1.0 - ADAM_B1 ** ADAM_STEP)
                v_hat = vn / (1.0 - ADAM_B2 ** ADAM_STEP)
                g_ref[...] = gv
                d_ref[...] = -ADAM_LR * (m_hat / (jnp.sqrt(v_hat) + ADAM_EPS) + ADAM_WD * w_ref[...])
                mo_ref[...] = mn
                vo_ref[...] = vn

    spec = pl.BlockSpec((rb, c_), lambda i, pos: (i, 0))
    seg_specs, seg_args = [], []
    for (slots, mine, _, _), (s0, nblk, b0, nbh) in zip(segs, plan):
        def in_seg(i, s0=s0, nblk=nblk):
            return jnp.clip(i - s0, 0, nblk - 1)
        seg_specs.append(pl.BlockSpec((N_CHIPS, rb, c_), lambda i, pos, f=in_seg, b0=b0: (0, b0 + f(i), 0)))
        seg_specs.append(pl.BlockSpec(
            (1, rb, c_), lambda i, pos, f=in_seg, b0=b0, nbh=nbh: (pos[1], jnp.clip(b0 + f(i) - pos[0] * nbh, 0, nbh - 1), 0)))
        seg_args += [slots, mine]
    return _call(body, w, m, v, *seg_args, carry=carry, prefetch=pos, name=name, grid=(r // rb,),
                 in_specs=[spec] * 3 + seg_specs, out_specs=[spec] * 4, out_shape=[SDS((r, c_), F32)] * 4,
                 sem=("arbitrary",))


def _rows_of(*vs):
    return jnp.concatenate([v.reshape(-1, v.shape[-1]) for v in vs], axis=0)


def kernel(x, c, ada_w, ada_b, norm_mix_g, norm_ffn_g, conv_w1, conv_b1, conv_wdw, conv_bdw, conv_ln_g, conv_ln_b, conv_w2, conv_b2, pool_w, pool_ls, ffn_w_gate, ffn_w_up, ffn_w_down, final_g, loss_target, m_ada_w, m_ada_b, m_norm_mix_g, m_norm_ffn_g, m_conv_w1, m_conv_b1, m_conv_wdw, m_conv_bdw, m_conv_ln_g, m_conv_ln_b, m_conv_w2, m_conv_b2, m_pool_w, m_pool_ls, m_ffn_w_gate, m_ffn_w_up, m_ffn_w_down, m_final_g, v_ada_w, v_ada_b, v_norm_mix_g, v_norm_ffn_g, v_conv_w1, v_conv_b1, v_conv_wdw, v_conv_bdw, v_conv_ln_g, v_conv_ln_b, v_conv_w2, v_conv_b2, v_pool_w, v_pool_ls, v_ffn_w_gate, v_ffn_w_up, v_ffn_w_down, v_final_g):
    s, d = x.shape[1], x.shape[2]
    depth = ada_w.shape[0]
    assert depth == 2 and conv_w1.shape[0] == 1 and pool_w.shape[0] == 1, "one conv layer then one pool layer"
    fq = ffn_w_gate.shape[2]
    n_ada = ada_b.shape[1] // d
    ada_cols = ada_w.shape[2]
    ng, gq, gd = pool_w.shape[1], pool_w.shape[2], pool_w.shape[3]
    cq = conv_wdw.shape[2]
    ax, ay, ac = _mesh_pos()
    myq = 2 * ax + ay
    me = 4 * ax + 2 * ay + ac
    x0 = x.reshape(s, d)
    tgt = loss_target.reshape(s, d)

    n_c = d // LANES
    small_in = jnp.concatenate(
        [c.reshape(n_c, LANES), conv_wdw[0].reshape(-1, LANES), pool_ls.reshape(-1, LANES)], axis=0)
    small_in_all = _all_gather8(small_in, "gather_c_and_small_shards").reshape(N_DEV, -1, LANES)
    c_all = small_in_all[:, :n_c].reshape(N_DEV, d)
    ada_b_cols = lax.dynamic_slice_in_dim(ada_b, myq * ada_cols, ada_cols, axis=1).reshape(depth, 1, ada_cols)
    conv_shards = [conv_w1[0].astype(_LOWP), conv_w2[0].astype(_LOWP)]
    (mod_part,), conv_w = _ada_forward(c_all, ada_w, ada_b_cols, carry=(_Exchange("gather", conv_shards), conv_shards))
    mod_all = _all_gather8(mod_part.reshape(-1, LANES), "gather_mod").reshape(N_DEV, depth, N_DEV, ada_cols)
    mod_mine = lax.dynamic_index_in_dim(mod_all[0::2], me, axis=2, keepdims=False)
    mod = jnp.transpose(mod_mine, (1, 0, 2)).reshape(depth, n_ada, d)

    def gathered(ex_out, shards):
        return [lax.dynamic_update_slice(g, sh[None], (myq, 0, 0)) for g, sh in zip(ex_out, shards)]

    pos = jnp.stack([ac, myq]).astype(jnp.int32)
    gate_t, up_t = jnp.swapaxes(ffn_w_gate, 1, 2), jnp.swapaxes(ffn_w_up, 1, 2)
    ffn0_shards = [gate_t[0].astype(_LOWP), up_t[0].astype(_LOWP), ffn_w_down[0].astype(_LOWP)]
    l1_shards = [pool_w[0].reshape(ng * gq, gd).astype(_LOWP), gate_t[1].astype(_LOWP), up_t[1].astype(_LOWP),
                 ffn_w_down[1].astype(_LOWP)]
    w1, w2_all = gathered(conv_w, conv_shards)
    w2 = w2_all.reshape(d, d)
    wdw_full_shards = small_in_all[0::2, n_c:].reshape(N_CHIPS, CONV_WIDTH + 1, cq)
    wdw = jnp.transpose(wdw_full_shards[:, :CONV_WIDTH], (1, 0, 2)).reshape(CONV_WIDTH, d)
    pls = wdw_full_shards[:, CONV_WIDTH].reshape(1, d)
    w8 = jnp.broadcast_to(wdw[:, None, :], (CONV_WIDTH, SUBLANES, d))

    vec_ca = _rows_of(norm_mix_g[0], mod[0, 1], mod[0, 0])
    h0, a0, g0, glu = _conv_a_fwd(x0, vec_ca, w1, conv_b1)
    vec_cb = _rows_of(conv_bdw, conv_ln_g, conv_ln_b, conv_b2, mod[0, 2])
    (cw, s0, y0, x1), ffn0_w = _conv_b_fwd(glu, w8, vec_cb, w2, x0, carry=(_Exchange("gather", ffn0_shards), ffn0_shards))
    wg0, wu0, wd0 = gathered(ffn0_w, ffn0_shards)
    vec_f0 = _rows_of(norm_ffn_g[0], mod[0, 4], mod[0, 3], mod[0, 5])
    (h2a, gta, upa, acta, y2a, x2), l1_w = _ffn_fwd(x1, vec_f0, wg0, wu0, wd0, "ffn0_fwd",
                                                   carry=(_Exchange("gather", l1_shards), l1_shards))
    pw_all, wg1, wu1, wd1 = gathered(l1_w, l1_shards)
    pw = jnp.transpose(pw_all.reshape(N_CHIPS, ng, gq, gd), (1, 0, 2, 3)).reshape(ng, N_CHIPS * gq, gd)
    vec_p = _rows_of(norm_mix_g[1], mod[1, 1], mod[1, 0], mod[1, 2], pls)
    mixed, yp, x3 = _pool_fwd(x2, vec_p, pw)
    vec_f1 = _rows_of(norm_ffn_g[1], mod[1, 4], mod[1, 3], mod[1, 5])
    (h2b, gtb, upb, actb, y2b, dx4, fin_sums), _ = _ffn_fwd(x3, vec_f1, wg1, wu1, wd1, "ffn1_fwd_loss",
                                                           head=(tgt, final_g.reshape(1, d)))

    def add_halves(partials, swapped, tag):
        return [_add_sibling_half(p, r, pos, f"grad_add_{tag}{k}") for k, (p, r) in enumerate(zip(partials, swapped))]

    def ffn_wgrads(h2, dgt, dup, act, dy, tag):
        gate_up = _wgrad(dgt, h2, _quarter_major(fq), _shared_rows(d), N_CHIPS, fq, d, s, "wgrad_gate" + tag, stack_blocks=2)
        gate_up = _wgrad(dup, h2, _quarter_major(fq), _shared_rows(d), N_CHIPS, fq, d, s, "wgrad_up" + tag,
                         stack=gate_up, stack_blocks=2, block=1)
        down = _wgrad(act, dy, _quarter_major(fq), _shared_rows(d), N_CHIPS, fq, d, s, "wgrad_down" + tag)
        return [gate_up, down]

    (dgtb, dupb, dyb, dx3, sums_f1), _ = _ffn_bwd(dx4, x3, vec_f1, gtb, upb, y2b, wg1, wu1, wd1, "ffn1_bwd")
    part_l1 = ffn_wgrads(h2b, dgtb, dupb, actb, dyb, "1")
    (dyp, dx2, sums_p), swapped_l1 = _pool_bwd(dx3, x2, yp, vec_p, pw, carry=(_SiblingSwap(part_l1), part_l1))
    cs_l1 = add_halves(part_l1, swapped_l1, "l1")
    dpw = _wgrad(mixed, dyp, _column_block(gd), _column_block(gd), ng, gd, gd, s, "wgrad_pool")
    dpw_q = jnp.transpose(dpw.reshape(ng, N_CHIPS, gq, gd), (1, 0, 2, 3)).reshape(N_CHIPS, ng * gq, gd)
    (dgta, dupa, dya, dx1, sums_f0), slots_l1 = _ffn_bwd(dx2, x1, vec_f0, gta, upa, y2a, wg0, wu0, wd0, "ffn0_bwd",
                                                        carry=(_Exchange("reduce", cs_l1), cs_l1))
    part_l0 = ffn_wgrads(h2a, dgta, dupa, acta, dya, "0") + [dpw_q]
    (dy0, dcw, sums_cb), swapped_l0 = _conv_b_bwd(dx1, cw, y0, vec_cb, w2, carry=(_SiblingSwap(part_l0), part_l0))
    dw2 = _wgrad(s0, dy0, _column_block(d // N_CHIPS), _shared_rows(d), N_CHIPS, d // N_CHIPS, d, s, "wgrad_conv2")
    cs_l0 = add_halves(part_l0, swapped_l0, "l0") + add_halves([dw2], _SiblingSwap([dw2]).alone([dw2], "grad_swap_w2"), "w2")
    (du, dx0, sums_ca, db1, dwdw), slots_l0 = _conv_a_bwd(dcw, glu, a0, g0, x0, dx1, vec_ca, w8, w1,
                                                          carry=(_Exchange("reduce", cs_l0), cs_l0))
    dw1 = _wgrad(h0, du, _shared_rows(d), _column_block(2 * d // N_CHIPS), N_CHIPS, d, 2 * d // N_CHIPS, s, "wgrad_conv1")
    cs_conv = add_halves([dw1], _SiblingSwap([dw1]).alone([dw1], "grad_swap_w1"), "w1")

    dmod = jnp.stack([
        _rows_of(sums_ca[2], sums_ca[1], sums_cb[0], sums_f0[2], sums_f0[1], sums_f0[3]),
        _rows_of(sums_p[2], sums_p[1], sums_p[3], sums_f1[2], sums_f1[1], sums_f1[3])])
    small = _rows_of(dmod.reshape(-1, d), sums_ca[0], sums_p[0], sums_f0[0], sums_f1[0], db1.reshape(2, d),
                     sums_cb[4], sums_cb[2], sums_cb[3], sums_cb[1], fin_sums[0], sums_p[4], dwdw, fin_sums[1])
    n_small = small.shape[0]
    pad = (-n_small) % SUBLANES
    small = jnp.concatenate([small, jnp.zeros((pad, d), F32)], axis=0) if pad else small
    small_all = _all_gather8(small.reshape(-1, LANES), "gather_small_grads").reshape(N_DEV, -1, LANES)
    small_sum = _sum_devices(small_all).reshape(-1, d)
    n_dm = depth * n_ada
    g_ada_b = small_sum[0:n_dm].reshape(depth, n_ada * d)
    g_norm_mix = small_sum[n_dm:n_dm + 2]
    g_norm_ffn = small_sum[n_dm + 2:n_dm + 4]
    g_b1 = small_sum[n_dm + 4:n_dm + 6].reshape(1, 2 * d)
    g_bdw, g_lng, g_lnb, g_b2 = (small_sum[n_dm + 6 + k].reshape(1, d) for k in range(4))
    g_final = small_sum[n_dm + 10]
    g_pls = lax.dynamic_slice_in_dim(small_sum[n_dm + 11].reshape(1, d), myq * cq, cq, axis=1)
    g_wdw = lax.dynamic_slice_in_dim(small_sum[n_dm + 12:n_dm + 12 + CONV_WIDTH], myq * cq, cq, axis=1).reshape(conv_wdw.shape)
    loss = small_sum[n_dm + 12 + CONV_WIDTH, 0]
    dmod_all = small_all.reshape(N_DEV, -1, d)[:, 0:n_dm].reshape(N_DEV, depth, n_ada * d)
    dmod_cols = lax.dynamic_slice_in_dim(jnp.transpose(dmod_all, (1, 0, 2)), myq * ada_cols, ada_cols, axis=2)
    g_ada_w = _ada_backward(c_all.T, dmod_cols)

    def small_pack(*vs):
        return jnp.concatenate([v.reshape(-1) for v in vs]).reshape(-1, LANES)

    names = ("ada_b", "norm_mix_g", "norm_ffn_g", "conv_b1", "conv_bdw", "conv_ln_g", "conv_ln_b", "conv_b2", "final_g")
    small_w = (ada_b, norm_mix_g, norm_ffn_g, conv_b1, conv_bdw, conv_ln_g, conv_ln_b, conv_b2, final_g)
    small_g = (g_ada_b, g_norm_mix, g_norm_ffn, g_b1, g_bdw, g_lng, g_lnb, g_b2, g_final)
    small_m = (m_ada_b, m_norm_mix_g, m_norm_ffn_g, m_conv_b1, m_conv_bdw, m_conv_ln_g, m_conv_ln_b, m_conv_b2, m_final_g)
    small_v = (v_ada_b, v_norm_mix_g, v_norm_ffn_g, v_conv_b1, v_conv_bdw, v_conv_ln_g, v_conv_ln_b, v_conv_b2, v_final_g)
    sd, sm, sv = _adamw(small_pack(*small_w), small_pack(*small_g), small_pack(*small_m), small_pack(*small_v), "adamw_small")

    def unpack(flat2d):
        flat = flat2d.reshape(-1)
        out, off = {}, 0
        for nm, wv in zip(names, small_w):
            out[nm] = flat[off:off + wv.size].reshape(wv.shape)
            off += wv.size
        return out

    sd, sm, sv = unpack(sd), unpack(sm), unpack(sv)
    grads = {
        "ada_w": g_ada_w, "ada_b": g_ada_b, "norm_mix_g": g_norm_mix, "norm_ffn_g": g_norm_ffn,
        "conv_b1": g_b1, "conv_wdw": g_wdw, "conv_bdw": g_bdw, "conv_ln_g": g_lng,
        "conv_ln_b": g_lnb, "conv_b2": g_b2, "pool_ls": g_pls, "final_g": g_final,
    }
    plain = {
        "ada_w": (ada_w, m_ada_w, v_ada_w), "conv_wdw": (conv_wdw, m_conv_wdw, v_conv_wdw),
        "pool_ls": (pool_ls, m_pool_ls, v_pool_ls),
    }
    from_slots = {
        "ffn_w_gate": (ffn_w_gate, m_ffn_w_gate, v_ffn_w_gate,
                       lambda: [(slots_l0[0], cs_l0[0], 0, fq), (slots_l1[0], cs_l1[0], 0, fq)]),
        "ffn_w_up": (ffn_w_up, m_ffn_w_up, v_ffn_w_up,
                     lambda: [(slots_l0[0], cs_l0[0], fq, fq), (slots_l1[0], cs_l1[0], fq, fq)]),
        "ffn_w_down": (ffn_w_down, m_ffn_w_down, v_ffn_w_down,
                       lambda: [(slots_l0[1], cs_l0[1], 0, fq), (slots_l1[1], cs_l1[1], 0, fq)]),
        "pool_w": (pool_w, m_pool_w, v_pool_w, lambda: [(slots_l0[2], cs_l0[2], 0, ng * gq)]),
        "conv_w2": (conv_w2, m_conv_w2, v_conv_w2, lambda: [(slots_l0[3], cs_l0[3], 0, d // N_CHIPS)]),
        "conv_w1": (conv_w1, m_conv_w1, v_conv_w1, lambda: [(slots_conv[0], cs_conv[0], 0, d)]),
    }
    order = ("ada_w", "ada_b", "norm_mix_g", "norm_ffn_g", "conv_w1", "conv_b1", "conv_wdw", "conv_bdw", "conv_ln_g",
             "conv_ln_b", "conv_w2", "conv_b2", "pool_w", "pool_ls", "ffn_w_gate", "ffn_w_up", "ffn_w_down", "final_g")
    delta, new_m, new_v = {}, {}, {}
    transposed = ("ffn_w_gate", "ffn_w_up")
    for nm, (wv, mv, vv, segs) in from_slots.items():
        if nm in transposed:
            wv, mv, vv = (jnp.swapaxes(t, 1, 2) for t in (wv, mv, vv))
        two_d = (-1, wv.shape[-1])
        carry = (_Exchange("reduce", cs_conv), cs_conv) if nm == "ffn_w_gate" else None
        outs, brought = _adamw_from_slots(wv.reshape(two_d), mv.reshape(two_d), vv.reshape(two_d), segs(), pos,
                                          "adamw_" + nm, carry=carry)
        if carry is not None:
            slots_conv = brought
        outs = [o.reshape(wv.shape) for o in outs]
        if nm in transposed:
            outs = [jnp.swapaxes(o, 1, 2) for o in outs]
        grads[nm], delta[nm], new_m[nm], new_v[nm] = outs
    for nm in order:
        if nm in from_slots:
            continue
        elif nm in plain:
            wv, mv, vv = plain[nm]
            grads[nm] = grads[nm].reshape(wv.shape)
            delta[nm], new_m[nm], new_v[nm] = _adamw_nd(wv, grads[nm], mv, vv, "adamw_" + nm)
        else:
            delta[nm], new_m[nm], new_v[nm] = sd[nm], sm[nm], sv[nm]
            grads[nm] = grads[nm].reshape(dict(zip(names, small_w))[nm].shape)

    return (loss, dx0.reshape(x.shape), *[grads[n] for n in order], *[delta[n] for n in order],
            *[new_m[n] for n in order], *[new_v[n] for n in order])
```

```python
import functools

import jax
import jax.numpy as jnp
from jax import lax
from jax.experimental import pallas as pl
from jax.experimental.pallas import tpu as pltpu

F32 = jnp.float32
_LOWP = jnp.bfloat16
EPS = 1e-6
CONV_WIDTH = 31
POOL_WINDOWS = (2, 4, 8, 16)
ADAM_LR = 0.001
ADAM_B1 = 0.9
ADAM_B2 = 0.999
ADAM_EPS = 1e-08
ADAM_WD = 0.01
ADAM_STEP = 10

N_CHIPS = 4
N_DEV = 8
LANES = 128
SUBLANES = 8
CONV_HALO = 32
POOL_HALO = 16
TILE_ROWS = 128
VMEM_LIMIT = 56 * 1024 * 1024
ADAM_BLOCK_BYTES = 1 << 20
WGRAD_ROWS = 4096
FFN_FWD_QUARTERS_PER_STEP = 4
FFN_BWD_QUARTERS_PER_STEP = 2
MESH = pl.DeviceIdType.MESH
SDS = jax.ShapeDtypeStruct


def _cparams(*sem):
    return pltpu.CompilerParams(dimension_semantics=sem if sem else None, vmem_limit_bytes=VMEM_LIMIT)


def _row_block(s):
    for tm in (512, 256, 128, 64, 32, 16):
        if s % tm == 0:
            return tm
    raise ValueError(f"sequence length {s} must be a multiple of 16")


def _rows_spec(tm, ncols):
    return pl.BlockSpec((tm, ncols), lambda i, *_: (i, 0))


def _const_spec(shape):
    nd = len(shape)
    return pl.BlockSpec(shape, lambda *_: (0,) * nd)


def _dot(a, b):
    return lax.dot_general(a.astype(_LOWP), b.astype(_LOWP), (((1,), (0,)), ((), ())), preferred_element_type=F32)


def _dot_nt(a, b):
    return lax.dot_general(a.astype(_LOWP), b.astype(_LOWP), (((1,), (1,)), ((), ())), preferred_element_type=F32)


def _dot_tn(a, b):
    return lax.dot_general(a.astype(_LOWP), b.astype(_LOWP), (((0,), (0,)), ((), ())), preferred_element_type=F32)


def _rms(x, gamma):
    r = lax.rsqrt(jnp.mean(x * x, axis=-1, keepdims=True) + EPS)
    xh = x * r
    return xh, r, xh * gamma


def _rms_bwd(dn, xh, r, gamma):
    dxh = dn * gamma
    return r * (dxh - xh * jnp.mean(dxh * xh, axis=-1, keepdims=True))


def _colsum(v):
    return jnp.sum(v, axis=0, keepdims=True)


def _sigmoid(v):
    return jax.nn.sigmoid(v)


def _dsilu(v):
    sg = _sigmoid(v)
    return sg * (1.0 + v * (1.0 - sg))


def _for_tiles(nrows, col0, ncols, fn):
    def step(t, carry):
        r0 = pl.multiple_of(t * TILE_ROWS, TILE_ROWS)
        for col in range(col0, col0 + ncols, LANES):
            fn(r0, col)
        return carry

    lax.fori_loop(0, nrows // TILE_ROWS, step, 0)


def _shift_up(win, b):
    return pltpu.roll(win, win.shape[0] - b, 0) if b else win


def _shift_down(win, b):
    return pltpu.roll(win, b, 0) if b else win


def _tiles3(v):
    return v.reshape(v.shape[0] // SUBLANES, SUBLANES, v.shape[1])


def _resident_scratch(arrays):
    return [pltpu.VMEM(a.shape, a.dtype) for a in arrays] + [pltpu.SemaphoreType.DMA((len(arrays),))]


def _load_resident(hbm_refs, vmem_refs, sems):
    copies = [pltpu.make_async_copy(src, dst, sems.at[k]) for k, (src, dst) in enumerate(zip(hbm_refs, vmem_refs))]
    for cp in copies:
        cp.start()
    for cp in copies:
        cp.wait()


def _mesh_pos():
    return lax.axis_index("x"), lax.axis_index("y"), lax.axis_index("c")


def _other_chips(x, y):
    return [(1 - x, y), (x, 1 - y), (1 - x, 1 - y)]


def _all_gather8(v, name):
    m, n = v.shape

    def body(x_ref, out_ref, send_sems, recv_sems, local_sem):
        x, y, c = _mesh_pos()
        me, sibling = (x, y, c), (x, y, 1 - c)
        chips = _other_chips(x, y)

        def rows(px, py, pc):
            return out_ref.at[pl.ds((4 * px + 2 * py + pc) * m, m), :]

        def copy(k, block, to, src=None):
            return pltpu.make_async_remote_copy(
                src_ref=rows(*block) if src is None else src, dst_ref=rows(*block),
                send_sem=send_sems.at[k], recv_sem=recv_sems.at[k], device_id=to, device_id_type=MESH)

        mine = pltpu.make_async_copy(x_ref, rows(*me), local_sem)
        mine.start()
        first = [copy(0, me, sibling, src=x_ref)]
        first += [copy(1 + j, me, (*chip, c), src=x_ref) for j, chip in enumerate(chips)]
        for cp in first:
            cp.start()
        passed = [copy(4 + j, (*chip, c), sibling) for j, chip in enumerate(chips)]
        for j, chip in enumerate(chips):
            copy(1 + j, (*chip, c), me).wait_recv()
            passed[j].start()
        copy(0, sibling, me).wait_recv()
        for j, chip in enumerate(chips):
            copy(4 + j, (*chip, 1 - c), me).wait_recv()
        for cp in first + passed:
            cp.wait_send()
        mine.wait()

    return pl.pallas_call(
        body, name=name,
        out_shape=SDS((N_DEV * m, n), v.dtype),
        in_specs=[pl.BlockSpec(memory_space=pltpu.VMEM)],
        out_specs=pl.BlockSpec(memory_space=pltpu.VMEM),
        scratch_shapes=[pltpu.SemaphoreType.DMA((7,)), pltpu.SemaphoreType.DMA((7,)), pltpu.SemaphoreType.DMA],
    )(v)


class _Exchange:
    N_SEMS = 7

    def __init__(self, kind, arrays):
        self.kind, self.n = kind, len(arrays)
        self.streams = 3
        if kind == "gather":
            self.out_shape = [SDS((N_CHIPS,) + a.shape, a.dtype) for a in arrays]
        else:
            self.out_shape = [SDS((N_CHIPS, 2 * a.shape[1], a.shape[2]), a.dtype) for a in arrays]
        self.scratch = [pltpu.SemaphoreType.DMA((self.n, self.N_SEMS)), pltpu.SemaphoreType.DMA((self.n, self.N_SEMS))]

    def _ctx(self, ins, outs, send_sems, recv_sems):
        x, y, c = _mesh_pos()
        myq = 2 * x + y
        sibling = (x, y, 1 - c)
        chips = _other_chips(x, y)
        gather = self.kind == "gather"
        ns = self.streams

        def rows(a, h):
            hr = outs[a].shape[1] // 2
            return pl.ds(h * hr, hr)

        def copy(a, ks, kr, src, dst, to):
            return pltpu.make_async_remote_copy(src_ref=src, dst_ref=dst, send_sem=send_sems.at[a, ks],
                                                recv_sem=recv_sems.at[a, kr], device_id=to, device_id_type=MESH)

        def place(a, t, h):
            if gather:
                px, py = chips[t]
                return outs[a].at[2 * px + py, rows(a, h)]
            return outs[a].at[t, rows(a, h)]

        def first(a, t):
            if gather:
                return copy(a, t, t, ins[a].at[rows(a, c)], outs[a].at[myq, rows(a, c)], (*chips[t], c))
            px, py = chips[t]
            return copy(a, t, t, ins[a].at[2 * px + py], outs[a].at[t, rows(a, c)], (px, py, c))

        def landed(a, t):
            return copy(a, t, t, place(a, t, c), place(a, t, c), sibling)

        def relay(a, t):
            return copy(a, ns + t, ns + t, place(a, t, c), place(a, t, c), sibling)

        def relayed(a, t):
            return copy(a, ns + t, ns + t, place(a, t, 1 - c), place(a, t, 1 - c), sibling)

        def own(a):
            return copy(a, 2 * ns, 2 * ns, ins[a].at[myq], outs[a].at[ns, rows(a, c)], sibling)

        return first, landed, relay, relayed, own

    def start(self, ins, outs, send_sems, recv_sems):
        first, _, _, _, own = self._ctx(ins, outs, send_sems, recv_sems)
        for a in range(self.n):
            for t in range(self.streams):
                first(a, t).start()
            if self.kind == "reduce":
                own(a).start()

    def relay(self, ins, outs, send_sems, recv_sems):
        _, landed, relay, _, _ = self._ctx(ins, outs, send_sems, recv_sems)
        for a in range(self.n):
            for t in range(self.streams):
                landed(a, t).wait_recv()
                relay(a, t).start()

    def finish(self, ins, outs, send_sems, recv_sems):
        first, _, relay, relayed, own = self._ctx(ins, outs, send_sems, recv_sems)
        for a in range(self.n):
            for t in range(self.streams):
                relayed(a, t).wait_recv()
            if self.kind == "reduce":
                own(a).wait_recv()
        for a in range(self.n):
            for t in range(self.streams):
                first(a, t).wait_send()
                relay(a, t).wait_send()
            if self.kind == "reduce":
                own(a).wait_send()

    def alone(self, arrays, name):
        n = self.n

        def body(*refs):
            args = (refs[:n], refs[n:2 * n], refs[2 * n], refs[2 * n + 1])
            self.start(*args)
            self.relay(*args)
            self.finish(*args)

        any_spec = pl.BlockSpec(memory_space=pl.ANY)
        return pl.pallas_call(body, name=name, out_shape=self.out_shape, in_specs=[any_spec] * n,
                              out_specs=[any_spec] * n, scratch_shapes=self.scratch)(*arrays)


RELAY_AT = (5, 8)


def _call(body, *args, carry=None, prefetch=None, name, grid, in_specs, out_specs, out_shape, scratch_shapes=(), sem):
    def run(fn, in_specs_, out_specs_, out_shape_, scratch_, sem_, operands):
        if prefetch is None:
            return pl.pallas_call(fn, name=name, grid=grid, in_specs=in_specs_, out_specs=out_specs_, out_shape=out_shape_,
                                  scratch_shapes=scratch_, compiler_params=_cparams(*sem_))(*operands)
        grid_spec = pltpu.PrefetchScalarGridSpec(num_scalar_prefetch=1, grid=grid, in_specs=in_specs_,
                                                 out_specs=out_specs_, scratch_shapes=scratch_)
        return pl.pallas_call(fn, name=name, grid_spec=grid_spec, out_shape=out_shape_,
                              compiler_params=_cparams(*sem_))(prefetch, *operands)

    if carry is None:
        return run(body, list(in_specs), list(out_specs), list(out_shape), list(scratch_shapes), sem, args), None
    ex, arrays = carry
    n_in, n_out, n_scr, n = len(in_specs), len(out_specs), len(scratch_shapes), ex.n
    n_pre = 0 if prefetch is None else 1
    total = 1
    for g in grid:
        total *= g
    relay_step = (total * RELAY_AT[0]) // RELAY_AT[1]

    def hosted(*refs):
        pre, refs = refs[:n_pre], refs[n_pre:]
        ins, cins = refs[:n_in], refs[n_in:n_in + n]
        outs, couts = refs[n_in + n:n_in + n + n_out], refs[n_in + n + n_out:n_in + 2 * n + n_out]
        scr = refs[n_in + 2 * n + n_out:n_in + 2 * n + n_out + n_scr]
        sems = refs[n_in + 2 * n + n_out + n_scr:]
        step = 0
        for ax, g in enumerate(grid):
            step = step * g + pl.program_id(ax)

        @pl.when(step == 0)
        def _():
            ex.start(cins, couts, *sems)

        body(*pre, *ins, *outs, *scr)

        @pl.when(step == relay_step)
        def _():
            ex.relay(cins, couts, *sems)

        @pl.when(step == total - 1)
        def _():
            ex.finish(cins, couts, *sems)

    any_spec = pl.BlockSpec(memory_space=pl.ANY)
    res = run(hosted, list(in_specs) + [any_spec] * n, list(out_specs) + [any_spec] * n, list(out_shape) + ex.out_shape,
              list(scratch_shapes) + ex.scratch, ("arbitrary",) * len(grid), (*args, *arrays))
    return res[:n_out], res[n_out:]


class _SiblingSwap:
    def __init__(self, arrays):
        self.n = len(arrays)
        self.out_shape = [SDS((a.shape[0], a.shape[1] // 2, a.shape[2]), a.dtype) for a in arrays]
        self.scratch = [pltpu.SemaphoreType.DMA((self.n,)), pltpu.SemaphoreType.DMA((self.n,))]

    def _copies(self, ins, outs, send_sems, recv_sems):
        x, y, c = _mesh_pos()
        cps = []
        for a in range(self.n):
            hr = ins[a].shape[1] // 2
            cps.append(pltpu.make_async_remote_copy(
                src_ref=ins[a].at[:, pl.ds((1 - c) * hr, hr)], dst_ref=outs[a],
                send_sem=send_sems.at[a], recv_sem=recv_sems.at[a], device_id=(x, y, 1 - c), device_id_type=MESH))
        return cps

    def start(self, *refs):
        for cp in self._copies(*refs):
            cp.start()

    def relay(self, *refs):
        pass

    def finish(self, *refs):
        for cp in self._copies(*refs):
            cp.wait()

    def alone(self, arrays, name):
        return _Exchange.alone(self, arrays, name)


def _divisor_block(rows, cands):
    for rb in cands:
        if rows % rb == 0:
            return rb
    raise ValueError(rows)


def _add_sibling_half(p, rb1, pos, name):
    nq, r, n = p.shape
    h = r // 2
    rb = _divisor_block(h, (512, 352, 256, 176, 128, 64, 32, 16))
    nb = h // rb

    def body(pos_ref, p_ref, r_ref, o_ref):
        o_ref[...] = (p_ref[...].astype(F32) + r_ref[...].astype(F32)).astype(o_ref.dtype)

    grid_spec = pltpu.PrefetchScalarGridSpec(
        num_scalar_prefetch=1, grid=(nq, nb),
        in_specs=[pl.BlockSpec((1, rb, n), lambda q, i, pos: (q, pos[0] * nb + i, 0)),
                  pl.BlockSpec((1, rb, n), lambda q, i, pos: (q, i, 0))],
        out_specs=pl.BlockSpec((1, rb, n), lambda q, i, pos: (q, i, 0)))
    return pl.pallas_call(
        body, name=name, grid_spec=grid_spec, out_shape=SDS((nq, h, n), p.dtype),
        compiler_params=_cparams("parallel", "parallel"),
    )(pos, p, rb1)


def _sum_devices(g):
    nd, m, n = g.shape

    def body(g_ref, o_ref):
        acc = g_ref[0]
        for d in range(1, nd):
            acc = acc + g_ref[d]
        o_ref[...] = acc

    return pl.pallas_call(body, name="small_grad_sum", out_shape=SDS((m, n), F32))(g)


def _ada_forward(c_all, ada_w, ada_b_cols, carry=None):
    nl, d, ncol = ada_w.shape
    nb = c_all.shape[0]
    tn = 512 if ncol % 512 == 0 else ncol

    def body(c_ref, w_ref, b_ref, o_ref):
        cv = c_ref[...]
        ca = cv * _sigmoid(cv)
        o_ref[0] = _dot(ca, w_ref[0]) + b_ref[0]

    return _call(
        body, c_all, ada_w, ada_b_cols, carry=carry, name="ada_forward", grid=(nl, ncol // tn),
        in_specs=[pl.BlockSpec((nb, d), lambda l, j: (0, 0)),
                  pl.BlockSpec((1, d, tn), lambda l, j: (l, 0, j)),
                  pl.BlockSpec((1, 1, tn), lambda l, j: (l, 0, j))],
        out_specs=[pl.BlockSpec((1, nb, tn), lambda l, j: (l, 0, j))],
        out_shape=[SDS((nl, nb, ncol), F32)], sem=("parallel", "parallel"))


def _ada_backward(c_all_t, dmod_cols):
    d, nb = c_all_t.shape
    nl, _, ncol = dmod_cols.shape
    tn = 512 if ncol % 512 == 0 else ncol

    def body(c_ref, g_ref, o_ref):
        cv = c_ref[...]
        ca = cv * _sigmoid(cv)
        o_ref[0] = _dot(ca, g_ref[0])

    return pl.pallas_call(
        body, name="ada_backward", grid=(nl, ncol // tn),
        in_specs=[pl.BlockSpec((d, nb), lambda l, j: (0, 0)),
                  pl.BlockSpec((1, nb, tn), lambda l, j: (l, 0, j))],
        out_specs=pl.BlockSpec((1, d, tn), lambda l, j: (l, 0, j)),
        out_shape=SDS((nl, d, ncol), F32), compiler_params=_cparams("parallel", "parallel"),
    )(c_all_t, dmod_cols)


def _conv_a_fwd(x, vec, w1, b1):
    s, d = x.shape
    nq, _, hq = w1.shape
    tm = _row_block(s)

    def body(x_ref, vec_ref, w1_ref, b1_ref, h_ref, a_ref, g_ref, glu_ref):
        vec = vec_ref[...]
        _, _, n = _rms(x_ref[...], vec[0:1])
        hb = (n * (1.0 + vec[1:2]) + vec[2:3]).astype(_LOWP)
        h_ref[...] = hb
        u = [_dot(hb, w1_ref[q]) + b1_ref[:, q * hq:(q + 1) * hq] for q in range(nq)]
        for k in range(nq // 2):
            av, gv = u[k], u[nq // 2 + k]
            cols = slice(k * hq, (k + 1) * hq)
            a_ref[:, cols] = av.astype(_LOWP)
            g_ref[:, cols] = gv.astype(_LOWP)
            glu_ref[:, cols] = av * _sigmoid(gv)

    return pl.pallas_call(
        body, name="conv_a_fwd", grid=(s // tm,),
        in_specs=[_rows_spec(tm, d), _const_spec(vec.shape), _const_spec(w1.shape), _const_spec(b1.shape)],
        out_specs=[_rows_spec(tm, d)] * 4,
        out_shape=[SDS((s, d), _LOWP)] * 3 + [SDS((s, d), F32)],
        compiler_params=_cparams("parallel"),
    )(x, vec, w1, b1)


def _conv_b_fwd(glu, w8, vec, w2, x, carry=None):
    s, d = x.shape
    tm = _row_block(s)
    hb = tm // CONV_HALO

    def body(glu_ref, halo_ref, w8_ref, vec_ref, w2_ref, x_ref, cw_ref, s_ref, y_ref, xo_ref, ext_ref):
        i = pl.program_id(0)
        vec = vec_ref[...]
        ext_ref[0:CONV_HALO, :] = jnp.where(i > 0, halo_ref[...], 0.0)
        ext_ref[CONV_HALO:, :] = glu_ref[...]

        def tile(r0, c0):
            lanes = slice(c0, c0 + LANES)
            win = ext_ref[pl.ds(r0, TILE_ROWS + CONV_HALO), lanes]
            acc = jnp.zeros((TILE_ROWS // SUBLANES, SUBLANES, LANES), F32)
            for b in range(SUBLANES):
                sh = _shift_up(win, b)
                for a in range(CONV_HALO // SUBLANES + 1):
                    k = SUBLANES * a + b - (CONV_HALO - CONV_WIDTH + 1)
                    if 0 <= k < CONV_WIDTH:
                        acc = acc + w8_ref[k, :, lanes][None] * _tiles3(sh[SUBLANES * a:SUBLANES * a + TILE_ROWS])
            cw_ref[pl.ds(r0, TILE_ROWS), lanes] = acc.reshape(TILE_ROWS, LANES)

        _for_tiles(tm, 0, d, tile)
        cw = cw_ref[...] + vec[0:1]
        cw_ref[...] = cw
        cc = cw - jnp.mean(cw, axis=-1, keepdims=True)
        ch = cc * lax.rsqrt(jnp.mean(cc * cc, axis=-1, keepdims=True) + EPS)
        lo = ch * vec[1:2] + vec[2:3]
        sv = (lo * _sigmoid(lo)).astype(_LOWP)
        s_ref[...] = sv
        y = _dot(sv, w2_ref[...]) + vec[3:4]
        y_ref[...] = y.astype(_LOWP)
        xo_ref[...] = x_ref[...] + (1.0 + vec[4:5]) * y

    return _call(
        body, glu, glu, w8, vec, w2, x, carry=carry, name="conv_b_fwd", grid=(s // tm,),
        in_specs=[_rows_spec(tm, d),
                  pl.BlockSpec((CONV_HALO, d), lambda i: (jnp.maximum(i * hb - 1, 0), 0)),
                  _const_spec(w8.shape), _const_spec(vec.shape), _const_spec(w2.shape), _rows_spec(tm, d)],
        out_specs=[_rows_spec(tm, d)] * 4,
        out_shape=[SDS((s, d), F32), SDS((s, d), _LOWP), SDS((s, d), _LOWP), SDS((s, d), F32)],
        scratch_shapes=[pltpu.VMEM((tm + CONV_HALO, d), F32)], sem=("parallel",))


def _ffn_fwd(x, vec, wg, wu, wd, name, carry=None, head=None):
    s, d = x.shape
    nq, fq, _ = wg.shape
    tm = _row_block(s)
    nsteps = s // tm
    n_head = 0 if head is None else 2
    qps = FFN_FWD_QUARTERS_PER_STEP

    def body(x_ref, vec_ref, wg_hbm, wu_hbm, wd_hbm, *rest):
        head_refs, rest = rest[:n_head], rest[n_head:]
        h_ref, gt_ref, up_ref, act_ref, y_ref, xo_ref = rest[:6]
        sums_ref = rest[6] if head is not None else None
        hb_ref, yacc_ref, wg_ref, wu_ref, wd_ref, w_sems = rest[6 + n_head // 2:]
        i, p = pl.program_id(0), pl.program_id(1)
        vec = vec_ref[...]

        @pl.when((i == 0) & (p == 0))
        def _():
            _load_resident((wg_hbm, wu_hbm, wd_hbm), (wg_ref, wu_ref, wd_ref), w_sems)
            if head is not None:
                sums_ref[...] = jnp.zeros_like(sums_ref)

        @pl.when(p == 0)
        def _():
            _, _, n = _rms(x_ref[...], vec[0:1])
            hb = (n * (1.0 + vec[1:2]) + vec[2:3]).astype(_LOWP)
            hb_ref[...] = hb
            h_ref[...] = hb
            yacc_ref[...] = jnp.zeros_like(yacc_ref)

        hb = hb_ref[...]
        for k in range(qps):
            q = p * qps + k
            gt = _dot_nt(hb, wg_ref[q])
            up = _dot_nt(hb, wu_ref[q])
            act = (gt * _sigmoid(gt) * up).astype(_LOWP)
            gt_ref[k] = gt.astype(_LOWP)
            up_ref[k] = up.astype(_LOWP)
            act_ref[k] = act
            yacc_ref[...] += _dot(act, wd_ref[q])

        @pl.when(p == nq // qps - 1)
        def _():
            y = yacc_ref[...]
            y_ref[...] = y.astype(_LOWP)
            xo = x_ref[...] + (1.0 + vec[3:4]) * y
            if head is None:
                xo_ref[...] = xo
            else:
                t_ref, g_ref = head_refs
                gamma_v = g_ref[...]
                xh, r, out = _rms(xo, gamma_v)
                err = out - t_ref[...]
                dout = err * (1.0 / d)
                sums_ref[0:1, :] += _colsum(dout * xh)
                sums_ref[1:2, :] += _colsum(0.5 * err * dout)
                xo_ref[...] = _rms_bwd(dout, xh, r, gamma_v)

        if head is not None:
            @pl.when((i == nsteps - 1) & (p == nq // qps - 1))
            def _():
                sums_ref[1:2, :] = jnp.broadcast_to(jnp.sum(sums_ref[1:2, :], axis=-1, keepdims=True), (1, d))

    rows = pl.BlockSpec((tm, d), lambda i, q: (i, 0))
    hid = pl.BlockSpec((qps, tm, fq), lambda i, q: (q, i, 0))
    any_spec = pl.BlockSpec(memory_space=pl.ANY)
    head_args = [] if head is None else [head[0], head[1]]
    head_specs = [] if head is None else [rows, pl.BlockSpec((1, d), lambda i, q: (0, 0))]
    sums_specs = [] if head is None else [pl.BlockSpec((2, d), lambda i, q: (0, 0))]
    sums_shape = [] if head is None else [SDS((2, d), F32)]
    return _call(
        body, x, vec, wg, wu, wd, *head_args, carry=carry, name=name, grid=(nsteps, nq // qps),
        in_specs=[rows, pl.BlockSpec(vec.shape, lambda i, q: (0, 0)), any_spec, any_spec, any_spec] + head_specs,
        out_specs=[rows, hid, hid, hid, rows, rows] + sums_specs,
        out_shape=[SDS((s, d), _LOWP)] + [SDS((nq, s, fq), _LOWP)] * 3 + [SDS((s, d), _LOWP), SDS((s, d), F32)] + sums_shape,
        scratch_shapes=[pltpu.VMEM((tm, d), _LOWP), pltpu.VMEM((tm, d), F32)] + _resident_scratch((wg, wu, wd)),
        sem=("arbitrary", "arbitrary"))


def _pool_inv_count(t0, nrows, w):
    t = t0 + lax.broadcasted_iota(jnp.int32, (nrows, LANES), 0)
    return 1.0 / jnp.minimum(t + 1, w).astype(F32)


def _pool_fwd(x, vec, pw):
    s, d = x.shape
    ng, gd, _ = pw.shape
    tm = _row_block(s)
    hb = tm // POOL_HALO

    def body(x_ref, halo_ref, vec_ref, pw_ref, mixed_ref, yp_ref, xo_ref, hext_ref):
        i = pl.program_id(0)
        vec = vec_ref[...]

        def modulated(xv):
            _, _, n = _rms(xv, vec[0:1])
            return n * (1.0 + vec[1:2]) + vec[2:3]

        hext_ref[0:POOL_HALO, :] = jnp.where(i > 0, modulated(halo_ref[...]), 0.0)
        hext_ref[POOL_HALO:, :] = modulated(x_ref[...])

        for g in range(ng):
            def tile(r0, c0, g=g):
                lanes = slice(c0, c0 + LANES)
                win = hext_ref[pl.ds(r0, TILE_ROWS + POOL_HALO), lanes]
                acc = win
                for step in range(g + 1):
                    acc = acc + _shift_down(acc, 2 ** step)
                inv = _pool_inv_count(i * tm + r0, TILE_ROWS, POOL_WINDOWS[g])
                mixed = acc[POOL_HALO:] * inv - win[POOL_HALO:]
                mixed_ref[pl.ds(r0, TILE_ROWS), lanes] = mixed.astype(_LOWP)

            _for_tiles(tm, g * gd, gd, tile)

        for g in range(ng):
            cols = slice(g * gd, (g + 1) * gd)
            yp = _dot(mixed_ref[:, cols], pw_ref[g])
            yp_ref[:, cols] = yp.astype(_LOWP)
            xo_ref[:, cols] = x_ref[:, cols] + (1.0 + vec[3:4, cols]) * (yp * vec[4:5, cols])

    return pl.pallas_call(
        body, name="pool_fwd", grid=(s // tm,),
        in_specs=[_rows_spec(tm, d),
                  pl.BlockSpec((POOL_HALO, d), lambda i: (jnp.maximum(i * hb - 1, 0), 0)),
                  _const_spec(vec.shape), _const_spec(pw.shape)],
        out_specs=[_rows_spec(tm, d)] * 3,
        out_shape=[SDS((s, d), _LOWP), SDS((s, d), _LOWP), SDS((s, d), F32)],
        scratch_shapes=[pltpu.VMEM((tm + POOL_HALO, d), F32)],
        compiler_params=_cparams("parallel"),
    )(x, x, vec, pw)


def _ffn_bwd(dxo, x, vec, gt, up, y, wg, wu, wd, name, carry=None):
    s, d = x.shape
    nq, fq, _ = wg.shape
    tm = _row_block(s)
    qps = FFN_BWD_QUARTERS_PER_STEP

    def body(dxo_ref, x_ref, vec_ref, gt_ref, up_ref, y_ref, wg_hbm, wu_hbm, wd_hbm,
             dgt_ref, dup_ref, dy_ref, dx_ref, sums_ref, dyb_ref, dh_ref, wg_ref, wu_ref, wd_ref, w_sems):
        i, p = pl.program_id(0), pl.program_id(1)
        vec = vec_ref[...]

        @pl.when((i == 0) & (p == 0))
        def _():
            _load_resident((wg_hbm, wu_hbm, wd_hbm), (wg_ref, wu_ref, wd_ref), w_sems)
            sums_ref[...] = jnp.zeros_like(sums_ref)

        @pl.when(p == 0)
        def _():
            dxv = dxo_ref[...]
            dyb = (dxv * (1.0 + vec[3:4])).astype(_LOWP)
            dyb_ref[...] = dyb
            dy_ref[...] = dyb
            dh_ref[...] = jnp.zeros_like(dh_ref)
            sums_ref[3:4, :] += _colsum(dxv * y_ref[...].astype(F32))

        halves = [slice(k * (tm // 2), (k + 1) * (tm // 2)) for k in range(2)]
        for k in range(qps):
            q = p * qps + k
            dacts = [_dot_nt(dyb_ref[rs, :], wd_ref[q]) for rs in halves]
            dgus = []
            for rs, dact in zip(halves, dacts):
                gtv = gt_ref[k, rs, :].astype(F32)
                sg = _sigmoid(gtv)
                dgt = (dact * up_ref[k, rs, :].astype(F32) * (sg * (1.0 + gtv * (1.0 - sg)))).astype(_LOWP)
                dup = (dact * (gtv * sg)).astype(_LOWP)
                dgt_ref[k, rs, :] = dgt
                dup_ref[k, rs, :] = dup
                dgus.append((dgt, dup))
            for rs, (dgt, dup) in zip(halves, dgus):
                dh_ref[rs, :] += _dot(dgt, wg_ref[q]) + _dot(dup, wu_ref[q])

        @pl.when(p == nq // qps - 1)
        def _():
            dh = dh_ref[...]
            xh, r, n = _rms(x_ref[...], vec[0:1])
            sums_ref[2:3, :] += _colsum(dh)
            sums_ref[1:2, :] += _colsum(dh * n)
            dn = dh * (1.0 + vec[1:2])
            sums_ref[0:1, :] += _colsum(dn * xh)
            dx_ref[...] = dxo_ref[...] + _rms_bwd(dn, xh, r, vec[0:1])

    rows = pl.BlockSpec((tm, d), lambda i, q: (i, 0))
    hid = pl.BlockSpec((qps, tm, fq), lambda i, q: (q, i, 0))
    return _call(
        body, dxo, x, vec, gt, up, y, wg, wu, wd, carry=carry, name=name, grid=(s // tm, nq // qps),
        in_specs=[rows, rows, pl.BlockSpec(vec.shape, lambda i, q: (0, 0)), hid, hid, rows] +
                 [pl.BlockSpec(memory_space=pl.ANY)] * 3,
        out_specs=[hid, hid, rows, rows, pl.BlockSpec((4, d), lambda i, q: (0, 0))],
        out_shape=[SDS((nq, s, fq), _LOWP)] * 2 + [SDS((s, d), _LOWP), SDS((s, d), F32), SDS((4, d), F32)],
        scratch_shapes=[pltpu.VMEM((tm, d), _LOWP), pltpu.VMEM((tm, d), F32)] + _resident_scratch((wg, wu, wd)),
        sem=("arbitrary", "arbitrary"))


def _wgrad(a, b, a_spec, b_spec, nq, ka, nb, s, name, stack=None, stack_blocks=1, block=0):
    tk = min(WGRAD_ROWS, s)
    nk = s // tk

    def body(a_ref, b_ref, *rest):
        o_ref, acc_ref = rest[-2:]
        k = pl.program_id(1)

        @pl.when(k == 0)
        def _():
            acc_ref[...] = jnp.zeros_like(acc_ref)

        av = a_ref[...].reshape(tk, ka)
        bv = b_ref[...].reshape(tk, nb)
        acc_ref[...] += _dot_tn(av, bv)

        @pl.when(k == nk - 1)
        def _():
            o_ref[0] = acc_ref[...].astype(o_ref.dtype)

    extra = [] if stack is None else [stack]
    return pl.pallas_call(
        body, name=name, grid=(nq, nk),
        in_specs=[a_spec(tk), b_spec(tk)] + [pl.BlockSpec(memory_space=pl.ANY)] * len(extra),
        out_specs=pl.BlockSpec((1, ka, nb), lambda q, k: (q, block, 0)),
        out_shape=SDS((nq, stack_blocks * ka, nb), _LOWP),
        scratch_shapes=[pltpu.VMEM((ka, nb), F32)],
        input_output_aliases={2: 0} if extra else {},
        compiler_params=_cparams("parallel", "arbitrary"),
    )(a, b, *extra)


def _shared_rows(ncols):
    return lambda tk: pl.BlockSpec((tk, ncols), lambda q, k: (k, 0))


def _column_block(ncols):
    return lambda tk: pl.BlockSpec((tk, ncols), lambda q, k: (k, q))


def _quarter_major(ncols):
    return lambda tk: pl.BlockSpec((1, tk, ncols), lambda q, k: (q, k, 0))


def _pool_bwd(dxo, x, yp, vec, pw, carry=None):
    s, d = x.shape
    ng, gd, _ = pw.shape
    tm = _row_block(s)
    hb = tm // POOL_HALO
    nsteps = s // tm
    last_halo = s // POOL_HALO - 1

    def body(dxo_ref, halo_ref, x_ref, yp_ref, vec_ref, pw_ref, dyp_ref, dx_ref, sums_ref, eext_ref, dh_ref):
        i = pl.program_id(0)
        vec = vec_ref[...]

        @pl.when(i == 0)
        def _():
            sums_ref[...] = jnp.zeros_like(sums_ref)

        dxv = dxo_ref[...]
        ypv = yp_ref[...].astype(F32)
        gate1 = 1.0 + vec[3:4]
        sums_ref[3:4, :] += _colsum(dxv * (ypv * vec[4:5]))
        sums_ref[4:5, :] += _colsum(dxv * gate1 * ypv)
        dyp = (dxv * gate1 * vec[4:5]).astype(_LOWP)
        dyp_ref[...] = dyp
        dyp_halo = (jnp.where(i < nsteps - 1, halo_ref[...], 0.0) * gate1 * vec[4:5]).astype(_LOWP)
        for g in range(ng):
            cols = slice(g * gd, (g + 1) * gd)
            dm = _dot_nt(dyp[:, cols], pw_ref[g])
            dh_ref[:, cols] = dm
            inv = _pool_inv_count(i * tm, tm, POOL_WINDOWS[g])
            eext_ref[0:tm, cols] = dm * jnp.concatenate([inv] * (gd // LANES), axis=1)
            eext_ref[tm:, cols] = _dot_nt(dyp_halo[:, cols], pw_ref[g]) * (1.0 / POOL_WINDOWS[g])

        for g in range(ng):
            def tile(r0, c0, g=g):
                acc = eext_ref[pl.ds(r0, TILE_ROWS + POOL_HALO), c0:c0 + LANES]
                for step in range(g + 1):
                    acc = acc + _shift_up(acc, 2 ** step)
                here = (pl.ds(r0, TILE_ROWS), slice(c0, c0 + LANES))
                dh_ref[here] = acc[:TILE_ROWS] - dh_ref[here]

            _for_tiles(tm, g * gd, gd, tile)

        dh = dh_ref[...]
        xh, r, n = _rms(x_ref[...], vec[0:1])
        sums_ref[2:3, :] += _colsum(dh)
        sums_ref[1:2, :] += _colsum(dh * n)
        dn = dh * (1.0 + vec[1:2])
        sums_ref[0:1, :] += _colsum(dn * xh)
        dx_ref[...] = dxv + _rms_bwd(dn, xh, r, vec[0:1])

    return _call(
        body, dxo, dxo, x, yp, vec, pw, carry=carry, name="pool_bwd", grid=(nsteps,),
        in_specs=[_rows_spec(tm, d),
                  pl.BlockSpec((POOL_HALO, d), lambda i: (jnp.minimum((i + 1) * hb, last_halo), 0)),
                  _rows_spec(tm, d), _rows_spec(tm, d), _const_spec(vec.shape), _const_spec(pw.shape)],
        out_specs=[_rows_spec(tm, d), _rows_spec(tm, d), _const_spec((5, d))],
        out_shape=[SDS((s, d), _LOWP), SDS((s, d), F32), SDS((5, d), F32)],
        scratch_shapes=[pltpu.VMEM((tm + POOL_HALO, d), F32), pltpu.VMEM((tm, d), F32)], sem=("arbitrary",))


def _conv_b_bwd(dxo, cw, y, vec, w2, carry=None):
    s, d = dxo.shape
    tm = _row_block(s)

    def body(dxo_ref, cw_ref, y_ref, vec_ref, w2_ref, dy_ref, dcw_ref, sums_ref):
        i = pl.program_id(0)
        vec = vec_ref[...]

        @pl.when(i == 0)
        def _():
            sums_ref[...] = jnp.zeros_like(sums_ref)

        dxv = dxo_ref[...]
        dy = dxv * (1.0 + vec[4:5])
        sums_ref[0:1, :] += _colsum(dxv * y_ref[...].astype(F32))
        sums_ref[1:2, :] += _colsum(dy)
        dyb = dy.astype(_LOWP)
        dy_ref[...] = dyb
        ds = _dot_nt(dyb, w2_ref[...])
        cw = cw_ref[...]
        cc = cw - jnp.mean(cw, axis=-1, keepdims=True)
        rs = lax.rsqrt(jnp.mean(cc * cc, axis=-1, keepdims=True) + EPS)
        ch = cc * rs
        lo = ch * vec[1:2] + vec[2:3]
        dlo = ds * _dsilu(lo)
        sums_ref[2:3, :] += _colsum(dlo * ch)
        sums_ref[3:4, :] += _colsum(dlo)
        dch = dlo * vec[1:2]
        dcw = rs * (dch - jnp.mean(dch, axis=-1, keepdims=True) - ch * jnp.mean(dch * ch, axis=-1, keepdims=True))
        sums_ref[4:5, :] += _colsum(dcw)
        dcw_ref[...] = dcw

    return _call(
        body, dxo, cw, y, vec, w2, carry=carry, name="conv_b_bwd", grid=(s // tm,),
        in_specs=[_rows_spec(tm, d), _rows_spec(tm, d), _rows_spec(tm, d), _const_spec(vec.shape), _const_spec(w2.shape)],
        out_specs=[_rows_spec(tm, d), _rows_spec(tm, d), _const_spec((5, d))],
        out_shape=[SDS((s, d), _LOWP), SDS((s, d), F32), SDS((5, d), F32)], sem=("arbitrary",))


def _conv_a_bwd(dcw, glu, a, g, x, dxo, vec, w8, w1, carry=None):
    s, d = x.shape
    nq, _, hq = w1.shape
    tm = _row_block(s)
    hb = tm // CONV_HALO
    nsteps = s // tm
    last_halo = s // CONV_HALO - 1
    lead = CONV_HALO - CONV_WIDTH + 1

    def body(dcw_ref, dhalo_ref, glu_ref, ghalo_ref, a_ref, g_ref, x_ref, dxo_ref, vec_ref, w8_ref, w1_ref,
             du_ref, dx_ref, sums_ref, db1_ref, dwdw_ref, dext_ref, gext_ref, dglu_ref, wacc_ref):
        i = pl.program_id(0)
        vec = vec_ref[...]

        @pl.when(i == 0)
        def _():
            sums_ref[...] = jnp.zeros_like(sums_ref)
            db1_ref[...] = jnp.zeros_like(db1_ref)
            wacc_ref[...] = jnp.zeros_like(wacc_ref)

        dext_ref[0:tm, :] = dcw_ref[...]
        dext_ref[tm:, :] = jnp.where(i < nsteps - 1, dhalo_ref[...], 0.0)
        gext_ref[0:CONV_HALO, :] = jnp.where(i > 0, ghalo_ref[...], 0.0)
        gext_ref[CONV_HALO:, :] = glu_ref[...]

        def tile(r0, c0):
            lanes = slice(c0, c0 + LANES)
            dwin = dext_ref[pl.ds(r0, TILE_ROWS + CONV_HALO), lanes]
            acc = jnp.zeros((TILE_ROWS // SUBLANES, SUBLANES, LANES), F32)
            for b in range(SUBLANES):
                sh = _shift_up(dwin, b)
                for a8 in range(CONV_HALO // SUBLANES):
                    k = CONV_WIDTH - 1 - (SUBLANES * a8 + b)
                    if 0 <= k < CONV_WIDTH:
                        acc = acc + w8_ref[k, :, lanes][None] * _tiles3(sh[SUBLANES * a8:SUBLANES * a8 + TILE_ROWS])
            dglu_ref[pl.ds(r0, TILE_ROWS), lanes] = acc.reshape(TILE_ROWS, LANES)

            dc3 = _tiles3(dwin[:TILE_ROWS])
            gwin = gext_ref[pl.ds(r0, TILE_ROWS + CONV_HALO), lanes]
            for b in range(SUBLANES):
                sh = _shift_up(gwin, b)
                for a8 in range(CONV_HALO // SUBLANES + 1):
                    k = SUBLANES * a8 + b - lead
                    if 0 <= k < CONV_WIDTH:
                        part = jnp.sum(dc3 * _tiles3(sh[SUBLANES * a8:SUBLANES * a8 + TILE_ROWS]), axis=0)
                        wacc_ref[k, :, lanes] += part

        _for_tiles(tm, 0, d, tile)

        dglu = dglu_ref[...]
        av, gv = a_ref[...].astype(F32), g_ref[...].astype(F32)
        sg = _sigmoid(gv)
        da = dglu * sg
        dg = dglu * av * sg * (1.0 - sg)
        du_ref[:, 0:d] = da.astype(_LOWP)
        du_ref[:, d:2 * d] = dg.astype(_LOWP)
        db1_ref[:, 0:d] += _colsum(da)
        db1_ref[:, d:2 * d] += _colsum(dg)
        dh = jnp.zeros((tm, d), F32)
        for q in range(nq):
            dh = dh + _dot_nt(du_ref[:, q * hq:(q + 1) * hq], w1_ref[q])
        xh, r, n = _rms(x_ref[...], vec[0:1])
        sums_ref[2:3, :] += _colsum(dh)
        sums_ref[1:2, :] += _colsum(dh * n)
        dn = dh * (1.0 + vec[1:2])
        sums_ref[0:1, :] += _colsum(dn * xh)
        dx_ref[...] = dxo_ref[...] + _rms_bwd(dn, xh, r, vec[0:1])

        @pl.when(i == nsteps - 1)
        def _():
            dwdw_ref[...] = jnp.sum(wacc_ref[...], axis=1)

    return _call(
        body, dcw, dcw, glu, glu, a, g, x, dxo, vec, w8, w1, carry=carry, name="conv_a_bwd", grid=(nsteps,),
        in_specs=[_rows_spec(tm, d),
                  pl.BlockSpec((CONV_HALO, d), lambda i: (jnp.minimum((i + 1) * hb, last_halo), 0)),
                  _rows_spec(tm, d),
                  pl.BlockSpec((CONV_HALO, d), lambda i: (jnp.maximum(i * hb - 1, 0), 0)),
                  _rows_spec(tm, d), _rows_spec(tm, d), _rows_spec(tm, d), _rows_spec(tm, d),
                  _const_spec(vec.shape), _const_spec(w8.shape), _const_spec(w1.shape)],
        out_specs=[_rows_spec(tm, 2 * d), _rows_spec(tm, d), _const_spec((3, d)), _const_spec((1, 2 * d)),
                   _const_spec((CONV_WIDTH, d))],
        out_shape=[SDS((s, 2 * d), _LOWP), SDS((s, d), F32), SDS((3, d), F32), SDS((1, 2 * d), F32),
                   SDS((CONV_WIDTH, d), F32)],
        scratch_shapes=[pltpu.VMEM((tm + CONV_HALO, d), F32), pltpu.VMEM((tm + CONV_HALO, d), F32),
                        pltpu.VMEM((tm, d), F32), pltpu.VMEM((CONV_WIDTH, SUBLANES, d), F32)],
        sem=("arbitrary",))


def _adamw(w, g, m, v, name):
    r, c = w.shape
    rb = r
    for cand in (512, 352, 256, 184, 128, 64, 32, 16, 8):
        if r % cand == 0 and cand * c * 4 <= ADAM_BLOCK_BYTES:
            rb = cand
            break

    def body(w_ref, g_ref, m_ref, v_ref, d_ref, mo_ref, vo_ref):
        gv = g_ref[...]
        mn = ADAM_B1 * m_ref[...] + (1.0 - ADAM_B1) * gv
        vn = ADAM_B2 * v_ref[...] + (1.0 - ADAM_B2) * (gv * gv)
        m_hat = mn / (1.0 - ADAM_B1 ** ADAM_STEP)
        v_hat = vn / (1.0 - ADAM_B2 ** ADAM_STEP)
        d_ref[...] = -ADAM_LR * (m_hat / (jnp.sqrt(v_hat) + ADAM_EPS) + ADAM_WD * w_ref[...])
        mo_ref[...] = mn
        vo_ref[...] = vn

    spec = pl.BlockSpec((rb, c), lambda i: (i, 0))
    return pl.pallas_call(
        body, name=name, grid=(r // rb,), in_specs=[spec] * 4, out_specs=[spec] * 3,
        out_shape=[SDS((r, c), F32)] * 3, compiler_params=_cparams("parallel"),
    )(w, g, m, v)


def _adamw_nd(w, g, m, v, name):
    shape = w.shape
    two_d = (-1, shape[-1])
    outs = _adamw(w.reshape(two_d), g.reshape(two_d), m.reshape(two_d), v.reshape(two_d), name)
    return [o.reshape(shape) for o in outs]


def _adamw_from_slots(w, m, v, segs, pos, name, carry=None):
    r, c_ = w.shape
    rb = _divisor_block(r, (256, 176, 128, 64, 32, 16))
    for slots, _, first, rows in segs:
        half = slots.shape[1] // 2
        while first % rb or rows % rb or half % rb:
            rb //= 2
    assert rb % 16 == 0, (name, rb)
    plan, start = [], 0
    for slots, _, first, rows in segs:
        plan.append((start, rows // rb, first // rb, slots.shape[1] // 2 // rb))
        start += rows // rb
    assert start * rb == r, (name, start, rb, r)

    def body(pos_ref, w_ref, m_ref, v_ref, *rest):
        seg_refs, (g_ref, d_ref, mo_ref, vo_ref) = rest[:2 * len(segs)], rest[2 * len(segs):]
        i = pl.program_id(0)
        for k, (s0, nblk, b0, nbh) in enumerate(plan):
            @pl.when((i >= s0) & (i < s0 + nblk))
            def _(k=k, s0=s0, b0=b0, nbh=nbh):
                slots = seg_refs[2 * k][...].astype(F32)
                blk = b0 + i - s0
                in_my_half = (blk >= pos_ref[0] * nbh) & (blk < (pos_ref[0] + 1) * nbh)
                own = jnp.where(in_my_half, seg_refs[2 * k + 1][0].astype(F32), slots[3])
                gv = ((own + slots[0]) + slots[1]) + slots[2]
                mn = ADAM_B1 * m_ref[...] + (1.0 - ADAM_B1) * gv
                vn = ADAM_B2 * v_ref[...] + (1.0 - ADAM_B2) * (gv * gv)
                m_hat = mn / (1.0 - ADAM_B1 ** ADAM_STEP)
                v_hat = vn / (1.0 - ADAM_B2 ** ADAM_STEP)
                g_ref[...] = gv
                d_ref[...] = -ADAM_LR * (m_hat / (jnp.sqrt(v_hat) + ADAM_EPS) + ADAM_WD * w_ref[...])
                mo_ref[...] = mn
                vo_ref[...] = vn

    spec = pl.BlockSpec((rb, c_), lambda i, pos: (i, 0))
    seg_specs, seg_args = [], []
    for (slots, mine, _, _), (s0, nblk, b0, nbh) in zip(segs, plan):
        def in_seg(i, s0=s0, nblk=nblk):
            return jnp.clip(i - s0, 0, nblk - 1)
        seg_specs.append(pl.BlockSpec((N_CHIPS, rb, c_), lambda i, pos, f=in_seg, b0=b0: (0, b0 + f(i), 0)))
        seg_specs.append(pl.BlockSpec(
            (1, rb, c_), lambda i, pos, f=in_seg, b0=b0, nbh=nbh: (pos[1], jnp.clip(b0 + f(i) - pos[0] * nbh, 0, nbh - 1), 0)))
        seg_args += [slots, mine]
    return _call(body, w, m, v, *seg_args, carry=carry, prefetch=pos, name=name, grid=(r // rb,),
                 in_specs=[spec] * 3 + seg_specs, out_specs=[spec] * 4, out_shape=[SDS((r, c_), F32)] * 4,
                 sem=("arbitrary",))


def _rows_of(*vs):
    return jnp.concatenate([v.reshape(-1, v.shape[-1]) for v in vs], axis=0)


def kernel(x, c, ada_w, ada_b, norm_mix_g, norm_ffn_g, conv_w1, conv_b1, conv_wdw, conv_bdw, conv_ln_g, conv_ln_b, conv_w2, conv_b2, pool_w, pool_ls, ffn_w_gate, ffn_w_up, ffn_w_down, final_g, loss_target, m_ada_w, m_ada_b, m_norm_mix_g, m_norm_ffn_g, m_conv_w1, m_conv_b1, m_conv_wdw, m_conv_bdw, m_conv_ln_g, m_conv_ln_b, m_conv_w2, m_conv_b2, m_pool_w, m_pool_ls, m_ffn_w_gate, m_ffn_w_up, m_ffn_w_down, m_final_g, v_ada_w, v_ada_b, v_norm_mix_g, v_norm_ffn_g, v_conv_w1, v_conv_b1, v_conv_wdw, v_conv_bdw, v_conv_ln_g, v_conv_ln_b, v_conv_w2, v_conv_b2, v_pool_w, v_pool_ls, v_ffn_w_gate, v_ffn_w_up, v_ffn_w_down, v_final_g):
    s, d = x.shape[1], x.shape[2]
    depth = ada_w.shape[0]
    assert depth == 2 and conv_w1.shape[0] == 1 and pool_w.shape[0] == 1, "one conv layer then one pool layer"
    fq = ffn_w_gate.shape[2]
    n_ada = ada_b.shape[1] // d
    ada_cols = ada_w.shape[2]
    ng, gq, gd = pool_w.shape[1], pool_w.shape[2], pool_w.shape[3]
    cq = conv_wdw.shape[2]
    ax, ay, ac = _mesh_pos()
    myq = 2 * ax + ay
    me = 4 * ax + 2 * ay + ac
    x0 = x.reshape(s, d)
    tgt = loss_target.reshape(s, d)

    n_c = d // LANES
    small_in = jnp.concatenate(
        [c.reshape(n_c, LANES), conv_wdw[0].reshape(-1, LANES), pool_ls.reshape(-1, LANES)], axis=0)
    small_in_all = _all_gather8(small_in, "gather_c_and_small_shards").reshape(N_DEV, -1, LANES)
    c_all = small_in_all[:, :n_c].reshape(N_DEV, d)
    ada_b_cols = lax.dynamic_slice_in_dim(ada_b, myq * ada_cols, ada_cols, axis=1).reshape(depth, 1, ada_cols)
    conv_shards = [conv_w1[0].astype(_LOWP), conv_w2[0].astype(_LOWP)]
    (mod_part,), conv_w = _ada_forward(c_all, ada_w, ada_b_cols, carry=(_Exchange("gather", conv_shards), conv_shards))
    mod_all = _all_gather8(mod_part.reshape(-1, LANES), "gather_mod").reshape(N_DEV, depth, N_DEV, ada_cols)
    mod_mine = lax.dynamic_index_in_dim(mod_all[0::2], me, axis=2, keepdims=False)
    mod = jnp.transpose(mod_mine, (1, 0, 2)).reshape(depth, n_ada, d)

    def gathered(ex_out, shards):
        return [lax.dynamic_update_slice(g, sh[None], (myq, 0, 0)) for g, sh in zip(ex_out, shards)]

    pos = jnp.stack([ac, myq]).astype(jnp.int32)
    gate_t, up_t = jnp.swapaxes(ffn_w_gate, 1, 2), jnp.swapaxes(ffn_w_up, 1, 2)
    ffn0_shards = [gate_t[0].astype(_LOWP), up_t[0].astype(_LOWP), ffn_w_down[0].astype(_LOWP)]
    l1_shards = [pool_w[0].reshape(ng * gq, gd).astype(_LOWP), gate_t[1].astype(_LOWP), up_t[1].astype(_LOWP),
                 ffn_w_down[1].astype(_LOWP)]
    w1, w2_all = gathered(conv_w, conv_shards)
    w2 = w2_all.reshape(d, d)
    wdw_full_shards = small_in_all[0::2, n_c:].reshape(N_CHIPS, CONV_WIDTH + 1, cq)
    wdw = jnp.transpose(wdw_full_shards[:, :CONV_WIDTH], (1, 0, 2)).reshape(CONV_WIDTH, d)
    pls = wdw_full_shards[:, CONV_WIDTH].reshape(1, d)
    w8 = jnp.broadcast_to(wdw[:, None, :], (CONV_WIDTH, SUBLANES, d))

    vec_ca = _rows_of(norm_mix_g[0], mod[0, 1], mod[0, 0])
    h0, a0, g0, glu = _conv_a_fwd(x0, vec_ca, w1, conv_b1)
    vec_cb = _rows_of(conv_bdw, conv_ln_g, conv_ln_b, conv_b2, mod[0, 2])
    (cw, s0, y0, x1), ffn0_w = _conv_b_fwd(glu, w8, vec_cb, w2, x0, carry=(_Exchange("gather", ffn0_shards), ffn0_shards))
    wg0, wu0, wd0 = gathered(ffn0_w, ffn0_shards)
    vec_f0 = _rows_of(norm_ffn_g[0], mod[0, 4], mod[0, 3], mod[0, 5])
    (h2a, gta, upa, acta, y2a, x2), l1_w = _ffn_fwd(x1, vec_f0, wg0, wu0, wd0, "ffn0_fwd",
                                                   carry=(_Exchange("gather", l1_shards), l1_shards))
    pw_all, wg1, wu1, wd1 = gathered(l1_w, l1_shards)
    pw = jnp.transpose(pw_all.reshape(N_CHIPS, ng, gq, gd), (1, 0, 2, 3)).reshape(ng, N_CHIPS * gq, gd)
    vec_p = _rows_of(norm_mix_g[1], mod[1, 1], mod[1, 0], mod[1, 2], pls)
    mixed, yp, x3 = _pool_fwd(x2, vec_p, pw)
    vec_f1 = _rows_of(norm_ffn_g[1], mod[1, 4], mod[1, 3], mod[1, 5])
    (h2b, gtb, upb, actb, y2b, dx4, fin_sums), _ = _ffn_fwd(x3, vec_f1, wg1, wu1, wd1, "ffn1_fwd_loss",
                                                           head=(tgt, final_g.reshape(1, d)))

    def add_halves(partials, swapped, tag):
        return [_add_sibling_half(p, r, pos, f"grad_add_{tag}{k}") for k, (p, r) in enumerate(zip(partials, swapped))]

    def ffn_wgrads(h2, dgt, dup, act, dy, tag):
        gate_up = _wgrad(dgt, h2, _quarter_major(fq), _shared_rows(d), N_CHIPS, fq, d, s, "wgrad_gate" + tag, stack_blocks=2)
        gate_up = _wgrad(dup, h2, _quarter_major(fq), _shared_rows(d), N_CHIPS, fq, d, s, "wgrad_up" + tag,
                         stack=gate_up, stack_blocks=2, block=1)
        down = _wgrad(act, dy, _quarter_major(fq), _shared_rows(d), N_CHIPS, fq, d, s, "wgrad_down" + tag)
        return [gate_up, down]

    (dgtb, dupb, dyb, dx3, sums_f1), _ = _ffn_bwd(dx4, x3, vec_f1, gtb, upb, y2b, wg1, wu1, wd1, "ffn1_bwd")
    part_l1 = ffn_wgrads(h2b, dgtb, dupb, actb, dyb, "1")
    (dyp, dx2, sums_p), swapped_l1 = _pool_bwd(dx3, x2, yp, vec_p, pw, carry=(_SiblingSwap(part_l1), part_l1))
    cs_l1 = add_halves(part_l1, swapped_l1, "l1")
    dpw = _wgrad(mixed, dyp, _column_block(gd), _column_block(gd), ng, gd, gd, s, "wgrad_pool")
    dpw_q = jnp.transpose(dpw.reshape(ng, N_CHIPS, gq, gd), (1, 0, 2, 3)).reshape(N_CHIPS, ng * gq, gd)
    (dgta, dupa, dya, dx1, sums_f0), slots_l1 = _ffn_bwd(dx2, x1, vec_f0, gta, upa, y2a, wg0, wu0, wd0, "ffn0_bwd",
                                                        carry=(_Exchange("reduce", cs_l1), cs_l1))
    part_l0 = ffn_wgrads(h2a, dgta, dupa, acta, dya, "0") + [dpw_q]
    (dy0, dcw, sums_cb), swapped_l0 = _conv_b_bwd(dx1, cw, y0, vec_cb, w2, carry=(_SiblingSwap(part_l0), part_l0))
    dw2 = _wgrad(s0, dy0, _column_block(d // N_CHIPS), _shared_rows(d), N_CHIPS, d // N_CHIPS, d, s, "wgrad_conv2")
    cs_l0 = add_halves(part_l0, swapped_l0, "l0") + add_halves([dw2], _SiblingSwap([dw2]).alone([dw2], "grad_swap_w2"), "w2")
    (du, dx0, sums_ca, db1, dwdw), slots_l0 = _conv_a_bwd(dcw, glu, a0, g0, x0, dx1, vec_ca, w8, w1,
                                                          carry=(_Exchange("reduce", cs_l0), cs_l0))
    dw1 = _wgrad(h0, du, _shared_rows(d), _column_block(2 * d // N_CHIPS), N_CHIPS, d, 2 * d // N_CHIPS, s, "wgrad_conv1")
    cs_conv = add_halves([dw1], _SiblingSwap([dw1]).alone([dw1], "grad_swap_w1"), "w1")

    dmod = jnp.stack([
        _rows_of(sums_ca[2], sums_ca[1], sums_cb[0], sums_f0[2], sums_f0[1], sums_f0[3]),
        _rows_of(sums_p[2], sums_p[1], sums_p[3], sums_f1[2], sums_f1[1], sums_f1[3])])
    small = _rows_of(dmod.reshape(-1, d), sums_ca[0], sums_p[0], sums_f0[0], sums_f1[0], db1.reshape(2, d),
                     sums_cb[4], sums_cb[2], sums_cb[3], sums_cb[1], fin_sums[0], sums_p[4], dwdw, fin_sums[1])
    n_small = small.shape[0]
    pad = (-n_small) % SUBLANES
    small = jnp.concatenate([small, jnp.zeros((pad, d), F32)], axis=0) if pad else small
    small_all = _all_gather8(small.reshape(-1, LANES), "gather_small_grads").reshape(N_DEV, -1, LANES)
    small_sum = _sum_devices(small_all).reshape(-1, d)
    n_dm = depth * n_ada
    g_ada_b = small_sum[0:n_dm].reshape(depth, n_ada * d)
    g_norm_mix = small_sum[n_dm:n_dm + 2]
    g_norm_ffn = small_sum[n_dm + 2:n_dm + 4]
    g_b1 = small_sum[n_dm + 4:n_dm + 6].reshape(1, 2 * d)
    g_bdw, g_lng, g_lnb, g_b2 = (small_sum[n_dm + 6 + k].reshape(1, d) for k in range(4))
    g_final = small_sum[n_dm + 10]
    g_pls = lax.dynamic_slice_in_dim(small_sum[n_dm + 11].reshape(1, d), myq * cq, cq, axis=1)
    g_wdw = lax.dynamic_slice_in_dim(small_sum[n_dm + 12:n_dm + 12 + CONV_WIDTH], myq * cq, cq, axis=1).reshape(conv_wdw.shape)
    loss = small_sum[n_dm + 12 + CONV_WIDTH, 0]
    dmod_all = small_all.reshape(N_DEV, -1, d)[:, 0:n_dm].reshape(N_DEV, depth, n_ada * d)
    dmod_cols = lax.dynamic_slice_in_dim(jnp.transpose(dmod_all, (1, 0, 2)), myq * ada_cols, ada_cols, axis=2)
    g_ada_w = _ada_backward(c_all.T, dmod_cols)

    def small_pack(*vs):
        return jnp.concatenate([v.reshape(-1) for v in vs]).reshape(-1, LANES)

    names = ("ada_b", "norm_mix_g", "norm_ffn_g", "conv_b1", "conv_bdw", "conv_ln_g", "conv_ln_b", "conv_b2", "final_g")
    small_w = (ada_b, norm_mix_g, norm_ffn_g, conv_b1, conv_bdw, conv_ln_g, conv_ln_b, conv_b2, final_g)
    small_g = (g_ada_b, g_norm_mix, g_norm_ffn, g_b1, g_bdw, g_lng, g_lnb, g_b2, g_final)
    small_m = (m_ada_b, m_norm_mix_g, m_norm_ffn_g, m_conv_b1, m_conv_bdw, m_conv_ln_g, m_conv_ln_b, m_conv_b2, m_final_g)
    small_v = (v_ada_b, v_norm_mix_g, v_norm_ffn_g, v_conv_b1, v_conv_bdw, v_conv_ln_g, v_conv_ln_b, v_conv_b2, v_final_g)
    sd, sm, sv = _adamw(small_pack(*small_w), small_pack(*small_g), small_pack(*small_m), small_pack(*small_v), "adamw_small")

    def unpack(flat2d):
        flat = flat2d.reshape(-1)
        out, off = {}, 0
        for nm, wv in zip(names, small_w):
            out[nm] = flat[off:off + wv.size].reshape(wv.shape)
            off += wv.size
        return out

    sd, sm, sv = unpack(sd), unpack(sm), unpack(sv)
    grads = {
        "ada_w": g_ada_w, "ada_b": g_ada_b, "norm_mix_g": g_norm_mix, "norm_ffn_g": g_norm_ffn,
        "conv_b1": g_b1, "conv_wdw": g_wdw, "conv_bdw": g_bdw, "conv_ln_g": g_lng,
        "conv_ln_b": g_lnb, "conv_b2": g_b2, "pool_ls": g_pls, "final_g": g_final,
    }
    plain = {
        "ada_w": (ada_w, m_ada_w, v_ada_w), "conv_wdw": (conv_wdw, m_conv_wdw, v_conv_wdw),
        "pool_ls": (pool_ls, m_pool_ls, v_pool_ls),
    }
    from_slots = {
        "ffn_w_gate": (ffn_w_gate, m_ffn_w_gate, v_ffn_w_gate,
                       lambda: [(slots_l0[0], cs_l0[0], 0, fq), (slots_l1[0], cs_l1[0], 0, fq)]),
        "ffn_w_up": (ffn_w_up, m_ffn_w_up, v_ffn_w_up,
                     lambda: [(slots_l0[0], cs_l0[0], fq, fq), (slots_l1[0], cs_l1[0], fq, fq)]),
        "ffn_w_down": (ffn_w_down, m_ffn_w_down, v_ffn_w_down,
                       lambda: [(slots_l0[1], cs_l0[1], 0, fq), (slots_l1[1], cs_l1[1], 0, fq)]),
        "pool_w": (pool_w, m_pool_w, v_pool_w, lambda: [(slots_l0[2], cs_l0[2], 0, ng * gq)]),
        "conv_w2": (conv_w2, m_conv_w2, v_conv_w2, lambda: [(slots_l0[3], cs_l0[3], 0, d // N_CHIPS)]),
        "conv_w1": (conv_w1, m_conv_w1, v_conv_w1, lambda: [(slots_conv[0], cs_conv[0], 0, d)]),
    }
    order = ("ada_w", "ada_b", "norm_mix_g", "norm_ffn_g", "conv_w1", "conv_b1", "conv_wdw", "conv_bdw", "conv_ln_g",
             "conv_ln_b", "conv_w2", "conv_b2", "pool_w", "pool_ls", "ffn_w_gate", "ffn_w_up", "ffn_w_down", "final_g")
    delta, new_m, new_v = {}, {}, {}
    transposed = ("ffn_w_gate", "ffn_w_up")
    for nm, (wv, mv, vv, segs) in from_slots.items():
        if nm in transposed:
            wv, mv, vv = (jnp.swapaxes(t, 1, 2) for t in (wv, mv, vv))
        two_d = (-1, wv.shape[-1])
        carry = (_Exchange("reduce", cs_conv), cs_conv) if nm == "ffn_w_gate" else None
        outs, brought = _adamw_from_slots(wv.reshape(two_d), mv.reshape(two_d), vv.reshape(two_d), segs(), pos,
                                          "adamw_" + nm, carry=carry)
        if carry is not None:
            slots_conv = brought
        outs = [o.reshape(wv.shape) for o in outs]
        if nm in transposed:
            outs = [jnp.swapaxes(o, 1, 2) for o in outs]
        grads[nm], delta[nm], new_m[nm], new_v[nm] = outs
    for nm in order:
        if nm in from_slots:
            continue
        elif nm in plain:
            wv, mv, vv = plain[nm]
            grads[nm] = grads[nm].reshape(wv.shape)
            delta[nm], new_m[nm], new_v[nm] = _adamw_nd(wv, grads[nm], mv, vv, "adamw_" + nm)
        else:
            delta[nm], new_m[nm], new_v[nm] = sd[nm], sm[nm], sv[nm]
            grads[nm] = grads[nm].reshape(dict(zip(names, small_w))[nm].shape)

    return (loss, dx0.reshape(x.shape), *[grads[n] for n in order], *[delta[n] for n in order],
            *[new_m[n] for n in order], *[new_v[n] for n in order])
```

```python
import functools

import jax
import jax.numpy as jnp
from jax import lax
from jax.experimental import pallas as pl
from jax.experimental.pallas import tpu as pltpu

F32 = jnp.float32
_LOWP = jnp.bfloat16
EPS = 1e-6
CONV_WIDTH = 31
POOL_WINDOWS = (2, 4, 8, 16)
ADAM_LR = 0.001
ADAM_B1 = 0.9
ADAM_B2 = 0.999
ADAM_EPS = 1e-08
ADAM_WD = 0.01
ADAM_STEP = 10

N_CHIPS = 4
N_DEV = 8
LANES = 128
SUBLANES = 8
CONV_HALO = 32
POOL_HALO = 16
TILE_ROWS = 128
VMEM_LIMIT = 56 * 1024 * 1024
ADAM_BLOCK_BYTES = 1 << 20
WGRAD_ROWS = 4096
FFN_FWD_QUARTERS_PER_STEP = 4
FFN_BWD_QUARTERS_PER_STEP = 2
MESH = pl.DeviceIdType.MESH
SDS = jax.ShapeDtypeStruct


def _cparams(*sem):
    return pltpu.CompilerParams(dimension_semantics=sem if sem else None, vmem_limit_bytes=VMEM_LIMIT)


def _row_block(s):
    for tm in (512, 256, 128, 64, 32, 16):
        if s % tm == 0:
            return tm
    raise ValueError(f"sequence length {s} must be a multiple of 16")


def _rows_spec(tm, ncols):
    return pl.BlockSpec((tm, ncols), lambda i, *_: (i, 0))


def _const_spec(shape):
    nd = len(shape)
    return pl.BlockSpec(shape, lambda *_: (0,) * nd)


def _dot(a, b):
    return lax.dot_general(a.astype(_LOWP), b.astype(_LOWP), (((1,), (0,)), ((), ())), preferred_element_type=F32)


def _dot_nt(a, b):
    return lax.dot_general(a.astype(_LOWP), b.astype(_LOWP), (((1,), (1,)), ((), ())), preferred_element_type=F32)


def _dot_tn(a, b):
    return lax.dot_general(a.astype(_LOWP), b.astype(_LOWP), (((0,), (0,)), ((), ())), preferred_element_type=F32)


def _rms(x, gamma):
    r = lax.rsqrt(jnp.mean(x * x, axis=-1, keepdims=True) + EPS)
    xh = x * r
    return xh, r, xh * gamma


def _rms_bwd(dn, xh, r, gamma):
    dxh = dn * gamma
    return r * (dxh - xh * jnp.mean(dxh * xh, axis=-1, keepdims=True))


def _colsum(v):
    return jnp.sum(v, axis=0, keepdims=True)


def _sigmoid(v):
    return jax.nn.sigmoid(v)


def _dsilu(v):
    sg = _sigmoid(v)
    return sg * (1.0 + v * (1.0 - sg))


def _for_tiles(nrows, col0, ncols, fn):
    def step(t, carry):
        r0 = pl.multiple_of(t * TILE_ROWS, TILE_ROWS)
        for col in range(col0, col0 + ncols, LANES):
            fn(r0, col)
        return carry

    lax.fori_loop(0, nrows // TILE_ROWS, step, 0)


def _shift_up(win, b):
    return pltpu.roll(win, win.shape[0] - b, 0) if b else win


def _shift_down(win, b):
    return pltpu.roll(win, b, 0) if b else win


def _tiles3(v):
    return v.reshape(v.shape[0] // SUBLANES, SUBLANES, v.shape[1])


def _resident_scratch(arrays):
    return [pltpu.VMEM(a.shape, a.dtype) for a in arrays] + [pltpu.SemaphoreType.DMA((len(arrays),))]


def _load_resident(hbm_refs, vmem_refs, sems):
    copies = [pltpu.make_async_copy(src, dst, sems.at[k]) for k, (src, dst) in enumerate(zip(hbm_refs, vmem_refs))]
    for cp in copies:
        cp.start()
    for cp in copies:
        cp.wait()


def _mesh_pos():
    return lax.axis_index("x"), lax.axis_index("y"), lax.axis_index("c")


def _other_chips(x, y):
    return [(1 - x, y), (x, 1 - y), (1 - x, 1 - y)]


def _all_gather8(v, name, direct=False):
    m, n = v.shape

    def body(x_ref, out_ref, send_sems, recv_sems, local_sem):
        x, y, c = _mesh_pos()
        me, sibling = (x, y, c), (x, y, 1 - c)
        chips = _other_chips(x, y)

        def rows(px, py, pc):
            return out_ref.at[pl.ds((4 * px + 2 * py + pc) * m, m), :]

        def copy(k, block, to, src=None):
            return pltpu.make_async_remote_copy(
                src_ref=rows(*block) if src is None else src, dst_ref=rows(*block),
                send_sem=send_sems.at[k], recv_sem=recv_sems.at[k], device_id=to, device_id_type=MESH)

        mine = pltpu.make_async_copy(x_ref, rows(*me), local_sem)
        mine.start()
        if direct:
            sends = [copy(0, me, sibling, src=x_ref)]
            for j, (px, py) in enumerate(chips):
                for e in range(2):
                    sends.append(pltpu.make_async_remote_copy(
                        src_ref=x_ref, dst_ref=rows(*me), send_sem=send_sems.at[1 + 2 * j + e],
                        recv_sem=recv_sems.at[1 + 2 * j + c], device_id=(px, py, e), device_id_type=MESH))
            for cp in sends:
                cp.start()
            copy(0, sibling, me).wait_recv()
            for j, chip in enumerate(chips):
                for e in range(2):
                    copy(1 + 2 * j + e, (*chip, e), me).wait_recv()
            for cp in sends:
                cp.wait_send()
            mine.wait()
            return
        first = [copy(0, me, sibling, src=x_ref)]
        first += [copy(1 + j, me, (*chip, c), src=x_ref) for j, chip in enumerate(chips)]
        for cp in first:
            cp.start()
        passed = [copy(4 + j, (*chip, c), sibling) for j, chip in enumerate(chips)]
        for j, chip in enumerate(chips):
            copy(1 + j, (*chip, c), me).wait_recv()
            passed[j].start()
        copy(0, sibling, me).wait_recv()
        for j, chip in enumerate(chips):
            copy(4 + j, (*chip, 1 - c), me).wait_recv()
        for cp in first + passed:
            cp.wait_send()
        mine.wait()

    return pl.pallas_call(
        body, name=name,
        out_shape=SDS((N_DEV * m, n), v.dtype),
        in_specs=[pl.BlockSpec(memory_space=pltpu.VMEM)],
        out_specs=pl.BlockSpec(memory_space=pltpu.VMEM),
        scratch_shapes=[pltpu.SemaphoreType.DMA((7,)), pltpu.SemaphoreType.DMA((7,)), pltpu.SemaphoreType.DMA],
    )(v)


class _Exchange:
    N_SEMS = 7

    def __init__(self, kind, arrays):
        self.kind, self.n = kind, len(arrays)
        self.streams = 3
        if kind == "gather":
            self.out_shape = [SDS((N_CHIPS,) + a.shape, a.dtype) for a in arrays]
        else:
            self.out_shape = [SDS((N_CHIPS, 2 * a.shape[1], a.shape[2]), a.dtype) for a in arrays]
        self.scratch = [pltpu.SemaphoreType.DMA((self.n, self.N_SEMS)), pltpu.SemaphoreType.DMA((self.n, self.N_SEMS))]

    def _ctx(self, ins, outs, send_sems, recv_sems):
        x, y, c = _mesh_pos()
        myq = 2 * x + y
        sibling = (x, y, 1 - c)
        chips = _other_chips(x, y)
        gather = self.kind == "gather"
        ns = self.streams

        def rows(a, h):
            hr = outs[a].shape[1] // 2
            return pl.ds(h * hr, hr)

        def copy(a, ks, kr, src, dst, to):
            return pltpu.make_async_remote_copy(src_ref=src, dst_ref=dst, send_sem=send_sems.at[a, ks],
                                                recv_sem=recv_sems.at[a, kr], device_id=to, device_id_type=MESH)

        def place(a, t, h):
            if gather:
                px, py = chips[t]
                return outs[a].at[2 * px + py, rows(a, h)]
            return outs[a].at[t, rows(a, h)]

        def first(a, t):
            if gather:
                return copy(a, t, t, ins[a].at[rows(a, c)], outs[a].at[myq, rows(a, c)], (*chips[t], c))
            px, py = chips[t]
            return copy(a, t, t, ins[a].at[2 * px + py], outs[a].at[t, rows(a, c)], (px, py, c))

        def landed(a, t):
            return copy(a, t, t, place(a, t, c), place(a, t, c), sibling)

        def relay(a, t):
            return copy(a, ns + t, ns + t, place(a, t, c), place(a, t, c), sibling)

        def relayed(a, t):
            return copy(a, ns + t, ns + t, place(a, t, 1 - c), place(a, t, 1 - c), sibling)

        def own(a):
            return copy(a, 2 * ns, 2 * ns, ins[a].at[myq], outs[a].at[ns, rows(a, c)], sibling)

        return first, landed, relay, relayed, own

    def start(self, ins, outs, send_sems, recv_sems):
        first, _, _, _, own = self._ctx(ins, outs, send_sems, recv_sems)
        for a in range(self.n):
            for t in range(self.streams):
                first(a, t).start()
            if self.kind == "reduce":
                own(a).start()

    def relay(self, ins, outs, send_sems, recv_sems):
        _, landed, relay, _, _ = self._ctx(ins, outs, send_sems, recv_sems)
        for a in range(self.n):
            for t in range(self.streams):
                landed(a, t).wait_recv()
                relay(a, t).start()

    def finish(self, ins, outs, send_sems, recv_sems):
        first, _, relay, relayed, own = self._ctx(ins, outs, send_sems, recv_sems)
        for a in range(self.n):
            for t in range(self.streams):
                relayed(a, t).wait_recv()
            if self.kind == "reduce":
                own(a).wait_recv()
        for a in range(self.n):
            for t in range(self.streams):
                first(a, t).wait_send()
                relay(a, t).wait_send()
            if self.kind == "reduce":
                own(a).wait_send()

    def alone(self, arrays, name):
        n = self.n

        def body(*refs):
            args = (refs[:n], refs[n:2 * n], refs[2 * n], refs[2 * n + 1])
            self.start(*args)
            self.relay(*args)
            self.finish(*args)

        any_spec = pl.BlockSpec(memory_space=pl.ANY)
        return pl.pallas_call(body, name=name, out_shape=self.out_shape, in_specs=[any_spec] * n,
                              out_specs=[any_spec] * n, scratch_shapes=self.scratch)(*arrays)


RELAY_AT = (5, 8)
LATE_RELAY_AT = (3, 4)


def _call(body, *args, carry=None, prefetch=None, relay_at=RELAY_AT, name, grid, in_specs, out_specs, out_shape,
          scratch_shapes=(), sem):
    def run(fn, in_specs_, out_specs_, out_shape_, scratch_, sem_, operands):
        if prefetch is None:
            return pl.pallas_call(fn, name=name, grid=grid, in_specs=in_specs_, out_specs=out_specs_, out_shape=out_shape_,
                                  scratch_shapes=scratch_, compiler_params=_cparams(*sem_))(*operands)
        grid_spec = pltpu.PrefetchScalarGridSpec(num_scalar_prefetch=1, grid=grid, in_specs=in_specs_,
                                                 out_specs=out_specs_, scratch_shapes=scratch_)
        return pl.pallas_call(fn, name=name, grid_spec=grid_spec, out_shape=out_shape_,
                              compiler_params=_cparams(*sem_))(prefetch, *operands)

    if carry is None:
        return run(body, list(in_specs), list(out_specs), list(out_shape), list(scratch_shapes), sem, args), None
    ex, arrays = carry
    n_in, n_out, n_scr, n = len(in_specs), len(out_specs), len(scratch_shapes), ex.n
    n_pre = 0 if prefetch is None else 1
    total = 1
    for g in grid:
        total *= g
    relay_step = (total * relay_at[0]) // relay_at[1]

    def hosted(*refs):
        pre, refs = refs[:n_pre], refs[n_pre:]
        ins, cins = refs[:n_in], refs[n_in:n_in + n]
        outs, couts = refs[n_in + n:n_in + n + n_out], refs[n_in + n + n_out:n_in + 2 * n + n_out]
        scr = refs[n_in + 2 * n + n_out:n_in + 2 * n + n_out + n_scr]
        sems = refs[n_in + 2 * n + n_out + n_scr:]
        step = 0
        for ax, g in enumerate(grid):
            step = step * g + pl.program_id(ax)

        @pl.when(step == 0)
        def _():
            ex.start(cins, couts, *sems)

        body(*pre, *ins, *outs, *scr)

        @pl.when(step == relay_step)
        def _():
            ex.relay(cins, couts, *sems)

        @pl.when(step == total - 1)
        def _():
            ex.finish(cins, couts, *sems)

    any_spec = pl.BlockSpec(memory_space=pl.ANY)
    res = run(hosted, list(in_specs) + [any_spec] * n, list(out_specs) + [any_spec] * n, list(out_shape) + ex.out_shape,
              list(scratch_shapes) + ex.scratch, ("arbitrary",) * len(grid), (*args, *arrays))
    return res[:n_out], res[n_out:]


class _SiblingSwap:
    def __init__(self, arrays):
        self.n = len(arrays)
        self.out_shape = [SDS((a.shape[0], a.shape[1] // 2, a.shape[2]), a.dtype) for a in arrays]
        self.scratch = [pltpu.SemaphoreType.DMA((self.n,)), pltpu.SemaphoreType.DMA((self.n,))]

    def _copies(self, ins, outs, send_sems, recv_sems):
        x, y, c = _mesh_pos()
        cps = []
        for a in range(self.n):
            hr = ins[a].shape[1] // 2
            cps.append(pltpu.make_async_remote_copy(
                src_ref=ins[a].at[:, pl.ds((1 - c) * hr, hr)], dst_ref=outs[a],
                send_sem=send_sems.at[a], recv_sem=recv_sems.at[a], device_id=(x, y, 1 - c), device_id_type=MESH))
        return cps

    def start(self, *refs):
        for cp in self._copies(*refs):
            cp.start()

    def relay(self, *refs):
        pass

    def finish(self, *refs):
        for cp in self._copies(*refs):
            cp.wait()

    def alone(self, arrays, name):
        return _Exchange.alone(self, arrays, name)


def _divisor_block(rows, cands):
    for rb in cands:
        if rows % rb == 0:
            return rb
    raise ValueError(rows)


def _add_sibling_half(p, rb1, pos, name):
    nq, r, n = p.shape
    h = r // 2
    rb = _divisor_block(h, (512, 352, 256, 176, 128, 64, 32, 16))
    nb = h // rb

    def body(pos_ref, p_ref, r_ref, o_ref):
        o_ref[...] = (p_ref[...].astype(F32) + r_ref[...].astype(F32)).astype(o_ref.dtype)

    grid_spec = pltpu.PrefetchScalarGridSpec(
        num_scalar_prefetch=1, grid=(nq, nb),
        in_specs=[pl.BlockSpec((1, rb, n), lambda q, i, pos: (q, pos[0] * nb + i, 0)),
                  pl.BlockSpec((1, rb, n), lambda q, i, pos: (q, i, 0))],
        out_specs=pl.BlockSpec((1, rb, n), lambda q, i, pos: (q, i, 0)))
    return pl.pallas_call(
        body, name=name, grid_spec=grid_spec, out_shape=SDS((nq, h, n), p.dtype),
        compiler_params=_cparams("parallel", "parallel"),
    )(pos, p, rb1)


def _sum_devices(g):
    nd, m, n = g.shape

    def body(g_ref, o_ref):
        acc = g_ref[0]
        for d in range(1, nd):
            acc = acc + g_ref[d]
        o_ref[...] = acc

    return pl.pallas_call(body, name="small_grad_sum", out_shape=SDS((m, n), F32))(g)


def _ada_forward(c_all, ada_w, ada_b_cols, carry=None):
    nl, d, ncol = ada_w.shape
    nb = c_all.shape[0]
    tn = 512 if ncol % 512 == 0 else ncol

    def body(c_ref, w_ref, b_ref, o_ref):
        cv = c_ref[...]
        ca = cv * _sigmoid(cv)
        o_ref[0] = _dot(ca, w_ref[0]) + b_ref[0]

    return _call(
        body, c_all, ada_w, ada_b_cols, carry=carry, name="ada_forward", grid=(nl, ncol // tn),
        in_specs=[pl.BlockSpec((nb, d), lambda l, j: (0, 0)),
                  pl.BlockSpec((1, d, tn), lambda l, j: (l, 0, j)),
                  pl.BlockSpec((1, 1, tn), lambda l, j: (l, 0, j))],
        out_specs=[pl.BlockSpec((1, nb, tn), lambda l, j: (l, 0, j))],
        out_shape=[SDS((nl, nb, ncol), F32)], sem=("parallel", "parallel"))


def _ada_backward(c_all_t, dmod_cols):
    d, nb = c_all_t.shape
    nl, _, ncol = dmod_cols.shape
    tn = 512 if ncol % 512 == 0 else ncol

    def body(c_ref, g_ref, o_ref):
        cv = c_ref[...]
        ca = cv * _sigmoid(cv)
        o_ref[0] = _dot(ca, g_ref[0])

    return pl.pallas_call(
        body, name="ada_backward", grid=(nl, ncol // tn),
        in_specs=[pl.BlockSpec((d, nb), lambda l, j: (0, 0)),
                  pl.BlockSpec((1, nb, tn), lambda l, j: (l, 0, j))],
        out_specs=pl.BlockSpec((1, d, tn), lambda l, j: (l, 0, j)),
        out_shape=SDS((nl, d, ncol), F32), compiler_params=_cparams("parallel", "parallel"),
    )(c_all_t, dmod_cols)


def _conv_a_fwd(x, vec, w1, b1, carry=None):
    s, d = x.shape
    nq, _, hq = w1.shape
    tm = _row_block(s)

    def body(x_ref, vec_ref, w1_ref, b1_ref, h_ref, a_ref, g_ref, glu_ref):
        vec = vec_ref[...]
        _, _, n = _rms(x_ref[...], vec[0:1])
        hb = (n * (1.0 + vec[1:2]) + vec[2:3]).astype(_LOWP)
        h_ref[...] = hb
        u = [_dot(hb, w1_ref[q]) + b1_ref[:, q * hq:(q + 1) * hq] for q in range(nq)]
        for k in range(nq // 2):
            av, gv = u[k], u[nq // 2 + k]
            cols = slice(k * hq, (k + 1) * hq)
            a_ref[:, cols] = av.astype(_LOWP)
            g_ref[:, cols] = gv.astype(_LOWP)
            glu_ref[:, cols] = av * _sigmoid(gv)

    return _call(
        body, x, vec, w1, b1, carry=carry, name="conv_a_fwd", grid=(s // tm,),
        in_specs=[_rows_spec(tm, d), _const_spec(vec.shape), _const_spec(w1.shape), _const_spec(b1.shape)],
        out_specs=[_rows_spec(tm, d)] * 4,
        out_shape=[SDS((s, d), _LOWP)] * 3 + [SDS((s, d), F32)], sem=("parallel",))


def _conv_b_fwd(glu, w8, vec, w2, x, carry=None):
    s, d = x.shape
    tm = _row_block(s)
    hb = tm // CONV_HALO

    def body(glu_ref, halo_ref, w8_ref, vec_ref, w2_ref, x_ref, cw_ref, s_ref, y_ref, xo_ref, ext_ref):
        i = pl.program_id(0)
        vec = vec_ref[...]
        ext_ref[0:CONV_HALO, :] = jnp.where(i > 0, halo_ref[...], 0.0)
        ext_ref[CONV_HALO:, :] = glu_ref[...]

        def tile(r0, c0):
            lanes = slice(c0, c0 + LANES)
            win = ext_ref[pl.ds(r0, TILE_ROWS + CONV_HALO), lanes]
            acc = jnp.zeros((TILE_ROWS // SUBLANES, SUBLANES, LANES), F32)
            for b in range(SUBLANES):
                sh = _shift_up(win, b)
                for a in range(CONV_HALO // SUBLANES + 1):
                    k = SUBLANES * a + b - (CONV_HALO - CONV_WIDTH + 1)
                    if 0 <= k < CONV_WIDTH:
                        acc = acc + w8_ref[k, :, lanes][None] * _tiles3(sh[SUBLANES * a:SUBLANES * a + TILE_ROWS])
            cw_ref[pl.ds(r0, TILE_ROWS), lanes] = acc.reshape(TILE_ROWS, LANES)

        _for_tiles(tm, 0, d, tile)
        cw = cw_ref[...] + vec[0:1]
        cw_ref[...] = cw
        cc = cw - jnp.mean(cw, axis=-1, keepdims=True)
        ch = cc * lax.rsqrt(jnp.mean(cc * cc, axis=-1, keepdims=True) + EPS)
        lo = ch * vec[1:2] + vec[2:3]
        sv = (lo * _sigmoid(lo)).astype(_LOWP)
        s_ref[...] = sv
        y = _dot(sv, w2_ref[...]) + vec[3:4]
        y_ref[...] = y.astype(_LOWP)
        xo_ref[...] = x_ref[...] + (1.0 + vec[4:5]) * y

    return _call(
        body, glu, glu, w8, vec, w2, x, carry=carry, relay_at=LATE_RELAY_AT, name="conv_b_fwd", grid=(s // tm,),
        in_specs=[_rows_spec(tm, d),
                  pl.BlockSpec((CONV_HALO, d), lambda i: (jnp.maximum(i * hb - 1, 0), 0)),
                  _const_spec(w8.shape), _const_spec(vec.shape), _const_spec(w2.shape), _rows_spec(tm, d)],
        out_specs=[_rows_spec(tm, d)] * 4,
        out_shape=[SDS((s, d), F32), SDS((s, d), _LOWP), SDS((s, d), _LOWP), SDS((s, d), F32)],
        scratch_shapes=[pltpu.VMEM((tm + CONV_HALO, d), F32)], sem=("parallel",))


def _ffn_fwd(x, vec, wg, wu, wd, name, carry=None, head=None):
    s, d = x.shape
    nq, fq, _ = wg.shape
    tm = _row_block(s)
    nsteps = s // tm
    n_head = 0 if head is None else 2
    qps = FFN_FWD_QUARTERS_PER_STEP

    def body(x_ref, vec_ref, wg_hbm, wu_hbm, wd_hbm, *rest):
        head_refs, rest = rest[:n_head], rest[n_head:]
        h_ref, gt_ref, up_ref, act_ref, y_ref, xo_ref = rest[:6]
        sums_ref = rest[6] if head is not None else None
        hb_ref, yacc_ref, wg_ref, wu_ref, wd_ref, w_sems = rest[6 + n_head // 2:]
        i, p = pl.program_id(0), pl.program_id(1)
        vec = vec_ref[...]

        @pl.when((i == 0) & (p == 0))
        def _():
            _load_resident((wg_hbm, wu_hbm, wd_hbm), (wg_ref, wu_ref, wd_ref), w_sems)
            if head is not None:
                sums_ref[...] = jnp.zeros_like(sums_ref)

        @pl.when(p == 0)
        def _():
            _, _, n = _rms(x_ref[...], vec[0:1])
            hb = (n * (1.0 + vec[1:2]) + vec[2:3]).astype(_LOWP)
            hb_ref[...] = hb
            h_ref[...] = hb
            yacc_ref[...] = jnp.zeros_like(yacc_ref)

        hb = hb_ref[...]
        for k in range(qps):
            q = p * qps + k
            gt = _dot_nt(hb, wg_ref[q])
            up = _dot_nt(hb, wu_ref[q])
            act = (gt * _sigmoid(gt) * up).astype(_LOWP)
            gt_ref[k] = gt.astype(_LOWP)
            up_ref[k] = up.astype(_LOWP)
            act_ref[k] = act
            yacc_ref[...] += _dot(act, wd_ref[q])

        @pl.when(p == nq // qps - 1)
        def _():
            y = yacc_ref[...]
            y_ref[...] = y.astype(_LOWP)
            xo = x_ref[...] + (1.0 + vec[3:4]) * y
            if head is None:
                xo_ref[...] = xo
            else:
                t_ref, g_ref = head_refs
                gamma_v = g_ref[...]
                xh, r, out = _rms(xo, gamma_v)
                err = out - t_ref[...]
                dout = err * (1.0 / d)
                sums_ref[0:1, :] += _colsum(dout * xh)
                sums_ref[1:2, :] += _colsum(0.5 * err * dout)
                xo_ref[...] = _rms_bwd(dout, xh, r, gamma_v)

        if head is not None:
            @pl.when((i == nsteps - 1) & (p == nq // qps - 1))
            def _():
                sums_ref[1:2, :] = jnp.broadcast_to(jnp.sum(sums_ref[1:2, :], axis=-1, keepdims=True), (1, d))

    rows = pl.BlockSpec((tm, d), lambda i, q: (i, 0))
    hid = pl.BlockSpec((qps, tm, fq), lambda i, q: (q, i, 0))
    any_spec = pl.BlockSpec(memory_space=pl.ANY)
    head_args = [] if head is None else [head[0], head[1]]
    head_specs = [] if head is None else [rows, pl.BlockSpec((1, d), lambda i, q: (0, 0))]
    sums_specs = [] if head is None else [pl.BlockSpec((2, d), lambda i, q: (0, 0))]
    sums_shape = [] if head is None else [SDS((2, d), F32)]
    return _call(
        body, x, vec, wg, wu, wd, *head_args, carry=carry, name=name, grid=(nsteps, nq // qps),
        in_specs=[rows, pl.BlockSpec(vec.shape, lambda i, q: (0, 0)), any_spec, any_spec, any_spec] + head_specs,
        out_specs=[rows, hid, hid, hid, rows, rows] + sums_specs,
        out_shape=[SDS((s, d), _LOWP)] + [SDS((nq, s, fq), _LOWP)] * 3 + [SDS((s, d), _LOWP), SDS((s, d), F32)] + sums_shape,
        scratch_shapes=[pltpu.VMEM((tm, d), _LOWP), pltpu.VMEM((tm, d), F32)] + _resident_scratch((wg, wu, wd)),
        sem=("arbitrary", "arbitrary"))


def _pool_inv_count(t0, nrows, w):
    t = t0 + lax.broadcasted_iota(jnp.int32, (nrows, LANES), 0)
    return 1.0 / jnp.minimum(t + 1, w).astype(F32)


def _pool_fwd(x, vec, pw):
    s, d = x.shape
    ng, gd, _ = pw.shape
    tm = _row_block(s)
    hb = tm // POOL_HALO

    def body(x_ref, halo_ref, vec_ref, pw_ref, mixed_ref, yp_ref, xo_ref, hext_ref):
        i = pl.program_id(0)
        vec = vec_ref[...]

        def modulated(xv):
            _, _, n = _rms(xv, vec[0:1])
            return n * (1.0 + vec[1:2]) + vec[2:3]

        hext_ref[0:POOL_HALO, :] = jnp.where(i > 0, modulated(halo_ref[...]), 0.0)
        hext_ref[POOL_HALO:, :] = modulated(x_ref[...])

        for g in range(ng):
            def tile(r0, c0, g=g):
                lanes = slice(c0, c0 + LANES)
                win = hext_ref[pl.ds(r0, TILE_ROWS + POOL_HALO), lanes]
                acc = win
                for step in range(g + 1):
                    acc = acc + _shift_down(acc, 2 ** step)
                inv = _pool_inv_count(i * tm + r0, TILE_ROWS, POOL_WINDOWS[g])
                mixed = acc[POOL_HALO:] * inv - win[POOL_HALO:]
                mixed_ref[pl.ds(r0, TILE_ROWS), lanes] = mixed.astype(_LOWP)

            _for_tiles(tm, g * gd, gd, tile)

        for g in range(ng):
            cols = slice(g * gd, (g + 1) * gd)
            yp = _dot(mixed_ref[:, cols], pw_ref[g])
            yp_ref[:, cols] = yp.astype(_LOWP)
            xo_ref[:, cols] = x_ref[:, cols] + (1.0 + vec[3:4, cols]) * (yp * vec[4:5, cols])

    return pl.pallas_call(
        body, name="pool_fwd", grid=(s // tm,),
        in_specs=[_rows_spec(tm, d),
                  pl.BlockSpec((POOL_HALO, d), lambda i: (jnp.maximum(i * hb - 1, 0), 0)),
                  _const_spec(vec.shape), _const_spec(pw.shape)],
        out_specs=[_rows_spec(tm, d)] * 3,
        out_shape=[SDS((s, d), _LOWP), SDS((s, d), _LOWP), SDS((s, d), F32)],
        scratch_shapes=[pltpu.VMEM((tm + POOL_HALO, d), F32)],
        compiler_params=_cparams("parallel"),
    )(x, x, vec, pw)


def _ffn_bwd(dxo, x, vec, gt, up, y, wg, wu, wd, name, carry=None):
    s, d = x.shape
    nq, fq, _ = wg.shape
    tm = _row_block(s)
    qps = FFN_BWD_QUARTERS_PER_STEP

    def body(dxo_ref, x_ref, vec_ref, gt_ref, up_ref, y_ref, wg_hbm, wu_hbm, wd_hbm,
             dgt_ref, dup_ref, dy_ref, dx_ref, sums_ref, dyb_ref, dh_ref, wg_ref, wu_ref, wd_ref, w_sems):
        i, p = pl.program_id(0), pl.program_id(1)
        vec = vec_ref[...]

        @pl.when((i == 0) & (p == 0))
        def _():
            _load_resident((wg_hbm, wu_hbm, wd_hbm), (wg_ref, wu_ref, wd_ref), w_sems)
            sums_ref[...] = jnp.zeros_like(sums_ref)

        @pl.when(p == 0)
        def _():
            dxv = dxo_ref[...]
            dyb = (dxv * (1.0 + vec[3:4])).astype(_LOWP)
            dyb_ref[...] = dyb
            dy_ref[...] = dyb
            dh_ref[...] = jnp.zeros_like(dh_ref)
            sums_ref[3:4, :] += _colsum(dxv * y_ref[...].astype(F32))

        halves = [slice(k * (tm // 2), (k + 1) * (tm // 2)) for k in range(2)]
        for k in range(qps):
            q = p * qps + k
            dacts = [_dot_nt(dyb_ref[rs, :], wd_ref[q]) for rs in halves]
            dgus = []
            for rs, dact in zip(halves, dacts):
                gtv = gt_ref[k, rs, :].astype(F32)
                sg = _sigmoid(gtv)
                dgt = (dact * up_ref[k, rs, :].astype(F32) * (sg * (1.0 + gtv * (1.0 - sg)))).astype(_LOWP)
                dup = (dact * (gtv * sg)).astype(_LOWP)
                dgt_ref[k, rs, :] = dgt
                dup_ref[k, rs, :] = dup
                dgus.append((dgt, dup))
            for rs, (dgt, dup) in zip(halves, dgus):
                dh_ref[rs, :] += _dot(dgt, wg_ref[q]) + _dot(dup, wu_ref[q])

        @pl.when(p == nq // qps - 1)
        def _():
            dh = dh_ref[...]
            xh, r, n = _rms(x_ref[...], vec[0:1])
            sums_ref[2:3, :] += _colsum(dh)
            sums_ref[1:2, :] += _colsum(dh * n)
            dn = dh * (1.0 + vec[1:2])
            sums_ref[0:1, :] += _colsum(dn * xh)
            dx_ref[...] = dxo_ref[...] + _rms_bwd(dn, xh, r, vec[0:1])

    rows = pl.BlockSpec((tm, d), lambda i, q: (i, 0))
    hid = pl.BlockSpec((qps, tm, fq), lambda i, q: (q, i, 0))
    return _call(
        body, dxo, x, vec, gt, up, y, wg, wu, wd, carry=carry, name=name, grid=(s // tm, nq // qps),
        in_specs=[rows, rows, pl.BlockSpec(vec.shape, lambda i, q: (0, 0)), hid, hid, rows] +
                 [pl.BlockSpec(memory_space=pl.ANY)] * 3,
        out_specs=[hid, hid, rows, rows, pl.BlockSpec((4, d), lambda i, q: (0, 0))],
        out_shape=[SDS((nq, s, fq), _LOWP)] * 2 + [SDS((s, d), _LOWP), SDS((s, d), F32), SDS((4, d), F32)],
        scratch_shapes=[pltpu.VMEM((tm, d), _LOWP), pltpu.VMEM((tm, d), F32)] + _resident_scratch((wg, wu, wd)),
        sem=("arbitrary", "arbitrary"))


def _wgrad(a, b, a_spec, b_spec, nq, ka, nb, s, name, stack=None, stack_blocks=1, block=0):
    tk = min(WGRAD_ROWS, s)
    nk = s // tk

    def body(a_ref, b_ref, *rest):
        o_ref, acc_ref = rest[-2:]
        k = pl.program_id(1)

        @pl.when(k == 0)
        def _():
            acc_ref[...] = jnp.zeros_like(acc_ref)

        av = a_ref[...].reshape(tk, ka)
        bv = b_ref[...].reshape(tk, nb)
        acc_ref[...] += _dot_tn(av, bv)

        @pl.when(k == nk - 1)
        def _():
            o_ref[0] = acc_ref[...].astype(o_ref.dtype)

    extra = [] if stack is None else [stack]
    return pl.pallas_call(
        body, name=name, grid=(nq, nk),
        in_specs=[a_spec(tk), b_spec(tk)] + [pl.BlockSpec(memory_space=pl.ANY)] * len(extra),
        out_specs=pl.BlockSpec((1, ka, nb), lambda q, k: (q, block, 0)),
        out_shape=SDS((nq, stack_blocks * ka, nb), _LOWP),
        scratch_shapes=[pltpu.VMEM((ka, nb), F32)],
        input_output_aliases={2: 0} if extra else {},
        compiler_params=_cparams("parallel", "arbitrary"),
    )(a, b, *extra)


def _shared_rows(ncols):
    return lambda tk: pl.BlockSpec((tk, ncols), lambda q, k: (k, 0))


def _column_block(ncols):
    return lambda tk: pl.BlockSpec((tk, ncols), lambda q, k: (k, q))


def _quarter_major(ncols):
    return lambda tk: pl.BlockSpec((1, tk, ncols), lambda q, k: (q, k, 0))


def _pool_bwd(dxo, x, yp, vec, pw, carry=None):
    s, d = x.shape
    ng, gd, _ = pw.shape
    tm = _row_block(s)
    hb = tm // POOL_HALO
    nsteps = s // tm
    last_halo = s // POOL_HALO - 1

    def body(dxo_ref, halo_ref, x_ref, yp_ref, vec_ref, pw_ref, dyp_ref, dx_ref, sums_ref, eext_ref, dh_ref):
        i = pl.program_id(0)
        vec = vec_ref[...]

        @pl.when(i == 0)
        def _():
            sums_ref[...] = jnp.zeros_like(sums_ref)

        dxv = dxo_ref[...]
        ypv = yp_ref[...].astype(F32)
        gate1 = 1.0 + vec[3:4]
        sums_ref[3:4, :] += _colsum(dxv * (ypv * vec[4:5]))
        sums_ref[4:5, :] += _colsum(dxv * gate1 * ypv)
        dyp = (dxv * gate1 * vec[4:5]).astype(_LOWP)
        dyp_ref[...] = dyp
        dyp_halo = (jnp.where(i < nsteps - 1, halo_ref[...], 0.0) * gate1 * vec[4:5]).astype(_LOWP)
        for g in range(ng):
            cols = slice(g * gd, (g + 1) * gd)
            dm = _dot_nt(dyp[:, cols], pw_ref[g])
            dh_ref[:, cols] = dm
            inv = _pool_inv_count(i * tm, tm, POOL_WINDOWS[g])
            eext_ref[0:tm, cols] = dm * jnp.concatenate([inv] * (gd // LANES), axis=1)
            eext_ref[tm:, cols] = _dot_nt(dyp_halo[:, cols], pw_ref[g]) * (1.0 / POOL_WINDOWS[g])

        for g in range(ng):
            def tile(r0, c0, g=g):
                acc = eext_ref[pl.ds(r0, TILE_ROWS + POOL_HALO), c0:c0 + LANES]
                for step in range(g + 1):
                    acc = acc + _shift_up(acc, 2 ** step)
                here = (pl.ds(r0, TILE_ROWS), slice(c0, c0 + LANES))
                dh_ref[here] = acc[:TILE_ROWS] - dh_ref[here]

            _for_tiles(tm, g * gd, gd, tile)

        dh = dh_ref[...]
        xh, r, n = _rms(x_ref[...], vec[0:1])
        sums_ref[2:3, :] += _colsum(dh)
        sums_ref[1:2, :] += _colsum(dh * n)
        dn = dh * (1.0 + vec[1:2])
        sums_ref[0:1, :] += _colsum(dn * xh)
        dx_ref[...] = dxv + _rms_bwd(dn, xh, r, vec[0:1])

    return _call(
        body, dxo, dxo, x, yp, vec, pw, carry=carry, name="pool_bwd", grid=(nsteps,),
        in_specs=[_rows_spec(tm, d),
                  pl.BlockSpec((POOL_HALO, d), lambda i: (jnp.minimum((i + 1) * hb, last_halo), 0)),
                  _rows_spec(tm, d), _rows_spec(tm, d), _const_spec(vec.shape), _const_spec(pw.shape)],
        out_specs=[_rows_spec(tm, d), _rows_spec(tm, d), _const_spec((5, d))],
        out_shape=[SDS((s, d), _LOWP), SDS((s, d), F32), SDS((5, d), F32)],
        scratch_shapes=[pltpu.VMEM((tm + POOL_HALO, d), F32), pltpu.VMEM((tm, d), F32)], sem=("arbitrary",))


def _conv_b_bwd(dxo, cw, y, vec, w2, carry=None):
    s, d = dxo.shape
    tm = _row_block(s)

    def body(dxo_ref, cw_ref, y_ref, vec_ref, w2_ref, dy_ref, dcw_ref, sums_ref):
        i = pl.program_id(0)
        vec = vec_ref[...]

        @pl.when(i == 0)
        def _():
            sums_ref[...] = jnp.zeros_like(sums_ref)

        dxv = dxo_ref[...]
        dy = dxv * (1.0 + vec[4:5])
        sums_ref[0:1, :] += _colsum(dxv * y_ref[...].astype(F32))
        sums_ref[1:2, :] += _colsum(dy)
        dyb = dy.astype(_LOWP)
        dy_ref[...] = dyb
        ds = _dot_nt(dyb, w2_ref[...])
        cw = cw_ref[...]
        cc = cw - jnp.mean(cw, axis=-1, keepdims=True)
        rs = lax.rsqrt(jnp.mean(cc * cc, axis=-1, keepdims=True) + EPS)
        ch = cc * rs
        lo = ch * vec[1:2] + vec[2:3]
        dlo = ds * _dsilu(lo)
        sums_ref[2:3, :] += _colsum(dlo * ch)
        sums_ref[3:4, :] += _colsum(dlo)
        dch = dlo * vec[1:2]
        dcw = rs * (dch - jnp.mean(dch, axis=-1, keepdims=True) - ch * jnp.mean(dch * ch, axis=-1, keepdims=True))
        sums_ref[4:5, :] += _colsum(dcw)
        dcw_ref[...] = dcw

    return _call(
        body, dxo, cw, y, vec, w2, carry=carry, name="conv_b_bwd", grid=(s // tm,),
        in_specs=[_rows_spec(tm, d), _rows_spec(tm, d), _rows_spec(tm, d), _const_spec(vec.shape), _const_spec(w2.shape)],
        out_specs=[_rows_spec(tm, d), _rows_spec(tm, d), _const_spec((5, d))],
        out_shape=[SDS((s, d), _LOWP), SDS((s, d), F32), SDS((5, d), F32)], sem=("arbitrary",))


def _conv_a_bwd(dcw, glu, a, g, x, dxo, vec, w8, w1, carry=None):
    s, d = x.shape
    nq, _, hq = w1.shape
    tm = _row_block(s)
    hb = tm // CONV_HALO
    nsteps = s // tm
    last_halo = s // CONV_HALO - 1
    lead = CONV_HALO - CONV_WIDTH + 1

    def body(dcw_ref, dhalo_ref, glu_ref, ghalo_ref, a_ref, g_ref, x_ref, dxo_ref, vec_ref, w8_ref, w1_ref,
             du_ref, dx_ref, sums_ref, db1_ref, dwdw_ref, dext_ref, gext_ref, dglu_ref, wacc_ref):
        i = pl.program_id(0)
        vec = vec_ref[...]

        @pl.when(i == 0)
        def _():
            sums_ref[...] = jnp.zeros_like(sums_ref)
            db1_ref[...] = jnp.zeros_like(db1_ref)
            wacc_ref[...] = jnp.zeros_like(wacc_ref)

        dext_ref[0:tm, :] = dcw_ref[...]
        dext_ref[tm:, :] = jnp.where(i < nsteps - 1, dhalo_ref[...], 0.0)
        gext_ref[0:CONV_HALO, :] = jnp.where(i > 0, ghalo_ref[...], 0.0)
        gext_ref[CONV_HALO:, :] = glu_ref[...]

        def tile(r0, c0):
            lanes = slice(c0, c0 + LANES)
            dwin = dext_ref[pl.ds(r0, TILE_ROWS + CONV_HALO), lanes]
            acc = jnp.zeros((TILE_ROWS // SUBLANES, SUBLANES, LANES), F32)
            for b in range(SUBLANES):
                sh = _shift_up(dwin, b)
                for a8 in range(CONV_HALO // SUBLANES):
                    k = CONV_WIDTH - 1 - (SUBLANES * a8 + b)
                    if 0 <= k < CONV_WIDTH:
                        acc = acc + w8_ref[k, :, lanes][None] * _tiles3(sh[SUBLANES * a8:SUBLANES * a8 + TILE_ROWS])
            dglu_ref[pl.ds(r0, TILE_ROWS), lanes] = acc.reshape(TILE_ROWS, LANES)

            dc3 = _tiles3(dwin[:TILE_ROWS])
            gwin = gext_ref[pl.ds(r0, TILE_ROWS + CONV_HALO), lanes]
            for b in range(SUBLANES):
                sh = _shift_up(gwin, b)
                for a8 in range(CONV_HALO // SUBLANES + 1):
                    k = SUBLANES * a8 + b - lead
                    if 0 <= k < CONV_WIDTH:
                        part = jnp.sum(dc3 * _tiles3(sh[SUBLANES * a8:SUBLANES * a8 + TILE_ROWS]), axis=0)
                        wacc_ref[k, :, lanes] += part

        _for_tiles(tm, 0, d, tile)

        dglu = dglu_ref[...]
        av, gv = a_ref[...].astype(F32), g_ref[...].astype(F32)
        sg = _sigmoid(gv)
        da = dglu * sg
        dg = dglu * av * sg * (1.0 - sg)
        du_ref[:, 0:d] = da.astype(_LOWP)
        du_ref[:, d:2 * d] = dg.astype(_LOWP)
        db1_ref[:, 0:d] += _colsum(da)
        db1_ref[:, d:2 * d] += _colsum(dg)
        dh = jnp.zeros((tm, d), F32)
        for q in range(nq):
            dh = dh + _dot_nt(du_ref[:, q * hq:(q + 1) * hq], w1_ref[q])
        xh, r, n = _rms(x_ref[...], vec[0:1])
        sums_ref[2:3, :] += _colsum(dh)
        sums_ref[1:2, :] += _colsum(dh * n)
        dn = dh * (1.0 + vec[1:2])
        sums_ref[0:1, :] += _colsum(dn * xh)
        dx_ref[...] = dxo_ref[...] + _rms_bwd(dn, xh, r, vec[0:1])

        @pl.when(i == nsteps - 1)
        def _():
            dwdw_ref[...] = jnp.sum(wacc_ref[...], axis=1)

    return _call(
        body, dcw, dcw, glu, glu, a, g, x, dxo, vec, w8, w1, carry=carry, name="conv_a_bwd", grid=(nsteps,),
        in_specs=[_rows_spec(tm, d),
                  pl.BlockSpec((CONV_HALO, d), lambda i: (jnp.minimum((i + 1) * hb, last_halo), 0)),
                  _rows_spec(tm, d),
                  pl.BlockSpec((CONV_HALO, d), lambda i: (jnp.maximum(i * hb - 1, 0), 0)),
                  _rows_spec(tm, d), _rows_spec(tm, d), _rows_spec(tm, d), _rows_spec(tm, d),
                  _const_spec(vec.shape), _const_spec(w8.shape), _const_spec(w1.shape)],
        out_specs=[_rows_spec(tm, 2 * d), _rows_spec(tm, d), _const_spec((3, d)), _const_spec((1, 2 * d)),
                   _const_spec((CONV_WIDTH, d))],
        out_shape=[SDS((s, 2 * d), _LOWP), SDS((s, d), F32), SDS((3, d), F32), SDS((1, 2 * d), F32),
                   SDS((CONV_WIDTH, d), F32)],
        scratch_shapes=[pltpu.VMEM((tm + CONV_HALO, d), F32), pltpu.VMEM((tm + CONV_HALO, d), F32),
                        pltpu.VMEM((tm, d), F32), pltpu.VMEM((CONV_WIDTH, SUBLANES, d), F32)],
        sem=("arbitrary",))


def _adamw(w, g, m, v, name):
    r, c = w.shape
    rb = r
    for cand in (512, 352, 256, 184, 128, 64, 32, 16, 8):
        if r % cand == 0 and cand * c * 4 <= ADAM_BLOCK_BYTES:
            rb = cand
            break

    def body(w_ref, g_ref, m_ref, v_ref, d_ref, mo_ref, vo_ref):
        gv = g_ref[...]
        mn = ADAM_B1 * m_ref[...] + (1.0 - ADAM_B1) * gv
        vn = ADAM_B2 * v_ref[...] + (1.0 - ADAM_B2) * (gv * gv)
        m_hat = mn / (1.0 - ADAM_B1 ** ADAM_STEP)
        v_hat = vn / (1.0 - ADAM_B2 ** ADAM_STEP)
        d_ref[...] = -ADAM_LR * (m_hat / (jnp.sqrt(v_hat) + ADAM_EPS) + ADAM_WD * w_ref[...])
        mo_ref[...] = mn
        vo_ref[...] = vn

    spec = pl.BlockSpec((rb, c), lambda i: (i, 0))
    return pl.pallas_call(
        body, name=name, grid=(r // rb,), in_specs=[spec] * 4, out_specs=[spec] * 3,
        out_shape=[SDS((r, c), F32)] * 3, compiler_params=_cparams("parallel"),
    )(w, g, m, v)


def _adamw_nd(w, g, m, v, name):
    shape = w.shape
    two_d = (-1, shape[-1])
    outs = _adamw(w.reshape(two_d), g.reshape(two_d), m.reshape(two_d), v.reshape(two_d), name)
    return [o.reshape(shape) for o in outs]


def _adamw_from_slots(w, m, v, segs, pos, name, carry=None):
    r, c_ = w.shape
    rb = _divisor_block(r, (256, 176, 128, 64, 32, 16))
    for slots, _, first, rows in segs:
        half = slots.shape[1] // 2
        while first % rb or rows % rb or half % rb:
            rb //= 2
    assert rb % 16 == 0, (name, rb)
    plan, start = [], 0
    for slots, _, first, rows in segs:
        plan.append((start, rows // rb, first // rb, slots.shape[1] // 2 // rb))
        start += rows // rb
    assert start * rb == r, (name, start, rb, r)

    def body(pos_ref, w_ref, m_ref, v_ref, *rest):
        seg_refs, (g_ref, d_ref, mo_ref, vo_ref) = rest[:2 * len(segs)], rest[2 * len(segs):]
        i = pl.program_id(0)
        for k, (s0, nblk, b0, nbh) in enumerate(plan):
            @pl.when((i >= s0) & (i < s0 + nblk))
            def _(k=k, s0=s0, b0=b0, nbh=nbh):
                slots = seg_refs[2 * k][...].astype(F32)
                blk = b0 + i - s0
                in_my_half = (blk >= pos_ref[0] * nbh) & (blk < (pos_ref[0] + 1) * nbh)
                own = jnp.where(in_my_half, seg_refs[2 * k + 1][0].astype(F32), slots[3])
                gv = ((own + slots[0]) + slots[1]) + slots[2]
                mn = ADAM_B1 * m_ref[...] + (1.0 - ADAM_B1) * gv
                vn = ADAM_B2 * v_ref[...] + (1.0 - ADAM_B2) * (gv * gv)
                m_hat = mn / (1.0 - ADAM_B1 ** ADAM_STEP)
                v_hat = vn / (1.0 - ADAM_B2 ** ADAM_STEP)
                g_ref[...] = gv
                d_ref[...] = -ADAM_LR * (m_hat / (jnp.sqrt(v_hat) + ADAM_EPS) + ADAM_WD * w_ref[...])
                mo_ref[...] = mn
                vo_ref[...] = vn

    spec = pl.BlockSpec((rb, c_), lambda i, pos: (i, 0))
    seg_specs, seg_args = [], []
    for (slots, mine, _, _), (s0, nblk, b0, nbh) in zip(segs, plan):
        def in_seg(i, s0=s0, nblk=nblk):
            return jnp.clip(i - s0, 0, nblk - 1)
        seg_specs.append(pl.BlockSpec((N_CHIPS, rb, c_), lambda i, pos, f=in_seg, b0=b0: (0, b0 + f(i), 0)))
        seg_specs.append(pl.BlockSpec(
            (1, rb, c_), lambda i, pos, f=in_seg, b0=b0, nbh=nbh: (pos[1], jnp.clip(b0 + f(i) - pos[0] * nbh, 0, nbh - 1), 0)))
        seg_args += [slots, mine]
    return _call(body, w, m, v, *seg_args, carry=carry, prefetch=pos, name=name, grid=(r // rb,),
                 in_specs=[spec] * 3 + seg_specs, out_specs=[spec] * 4, out_shape=[SDS((r, c_), F32)] * 4,
                 sem=("arbitrary",))


def _rows_of(*vs):
    return jnp.concatenate([v.reshape(-1, v.shape[-1]) for v in vs], axis=0)


def kernel(x, c, ada_w, ada_b, norm_mix_g, norm_ffn_g, conv_w1, conv_b1, conv_wdw, conv_bdw, conv_ln_g, conv_ln_b, conv_w2, conv_b2, pool_w, pool_ls, ffn_w_gate, ffn_w_up, ffn_w_down, final_g, loss_target, m_ada_w, m_ada_b, m_norm_mix_g, m_norm_ffn_g, m_conv_w1, m_conv_b1, m_conv_wdw, m_conv_bdw, m_conv_ln_g, m_conv_ln_b, m_conv_w2, m_conv_b2, m_pool_w, m_pool_ls, m_ffn_w_gate, m_ffn_w_up, m_ffn_w_down, m_final_g, v_ada_w, v_ada_b, v_norm_mix_g, v_norm_ffn_g, v_conv_w1, v_conv_b1, v_conv_wdw, v_conv_bdw, v_conv_ln_g, v_conv_ln_b, v_conv_w2, v_conv_b2, v_pool_w, v_pool_ls, v_ffn_w_gate, v_ffn_w_up, v_ffn_w_down, v_final_g):
    s, d = x.shape[1], x.shape[2]
    depth = ada_w.shape[0]
    assert depth == 2 and conv_w1.shape[0] == 1 and pool_w.shape[0] == 1, "one conv layer then one pool layer"
    fq = ffn_w_gate.shape[2]
    n_ada = ada_b.shape[1] // d
    ada_cols = ada_w.shape[2]
    ng, gq, gd = pool_w.shape[1], pool_w.shape[2], pool_w.shape[3]
    cq = conv_wdw.shape[2]
    ax, ay, ac = _mesh_pos()
    myq = 2 * ax + ay
    me = 4 * ax + 2 * ay + ac
    x0 = x.reshape(s, d)
    tgt = loss_target.reshape(s, d)

    n_c = d // LANES
    small_in = jnp.concatenate(
        [c.reshape(n_c, LANES), conv_wdw[0].reshape(-1, LANES), pool_ls.reshape(-1, LANES)], axis=0)
    small_in_all = _all_gather8(small_in, "gather_c_and_small_shards", direct=True).reshape(N_DEV, -1, LANES)
    c_all = small_in_all[:, :n_c].reshape(N_DEV, d)
    ada_b_cols = lax.dynamic_slice_in_dim(ada_b, myq * ada_cols, ada_cols, axis=1).reshape(depth, 1, ada_cols)
    w1_shard, w2_shard = [conv_w1[0].astype(_LOWP)], [conv_w2[0].astype(_LOWP)]
    (mod_part,), w1_all = _ada_forward(c_all, ada_w, ada_b_cols, carry=(_Exchange("gather", w1_shard), w1_shard))
    mod_all = _all_gather8(mod_part.reshape(-1, LANES), "gather_mod", direct=True).reshape(N_DEV, depth, N_DEV, ada_cols)
    mod_mine = lax.dynamic_index_in_dim(mod_all[0::2], me, axis=2, keepdims=False)
    mod = jnp.transpose(mod_mine, (1, 0, 2)).reshape(depth, n_ada, d)

    def gathered(ex_out, shards):
        return [lax.dynamic_update_slice(g, sh[None], (myq, 0, 0)) for g, sh in zip(ex_out, shards)]

    pos = jnp.stack([ac, myq]).astype(jnp.int32)
    gate_t, up_t = jnp.swapaxes(ffn_w_gate, 1, 2), jnp.swapaxes(ffn_w_up, 1, 2)
    ffn0_shards = [gate_t[0].astype(_LOWP), up_t[0].astype(_LOWP), ffn_w_down[0].astype(_LOWP)]
    l1_shards = [pool_w[0].reshape(ng * gq, gd).astype(_LOWP), gate_t[1].astype(_LOWP), up_t[1].astype(_LOWP),
                 ffn_w_down[1].astype(_LOWP)]
    (w1,) = gathered(w1_all, w1_shard)
    wdw_full_shards = small_in_all[0::2, n_c:].reshape(N_CHIPS, CONV_WIDTH + 1, cq)
    wdw = jnp.transpose(wdw_full_shards[:, :CONV_WIDTH], (1, 0, 2)).reshape(CONV_WIDTH, d)
    pls = wdw_full_shards[:, CONV_WIDTH].reshape(1, d)
    w8 = jnp.broadcast_to(wdw[:, None, :], (CONV_WIDTH, SUBLANES, d))

    vec_ca = _rows_of(norm_mix_g[0], mod[0, 1], mod[0, 0])
    (h0, a0, g0, glu), w2_all = _conv_a_fwd(x0, vec_ca, w1, conv_b1, carry=(_Exchange("gather", w2_shard), w2_shard))
    w2 = gathered(w2_all, w2_shard)[0].reshape(d, d)
    vec_cb = _rows_of(conv_bdw, conv_ln_g, conv_ln_b, conv_b2, mod[0, 2])
    (cw, s0, y0, x1), ffn0_w = _conv_b_fwd(glu, w8, vec_cb, w2, x0, carry=(_Exchange("gather", ffn0_shards), ffn0_shards))
    wg0, wu0, wd0 = gathered(ffn0_w, ffn0_shards)
    vec_f0 = _rows_of(norm_ffn_g[0], mod[0, 4], mod[0, 3], mod[0, 5])
    (h2a, gta, upa, acta, y2a, x2), l1_w = _ffn_fwd(x1, vec_f0, wg0, wu0, wd0, "ffn0_fwd",
                                                   carry=(_Exchange("gather", l1_shards), l1_shards))
    pw_all, wg1, wu1, wd1 = gathered(l1_w, l1_shards)
    pw = jnp.transpose(pw_all.reshape(N_CHIPS, ng, gq, gd), (1, 0, 2, 3)).reshape(ng, N_CHIPS * gq, gd)
    vec_p = _rows_of(norm_mix_g[1], mod[1, 1], mod[1, 0], mod[1, 2], pls)
    mixed, yp, x3 = _pool_fwd(x2, vec_p, pw)
    vec_f1 = _rows_of(norm_ffn_g[1], mod[1, 4], mod[1, 3], mod[1, 5])
    (h2b, gtb, upb, actb, y2b, dx4, fin_sums), _ = _ffn_fwd(x3, vec_f1, wg1, wu1, wd1, "ffn1_fwd_loss",
                                                           head=(tgt, final_g.reshape(1, d)))

    def add_halves(partials, swapped, tag):
        return [_add_sibling_half(p, r, pos, f"grad_add_{tag}{k}") for k, (p, r) in enumerate(zip(partials, swapped))]

    def ffn_wgrads(h2, dgt, dup, act, dy, tag):
        gate_up = _wgrad(dgt, h2, _quarter_major(fq), _shared_rows(d), N_CHIPS, fq, d, s, "wgrad_gate" + tag, stack_blocks=2)
        gate_up = _wgrad(dup, h2, _quarter_major(fq), _shared_rows(d), N_CHIPS, fq, d, s, "wgrad_up" + tag,
                         stack=gate_up, stack_blocks=2, block=1)
        down = _wgrad(act, dy, _quarter_major(fq), _shared_rows(d), N_CHIPS, fq, d, s, "wgrad_down" + tag)
        return [gate_up, down]

    (dgtb, dupb, dyb, dx3, sums_f1), _ = _ffn_bwd(dx4, x3, vec_f1, gtb, upb, y2b, wg1, wu1, wd1, "ffn1_bwd")
    part_l1 = ffn_wgrads(h2b, dgtb, dupb, actb, dyb, "1")
    (dyp, dx2, sums_p), swapped_l1 = _pool_bwd(dx3, x2, yp, vec_p, pw, carry=(_SiblingSwap(part_l1), part_l1))
    cs_l1 = add_halves(part_l1, swapped_l1, "l1")
    dpw = _wgrad(mixed, dyp, _column_block(gd), _column_block(gd), ng, gd, gd, s, "wgrad_pool")
    dpw_q = jnp.transpose(dpw.reshape(ng, N_CHIPS, gq, gd), (1, 0, 2, 3)).reshape(N_CHIPS, ng * gq, gd)
    (dgta, dupa, dya, dx1, sums_f0), slots_l1 = _ffn_bwd(dx2, x1, vec_f0, gta, upa, y2a, wg0, wu0, wd0, "ffn0_bwd",
                                                        carry=(_Exchange("reduce", cs_l1), cs_l1))
    part_l0 = ffn_wgrads(h2a, dgta, dupa, acta, dya, "0") + [dpw_q]
    (dy0, dcw, sums_cb), swapped_l0 = _conv_b_bwd(dx1, cw, y0, vec_cb, w2, carry=(_SiblingSwap(part_l0), part_l0))
    dw2 = _wgrad(s0, dy0, _column_block(d // N_CHIPS), _shared_rows(d), N_CHIPS, d // N_CHIPS, d, s, "wgrad_conv2")
    cs_l0 = add_halves(part_l0, swapped_l0, "l0") + add_halves([dw2], _SiblingSwap([dw2]).alone([dw2], "grad_swap_w2"), "w2")
    (du, dx0, sums_ca, db1, dwdw), slots_l0 = _conv_a_bwd(dcw, glu, a0, g0, x0, dx1, vec_ca, w8, w1,
                                                          carry=(_Exchange("reduce", cs_l0), cs_l0))
    dw1 = _wgrad(h0, du, _shared_rows(d), _column_block(2 * d // N_CHIPS), N_CHIPS, d, 2 * d // N_CHIPS, s, "wgrad_conv1")
    cs_conv = add_halves([dw1], _SiblingSwap([dw1]).alone([dw1], "grad_swap_w1"), "w1")

    dmod = jnp.stack([
        _rows_of(sums_ca[2], sums_ca[1], sums_cb[0], sums_f0[2], sums_f0[1], sums_f0[3]),
        _rows_of(sums_p[2], sums_p[1], sums_p[3], sums_f1[2], sums_f1[1], sums_f1[3])])
    small = _rows_of(dmod.reshape(-1, d), sums_ca[0], sums_p[0], sums_f0[0], sums_f1[0], db1.reshape(2, d),
                     sums_cb[4], sums_cb[2], sums_cb[3], sums_cb[1], fin_sums[0], sums_p[4], dwdw, fin_sums[1])
    n_small = small.shape[0]
    pad = (-n_small) % SUBLANES
    small = jnp.concatenate([small, jnp.zeros((pad, d), F32)], axis=0) if pad else small
    small_all = _all_gather8(small.reshape(-1, LANES), "gather_small_grads").reshape(N_DEV, -1, LANES)
    small_sum = _sum_devices(small_all).reshape(-1, d)
    n_dm = depth * n_ada
    g_ada_b = small_sum[0:n_dm].reshape(depth, n_ada * d)
    g_norm_mix = small_sum[n_dm:n_dm + 2]
    g_norm_ffn = small_sum[n_dm + 2:n_dm + 4]
    g_b1 = small_sum[n_dm + 4:n_dm + 6].reshape(1, 2 * d)
    g_bdw, g_lng, g_lnb, g_b2 = (small_sum[n_dm + 6 + k].reshape(1, d) for k in range(4))
    g_final = small_sum[n_dm + 10]
    g_pls = lax.dynamic_slice_in_dim(small_sum[n_dm + 11].reshape(1, d), myq * cq, cq, axis=1)
    g_wdw = lax.dynamic_slice_in_dim(small_sum[n_dm + 12:n_dm + 12 + CONV_WIDTH], myq * cq, cq, axis=1).reshape(conv_wdw.shape)
    loss = small_sum[n_dm + 12 + CONV_WIDTH, 0]
    dmod_all = small_all.reshape(N_DEV, -1, d)[:, 0:n_dm].reshape(N_DEV, depth, n_ada * d)
    dmod_cols = lax.dynamic_slice_in_dim(jnp.transpose(dmod_all, (1, 0, 2)), myq * ada_cols, ada_cols, axis=2)
    g_ada_w = _ada_backward(c_all.T, dmod_cols)

    def small_pack(*vs):
        return jnp.concatenate([v.reshape(-1) for v in vs]).reshape(-1, LANES)

    names = ("ada_b", "norm_mix_g", "norm_ffn_g", "conv_b1", "conv_bdw", "conv_ln_g", "conv_ln_b", "conv_b2", "final_g")
    small_w = (ada_b, norm_mix_g, norm_ffn_g, conv_b1, conv_bdw, conv_ln_g, conv_ln_b, conv_b2, final_g)
    small_g = (g_ada_b, g_norm_mix, g_norm_ffn, g_b1, g_bdw, g_lng, g_lnb, g_b2, g_final)
    small_m = (m_ada_b, m_norm_mix_g, m_norm_ffn_g, m_conv_b1, m_conv_bdw, m_conv_ln_g, m_conv_ln_b, m_conv_b2, m_final_g)
    small_v = (v_ada_b, v_norm_mix_g, v_norm_ffn_g, v_conv_b1, v_conv_bdw, v_conv_ln_g, v_conv_ln_b, v_conv_b2, v_final_g)
    sd, sm, sv = _adamw(small_pack(*small_w), small_pack(*small_g), small_pack(*small_m), small_pack(*small_v), "adamw_small")

    def unpack(flat2d):
        flat = flat2d.reshape(-1)
        out, off = {}, 0
        for nm, wv in zip(names, small_w):
            out[nm] = flat[off:off + wv.size].reshape(wv.shape)
            off += wv.size
        return out

    sd, sm, sv = unpack(sd), unpack(sm), unpack(sv)
    grads = {
        "ada_w": g_ada_w, "ada_b": g_ada_b, "norm_mix_g": g_norm_mix, "norm_ffn_g": g_norm_ffn,
        "conv_b1": g_b1, "conv_wdw": g_wdw, "conv_bdw": g_bdw, "conv_ln_g": g_lng,
        "conv_ln_b": g_lnb, "conv_b2": g_b2, "pool_ls": g_pls, "final_g": g_final,
    }
    plain = {
        "ada_w": (ada_w, m_ada_w, v_ada_w), "conv_wdw": (conv_wdw, m_conv_wdw, v_conv_wdw),
        "pool_ls": (pool_ls, m_pool_ls, v_pool_ls),
    }
    from_slots = {
        "ffn_w_gate": (ffn_w_gate, m_ffn_w_gate, v_ffn_w_gate,
                       lambda: [(slots_l0[0], cs_l0[0], 0, fq), (slots_l1[0], cs_l1[0], 0, fq)]),
        "ffn_w_up": (ffn_w_up, m_ffn_w_up, v_ffn_w_up,
                     lambda: [(slots_l0[0], cs_l0[0], fq, fq), (slots_l1[0], cs_l1[0], fq, fq)]),
        "ffn_w_down": (ffn_w_down, m_ffn_w_down, v_ffn_w_down,
                       lambda: [(slots_l0[1], cs_l0[1], 0, fq), (slots_l1[1], cs_l1[1], 0, fq)]),
        "pool_w": (pool_w, m_pool_w, v_pool_w, lambda: [(slots_l0[2], cs_l0[2], 0, ng * gq)]),
        "conv_w2": (conv_w2, m_conv_w2, v_conv_w2, lambda: [(slots_l0[3], cs_l0[3], 0, d // N_CHIPS)]),
        "conv_w1": (conv_w1, m_conv_w1, v_conv_w1, lambda: [(slots_conv[0], cs_conv[0], 0, d)]),
    }
    order = ("ada_w", "ada_b", "norm_mix_g", "norm_ffn_g", "conv_w1", "conv_b1", "conv_wdw", "conv_bdw", "conv_ln_g",
             "conv_ln_b", "conv_w2", "conv_b2", "pool_w", "pool_ls", "ffn_w_gate", "ffn_w_up", "ffn_w_down", "final_g")
    delta, new_m, new_v = {}, {}, {}
    transposed = ("ffn_w_gate", "ffn_w_up")
    for nm, (wv, mv, vv, segs) in from_slots.items():
        if nm in transposed:
            wv, mv, vv = (jnp.swapaxes(t, 1, 2) for t in (wv, mv, vv))
        two_d = (-1, wv.shape[-1])
        carry = (_Exchange("reduce", cs_conv), cs_conv) if nm == "ffn_w_gate" else None
        outs, brought = _adamw_from_slots(wv.reshape(two_d), mv.reshape(two_d), vv.reshape(two_d), segs(), pos,
                                          "adamw_" + nm, carry=carry)
        if carry is not None:
            slots_conv = brought
        outs = [o.reshape(wv.shape) for o in outs]
        if nm in transposed:
            outs = [jnp.swapaxes(o, 1, 2) for o in outs]
        grads[nm], delta[nm], new_m[nm], new_v[nm] = outs
    for nm in order:
        if nm in from_slots:
            continue
        elif nm in plain:
            wv, mv, vv = plain[nm]
            grads[nm] = grads[nm].reshape(wv.shape)
            delta[nm], new_m[nm], new_v[nm] = _adamw_nd(wv, grads[nm], mv, vv, "adamw_" + nm)
        else:
            delta[nm], new_m[nm], new_v[nm] = sd[nm], sm[nm], sv[nm]
            grads[nm] = grads[nm].reshape(dict(zip(names, small_w))[nm].shape)

    return (loss, dx0.reshape(x.shape), *[grads[n] for n in order], *[delta[n] for n in order],
            *[new_m[n] for n in order], *[new_v[n] for n in order])
```

```python
import functools

import jax
import jax.numpy as jnp
from jax import lax
from jax.experimental import pallas as pl
from jax.experimental.pallas import tpu as pltpu

F32 = jnp.float32
_LOWP = jnp.bfloat16
EPS = 1e-6
CONV_WIDTH = 31
POOL_WINDOWS = (2, 4, 8, 16)
ADAM_LR = 0.001
ADAM_B1 = 0.9
ADAM_B2 = 0.999
ADAM_EPS = 1e-08
ADAM_WD = 0.01
ADAM_STEP = 10

N_CHIPS = 4
N_DEV = 8
LANES = 128
SUBLANES = 8
CONV_HALO = 32
POOL_HALO = 16
TILE_ROWS = 128
VMEM_LIMIT = 56 * 1024 * 1024
ADAM_BLOCK_BYTES = 1 << 20
WGRAD_ROWS = 4096
FFN_FWD_QUARTERS_PER_STEP = 4
FFN_BWD_QUARTERS_PER_STEP = 2
MESH = pl.DeviceIdType.MESH
SDS = jax.ShapeDtypeStruct


def _cparams(*sem):
    return pltpu.CompilerParams(dimension_semantics=sem if sem else None, vmem_limit_bytes=VMEM_LIMIT)


def _row_block(s):
    for tm in (512, 256, 128, 64, 32, 16):
        if s % tm == 0:
            return tm
    raise ValueError(f"sequence length {s} must be a multiple of 16")


def _rows_spec(tm, ncols):
    return pl.BlockSpec((tm, ncols), lambda i, *_: (i, 0))


def _const_spec(shape):
    nd = len(shape)
    return pl.BlockSpec(shape, lambda *_: (0,) * nd)


def _dot(a, b):
    return lax.dot_general(a.astype(_LOWP), b.astype(_LOWP), (((1,), (0,)), ((), ())), preferred_element_type=F32)


def _dot_nt(a, b):
    return lax.dot_general(a.astype(_LOWP), b.astype(_LOWP), (((1,), (1,)), ((), ())), preferred_element_type=F32)


def _dot_tn(a, b):
    return lax.dot_general(a.astype(_LOWP), b.astype(_LOWP), (((0,), (0,)), ((), ())), preferred_element_type=F32)


def _rms(x, gamma):
    r = lax.rsqrt(jnp.mean(x * x, axis=-1, keepdims=True) + EPS)
    xh = x * r
    return xh, r, xh * gamma


def _rms_bwd(dn, xh, r, gamma):
    dxh = dn * gamma
    return r * (dxh - xh * jnp.mean(dxh * xh, axis=-1, keepdims=True))


def _colsum(v):
    return jnp.sum(v, axis=0, keepdims=True)


def _sigmoid(v):
    return jax.nn.sigmoid(v)


def _dsilu(v):
    sg = _sigmoid(v)
    return sg * (1.0 + v * (1.0 - sg))


def _for_tiles(nrows, col0, ncols, fn):
    def step(t, carry):
        r0 = pl.multiple_of(t * TILE_ROWS, TILE_ROWS)
        for col in range(col0, col0 + ncols, LANES):
            fn(r0, col)
        return carry

    lax.fori_loop(0, nrows // TILE_ROWS, step, 0)


def _shift_up(win, b):
    return pltpu.roll(win, win.shape[0] - b, 0) if b else win


def _shift_down(win, b):
    return pltpu.roll(win, b, 0) if b else win


def _tiles3(v):
    return v.reshape(v.shape[0] // SUBLANES, SUBLANES, v.shape[1])


def _resident_scratch(arrays):
    return [pltpu.VMEM(a.shape, a.dtype) for a in arrays] + [pltpu.SemaphoreType.DMA((len(arrays),))]


def _load_resident(hbm_refs, vmem_refs, sems):
    copies = [pltpu.make_async_copy(src, dst, sems.at[k]) for k, (src, dst) in enumerate(zip(hbm_refs, vmem_refs))]
    for cp in copies:
        cp.start()
    for cp in copies:
        cp.wait()


def _mesh_pos():
    return lax.axis_index("x"), lax.axis_index("y"), lax.axis_index("c")


def _other_chips(x, y):
    return [(1 - x, y), (x, 1 - y), (1 - x, 1 - y)]


def _all_gather8(v, name):
    m, n = v.shape

    def body(x_ref, out_ref, send_sems, recv_sems, local_sem):
        x, y, c = _mesh_pos()
        me, sibling = (x, y, c), (x, y, 1 - c)
        chips = _other_chips(x, y)

        def rows(px, py, pc):
            return out_ref.at[pl.ds((4 * px + 2 * py + pc) * m, m), :]

        def copy(k, block, to, src=None):
            return pltpu.make_async_remote_copy(
                src_ref=rows(*block) if src is None else src, dst_ref=rows(*block),
                send_sem=send_sems.at[k], recv_sem=recv_sems.at[k], device_id=to, device_id_type=MESH)

        mine = pltpu.make_async_copy(x_ref, rows(*me), local_sem)
        mine.start()
        first = [copy(0, me, sibling, src=x_ref)]
        first += [copy(1 + j, me, (*chip, c), src=x_ref) for j, chip in enumerate(chips)]
        for cp in first:
            cp.start()
        passed = [copy(4 + j, (*chip, c), sibling) for j, chip in enumerate(chips)]
        for j, chip in enumerate(chips):
            copy(1 + j, (*chip, c), me).wait_recv()
            passed[j].start()
        copy(0, sibling, me).wait_recv()
        for j, chip in enumerate(chips):
            copy(4 + j, (*chip, 1 - c), me).wait_recv()
        for cp in first + passed:
            cp.wait_send()
        mine.wait()

    return pl.pallas_call(
        body, name=name,
        out_shape=SDS((N_DEV * m, n), v.dtype),
        in_specs=[pl.BlockSpec(memory_space=pltpu.VMEM)],
        out_specs=pl.BlockSpec(memory_space=pltpu.VMEM),
        scratch_shapes=[pltpu.SemaphoreType.DMA((7,)), pltpu.SemaphoreType.DMA((7,)), pltpu.SemaphoreType.DMA],
    )(v)


class _Exchange:
    N_SEMS = 7

    def __init__(self, kind, arrays):
        self.kind, self.n = kind, len(arrays)
        self.streams = 3
        if kind == "gather":
            self.out_shape = [SDS((N_CHIPS,) + a.shape, a.dtype) for a in arrays]
        else:
            self.out_shape = [SDS((N_CHIPS, 2 * a.shape[1], a.shape[2]), a.dtype) for a in arrays]
        self.scratch = [pltpu.SemaphoreType.DMA((self.n, self.N_SEMS)), pltpu.SemaphoreType.DMA((self.n, self.N_SEMS))]

    def _ctx(self, ins, outs, send_sems, recv_sems):
        x, y, c = _mesh_pos()
        myq = 2 * x + y
        sibling = (x, y, 1 - c)
        chips = _other_chips(x, y)
        gather = self.kind == "gather"
        ns = self.streams

        def rows(a, h):
            hr = outs[a].shape[1] // 2
            return pl.ds(h * hr, hr)

        def copy(a, ks, kr, src, dst, to):
            return pltpu.make_async_remote_copy(src_ref=src, dst_ref=dst, send_sem=send_sems.at[a, ks],
                                                recv_sem=recv_sems.at[a, kr], device_id=to, device_id_type=MESH)

        def place(a, t, h):
            if gather:
                px, py = chips[t]
                return outs[a].at[2 * px + py, rows(a, h)]
            return outs[a].at[t, rows(a, h)]

        def first(a, t):
            if gather:
                return copy(a, t, t, ins[a].at[rows(a, c)], outs[a].at[myq, rows(a, c)], (*chips[t], c))
            px, py = chips[t]
            return copy(a, t, t, ins[a].at[2 * px + py], outs[a].at[t, rows(a, c)], (px, py, c))

        def landed(a, t):
            return copy(a, t, t, place(a, t, c), place(a, t, c), sibling)

        def relay(a, t):
            return copy(a, ns + t, ns + t, place(a, t, c), place(a, t, c), sibling)

        def relayed(a, t):
            return copy(a, ns + t, ns + t, place(a, t, 1 - c), place(a, t, 1 - c), sibling)

        def own(a):
            return copy(a, 2 * ns, 2 * ns, ins[a].at[myq], outs[a].at[ns, rows(a, c)], sibling)

        return first, landed, relay, relayed, own

    def start(self, ins, outs, send_sems, recv_sems):
        first, _, _, _, own = self._ctx(ins, outs, send_sems, recv_sems)
        for a in range(self.n):
            for t in range(self.streams):
                first(a, t).start()
            if self.kind == "reduce":
                own(a).start()

    def relay(self, ins, outs, send_sems, recv_sems):
        _, landed, relay, _, _ = self._ctx(ins, outs, send_sems, recv_sems)
        for a in range(self.n):
            for t in range(self.streams):
                landed(a, t).wait_recv()
                relay(a, t).start()

    def finish(self, ins, outs, send_sems, recv_sems):
        first, _, relay, relayed, own = self._ctx(ins, outs, send_sems, recv_sems)
        for a in range(self.n):
            for t in range(self.streams):
                relayed(a, t).wait_recv()
            if self.kind == "reduce":
                own(a).wait_recv()
        for a in range(self.n):
            for t in range(self.streams):
                first(a, t).wait_send()
                relay(a, t).wait_send()
            if self.kind == "reduce":
                own(a).wait_send()

    def alone(self, arrays, name):
        n = self.n

        def body(*refs):
            args = (refs[:n], refs[n:2 * n], refs[2 * n], refs[2 * n + 1])
            self.start(*args)
            self.relay(*args)
            self.finish(*args)

        any_spec = pl.BlockSpec(memory_space=pl.ANY)
        return pl.pallas_call(body, name=name, out_shape=self.out_shape, in_specs=[any_spec] * n,
                              out_specs=[any_spec] * n, scratch_shapes=self.scratch)(*arrays)


RELAY_AT = (5, 8)
LATE_RELAY_AT = (3, 4)


def _call(body, *args, carry=None, prefetch=None, relay_at=RELAY_AT, name, grid, in_specs, out_specs, out_shape,
          scratch_shapes=(), sem):
    def run(fn, in_specs_, out_specs_, out_shape_, scratch_, sem_, operands):
        if prefetch is None:
            return pl.pallas_call(fn, name=name, grid=grid, in_specs=in_specs_, out_specs=out_specs_, out_shape=out_shape_,
                                  scratch_shapes=scratch_, compiler_params=_cparams(*sem_))(*operands)
        grid_spec = pltpu.PrefetchScalarGridSpec(num_scalar_prefetch=1, grid=grid, in_specs=in_specs_,
                                                 out_specs=out_specs_, scratch_shapes=scratch_)
        return pl.pallas_call(fn, name=name, grid_spec=grid_spec, out_shape=out_shape_,
                              compiler_params=_cparams(*sem_))(prefetch, *operands)

    if carry is None:
        return run(body, list(in_specs), list(out_specs), list(out_shape), list(scratch_shapes), sem, args), None
    ex, arrays = carry
    n_in, n_out, n_scr, n = len(in_specs), len(out_specs), len(scratch_shapes), ex.n
    n_pre = 0 if prefetch is None else 1
    total = 1
    for g in grid:
        total *= g
    relay_step = (total * relay_at[0]) // relay_at[1]

    def hosted(*refs):
        pre, refs = refs[:n_pre], refs[n_pre:]
        ins, cins = refs[:n_in], refs[n_in:n_in + n]
        outs, couts = refs[n_in + n:n_in + n + n_out], refs[n_in + n + n_out:n_in + 2 * n + n_out]
        scr = refs[n_in + 2 * n + n_out:n_in + 2 * n + n_out + n_scr]
        sems = refs[n_in + 2 * n + n_out + n_scr:]
        step = 0
        for ax, g in enumerate(grid):
            step = step * g + pl.program_id(ax)

        @pl.when(step == 0)
        def _():
            ex.start(cins, couts, *sems)

        body(*pre, *ins, *outs, *scr)

        @pl.when(step == relay_step)
        def _():
            ex.relay(cins, couts, *sems)

        @pl.when(step == total - 1)
        def _():
            ex.finish(cins, couts, *sems)

    any_spec = pl.BlockSpec(memory_space=pl.ANY)
    res = run(hosted, list(in_specs) + [any_spec] * n, list(out_specs) + [any_spec] * n, list(out_shape) + ex.out_shape,
              list(scratch_shapes) + ex.scratch, ("arbitrary",) * len(grid), (*args, *arrays))
    return res[:n_out], res[n_out:]


class _SiblingSwap:
    def __init__(self, arrays):
        self.n = len(arrays)
        self.out_shape = [SDS((a.shape[0], a.shape[1] // 2, a.shape[2]), a.dtype) for a in arrays]
        self.scratch = [pltpu.SemaphoreType.DMA((self.n,)), pltpu.SemaphoreType.DMA((self.n,))]

    def _copies(self, ins, outs, send_sems, recv_sems):
        x, y, c = _mesh_pos()
        cps = []
        for a in range(self.n):
            hr = ins[a].shape[1] // 2
            cps.append(pltpu.make_async_remote_copy(
                src_ref=ins[a].at[:, pl.ds((1 - c) * hr, hr)], dst_ref=outs[a],
                send_sem=send_sems.at[a], recv_sem=recv_sems.at[a], device_id=(x, y, 1 - c), device_id_type=MESH))
        return cps

    def start(self, *refs):
        for cp in self._copies(*refs):
            cp.start()

    def relay(self, *refs):
        pass

    def finish(self, *refs):
        for cp in self._copies(*refs):
            cp.wait()

    def alone(self, arrays, name):
        return _Exchange.alone(self, arrays, name)


def _divisor_block(rows, cands):
    for rb in cands:
        if rows % rb == 0:
            return rb
    raise ValueError(rows)


def _add_sibling_half(p, rb1, pos, name):
    nq, r, n = p.shape
    h = r // 2
    rb = _divisor_block(h, (512, 352, 256, 176, 128, 64, 32, 16))
    nb = h // rb

    def body(pos_ref, p_ref, r_ref, o_ref):
        o_ref[...] = (p_ref[...].astype(F32) + r_ref[...].astype(F32)).astype(o_ref.dtype)

    grid_spec = pltpu.PrefetchScalarGridSpec(
        num_scalar_prefetch=1, grid=(nq, nb),
        in_specs=[pl.BlockSpec((1, rb, n), lambda q, i, pos: (q, pos[0] * nb + i, 0)),
                  pl.BlockSpec((1, rb, n), lambda q, i, pos: (q, i, 0))],
        out_specs=pl.BlockSpec((1, rb, n), lambda q, i, pos: (q, i, 0)))
    return pl.pallas_call(
        body, name=name, grid_spec=grid_spec, out_shape=SDS((nq, h, n), p.dtype),
        compiler_params=_cparams("parallel", "parallel"),
    )(pos, p, rb1)


def _sum_devices(g):
    nd, m, n = g.shape

    def body(g_ref, o_ref):
        acc = g_ref[0]
        for d in range(1, nd):
            acc = acc + g_ref[d]
        o_ref[...] = acc

    return pl.pallas_call(body, name="small_grad_sum", out_shape=SDS((m, n), F32))(g)


def _ada_forward(c_all, ada_w, ada_b_cols, carry=None):
    nl, d, ncol = ada_w.shape
    nb = c_all.shape[0]
    tn = 512 if ncol % 512 == 0 else ncol

    def body(c_ref, w_ref, b_ref, o_ref):
        cv = c_ref[...]
        ca = cv * _sigmoid(cv)
        o_ref[0] = _dot(ca, w_ref[0]) + b_ref[0]

    return _call(
        body, c_all, ada_w, ada_b_cols, carry=carry, name="ada_forward", grid=(nl, ncol // tn),
        in_specs=[pl.BlockSpec((nb, d), lambda l, j: (0, 0)),
                  pl.BlockSpec((1, d, tn), lambda l, j: (l, 0, j)),
                  pl.BlockSpec((1, 1, tn), lambda l, j: (l, 0, j))],
        out_specs=[pl.BlockSpec((1, nb, tn), lambda l, j: (l, 0, j))],
        out_shape=[SDS((nl, nb, ncol), F32)], sem=("parallel", "parallel"))


def _ada_backward(c_all_t, dmod_cols):
    d, nb = c_all_t.shape
    nl, _, ncol = dmod_cols.shape
    tn = 512 if ncol % 512 == 0 else ncol

    def body(c_ref, g_ref, o_ref):
        cv = c_ref[...]
        ca = cv * _sigmoid(cv)
        o_ref[0] = _dot(ca, g_ref[0])

    return pl.pallas_call(
        body, name="ada_backward", grid=(nl, ncol // tn),
        in_specs=[pl.BlockSpec((d, nb), lambda l, j: (0, 0)),
                  pl.BlockSpec((1, nb, tn), lambda l, j: (l, 0, j))],
        out_specs=pl.BlockSpec((1, d, tn), lambda l, j: (l, 0, j)),
        out_shape=SDS((nl, d, ncol), F32), compiler_params=_cparams("parallel", "parallel"),
    )(c_all_t, dmod_cols)


def _conv_a_fwd(x, vec, w1, b1, carry=None):
    s, d = x.shape
    nq, _, hq = w1.shape
    tm = _row_block(s)

    def body(x_ref, vec_ref, w1_ref, b1_ref, h_ref, a_ref, g_ref, glu_ref):
        vec = vec_ref[...]
        _, _, n = _rms(x_ref[...], vec[0:1])
        hb = (n * (1.0 + vec[1:2]) + vec[2:3]).astype(_LOWP)
        h_ref[...] = hb
        u = [_dot(hb, w1_ref[q]) + b1_ref[:, q * hq:(q + 1) * hq] for q in range(nq)]
        for k in range(nq // 2):
            av, gv = u[k], u[nq // 2 + k]
            cols = slice(k * hq, (k + 1) * hq)
            a_ref[:, cols] = av.astype(_LOWP)
            g_ref[:, cols] = gv.astype(_LOWP)
            glu_ref[:, cols] = av * _sigmoid(gv)

    return _call(
        body, x, vec, w1, b1, carry=carry, name="conv_a_fwd", grid=(s // tm,),
        in_specs=[_rows_spec(tm, d), _const_spec(vec.shape), _const_spec(w1.shape), _const_spec(b1.shape)],
        out_specs=[_rows_spec(tm, d)] * 4,
        out_shape=[SDS((s, d), _LOWP)] * 3 + [SDS((s, d), F32)], sem=("parallel",))


def _conv_b_fwd(glu, w8, vec, w2, x, carry=None):
    s, d = x.shape
    tm = _row_block(s)
    hb = tm // CONV_HALO

    def body(glu_ref, halo_ref, w8_ref, vec_ref, w2_ref, x_ref, cw_ref, s_ref, y_ref, xo_ref, ext_ref):
        i = pl.program_id(0)
        vec = vec_ref[...]
        ext_ref[0:CONV_HALO, :] = jnp.where(i > 0, halo_ref[...], 0.0)
        ext_ref[CONV_HALO:, :] = glu_ref[...]

        def tile(r0, c0):
            lanes = slice(c0, c0 + LANES)
            win = ext_ref[pl.ds(r0, TILE_ROWS + CONV_HALO), lanes]
            acc = jnp.zeros((TILE_ROWS // SUBLANES, SUBLANES, LANES), F32)
            for b in range(SUBLANES):
                sh = _shift_up(win, b)
                for a in range(CONV_HALO // SUBLANES + 1):
                    k = SUBLANES * a + b - (CONV_HALO - CONV_WIDTH + 1)
                    if 0 <= k < CONV_WIDTH:
                        acc = acc + w8_ref[k, :, lanes][None] * _tiles3(sh[SUBLANES * a:SUBLANES * a + TILE_ROWS])
            cw_ref[pl.ds(r0, TILE_ROWS), lanes] = acc.reshape(TILE_ROWS, LANES)

        _for_tiles(tm, 0, d, tile)
        cw = cw_ref[...] + vec[0:1]
        cw_ref[...] = cw
        cc = cw - jnp.mean(cw, axis=-1, keepdims=True)
        ch = cc * lax.rsqrt(jnp.mean(cc * cc, axis=-1, keepdims=True) + EPS)
        lo = ch * vec[1:2] + vec[2:3]
        sv = (lo * _sigmoid(lo)).astype(_LOWP)
        s_ref[...] = sv
        y = _dot(sv, w2_ref[...]) + vec[3:4]
        y_ref[...] = y.astype(_LOWP)
        xo_ref[...] = x_ref[...] + (1.0 + vec[4:5]) * y

    return _call(
        body, glu, glu, w8, vec, w2, x, carry=carry, relay_at=LATE_RELAY_AT, name="conv_b_fwd", grid=(s // tm,),
        in_specs=[_rows_spec(tm, d),
                  pl.BlockSpec((CONV_HALO, d), lambda i: (jnp.maximum(i * hb - 1, 0), 0)),
                  _const_spec(w8.shape), _const_spec(vec.shape), _const_spec(w2.shape), _rows_spec(tm, d)],
        out_specs=[_rows_spec(tm, d)] * 4,
        out_shape=[SDS((s, d), F32), SDS((s, d), _LOWP), SDS((s, d), _LOWP), SDS((s, d), F32)],
        scratch_shapes=[pltpu.VMEM((tm + CONV_HALO, d), F32)], sem=("parallel",))


def _ffn_fwd(x, vec, wg, wu, wd, name, carry=None, head=None):
    s, d = x.shape
    nq, fq, _ = wg.shape
    tm = _row_block(s)
    nsteps = s // tm
    n_head = 0 if head is None else 2
    qps = FFN_FWD_QUARTERS_PER_STEP

    def body(x_ref, vec_ref, wg_hbm, wu_hbm, wd_hbm, *rest):
        head_refs, rest = rest[:n_head], rest[n_head:]
        h_ref, gt_ref, up_ref, act_ref, y_ref, xo_ref = rest[:6]
        sums_ref = rest[6] if head is not None else None
        hb_ref, yacc_ref, wg_ref, wu_ref, wd_ref, w_sems = rest[6 + n_head // 2:]
        i, p = pl.program_id(0), pl.program_id(1)
        vec = vec_ref[...]

        @pl.when((i == 0) & (p == 0))
        def _():
            _load_resident((wg_hbm, wu_hbm, wd_hbm), (wg_ref, wu_ref, wd_ref), w_sems)
            if head is not None:
                sums_ref[...] = jnp.zeros_like(sums_ref)

        @pl.when(p == 0)
        def _():
            _, _, n = _rms(x_ref[...], vec[0:1])
            hb = (n * (1.0 + vec[1:2]) + vec[2:3]).astype(_LOWP)
            hb_ref[...] = hb
            h_ref[...] = hb
            yacc_ref[...] = jnp.zeros_like(yacc_ref)

        hb = hb_ref[...]
        for k in range(qps):
            q = p * qps + k
            gt = _dot_nt(hb, wg_ref[q])
            up = _dot_nt(hb, wu_ref[q])
            act = (gt * _sigmoid(gt) * up).astype(_LOWP)
            gt_ref[k] = gt.astype(_LOWP)
            up_ref[k] = up.astype(_LOWP)
            act_ref[k] = act
            yacc_ref[...] += _dot(act, wd_ref[q])

        @pl.when(p == nq // qps - 1)
        def _():
            y = yacc_ref[...]
            y_ref[...] = y.astype(_LOWP)
            xo = x_ref[...] + (1.0 + vec[3:4]) * y
            if head is None:
                xo_ref[...] = xo
            else:
                t_ref, g_ref = head_refs
                gamma_v = g_ref[...]
                xh, r, out = _rms(xo, gamma_v)
                err = out - t_ref[...]
                dout = err * (1.0 / d)
                sums_ref[0:1, :] += _colsum(dout * xh)
                sums_ref[1:2, :] += _colsum(0.5 * err * dout)
                xo_ref[...] = _rms_bwd(dout, xh, r, gamma_v)

        if head is not None:
            @pl.when((i == nsteps - 1) & (p == nq // qps - 1))
            def _():
                sums_ref[1:2, :] = jnp.broadcast_to(jnp.sum(sums_ref[1:2, :], axis=-1, keepdims=True), (1, d))

    rows = pl.BlockSpec((tm, d), lambda i, q: (i, 0))
    hid = pl.BlockSpec((qps, tm, fq), lambda i, q: (q, i, 0))
    any_spec = pl.BlockSpec(memory_space=pl.ANY)
    head_args = [] if head is None else [head[0], head[1]]
    head_specs = [] if head is None else [rows, pl.BlockSpec((1, d), lambda i, q: (0, 0))]
    sums_specs = [] if head is None else [pl.BlockSpec((2, d), lambda i, q: (0, 0))]
    sums_shape = [] if head is None else [SDS((2, d), F32)]
    return _call(
        body, x, vec, wg, wu, wd, *head_args, carry=carry, name=name, grid=(nsteps, nq // qps),
        in_specs=[rows, pl.BlockSpec(vec.shape, lambda i, q: (0, 0)), any_spec, any_spec, any_spec] + head_specs,
        out_specs=[rows, hid, hid, hid, rows, rows] + sums_specs,
        out_shape=[SDS((s, d), _LOWP)] + [SDS((nq, s, fq), _LOWP)] * 3 + [SDS((s, d), _LOWP), SDS((s, d), F32)] + sums_shape,
        scratch_shapes=[pltpu.VMEM((tm, d), _LOWP), pltpu.VMEM((tm, d), F32)] + _resident_scratch((wg, wu, wd)),
        sem=("arbitrary", "arbitrary"))


def _pool_inv_count(t0, nrows, w):
    t = t0 + lax.broadcasted_iota(jnp.int32, (nrows, LANES), 0)
    return 1.0 / jnp.minimum(t + 1, w).astype(F32)


def _pool_fwd(x, vec, pw):
    s, d = x.shape
    ng, gd, _ = pw.shape
    tm = _row_block(s)
    hb = tm // POOL_HALO

    def body(x_ref, halo_ref, vec_ref, pw_ref, mixed_ref, yp_ref, xo_ref, hext_ref):
        i = pl.program_id(0)
        vec = vec_ref[...]

        def modulated(xv):
            _, _, n = _rms(xv, vec[0:1])
            return n * (1.0 + vec[1:2]) + vec[2:3]

        hext_ref[0:POOL_HALO, :] = jnp.where(i > 0, modulated(halo_ref[...]), 0.0)
        hext_ref[POOL_HALO:, :] = modulated(x_ref[...])

        for g in range(ng):
            def tile(r0, c0, g=g):
                lanes = slice(c0, c0 + LANES)
                win = hext_ref[pl.ds(r0, TILE_ROWS + POOL_HALO), lanes]
                acc = win
                for step in range(g + 1):
                    acc = acc + _shift_down(acc, 2 ** step)
                inv = _pool_inv_count(i * tm + r0, TILE_ROWS, POOL_WINDOWS[g])
                mixed = acc[POOL_HALO:] * inv - win[POOL_HALO:]
                mixed_ref[pl.ds(r0, TILE_ROWS), lanes] = mixed.astype(_LOWP)

            _for_tiles(tm, g * gd, gd, tile)

        for g in range(ng):
            cols = slice(g * gd, (g + 1) * gd)
            yp = _dot(mixed_ref[:, cols], pw_ref[g])
            yp_ref[:, cols] = yp.astype(_LOWP)
            xo_ref[:, cols] = x_ref[:, cols] + (1.0 + vec[3:4, cols]) * (yp * vec[4:5, cols])

    return pl.pallas_call(
        body, name="pool_fwd", grid=(s // tm,),
        in_specs=[_rows_spec(tm, d),
                  pl.BlockSpec((POOL_HALO, d), lambda i: (jnp.maximum(i * hb - 1, 0), 0)),
                  _const_spec(vec.shape), _const_spec(pw.shape)],
        out_specs=[_rows_spec(tm, d)] * 3,
        out_shape=[SDS((s, d), _LOWP), SDS((s, d), _LOWP), SDS((s, d), F32)],
        scratch_shapes=[pltpu.VMEM((tm + POOL_HALO, d), F32)],
        compiler_params=_cparams("parallel"),
    )(x, x, vec, pw)


def _ffn_bwd(dxo, x, vec, gt, up, y, wg, wu, wd, name, carry=None):
    s, d = x.shape
    nq, fq, _ = wg.shape
    tm = _row_block(s)
    qps = FFN_BWD_QUARTERS_PER_STEP

    def body(dxo_ref, x_ref, vec_ref, gt_ref, up_ref, y_ref, wg_hbm, wu_hbm, wd_hbm,
             dgt_ref, dup_ref, dy_ref, dx_ref, sums_ref, dyb_ref, dh_ref, wg_ref, wu_ref, wd_ref, w_sems):
        i, p = pl.program_id(0), pl.program_id(1)
        vec = vec_ref[...]

        @pl.when((i == 0) & (p == 0))
        def _():
            _load_resident((wg_hbm, wu_hbm, wd_hbm), (wg_ref, wu_ref, wd_ref), w_sems)
            sums_ref[...] = jnp.zeros_like(sums_ref)

        @pl.when(p == 0)
        def _():
            dxv = dxo_ref[...]
            dyb = (dxv * (1.0 + vec[3:4])).astype(_LOWP)
            dyb_ref[...] = dyb
            dy_ref[...] = dyb
            dh_ref[...] = jnp.zeros_like(dh_ref)
            sums_ref[3:4, :] += _colsum(dxv * y_ref[...].astype(F32))

        halves = [slice(k * (tm // 2), (k + 1) * (tm // 2)) for k in range(2)]
        for k in range(qps):
            q = p * qps + k
            dacts = [_dot_nt(dyb_ref[rs, :], wd_ref[q]) for rs in halves]
            dgus = []
            for rs, dact in zip(halves, dacts):
                gtv = gt_ref[k, rs, :].astype(F32)
                sg = _sigmoid(gtv)
                dgt = (dact * up_ref[k, rs, :].astype(F32) * (sg * (1.0 + gtv * (1.0 - sg)))).astype(_LOWP)
                dup = (dact * (gtv * sg)).astype(_LOWP)
                dgt_ref[k, rs, :] = dgt
                dup_ref[k, rs, :] = dup
                dgus.append((dgt, dup))
            for rs, (dgt, dup) in zip(halves, dgus):
                dh_ref[rs, :] += _dot(dgt, wg_ref[q]) + _dot(dup, wu_ref[q])

        @pl.when(p == nq // qps - 1)
        def _():
            dh = dh_ref[...]
            xh, r, n = _rms(x_ref[...], vec[0:1])
            sums_ref[2:3, :] += _colsum(dh)
            sums_ref[1:2, :] += _colsum(dh * n)
            dn = dh * (1.0 + vec[1:2])
            sums_ref[0:1, :] += _colsum(dn * xh)
            dx_ref[...] = dxo_ref[...] + _rms_bwd(dn, xh, r, vec[0:1])

    rows = pl.BlockSpec((tm, d), lambda i, q: (i, 0))
    hid = pl.BlockSpec((qps, tm, fq), lambda i, q: (q, i, 0))
    return _call(
        body, dxo, x, vec, gt, up, y, wg, wu, wd, carry=carry, name=name, grid=(s // tm, nq // qps),
        in_specs=[rows, rows, pl.BlockSpec(vec.shape, lambda i, q: (0, 0)), hid, hid, rows] +
                 [pl.BlockSpec(memory_space=pl.ANY)] * 3,
        out_specs=[hid, hid, rows, rows, pl.BlockSpec((4, d), lambda i, q: (0, 0))],
        out_shape=[SDS((nq, s, fq), _LOWP)] * 2 + [SDS((s, d), _LOWP), SDS((s, d), F32), SDS((4, d), F32)],
        scratch_shapes=[pltpu.VMEM((tm, d), _LOWP), pltpu.VMEM((tm, d), F32)] + _resident_scratch((wg, wu, wd)),
        sem=("arbitrary", "arbitrary"))


def _wgrad(a, b, a_spec, b_spec, nq, ka, nb, s, name, stack=None, stack_blocks=1, block=0):
    tk = min(WGRAD_ROWS, s)
    nk = s // tk

    def body(a_ref, b_ref, *rest):
        o_ref, acc_ref = rest[-2:]
        k = pl.program_id(1)

        @pl.when(k == 0)
        def _():
            acc_ref[...] = jnp.zeros_like(acc_ref)

        av = a_ref[...].reshape(tk, ka)
        bv = b_ref[...].reshape(tk, nb)
        acc_ref[...] += _dot_tn(av, bv)

        @pl.when(k == nk - 1)
        def _():
            o_ref[0] = acc_ref[...].astype(o_ref.dtype)

    extra = [] if stack is None else [stack]
    return pl.pallas_call(
        body, name=name, grid=(nq, nk),
        in_specs=[a_spec(tk), b_spec(tk)] + [pl.BlockSpec(memory_space=pl.ANY)] * len(extra),
        out_specs=pl.BlockSpec((1, ka, nb), lambda q, k: (q, block, 0)),
        out_shape=SDS((nq, stack_blocks * ka, nb), _LOWP),
        scratch_shapes=[pltpu.VMEM((ka, nb), F32)],
        input_output_aliases={2: 0} if extra else {},
        compiler_params=_cparams("parallel", "arbitrary"),
    )(a, b, *extra)


def _shared_rows(ncols):
    return lambda tk: pl.BlockSpec((tk, ncols), lambda q, k: (k, 0))


def _column_block(ncols):
    return lambda tk: pl.BlockSpec((tk, ncols), lambda q, k: (k, q))


def _quarter_major(ncols):
    return lambda tk: pl.BlockSpec((1, tk, ncols), lambda q, k: (q, k, 0))


def _pool_bwd(dxo, x, yp, vec, pw, carry=None):
    s, d = x.shape
    ng, gd, _ = pw.shape
    tm = _row_block(s)
    hb = tm // POOL_HALO
    nsteps = s // tm
    last_halo = s // POOL_HALO - 1

    def body(dxo_ref, halo_ref, x_ref, yp_ref, vec_ref, pw_ref, dyp_ref, dx_ref, sums_ref, eext_ref, dh_ref):
        i = pl.program_id(0)
        vec = vec_ref[...]

        @pl.when(i == 0)
        def _():
            sums_ref[...] = jnp.zeros_like(sums_ref)

        dxv = dxo_ref[...]
        ypv = yp_ref[...].astype(F32)
        gate1 = 1.0 + vec[3:4]
        sums_ref[3:4, :] += _colsum(dxv * (ypv * vec[4:5]))
        sums_ref[4:5, :] += _colsum(dxv * gate1 * ypv)
        dyp = (dxv * gate1 * vec[4:5]).astype(_LOWP)
        dyp_ref[...] = dyp
        dyp_halo = (jnp.where(i < nsteps - 1, halo_ref[...], 0.0) * gate1 * vec[4:5]).astype(_LOWP)
        for g in range(ng):
            cols = slice(g * gd, (g + 1) * gd)
            dm = _dot_nt(dyp[:, cols], pw_ref[g])
            dh_ref[:, cols] = dm
            inv = _pool_inv_count(i * tm, tm, POOL_WINDOWS[g])
            eext_ref[0:tm, cols] = dm * jnp.concatenate([inv] * (gd // LANES), axis=1)
            eext_ref[tm:, cols] = _dot_nt(dyp_halo[:, cols], pw_ref[g]) * (1.0 / POOL_WINDOWS[g])

        for g in range(ng):
            def tile(r0, c0, g=g):
                acc = eext_ref[pl.ds(r0, TILE_ROWS + POOL_HALO), c0:c0 + LANES]
                for step in range(g + 1):
                    acc = acc + _shift_up(acc, 2 ** step)
                here = (pl.ds(r0, TILE_ROWS), slice(c0, c0 + LANES))
                dh_ref[here] = acc[:TILE_ROWS] - dh_ref[here]

            _for_tiles(tm, g * gd, gd, tile)

        dh = dh_ref[...]
        xh, r, n = _rms(x_ref[...], vec[0:1])
        sums_ref[2:3, :] += _colsum(dh)
        sums_ref[1:2, :] += _colsum(dh * n)
        dn = dh * (1.0 + vec[1:2])
        sums_ref[0:1, :] += _colsum(dn * xh)
        dx_ref[...] = dxv + _rms_bwd(dn, xh, r, vec[0:1])

    return _call(
        body, dxo, dxo, x, yp, vec, pw, carry=carry, name="pool_bwd", grid=(nsteps,),
        in_specs=[_rows_spec(tm, d),
                  pl.BlockSpec((POOL_HALO, d), lambda i: (jnp.minimum((i + 1) * hb, last_halo), 0)),
                  _rows_spec(tm, d), _rows_spec(tm, d), _const_spec(vec.shape), _const_spec(pw.shape)],
        out_specs=[_rows_spec(tm, d), _rows_spec(tm, d), _const_spec((5, d))],
        out_shape=[SDS((s, d), _LOWP), SDS((s, d), F32), SDS((5, d), F32)],
        scratch_shapes=[pltpu.VMEM((tm + POOL_HALO, d), F32), pltpu.VMEM((tm, d), F32)], sem=("arbitrary",))


def _conv_b_bwd(dxo, cw, y, vec, w2, carry=None):
    s, d = dxo.shape
    tm = _row_block(s)

    def body(dxo_ref, cw_ref, y_ref, vec_ref, w2_ref, dy_ref, dcw_ref, sums_ref):
        i = pl.program_id(0)
        vec = vec_ref[...]

        @pl.when(i == 0)
        def _():
            sums_ref[...] = jnp.zeros_like(sums_ref)

        dxv = dxo_ref[...]
        dy = dxv * (1.0 + vec[4:5])
        sums_ref[0:1, :] += _colsum(dxv * y_ref[...].astype(F32))
        sums_ref[1:2, :] += _colsum(dy)
        dyb = dy.astype(_LOWP)
        dy_ref[...] = dyb
        ds = _dot_nt(dyb, w2_ref[...])
        cw = cw_ref[...]
        cc = cw - jnp.mean(cw, axis=-1, keepdims=True)
        rs = lax.rsqrt(jnp.mean(cc * cc, axis=-1, keepdims=True) + EPS)
        ch = cc * rs
        lo = ch * vec[1:2] + vec[2:3]
        dlo = ds * _dsilu(lo)
        sums_ref[2:3, :] += _colsum(dlo * ch)
        sums_ref[3:4, :] += _colsum(dlo)
        dch = dlo * vec[1:2]
        dcw = rs * (dch - jnp.mean(dch, axis=-1, keepdims=True) - ch * jnp.mean(dch * ch, axis=-1, keepdims=True))
        sums_ref[4:5, :] += _colsum(dcw)
        dcw_ref[...] = dcw

    return _call(
        body, dxo, cw, y, vec, w2, carry=carry, name="conv_b_bwd", grid=(s // tm,),
        in_specs=[_rows_spec(tm, d), _rows_spec(tm, d), _rows_spec(tm, d), _const_spec(vec.shape), _const_spec(w2.shape)],
        out_specs=[_rows_spec(tm, d), _rows_spec(tm, d), _const_spec((5, d))],
        out_shape=[SDS((s, d), _LOWP), SDS((s, d), F32), SDS((5, d), F32)], sem=("arbitrary",))


def _conv_a_bwd(dcw, glu, a, g, x, dxo, vec, w8, w1, carry=None):
    s, d = x.shape
    nq, _, hq = w1.shape
    tm = _row_block(s)
    hb = tm // CONV_HALO
    nsteps = s // tm
    last_halo = s // CONV_HALO - 1
    lead = CONV_HALO - CONV_WIDTH + 1

    def body(dcw_ref, dhalo_ref, glu_ref, ghalo_ref, a_ref, g_ref, x_ref, dxo_ref, vec_ref, w8_ref, w1_ref,
             du_ref, dx_ref, sums_ref, db1_ref, dwdw_ref, dext_ref, gext_ref, dglu_ref, wacc_ref):
        i = pl.program_id(0)
        vec = vec_ref[...]

        @pl.when(i == 0)
        def _():
            sums_ref[...] = jnp.zeros_like(sums_ref)
            db1_ref[...] = jnp.zeros_like(db1_ref)
            wacc_ref[...] = jnp.zeros_like(wacc_ref)

        dext_ref[0:tm, :] = dcw_ref[...]
        dext_ref[tm:, :] = jnp.where(i < nsteps - 1, dhalo_ref[...], 0.0)
        gext_ref[0:CONV_HALO, :] = jnp.where(i > 0, ghalo_ref[...], 0.0)
        gext_ref[CONV_HALO:, :] = glu_ref[...]

        def tile(r0, c0):
            lanes = slice(c0, c0 + LANES)
            dwin = dext_ref[pl.ds(r0, TILE_ROWS + CONV_HALO), lanes]
            acc = jnp.zeros((TILE_ROWS // SUBLANES, SUBLANES, LANES), F32)
            for b in range(SUBLANES):
                sh = _shift_up(dwin, b)
                for a8 in range(CONV_HALO // SUBLANES):
                    k = CONV_WIDTH - 1 - (SUBLANES * a8 + b)
                    if 0 <= k < CONV_WIDTH:
                        acc = acc + w8_ref[k, :, lanes][None] * _tiles3(sh[SUBLANES * a8:SUBLANES * a8 + TILE_ROWS])
            dglu_ref[pl.ds(r0, TILE_ROWS), lanes] = acc.reshape(TILE_ROWS, LANES)

            dc3 = _tiles3(dwin[:TILE_ROWS])
            gwin = gext_ref[pl.ds(r0, TILE_ROWS + CONV_HALO), lanes]
            for b in range(SUBLANES):
                sh = _shift_up(gwin, b)
                for a8 in range(CONV_HALO // SUBLANES + 1):
                    k = SUBLANES * a8 + b - lead
                    if 0 <= k < CONV_WIDTH:
                        part = jnp.sum(dc3 * _tiles3(sh[SUBLANES * a8:SUBLANES * a8 + TILE_ROWS]), axis=0)
                        wacc_ref[k, :, lanes] += part

        _for_tiles(tm, 0, d, tile)

        dglu = dglu_ref[...]
        av, gv = a_ref[...].astype(F32), g_ref[...].astype(F32)
        sg = _sigmoid(gv)
        da = dglu * sg
        dg = dglu * av * sg * (1.0 - sg)
        du_ref[:, 0:d] = da.astype(_LOWP)
        du_ref[:, d:2 * d] = dg.astype(_LOWP)
        db1_ref[:, 0:d] += _colsum(da)
        db1_ref[:, d:2 * d] += _colsum(dg)
        dh = jnp.zeros((tm, d), F32)
        for q in range(nq):
            dh = dh + _dot_nt(du_ref[:, q * hq:(q + 1) * hq], w1_ref[q])
        xh, r, n = _rms(x_ref[...], vec[0:1])
        sums_ref[2:3, :] += _colsum(dh)
        sums_ref[1:2, :] += _colsum(dh * n)
        dn = dh * (1.0 + vec[1:2])
        sums_ref[0:1, :] += _colsum(dn * xh)
        dx_ref[...] = dxo_ref[...] + _rms_bwd(dn, xh, r, vec[0:1])

        @pl.when(i == nsteps - 1)
        def _():
            dwdw_ref[...] = jnp.sum(wacc_ref[...], axis=1)

    return _call(
        body, dcw, dcw, glu, glu, a, g, x, dxo, vec, w8, w1, carry=carry, name="conv_a_bwd", grid=(nsteps,),
        in_specs=[_rows_spec(tm, d),
                  pl.BlockSpec((CONV_HALO, d), lambda i: (jnp.minimum((i + 1) * hb, last_halo), 0)),
                  _rows_spec(tm, d),
                  pl.BlockSpec((CONV_HALO, d), lambda i: (jnp.maximum(i * hb - 1, 0), 0)),
                  _rows_spec(tm, d), _rows_spec(tm, d), _rows_spec(tm, d), _rows_spec(tm, d),
                  _const_spec(vec.shape), _const_spec(w8.shape), _const_spec(w1.shape)],
        out_specs=[_rows_spec(tm, 2 * d), _rows_spec(tm, d), _const_spec((3, d)), _const_spec((1, 2 * d)),
                   _const_spec((CONV_WIDTH, d))],
        out_shape=[SDS((s, 2 * d), _LOWP), SDS((s, d), F32), SDS((3, d), F32), SDS((1, 2 * d), F32),
                   SDS((CONV_WIDTH, d), F32)],
        scratch_shapes=[pltpu.VMEM((tm + CONV_HALO, d), F32), pltpu.VMEM((tm + CONV_HALO, d), F32),
                        pltpu.VMEM((tm, d), F32), pltpu.VMEM((CONV_WIDTH, SUBLANES, d), F32)],
        sem=("arbitrary",))


def _adamw(w, g, m, v, name):
    r, c = w.shape
    rb = r
    for cand in (512, 352, 256, 184, 128, 64, 32, 16, 8):
        if r % cand == 0 and cand * c * 4 <= ADAM_BLOCK_BYTES:
            rb = cand
            break

    def body(w_ref, g_ref, m_ref, v_ref, d_ref, mo_ref, vo_ref):
        gv = g_ref[...]
        mn = ADAM_B1 * m_ref[...] + (1.0 - ADAM_B1) * gv
        vn = ADAM_B2 * v_ref[...] + (1.0 - ADAM_B2) * (gv * gv)
        m_hat = mn / (1.0 - ADAM_B1 ** ADAM_STEP)
        v_hat = vn / (1.0 - ADAM_B2 ** ADAM_STEP)
        d_ref[...] = -ADAM_LR * (m_hat / (jnp.sqrt(v_hat) + ADAM_EPS) + ADAM_WD * w_ref[...])
        mo_ref[...] = mn
        vo_ref[...] = vn

    spec = pl.BlockSpec((rb, c), lambda i: (i, 0))
    return pl.pallas_call(
        body, name=name, grid=(r // rb,), in_specs=[spec] * 4, out_specs=[spec] * 3,
        out_shape=[SDS((r, c), F32)] * 3, compiler_params=_cparams("parallel"),
    )(w, g, m, v)


def _adamw_nd(w, g, m, v, name):
    shape = w.shape
    two_d = (-1, shape[-1])
    outs = _adamw(w.reshape(two_d), g.reshape(two_d), m.reshape(two_d), v.reshape(two_d), name)
    return [o.reshape(shape) for o in outs]


def _adamw_from_slots(w, m, v, segs, pos, name, carry=None):
    r, c_ = w.shape
    rb = _divisor_block(r, (256, 176, 128, 64, 32, 16))
    for slots, _, first, rows in segs:
        half = slots.shape[1] // 2
        while first % rb or rows % rb or half % rb:
            rb //= 2
    assert rb % 16 == 0, (name, rb)
    plan, start = [], 0
    for slots, _, first, rows in segs:
        plan.append((start, rows // rb, first // rb, slots.shape[1] // 2 // rb))
        start += rows // rb
    assert start * rb == r, (name, start, rb, r)

    def body(pos_ref, w_ref, m_ref, v_ref, *rest):
        seg_refs, (g_ref, d_ref, mo_ref, vo_ref) = rest[:2 * len(segs)], rest[2 * len(segs):]
        i = pl.program_id(0)
        for k, (s0, nblk, b0, nbh) in enumerate(plan):
            @pl.when((i >= s0) & (i < s0 + nblk))
            def _(k=k, s0=s0, b0=b0, nbh=nbh):
                slots = seg_refs[2 * k][...].astype(F32)
                blk = b0 + i - s0
                in_my_half = (blk >= pos_ref[0] * nbh) & (blk < (pos_ref[0] + 1) * nbh)
                own = jnp.where(in_my_half, seg_refs[2 * k + 1][0].astype(F32), slots[3])
                gv = ((own + slots[0]) + slots[1]) + slots[2]
                mn = ADAM_B1 * m_ref[...] + (1.0 - ADAM_B1) * gv
                vn = ADAM_B2 * v_ref[...] + (1.0 - ADAM_B2) * (gv * gv)
                m_hat = mn / (1.0 - ADAM_B1 ** ADAM_STEP)
                v_hat = vn / (1.0 - ADAM_B2 ** ADAM_STEP)
                g_ref[...] = gv
                d_ref[...] = -ADAM_LR * (m_hat / (jnp.sqrt(v_hat) + ADAM_EPS) + ADAM_WD * w_ref[...])
                mo_ref[...] = mn
                vo_ref[...] = vn

    spec = pl.BlockSpec((rb, c_), lambda i, pos: (i, 0))
    seg_specs, seg_args = [], []
    for (slots, mine, _, _), (s0, nblk, b0, nbh) in zip(segs, plan):
        def in_seg(i, s0=s0, nblk=nblk):
            return jnp.clip(i - s0, 0, nblk - 1)
        seg_specs.append(pl.BlockSpec((N_CHIPS, rb, c_), lambda i, pos, f=in_seg, b0=b0: (0, b0 + f(i), 0)))
        seg_specs.append(pl.BlockSpec(
            (1, rb, c_), lambda i, pos, f=in_seg, b0=b0, nbh=nbh: (pos[1], jnp.clip(b0 + f(i) - pos[0] * nbh, 0, nbh - 1), 0)))
        seg_args += [slots, mine]
    return _call(body, w, m, v, *seg_args, carry=carry, prefetch=pos, name=name, grid=(r // rb,),
                 in_specs=[spec] * 3 + seg_specs, out_specs=[spec] * 4, out_shape=[SDS((r, c_), F32)] * 4,
                 sem=("arbitrary",))


def _rows_of(*vs):
    return jnp.concatenate([v.reshape(-1, v.shape[-1]) for v in vs], axis=0)


def kernel(x, c, ada_w, ada_b, norm_mix_g, norm_ffn_g, conv_w1, conv_b1, conv_wdw, conv_bdw, conv_ln_g, conv_ln_b, conv_w2, conv_b2, pool_w, pool_ls, ffn_w_gate, ffn_w_up, ffn_w_down, final_g, loss_target, m_ada_w, m_ada_b, m_norm_mix_g, m_norm_ffn_g, m_conv_w1, m_conv_b1, m_conv_wdw, m_conv_bdw, m_conv_ln_g, m_conv_ln_b, m_conv_w2, m_conv_b2, m_pool_w, m_pool_ls, m_ffn_w_gate, m_ffn_w_up, m_ffn_w_down, m_final_g, v_ada_w, v_ada_b, v_norm_mix_g, v_norm_ffn_g, v_conv_w1, v_conv_b1, v_conv_wdw, v_conv_bdw, v_conv_ln_g, v_conv_ln_b, v_conv_w2, v_conv_b2, v_pool_w, v_pool_ls, v_ffn_w_gate, v_ffn_w_up, v_ffn_w_down, v_final_g):
    s, d = x.shape[1], x.shape[2]
    depth = ada_w.shape[0]
    assert depth == 2 and conv_w1.shape[0] == 1 and pool_w.shape[0] == 1, "one conv layer then one pool layer"
    fq = ffn_w_gate.shape[2]
    n_ada = ada_b.shape[1] // d
    ada_cols = ada_w.shape[2]
    ng, gq, gd = pool_w.shape[1], pool_w.shape[2], pool_w.shape[3]
    cq = conv_wdw.shape[2]
    ax, ay, ac = _mesh_pos()
    myq = 2 * ax + ay
    me = 4 * ax + 2 * ay + ac
    x0 = x.reshape(s, d)
    tgt = loss_target.reshape(s, d)

    n_c = d // LANES
    small_in = jnp.concatenate(
        [c.reshape(n_c, LANES), conv_wdw[0].reshape(-1, LANES), pool_ls.reshape(-1, LANES)], axis=0)
    small_in_all = _all_gather8(small_in, "gather_c_and_small_shards").reshape(N_DEV, -1, LANES)
    c_all = small_in_all[:, :n_c].reshape(N_DEV, d)
    ada_b_cols = lax.dynamic_slice_in_dim(ada_b, myq * ada_cols, ada_cols, axis=1).reshape(depth, 1, ada_cols)
    w1_shard, w2_shard = [conv_w1[0].astype(_LOWP)], [conv_w2[0].astype(_LOWP)]
    (mod_part,), w1_all = _ada_forward(c_all, ada_w, ada_b_cols, carry=(_Exchange("gather", w1_shard), w1_shard))
    mod_all = _all_gather8(mod_part.reshape(-1, LANES), "gather_mod").reshape(N_DEV, depth, N_DEV, ada_cols)
    mod_mine = lax.dynamic_index_in_dim(mod_all[0::2], me, axis=2, keepdims=False)
    mod = jnp.transpose(mod_mine, (1, 0, 2)).reshape(depth, n_ada, d)

    def gathered(ex_out, shards):
        return [lax.dynamic_update_slice(g, sh[None], (myq, 0, 0)) for g, sh in zip(ex_out, shards)]

    pos = jnp.stack([ac, myq]).astype(jnp.int32)
    gate_t, up_t = jnp.swapaxes(ffn_w_gate, 1, 2), jnp.swapaxes(ffn_w_up, 1, 2)
    ffn0_shards = [gate_t[0].astype(_LOWP), up_t[0].astype(_LOWP), ffn_w_down[0].astype(_LOWP)]
    l1_shards = [pool_w[0].reshape(ng * gq, gd).astype(_LOWP), gate_t[1].astype(_LOWP), up_t[1].astype(_LOWP),
                 ffn_w_down[1].astype(_LOWP)]
    (w1,) = gathered(w1_all, w1_shard)
    wdw_full_shards = small_in_all[0::2, n_c:].reshape(N_CHIPS, CONV_WIDTH + 1, cq)
    wdw = jnp.transpose(wdw_full_shards[:, :CONV_WIDTH], (1, 0, 2)).reshape(CONV_WIDTH, d)
    pls = wdw_full_shards[:, CONV_WIDTH].reshape(1, d)
    w8 = jnp.broadcast_to(wdw[:, None, :], (CONV_WIDTH, SUBLANES, d))

    vec_ca = _rows_of(norm_mix_g[0], mod[0, 1], mod[0, 0])
    (h0, a0, g0, glu), w2_all = _conv_a_fwd(x0, vec_ca, w1, conv_b1, carry=(_Exchange("gather", w2_shard), w2_shard))
    w2 = gathered(w2_all, w2_shard)[0].reshape(d, d)
    vec_cb = _rows_of(conv_bdw, conv_ln_g, conv_ln_b, conv_b2, mod[0, 2])
    (cw, s0, y0, x1), ffn0_w = _conv_b_fwd(glu, w8, vec_cb, w2, x0, carry=(_Exchange("gather", ffn0_shards), ffn0_shards))
    wg0, wu0, wd0 = gathered(ffn0_w, ffn0_shards)
    vec_f0 = _rows_of(norm_ffn_g[0], mod[0, 4], mod[0, 3], mod[0, 5])
    (h2a, gta, upa, acta, y2a, x2), l1_w = _ffn_fwd(x1, vec_f0, wg0, wu0, wd0, "ffn0_fwd",
                                                   carry=(_Exchange("gather", l1_shards), l1_shards))
    pw_all, wg1, wu1, wd1 = gathered(l1_w, l1_shards)
    pw = jnp.transpose(pw_all.reshape(N_CHIPS, ng, gq, gd), (1, 0, 2, 3)).reshape(ng, N_CHIPS * gq, gd)
    vec_p = _rows_of(norm_mix_g[1], mod[1, 1], mod[1, 0], mod[1, 2], pls)
    mixed, yp, x3 = _pool_fwd(x2, vec_p, pw)
    vec_f1 = _rows_of(norm_ffn_g[1], mod[1, 4], mod[1, 3], mod[1, 5])
    (h2b, gtb, upb, actb, y2b, dx4, fin_sums), _ = _ffn_fwd(x3, vec_f1, wg1, wu1, wd1, "ffn1_fwd_loss",
                                                           head=(tgt, final_g.reshape(1, d)))

    def add_halves(partials, swapped, tag):
        return [_add_sibling_half(p, r, pos, f"grad_add_{tag}{k}") for k, (p, r) in enumerate(zip(partials, swapped))]

    def ffn_wgrads(h2, dgt, dup, act, dy, tag):
        gate_up = _wgrad(dgt, h2, _quarter_major(fq), _shared_rows(d), N_CHIPS, fq, d, s, "wgrad_gate" + tag, stack_blocks=2)
        gate_up = _wgrad(dup, h2, _quarter_major(fq), _shared_rows(d), N_CHIPS, fq, d, s, "wgrad_up" + tag,
                         stack=gate_up, stack_blocks=2, block=1)
        down = _wgrad(act, dy, _quarter_major(fq), _shared_rows(d), N_CHIPS, fq, d, s, "wgrad_down" + tag)
        return [gate_up, down]

    (dgtb, dupb, dyb, dx3, sums_f1), _ = _ffn_bwd(dx4, x3, vec_f1, gtb, upb, y2b, wg1, wu1, wd1, "ffn1_bwd")
    part_l1 = ffn_wgrads(h2b, dgtb, dupb, actb, dyb, "1")
    (dyp, dx2, sums_p), swapped_l1 = _pool_bwd(dx3, x2, yp, vec_p, pw, carry=(_SiblingSwap(part_l1), part_l1))
    cs_l1 = add_halves(part_l1, swapped_l1, "l1")
    dpw = _wgrad(mixed, dyp, _column_block(gd), _column_block(gd), ng, gd, gd, s, "wgrad_pool")
    dpw_q = jnp.transpose(dpw.reshape(ng, N_CHIPS, gq, gd), (1, 0, 2, 3)).reshape(N_CHIPS, ng * gq, gd)
    (dgta, dupa, dya, dx1, sums_f0), slots_l1 = _ffn_bwd(dx2, x1, vec_f0, gta, upa, y2a, wg0, wu0, wd0, "ffn0_bwd",
                                                        carry=(_Exchange("reduce", cs_l1), cs_l1))
    part_l0 = ffn_wgrads(h2a, dgta, dupa, acta, dya, "0") + [dpw_q]
    (dy0, dcw, sums_cb), swapped_l0 = _conv_b_bwd(dx1, cw, y0, vec_cb, w2, carry=(_SiblingSwap(part_l0), part_l0))
    dw2 = _wgrad(s0, dy0, _column_block(d // N_CHIPS), _shared_rows(d), N_CHIPS, d // N_CHIPS, d, s, "wgrad_conv2")
    cs_l0 = add_halves(part_l0, swapped_l0, "l0") + add_halves([dw2], _SiblingSwap([dw2]).alone([dw2], "grad_swap_w2"), "w2")
    (du, dx0, sums_ca, db1, dwdw), slots_l0 = _conv_a_bwd(dcw, glu, a0, g0, x0, dx1, vec_ca, w8, w1,
                                                          carry=(_Exchange("reduce", cs_l0), cs_l0))
    dw1 = _wgrad(h0, du, _shared_rows(d), _column_block(2 * d // N_CHIPS), N_CHIPS, d, 2 * d // N_CHIPS, s, "wgrad_conv1")
    cs_conv = add_halves([dw1], _SiblingSwap([dw1]).alone([dw1], "grad_swap_w1"), "w1")

    dmod = jnp.stack([
        _rows_of(sums_ca[2], sums_ca[1], sums_cb[0], sums_f0[2], sums_f0[1], sums_f0[3]),
        _rows_of(sums_p[2], sums_p[1], sums_p[3], sums_f1[2], sums_f1[1], sums_f1[3])])
    small = _rows_of(dmod.reshape(-1, d), sums_ca[0], sums_p[0], sums_f0[0], sums_f1[0], db1.reshape(2, d),
                     sums_cb[4], sums_cb[2], sums_cb[3], sums_cb[1], fin_sums[0], sums_p[4], dwdw, fin_sums[1])
    n_small = small.shape[0]
    pad = (-n_small) % SUBLANES
    small = jnp.concatenate([small, jnp.zeros((pad, d), F32)], axis=0) if pad else small
    small_all = _all_gather8(small.reshape(-1, LANES), "gather_small_grads").reshape(N_DEV, -1, LANES)
    small_sum = _sum_devices(small_all).reshape(-1, d)
    n_dm = depth * n_ada
    g_ada_b = small_sum[0:n_dm].reshape(depth, n_ada * d)
    g_norm_mix = small_sum[n_dm:n_dm + 2]
    g_norm_ffn = small_sum[n_dm + 2:n_dm + 4]
    g_b1 = small_sum[n_dm + 4:n_dm + 6].reshape(1, 2 * d)
    g_bdw, g_lng, g_lnb, g_b2 = (small_sum[n_dm + 6 + k].reshape(1, d) for k in range(4))
    g_final = small_sum[n_dm + 10]
    g_pls = lax.dynamic_slice_in_dim(small_sum[n_dm + 11].reshape(1, d), myq * cq, cq, axis=1)
    g_wdw = lax.dynamic_slice_in_dim(small_sum[n_dm + 12:n_dm + 12 + CONV_WIDTH], myq * cq, cq, axis=1).reshape(conv_wdw.shape)
    loss = small_sum[n_dm + 12 + CONV_WIDTH, 0]
    dmod_all = small_all.reshape(N_DEV, -1, d)[:, 0:n_dm].reshape(N_DEV, depth, n_ada * d)
    dmod_cols = lax.dynamic_slice_in_dim(jnp.transpose(dmod_all, (1, 0, 2)), myq * ada_cols, ada_cols, axis=2)
    g_ada_w = _ada_backward(c_all.T, dmod_cols)

    def small_pack(*vs):
        return jnp.concatenate([v.reshape(-1) for v in vs]).reshape(-1, LANES)

    names = ("ada_b", "norm_mix_g", "norm_ffn_g", "conv_b1", "conv_bdw", "conv_ln_g", "conv_ln_b", "conv_b2", "final_g")
    small_w = (ada_b, norm_mix_g, norm_ffn_g, conv_b1, conv_bdw, conv_ln_g, conv_ln_b, conv_b2, final_g)
    small_g = (g_ada_b, g_norm_mix, g_norm_ffn, g_b1, g_bdw, g_lng, g_lnb, g_b2, g_final)
    small_m = (m_ada_b, m_norm_mix_g, m_norm_ffn_g, m_conv_b1, m_conv_bdw, m_conv_ln_g, m_conv_ln_b, m_conv_b2, m_final_g)
    small_v = (v_ada_b, v_norm_mix_g, v_norm_ffn_g, v_conv_b1, v_conv_bdw, v_conv_ln_g, v_conv_ln_b, v_conv_b2, v_final_g)
    sd, sm, sv = _adamw(small_pack(*small_w), small_pack(*small_g), small_pack(*small_m), small_pack(*small_v), "adamw_small")

    def unpack(flat2d):
        flat = flat2d.reshape(-1)
        out, off = {}, 0
        for nm, wv in zip(names, small_w):
            out[nm] = flat[off:off + wv.size].reshape(wv.shape)
            off += wv.size
        return out

    sd, sm, sv = unpack(sd), unpack(sm), unpack(sv)
    grads = {
        "ada_w": g_ada_w, "ada_b": g_ada_b, "norm_mix_g": g_norm_mix, "norm_ffn_g": g_norm_ffn,
        "conv_b1": g_b1, "conv_wdw": g_wdw, "conv_bdw": g_bdw, "conv_ln_g": g_lng,
        "conv_ln_b": g_lnb, "conv_b2": g_b2, "pool_ls": g_pls, "final_g": g_final,
    }
    plain = {
        "ada_w": (ada_w, m_ada_w, v_ada_w), "conv_wdw": (conv_wdw, m_conv_wdw, v_conv_wdw),
        "pool_ls": (pool_ls, m_pool_ls, v_pool_ls),
    }
    from_slots = {
        "ffn_w_gate": (ffn_w_gate, m_ffn_w_gate, v_ffn_w_gate,
                       lambda: [(slots_l0[0], cs_l0[0], 0, fq), (slots_l1[0], cs_l1[0], 0, fq)]),
        "ffn_w_up": (ffn_w_up, m_ffn_w_up, v_ffn_w_up,
                     lambda: [(slots_l0[0], cs_l0[0], fq, fq), (slots_l1[0], cs_l1[0], fq, fq)]),
        "ffn_w_down": (ffn_w_down, m_ffn_w_down, v_ffn_w_down,
                       lambda: [(slots_l0[1], cs_l0[1], 0, fq), (slots_l1[1], cs_l1[1], 0, fq)]),
        "pool_w": (pool_w, m_pool_w, v_pool_w, lambda: [(slots_l0[2], cs_l0[2], 0, ng * gq)]),
        "conv_w2": (conv_w2, m_conv_w2, v_conv_w2, lambda: [(slots_l0[3], cs_l0[3], 0, d // N_CHIPS)]),
        "conv_w1": (conv_w1, m_conv_w1, v_conv_w1, lambda: [(slots_conv[0], cs_conv[0], 0, d)]),
    }
    order = ("ada_w", "ada_b", "norm_mix_g", "norm_ffn_g", "conv_w1", "conv_b1", "conv_wdw", "conv_bdw", "conv_ln_g",
             "conv_ln_b", "conv_w2", "conv_b2", "pool_w", "pool_ls", "ffn_w_gate", "ffn_w_up", "ffn_w_down", "final_g")
    delta, new_m, new_v = {}, {}, {}
    transposed = ("ffn_w_gate", "ffn_w_up")
    for nm, (wv, mv, vv, segs) in from_slots.items():
        if nm in transposed:
            wv, mv, vv = (jnp.swapaxes(t, 1, 2) for t in (wv, mv, vv))
        two_d = (-1, wv.shape[-1])
        carry = (_Exchange("reduce", cs_conv), cs_conv) if nm == "ffn_w_gate" else None
        outs, brought = _adamw_from_slots(wv.reshape(two_d), mv.reshape(two_d), vv.reshape(two_d), segs(), pos,
                                          "adamw_" + nm, carry=carry)
        if carry is not None:
            slots_conv = brought
        outs = [o.reshape(wv.shape) for o in outs]
        if nm in transposed:
            outs = [jnp.swapaxes(o, 1, 2) for o in outs]
        grads[nm], delta[nm], new_m[nm], new_v[nm] = outs
    for nm in order:
        if nm in from_slots:
            continue
        elif nm in plain:
            wv, mv, vv = plain[nm]
            grads[nm] = grads[nm].reshape(wv.shape)
            delta[nm], new_m[nm], new_v[nm] = _adamw_nd(wv, grads[nm], mv, vv, "adamw_" + nm)
        else:
            delta[nm], new_m[nm], new_v[nm] = sd[nm], sm[nm], sv[nm]
            grads[nm] = grads[nm].reshape(dict(zip(names, small_w))[nm].shape)

    return (loss, dx0.reshape(x.shape), *[grads[n] for n in order], *[delta[n] for n in order],
            *[new_m[n] for n in order], *[new_v[n] for n in order])
```

```python
import functools

import jax
import jax.numpy as jnp
from jax import lax
from jax.experimental import pallas as pl
from jax.experimental.pallas import tpu as pltpu

F32 = jnp.float32
_LOWP = jnp.bfloat16
EPS = 1e-6
CONV_WIDTH = 31
POOL_WINDOWS = (2, 4, 8, 16)
ADAM_LR = 0.001
ADAM_B1 = 0.9
ADAM_B2 = 0.999
ADAM_EPS = 1e-08
ADAM_WD = 0.01
ADAM_STEP = 10

N_CHIPS = 4
N_DEV = 8
LANES = 128
SUBLANES = 8
CONV_HALO = 32
POOL_HALO = 16
TILE_ROWS = 128
VMEM_LIMIT = 56 * 1024 * 1024
ADAM_BLOCK_BYTES = 1 << 20
WGRAD_ROWS = 2048
WGRAD_RING = 4
FFN_FWD_QUARTERS_PER_STEP = 4
FFN_BWD_QUARTERS_PER_STEP = 2
MESH = pl.DeviceIdType.MESH
SDS = jax.ShapeDtypeStruct


def _cparams(*sem):
    return pltpu.CompilerParams(dimension_semantics=sem if sem else None, vmem_limit_bytes=VMEM_LIMIT)


def _row_block(s):
    for tm in (512, 256, 128, 64, 32, 16):
        if s % tm == 0:
            return tm
    raise ValueError(f"sequence length {s} must be a multiple of 16")


def _rows_spec(tm, ncols):
    return pl.BlockSpec((tm, ncols), lambda i, *_: (i, 0))


def _const_spec(shape):
    nd = len(shape)
    return pl.BlockSpec(shape, lambda *_: (0,) * nd)


def _dot(a, b):
    return lax.dot_general(a.astype(_LOWP), b.astype(_LOWP), (((1,), (0,)), ((), ())), preferred_element_type=F32)


def _dot_nt(a, b):
    return lax.dot_general(a.astype(_LOWP), b.astype(_LOWP), (((1,), (1,)), ((), ())), preferred_element_type=F32)


def _dot_tn(a, b):
    return lax.dot_general(a.astype(_LOWP), b.astype(_LOWP), (((0,), (0,)), ((), ())), preferred_element_type=F32)


def _rms(x, gamma):
    r = lax.rsqrt(jnp.mean(x * x, axis=-1, keepdims=True) + EPS)
    xh = x * r
    return xh, r, xh * gamma


def _rms_bwd(dn, xh, r, gamma):
    dxh = dn * gamma
    return r * (dxh - xh * jnp.mean(dxh * xh, axis=-1, keepdims=True))


def _colsum(v):
    return jnp.sum(v, axis=0, keepdims=True)


def _sigmoid(v):
    return jax.nn.sigmoid(v)


def _dsilu(v):
    sg = _sigmoid(v)
    return sg * (1.0 + v * (1.0 - sg))


def _for_tiles(nrows, col0, ncols, fn):
    def step(t, carry):
        r0 = pl.multiple_of(t * TILE_ROWS, TILE_ROWS)
        for col in range(col0, col0 + ncols, LANES):
            fn(r0, col)
        return carry

    lax.fori_loop(0, nrows // TILE_ROWS, step, 0)


def _shift_up(win, b):
    return pltpu.roll(win, win.shape[0] - b, 0) if b else win


def _shift_down(win, b):
    return pltpu.roll(win, b, 0) if b else win


def _tiles3(v):
    return v.reshape(v.shape[0] // SUBLANES, SUBLANES, v.shape[1])


def _resident_scratch(arrays):
    return [pltpu.VMEM(a.shape, a.dtype) for a in arrays] + [pltpu.SemaphoreType.DMA((len(arrays),))]


def _load_resident(hbm_refs, vmem_refs, sems):
    copies = [pltpu.make_async_copy(src, dst, sems.at[k]) for k, (src, dst) in enumerate(zip(hbm_refs, vmem_refs))]
    for cp in copies:
        cp.start()
    for cp in copies:
        cp.wait()


def _mesh_pos():
    return lax.axis_index("x"), lax.axis_index("y"), lax.axis_index("c")


def _other_chips(x, y):
    return [(1 - x, y), (x, 1 - y), (1 - x, 1 - y)]


def _all_gather8(v, name):
    m, n = v.shape

    def body(x_ref, out_ref, send_sems, recv_sems, local_sem):
        x, y, c = _mesh_pos()
        me, sibling = (x, y, c), (x, y, 1 - c)
        chips = _other_chips(x, y)

        def rows(px, py, pc):
            return out_ref.at[pl.ds((4 * px + 2 * py + pc) * m, m), :]

        def copy(k, block, to, src=None):
            return pltpu.make_async_remote_copy(
                src_ref=rows(*block) if src is None else src, dst_ref=rows(*block),
                send_sem=send_sems.at[k], recv_sem=recv_sems.at[k], device_id=to, device_id_type=MESH)

        mine = pltpu.make_async_copy(x_ref, rows(*me), local_sem)
        mine.start()
        first = [copy(0, me, sibling, src=x_ref)]
        first += [copy(1 + j, me, (*chip, c), src=x_ref) for j, chip in enumerate(chips)]
        for cp in first:
            cp.start()
        passed = [copy(4 + j, (*chip, c), sibling) for j, chip in enumerate(chips)]
        for j, chip in enumerate(chips):
            copy(1 + j, (*chip, c), me).wait_recv()
            passed[j].start()
        copy(0, sibling, me).wait_recv()
        for j, chip in enumerate(chips):
            copy(4 + j, (*chip, 1 - c), me).wait_recv()
        for cp in first + passed:
            cp.wait_send()
        mine.wait()

    return pl.pallas_call(
        body, name=name,
        out_shape=SDS((N_DEV * m, n), v.dtype),
        in_specs=[pl.BlockSpec(memory_space=pltpu.VMEM)],
        out_specs=pl.BlockSpec(memory_space=pltpu.VMEM),
        scratch_shapes=[pltpu.SemaphoreType.DMA((7,)), pltpu.SemaphoreType.DMA((7,)), pltpu.SemaphoreType.DMA],
    )(v)


class _Exchange:
    N_SEMS = 7

    def __init__(self, kind, arrays):
        self.kind, self.n = kind, len(arrays)
        self.streams = 3
        if kind == "gather":
            self.out_shape = [SDS((N_CHIPS,) + a.shape, a.dtype) for a in arrays]
        else:
            self.out_shape = [SDS((N_CHIPS, 2 * a.shape[1], a.shape[2]), a.dtype) for a in arrays]
        self.scratch = [pltpu.SemaphoreType.DMA((self.n, self.N_SEMS)), pltpu.SemaphoreType.DMA((self.n, self.N_SEMS))]

    def _ctx(self, ins, outs, send_sems, recv_sems):
        x, y, c = _mesh_pos()
        myq = 2 * x + y
        sibling = (x, y, 1 - c)
        chips = _other_chips(x, y)
        gather = self.kind == "gather"
        ns = self.streams

        def rows(a, h):
            hr = outs[a].shape[1] // 2
            return pl.ds(h * hr, hr)

        def copy(a, ks, kr, src, dst, to):
            return pltpu.make_async_remote_copy(src_ref=src, dst_ref=dst, send_sem=send_sems.at[a, ks],
                                                recv_sem=recv_sems.at[a, kr], device_id=to, device_id_type=MESH)

        def place(a, t, h):
            if gather:
                px, py = chips[t]
                return outs[a].at[2 * px + py, rows(a, h)]
            return outs[a].at[t, rows(a, h)]

        def first(a, t):
            if gather:
                return copy(a, t, t, ins[a].at[rows(a, c)], outs[a].at[myq, rows(a, c)], (*chips[t], c))
            px, py = chips[t]
            return copy(a, t, t, ins[a].at[2 * px + py], outs[a].at[t, rows(a, c)], (px, py, c))

        def landed(a, t):
            return copy(a, t, t, place(a, t, c), place(a, t, c), sibling)

        def relay(a, t):
            return copy(a, ns + t, ns + t, place(a, t, c), place(a, t, c), sibling)

        def relayed(a, t):
            return copy(a, ns + t, ns + t, place(a, t, 1 - c), place(a, t, 1 - c), sibling)

        def own(a):
            return copy(a, 2 * ns, 2 * ns, ins[a].at[myq], outs[a].at[ns, rows(a, c)], sibling)

        return first, landed, relay, relayed, own

    def start(self, ins, outs, send_sems, recv_sems):
        first, _, _, _, own = self._ctx(ins, outs, send_sems, recv_sems)
        for a in range(self.n):
            for t in range(self.streams):
                first(a, t).start()
            if self.kind == "reduce":
                own(a).start()

    def relay(self, ins, outs, send_sems, recv_sems):
        _, landed, relay, _, _ = self._ctx(ins, outs, send_sems, recv_sems)
        for a in range(self.n):
            for t in range(self.streams):
                landed(a, t).wait_recv()
                relay(a, t).start()

    def finish(self, ins, outs, send_sems, recv_sems):
        first, _, relay, relayed, own = self._ctx(ins, outs, send_sems, recv_sems)
        for a in range(self.n):
            for t in range(self.streams):
                relayed(a, t).wait_recv()
            if self.kind == "reduce":
                own(a).wait_recv()
        for a in range(self.n):
            for t in range(self.streams):
                first(a, t).wait_send()
                relay(a, t).wait_send()
            if self.kind == "reduce":
                own(a).wait_send()

    def alone(self, arrays, name):
        n = self.n

        def body(*refs):
            args = (refs[:n], refs[n:2 * n], refs[2 * n], refs[2 * n + 1])
            self.start(*args)
            self.relay(*args)
            self.finish(*args)

        any_spec = pl.BlockSpec(memory_space=pl.ANY)
        return pl.pallas_call(body, name=name, out_shape=self.out_shape, in_specs=[any_spec] * n,
                              out_specs=[any_spec] * n, scratch_shapes=self.scratch)(*arrays)


RELAY_AT = (5, 8)
LATE_RELAY_AT = (3, 4)


def _call(body, *args, carry=None, prefetch=None, relay_at=RELAY_AT, name, grid, in_specs, out_specs, out_shape,
          scratch_shapes=(), sem):
    def run(fn, in_specs_, out_specs_, out_shape_, scratch_, sem_, operands):
        if prefetch is None:
            return pl.pallas_call(fn, name=name, grid=grid, in_specs=in_specs_, out_specs=out_specs_, out_shape=out_shape_,
                                  scratch_shapes=scratch_, compiler_params=_cparams(*sem_))(*operands)
        grid_spec = pltpu.PrefetchScalarGridSpec(num_scalar_prefetch=1, grid=grid, in_specs=in_specs_,
                                                 out_specs=out_specs_, scratch_shapes=scratch_)
        return pl.pallas_call(fn, name=name, grid_spec=grid_spec, out_shape=out_shape_,
                              compiler_params=_cparams(*sem_))(prefetch, *operands)

    if carry is None:
        return run(body, list(in_specs), list(out_specs), list(out_shape), list(scratch_shapes), sem, args), None
    ex, arrays = carry
    n_in, n_out, n_scr, n = len(in_specs), len(out_specs), len(scratch_shapes), ex.n
    n_pre = 0 if prefetch is None else 1
    total = 1
    for g in grid:
        total *= g
    relay_step = (total * relay_at[0]) // relay_at[1]

    def hosted(*refs):
        pre, refs = refs[:n_pre], refs[n_pre:]
        ins, cins = refs[:n_in], refs[n_in:n_in + n]
        outs, couts = refs[n_in + n:n_in + n + n_out], refs[n_in + n + n_out:n_in + 2 * n + n_out]
        scr = refs[n_in + 2 * n + n_out:n_in + 2 * n + n_out + n_scr]
        sems = refs[n_in + 2 * n + n_out + n_scr:]
        step = 0
        for ax, g in enumerate(grid):
            step = step * g + pl.program_id(ax)

        @pl.when(step == 0)
        def _():
            ex.start(cins, couts, *sems)

        body(*pre, *ins, *outs, *scr)

        @pl.when(step == relay_step)
        def _():
            ex.relay(cins, couts, *sems)

        @pl.when(step == total - 1)
        def _():
            ex.finish(cins, couts, *sems)

    any_spec = pl.BlockSpec(memory_space=pl.ANY)
    res = run(hosted, list(in_specs) + [any_spec] * n, list(out_specs) + [any_spec] * n, list(out_shape) + ex.out_shape,
              list(scratch_shapes) + ex.scratch, ("arbitrary",) * len(grid), (*args, *arrays))
    return res[:n_out], res[n_out:]


class _SiblingSwap:
    def __init__(self, arrays):
        self.n = len(arrays)
        self.out_shape = [SDS((a.shape[0], a.shape[1] // 2, a.shape[2]), a.dtype) for a in arrays]
        self.scratch = [pltpu.SemaphoreType.DMA((self.n,)), pltpu.SemaphoreType.DMA((self.n,))]

    def _copies(self, ins, outs, send_sems, recv_sems):
        x, y, c = _mesh_pos()
        cps = []
        for a in range(self.n):
            hr = ins[a].shape[1] // 2
            cps.append(pltpu.make_async_remote_copy(
                src_ref=ins[a].at[:, pl.ds((1 - c) * hr, hr)], dst_ref=outs[a],
                send_sem=send_sems.at[a], recv_sem=recv_sems.at[a], device_id=(x, y, 1 - c), device_id_type=MESH))
        return cps

    def start(self, *refs):
        for cp in self._copies(*refs):
            cp.start()

    def relay(self, *refs):
        pass

    def finish(self, *refs):
        for cp in self._copies(*refs):
            cp.wait()

    def alone(self, arrays, name):
        return _Exchange.alone(self, arrays, name)


def _divisor_block(rows, cands):
    for rb in cands:
        if rows % rb == 0:
            return rb
    raise ValueError(rows)


def _add_sibling_half(p, rb1, pos, name):
    nq, r, n = p.shape
    h = r // 2
    rb = _divisor_block(h, (512, 352, 256, 176, 128, 64, 32, 16))
    nb = h // rb

    def body(pos_ref, p_ref, r_ref, o_ref):
        o_ref[...] = (p_ref[...].astype(F32) + r_ref[...].astype(F32)).astype(o_ref.dtype)

    grid_spec = pltpu.PrefetchScalarGridSpec(
        num_scalar_prefetch=1, grid=(nq, nb),
        in_specs=[pl.BlockSpec((1, rb, n), lambda q, i, pos: (q, pos[0] * nb + i, 0)),
                  pl.BlockSpec((1, rb, n), lambda q, i, pos: (q, i, 0))],
        out_specs=pl.BlockSpec((1, rb, n), lambda q, i, pos: (q, i, 0)))
    return pl.pallas_call(
        body, name=name, grid_spec=grid_spec, out_shape=SDS((nq, h, n), p.dtype),
        compiler_params=_cparams("parallel", "parallel"),
    )(pos, p, rb1)


def _sum_devices(g):
    nd, m, n = g.shape

    def body(g_ref, o_ref):
        acc = g_ref[0]
        for d in range(1, nd):
            acc = acc + g_ref[d]
        o_ref[...] = acc

    return pl.pallas_call(body, name="small_grad_sum", out_shape=SDS((m, n), F32))(g)


def _ada_forward(c_all, ada_w, ada_b_cols, carry=None):
    nl, d, ncol = ada_w.shape
    nb = c_all.shape[0]
    tn = 512 if ncol % 512 == 0 else ncol

    def body(c_ref, w_ref, b_ref, o_ref):
        cv = c_ref[...]
        ca = cv * _sigmoid(cv)
        o_ref[0] = _dot(ca, w_ref[0]) + b_ref[0]

    return _call(
        body, c_all, ada_w, ada_b_cols, carry=carry, name="ada_forward", grid=(nl, ncol // tn),
        in_specs=[pl.BlockSpec((nb, d), lambda l, j: (0, 0)),
                  pl.BlockSpec((1, d, tn), lambda l, j: (l, 0, j)),
                  pl.BlockSpec((1, 1, tn), lambda l, j: (l, 0, j))],
        out_specs=[pl.BlockSpec((1, nb, tn), lambda l, j: (l, 0, j))],
        out_shape=[SDS((nl, nb, ncol), F32)], sem=("parallel", "parallel"))


def _ada_backward(c_all_t, dmod_cols):
    d, nb = c_all_t.shape
    nl, _, ncol = dmod_cols.shape
    tn = 512 if ncol % 512 == 0 else ncol

    def body(c_ref, g_ref, o_ref):
        cv = c_ref[...]
        ca = cv * _sigmoid(cv)
        o_ref[0] = _dot(ca, g_ref[0])

    return pl.pallas_call(
        body, name="ada_backward", grid=(nl, ncol // tn),
        in_specs=[pl.BlockSpec((d, nb), lambda l, j: (0, 0)),
                  pl.BlockSpec((1, nb, tn), lambda l, j: (l, 0, j))],
        out_specs=pl.BlockSpec((1, d, tn), lambda l, j: (l, 0, j)),
        out_shape=SDS((nl, d, ncol), F32), compiler_params=_cparams("parallel", "parallel"),
    )(c_all_t, dmod_cols)


def _conv_a_fwd(x, vec, w1, b1, carry=None):
    s, d = x.shape
    nq, _, hq = w1.shape
    tm = _row_block(s)

    def body(x_ref, vec_ref, w1_ref, b1_ref, h_ref, a_ref, g_ref, glu_ref):
        vec = vec_ref[...]
        _, _, n = _rms(x_ref[...], vec[0:1])
        hb = (n * (1.0 + vec[1:2]) + vec[2:3]).astype(_LOWP)
        h_ref[...] = hb
        u = [_dot(hb, w1_ref[q]) + b1_ref[:, q * hq:(q + 1) * hq] for q in range(nq)]
        for k in range(nq // 2):
            av, gv = u[k], u[nq // 2 + k]
            cols = slice(k * hq, (k + 1) * hq)
            a_ref[:, cols] = av.astype(_LOWP)
            g_ref[:, cols] = gv.astype(_LOWP)
            glu_ref[:, cols] = av * _sigmoid(gv)

    return _call(
        body, x, vec, w1, b1, carry=carry, name="conv_a_fwd", grid=(s // tm,),
        in_specs=[_rows_spec(tm, d), _const_spec(vec.shape), _const_spec(w1.shape), _const_spec(b1.shape)],
        out_specs=[_rows_spec(tm, d)] * 4,
        out_shape=[SDS((s, d), _LOWP)] * 3 + [SDS((s, d), F32)], sem=("parallel",))


def _conv_b_fwd(glu, w8, vec, w2, x, carry=None):
    s, d = x.shape
    tm = _row_block(s)
    hb = tm // CONV_HALO

    def body(glu_ref, halo_ref, w8_ref, vec_ref, w2_ref, x_ref, cw_ref, s_ref, y_ref, xo_ref, ext_ref):
        i = pl.program_id(0)
        vec = vec_ref[...]
        ext_ref[0:CONV_HALO, :] = jnp.where(i > 0, halo_ref[...], 0.0)
        ext_ref[CONV_HALO:, :] = glu_ref[...]

        def tile(r0, c0):
            lanes = slice(c0, c0 + LANES)
            win = ext_ref[pl.ds(r0, TILE_ROWS + CONV_HALO), lanes]
            acc = jnp.zeros((TILE_ROWS // SUBLANES, SUBLANES, LANES), F32)
            for b in range(SUBLANES):
                sh = _shift_up(win, b)
                for a in range(CONV_HALO // SUBLANES + 1):
                    k = SUBLANES * a + b - (CONV_HALO - CONV_WIDTH + 1)
                    if 0 <= k < CONV_WIDTH:
                        acc = acc + w8_ref[k, :, lanes][None] * _tiles3(sh[SUBLANES * a:SUBLANES * a + TILE_ROWS])
            cw_ref[pl.ds(r0, TILE_ROWS), lanes] = acc.reshape(TILE_ROWS, LANES)

        _for_tiles(tm, 0, d, tile)
        cw = cw_ref[...] + vec[0:1]
        cw_ref[...] = cw
        cc = cw - jnp.mean(cw, axis=-1, keepdims=True)
        ch = cc * lax.rsqrt(jnp.mean(cc * cc, axis=-1, keepdims=True) + EPS)
        lo = ch * vec[1:2] + vec[2:3]
        sv = (lo * _sigmoid(lo)).astype(_LOWP)
        s_ref[...] = sv
        y = _dot(sv, w2_ref[...]) + vec[3:4]
        y_ref[...] = y.astype(_LOWP)
        xo_ref[...] = x_ref[...] + (1.0 + vec[4:5]) * y

    return _call(
        body, glu, glu, w8, vec, w2, x, carry=carry, relay_at=LATE_RELAY_AT, name="conv_b_fwd", grid=(s // tm,),
        in_specs=[_rows_spec(tm, d),
                  pl.BlockSpec((CONV_HALO, d), lambda i: (jnp.maximum(i * hb - 1, 0), 0)),
                  _const_spec(w8.shape), _const_spec(vec.shape), _const_spec(w2.shape), _rows_spec(tm, d)],
        out_specs=[_rows_spec(tm, d)] * 4,
        out_shape=[SDS((s, d), F32), SDS((s, d), _LOWP), SDS((s, d), _LOWP), SDS((s, d), F32)],
        scratch_shapes=[pltpu.VMEM((tm + CONV_HALO, d), F32)], sem=("parallel",))


def _ffn_fwd(x, vec, wg, wu, wd, name, carry=None, head=None):
    s, d = x.shape
    nq, fq, _ = wg.shape
    tm = _row_block(s)
    nsteps = s // tm
    n_head = 0 if head is None else 2
    qps = FFN_FWD_QUARTERS_PER_STEP

    def body(x_ref, vec_ref, wg_hbm, wu_hbm, wd_hbm, *rest):
        head_refs, rest = rest[:n_head], rest[n_head:]
        h_ref, gt_ref, up_ref, act_ref, y_ref, xo_ref = rest[:6]
        sums_ref = rest[6] if head is not None else None
        hb_ref, yacc_ref, wg_ref, wu_ref, wd_ref, w_sems = rest[6 + n_head // 2:]
        i, p = pl.program_id(0), pl.program_id(1)
        vec = vec_ref[...]

        @pl.when((i == 0) & (p == 0))
        def _():
            _load_resident((wg_hbm, wu_hbm, wd_hbm), (wg_ref, wu_ref, wd_ref), w_sems)
            if head is not None:
                sums_ref[...] = jnp.zeros_like(sums_ref)

        @pl.when(p == 0)
        def _():
            _, _, n = _rms(x_ref[...], vec[0:1])
            hb = (n * (1.0 + vec[1:2]) + vec[2:3]).astype(_LOWP)
            hb_ref[...] = hb
            h_ref[...] = hb
            yacc_ref[...] = jnp.zeros_like(yacc_ref)

        hb = hb_ref[...]
        for k in range(qps):
            q = p * qps + k
            gt = _dot_nt(hb, wg_ref[q])
            up = _dot_nt(hb, wu_ref[q])
            act = (gt * _sigmoid(gt) * up).astype(_LOWP)
            gt_ref[k] = gt.astype(_LOWP)
            up_ref[k] = up.astype(_LOWP)
            act_ref[k] = act
            yacc_ref[...] += _dot(act, wd_ref[q])

        @pl.when(p == nq // qps - 1)
        def _():
            y = yacc_ref[...]
            y_ref[...] = y.astype(_LOWP)
            xo = x_ref[...] + (1.0 + vec[3:4]) * y
            if head is None:
                xo_ref[...] = xo
            else:
                t_ref, g_ref = head_refs
                gamma_v = g_ref[...]
                xh, r, out = _rms(xo, gamma_v)
                err = out - t_ref[...]
                dout = err * (1.0 / d)
                sums_ref[0:1, :] += _colsum(dout * xh)
                sums_ref[1:2, :] += _colsum(0.5 * err * dout)
                xo_ref[...] = _rms_bwd(dout, xh, r, gamma_v)

        if head is not None:
            @pl.when((i == nsteps - 1) & (p == nq // qps - 1))
            def _():
                sums_ref[1:2, :] = jnp.broadcast_to(jnp.sum(sums_ref[1:2, :], axis=-1, keepdims=True), (1, d))

    rows = pl.BlockSpec((tm, d), lambda i, q: (i, 0))
    hid = pl.BlockSpec((qps, tm, fq), lambda i, q: (q, i, 0))
    any_spec = pl.BlockSpec(memory_space=pl.ANY)
    head_args = [] if head is None else [head[0], head[1]]
    head_specs = [] if head is None else [rows, pl.BlockSpec((1, d), lambda i, q: (0, 0))]
    sums_specs = [] if head is None else [pl.BlockSpec((2, d), lambda i, q: (0, 0))]
    sums_shape = [] if head is None else [SDS((2, d), F32)]
    return _call(
        body, x, vec, wg, wu, wd, *head_args, carry=carry, name=name, grid=(nsteps, nq // qps),
        in_specs=[rows, pl.BlockSpec(vec.shape, lambda i, q: (0, 0)), any_spec, any_spec, any_spec] + head_specs,
        out_specs=[rows, hid, hid, hid, rows, rows] + sums_specs,
        out_shape=[SDS((s, d), _LOWP)] + [SDS((nq, s, fq), _LOWP)] * 3 + [SDS((s, d), _LOWP), SDS((s, d), F32)] + sums_shape,
        scratch_shapes=[pltpu.VMEM((tm, d), _LOWP), pltpu.VMEM((tm, d), F32)] + _resident_scratch((wg, wu, wd)),
        sem=("arbitrary", "arbitrary"))


def _pool_inv_count(t0, nrows, w):
    t = t0 + lax.broadcasted_iota(jnp.int32, (nrows, LANES), 0)
    return 1.0 / jnp.minimum(t + 1, w).astype(F32)


def _pool_fwd(x, vec, pw):
    s, d = x.shape
    ng, gd, _ = pw.shape
    tm = _row_block(s)
    hb = tm // POOL_HALO

    def body(x_ref, halo_ref, vec_ref, pw_ref, mixed_ref, yp_ref, xo_ref, hext_ref):
        i = pl.program_id(0)
        vec = vec_ref[...]

        def modulated(xv):
            _, _, n = _rms(xv, vec[0:1])
            return n * (1.0 + vec[1:2]) + vec[2:3]

        hext_ref[0:POOL_HALO, :] = jnp.where(i > 0, modulated(halo_ref[...]), 0.0)
        hext_ref[POOL_HALO:, :] = modulated(x_ref[...])

        for g in range(ng):
            def tile(r0, c0, g=g):
                lanes = slice(c0, c0 + LANES)
                win = hext_ref[pl.ds(r0, TILE_ROWS + POOL_HALO), lanes]
                acc = win
                for step in range(g + 1):
                    acc = acc + _shift_down(acc, 2 ** step)
                inv = _pool_inv_count(i * tm + r0, TILE_ROWS, POOL_WINDOWS[g])
                mixed = acc[POOL_HALO:] * inv - win[POOL_HALO:]
                mixed_ref[pl.ds(r0, TILE_ROWS), lanes] = mixed.astype(_LOWP)

            _for_tiles(tm, g * gd, gd, tile)

        for g in range(ng):
            cols = slice(g * gd, (g + 1) * gd)
            yp = _dot(mixed_ref[:, cols], pw_ref[g])
            yp_ref[:, cols] = yp.astype(_LOWP)
            xo_ref[:, cols] = x_ref[:, cols] + (1.0 + vec[3:4, cols]) * (yp * vec[4:5, cols])

    return pl.pallas_call(
        body, name="pool_fwd", grid=(s // tm,),
        in_specs=[_rows_spec(tm, d),
                  pl.BlockSpec((POOL_HALO, d), lambda i: (jnp.maximum(i * hb - 1, 0), 0)),
                  _const_spec(vec.shape), _const_spec(pw.shape)],
        out_specs=[_rows_spec(tm, d)] * 3,
        out_shape=[SDS((s, d), _LOWP), SDS((s, d), _LOWP), SDS((s, d), F32)],
        scratch_shapes=[pltpu.VMEM((tm + POOL_HALO, d), F32)],
        compiler_params=_cparams("parallel"),
    )(x, x, vec, pw)


def _ffn_bwd(dxo, x, vec, gt, up, y, wg, wu, wd, name, carry=None):
    s, d = x.shape
    nq, fq, _ = wg.shape
    tm = _row_block(s)
    qps = FFN_BWD_QUARTERS_PER_STEP

    def body(dxo_ref, x_ref, vec_ref, gt_ref, up_ref, y_ref, wg_hbm, wu_hbm, wd_hbm,
             dgt_ref, dup_ref, dy_ref, dx_ref, sums_ref, dyb_ref, dh_ref, wg_ref, wu_ref, wd_ref, w_sems):
        i, p = pl.program_id(0), pl.program_id(1)
        vec = vec_ref[...]

        @pl.when((i == 0) & (p == 0))
        def _():
            _load_resident((wg_hbm, wu_hbm, wd_hbm), (wg_ref, wu_ref, wd_ref), w_sems)
            sums_ref[...] = jnp.zeros_like(sums_ref)

        @pl.when(p == 0)
        def _():
            dxv = dxo_ref[...]
            dyb = (dxv * (1.0 + vec[3:4])).astype(_LOWP)
            dyb_ref[...] = dyb
            dy_ref[...] = dyb
            dh_ref[...] = jnp.zeros_like(dh_ref)
            sums_ref[3:4, :] += _colsum(dxv * y_ref[...].astype(F32))

        halves = [slice(k * (tm // 2), (k + 1) * (tm // 2)) for k in range(2)]
        for k in range(qps):
            q = p * qps + k
            dacts = [_dot_nt(dyb_ref[rs, :], wd_ref[q]) for rs in halves]
            dgus = []
            for rs, dact in zip(halves, dacts):
                gtv = gt_ref[k, rs, :].astype(F32)
                sg = _sigmoid(gtv)
                dgt = (dact * up_ref[k, rs, :].astype(F32) * (sg * (1.0 + gtv * (1.0 - sg)))).astype(_LOWP)
                dup = (dact * (gtv * sg)).astype(_LOWP)
                dgt_ref[k, rs, :] = dgt
                dup_ref[k, rs, :] = dup
                dgus.append((dgt, dup))
            for rs, (dgt, dup) in zip(halves, dgus):
                dh_ref[rs, :] += _dot(dgt, wg_ref[q]) + _dot(dup, wu_ref[q])

        @pl.when(p == nq // qps - 1)
        def _():
            dh = dh_ref[...]
            xh, r, n = _rms(x_ref[...], vec[0:1])
            sums_ref[2:3, :] += _colsum(dh)
            sums_ref[1:2, :] += _colsum(dh * n)
            dn = dh * (1.0 + vec[1:2])
            sums_ref[0:1, :] += _colsum(dn * xh)
            dx_ref[...] = dxo_ref[...] + _rms_bwd(dn, xh, r, vec[0:1])

    rows = pl.BlockSpec((tm, d), lambda i, q: (i, 0))
    hid = pl.BlockSpec((qps, tm, fq), lambda i, q: (q, i, 0))
    return _call(
        body, dxo, x, vec, gt, up, y, wg, wu, wd, carry=carry, name=name, grid=(s // tm, nq // qps),
        in_specs=[rows, rows, pl.BlockSpec(vec.shape, lambda i, q: (0, 0)), hid, hid, rows] +
                 [pl.BlockSpec(memory_space=pl.ANY)] * 3,
        out_specs=[hid, hid, rows, rows, pl.BlockSpec((4, d), lambda i, q: (0, 0))],
        out_shape=[SDS((nq, s, fq), _LOWP)] * 2 + [SDS((s, d), _LOWP), SDS((s, d), F32), SDS((4, d), F32)],
        scratch_shapes=[pltpu.VMEM((tm, d), _LOWP), pltpu.VMEM((tm, d), F32)] + _resident_scratch((wg, wu, wd)),
        sem=("arbitrary", "arbitrary"))


def _wgrad(a, b, a_spec, b_spec, nq, ka, nb, s, name, stack=None, stack_blocks=1, block=0):
    tk = min(WGRAD_ROWS, s)
    nk = s // tk
    total, depth = nq * nk, WGRAD_RING

    def body(a_hbm, b_hbm, *rest):
        o_ref, acc_ref, a_buf, b_buf, sems = rest[-5:]
        q, k = pl.program_id(0), pl.program_id(1)
        t = q * nk + k

        def copies(tt, slot):
            qq, kk = tt // nk, tt % nk
            return (pltpu.make_async_copy(a_spec(a_hbm, qq, kk, tk), a_buf.at[slot], sems.at[0, slot]),
                    pltpu.make_async_copy(b_spec(b_hbm, qq, kk, tk), b_buf.at[slot], sems.at[1, slot]))

        @pl.when(t == 0)
        def _():
            for tt in range(min(depth - 1, total)):
                for cp in copies(tt, tt % depth):
                    cp.start()

        ahead = t + depth - 1

        @pl.when(ahead < total)
        def _():
            for cp in copies(ahead, ahead % depth):
                cp.start()

        slot = t % depth
        for cp in copies(t, slot):
            cp.wait()

        @pl.when(k == 0)
        def _():
            acc_ref[...] = jnp.zeros_like(acc_ref)

        acc_ref[...] += _dot_tn(a_buf[slot], b_buf[slot])

        @pl.when(k == nk - 1)
        def _():
            o_ref[0] = acc_ref[...].astype(o_ref.dtype)

    extra = [] if stack is None else [stack]
    any_spec = pl.BlockSpec(memory_space=pl.ANY)
    return pl.pallas_call(
        body, name=name, grid=(nq, nk),
        in_specs=[any_spec, any_spec] + [any_spec] * len(extra),
        out_specs=pl.BlockSpec((1, ka, nb), lambda q, k: (q, block, 0)),
        out_shape=SDS((nq, stack_blocks * ka, nb), _LOWP),
        scratch_shapes=[pltpu.VMEM((ka, nb), F32), pltpu.VMEM((depth, tk, ka), a.dtype), pltpu.VMEM((depth, tk, nb), b.dtype),
                        pltpu.SemaphoreType.DMA((2, depth))],
        input_output_aliases={2: 0} if extra else {},
        compiler_params=_cparams("arbitrary", "arbitrary"),
    )(a, b, *extra)


def _row_window(k, tk):
    return pl.ds(pl.multiple_of(k * tk, tk), tk)


def _shared_rows(ncols):
    return lambda ref, q, k, tk: ref.at[_row_window(k, tk), :]


def _column_block(ncols):
    return lambda ref, q, k, tk: ref.at[_row_window(k, tk), pl.ds(pl.multiple_of(q * ncols, ncols), ncols)]


def _quarter_major(ncols):
    return lambda ref, q, k, tk: ref.at[q, _row_window(k, tk), :]


def _pool_bwd(dxo, x, yp, vec, pw, carry=None):
    s, d = x.shape
    ng, gd, _ = pw.shape
    tm = _row_block(s)
    hb = tm // POOL_HALO
    nsteps = s // tm
    last_halo = s // POOL_HALO - 1

    def body(dxo_ref, halo_ref, x_ref, yp_ref, vec_ref, pw_ref, dyp_ref, dx_ref, sums_ref, eext_ref, dh_ref):
        i = pl.program_id(0)
        vec = vec_ref[...]

        @pl.when(i == 0)
        def _():
            sums_ref[...] = jnp.zeros_like(sums_ref)

        dxv = dxo_ref[...]
        ypv = yp_ref[...].astype(F32)
        gate1 = 1.0 + vec[3:4]
        sums_ref[3:4, :] += _colsum(dxv * (ypv * vec[4:5]))
        sums_ref[4:5, :] += _colsum(dxv * gate1 * ypv)
        dyp = (dxv * gate1 * vec[4:5]).astype(_LOWP)
        dyp_ref[...] = dyp
        dyp_halo = (jnp.where(i < nsteps - 1, halo_ref[...], 0.0) * gate1 * vec[4:5]).astype(_LOWP)
        for g in range(ng):
            cols = slice(g * gd, (g + 1) * gd)
            dm = _dot_nt(dyp[:, cols], pw_ref[g])
            dh_ref[:, cols] = dm
            inv = _pool_inv_count(i * tm, tm, POOL_WINDOWS[g])
            eext_ref[0:tm, cols] = dm * jnp.concatenate([inv] * (gd // LANES), axis=1)
            eext_ref[tm:, cols] = _dot_nt(dyp_halo[:, cols], pw_ref[g]) * (1.0 / POOL_WINDOWS[g])

        for g in range(ng):
            def tile(r0, c0, g=g):
                acc = eext_ref[pl.ds(r0, TILE_ROWS + POOL_HALO), c0:c0 + LANES]
                for step in range(g + 1):
                    acc = acc + _shift_up(acc, 2 ** step)
                here = (pl.ds(r0, TILE_ROWS), slice(c0, c0 + LANES))
                dh_ref[here] = acc[:TILE_ROWS] - dh_ref[here]

            _for_tiles(tm, g * gd, gd, tile)

        dh = dh_ref[...]
        xh, r, n = _rms(x_ref[...], vec[0:1])
        sums_ref[2:3, :] += _colsum(dh)
        sums_ref[1:2, :] += _colsum(dh * n)
        dn = dh * (1.0 + vec[1:2])
        sums_ref[0:1, :] += _colsum(dn * xh)
        dx_ref[...] = dxv + _rms_bwd(dn, xh, r, vec[0:1])

    return _call(
        body, dxo, dxo, x, yp, vec, pw, carry=carry, name="pool_bwd", grid=(nsteps,),
        in_specs=[_rows_spec(tm, d),
                  pl.BlockSpec((POOL_HALO, d), lambda i: (jnp.minimum((i + 1) * hb, last_halo), 0)),
                  _rows_spec(tm, d), _rows_spec(tm, d), _const_spec(vec.shape), _const_spec(pw.shape)],
        out_specs=[_rows_spec(tm, d), _rows_spec(tm, d), _const_spec((5, d))],
        out_shape=[SDS((s, d), _LOWP), SDS((s, d), F32), SDS((5, d), F32)],
        scratch_shapes=[pltpu.VMEM((tm + POOL_HALO, d), F32), pltpu.VMEM((tm, d), F32)], sem=("arbitrary",))


def _conv_b_bwd(dxo, cw, y, vec, w2, carry=None):
    s, d = dxo.shape
    tm = _row_block(s)

    def body(dxo_ref, cw_ref, y_ref, vec_ref, w2_ref, dy_ref, dcw_ref, sums_ref):
        i = pl.program_id(0)
        vec = vec_ref[...]

        @pl.when(i == 0)
        def _():
            sums_ref[...] = jnp.zeros_like(sums_ref)

        dxv = dxo_ref[...]
        dy = dxv * (1.0 + vec[4:5])
        sums_ref[0:1, :] += _colsum(dxv * y_ref[...].astype(F32))
        sums_ref[1:2, :] += _colsum(dy)
        dyb = dy.astype(_LOWP)
        dy_ref[...] = dyb
        ds = _dot_nt(dyb, w2_ref[...])
        cw = cw_ref[...]
        cc = cw - jnp.mean(cw, axis=-1, keepdims=True)
        rs = lax.rsqrt(jnp.mean(cc * cc, axis=-1, keepdims=True) + EPS)
        ch = cc * rs
        lo = ch * vec[1:2] + vec[2:3]
        dlo = ds * _dsilu(lo)
        sums_ref[2:3, :] += _colsum(dlo * ch)
        sums_ref[3:4, :] += _colsum(dlo)
        dch = dlo * vec[1:2]
        dcw = rs * (dch - jnp.mean(dch, axis=-1, keepdims=True) - ch * jnp.mean(dch * ch, axis=-1, keepdims=True))
        sums_ref[4:5, :] += _colsum(dcw)
        dcw_ref[...] = dcw

    return _call(
        body, dxo, cw, y, vec, w2, carry=carry, name="conv_b_bwd", grid=(s // tm,),
        in_specs=[_rows_spec(tm, d), _rows_spec(tm, d), _rows_spec(tm, d), _const_spec(vec.shape), _const_spec(w2.shape)],
        out_specs=[_rows_spec(tm, d), _rows_spec(tm, d), _const_spec((5, d))],
        out_shape=[SDS((s, d), _LOWP), SDS((s, d), F32), SDS((5, d), F32)], sem=("arbitrary",))


def _conv_a_bwd(dcw, glu, a, g, x, dxo, vec, w8, w1, carry=None):
    s, d = x.shape
    nq, _, hq = w1.shape
    tm = _row_block(s)
    hb = tm // CONV_HALO
    nsteps = s // tm
    last_halo = s // CONV_HALO - 1
    lead = CONV_HALO - CONV_WIDTH + 1

    def body(dcw_ref, dhalo_ref, glu_ref, ghalo_ref, a_ref, g_ref, x_ref, dxo_ref, vec_ref, w8_ref, w1_ref,
             du_ref, dx_ref, sums_ref, db1_ref, dwdw_ref, dext_ref, gext_ref, dglu_ref, wacc_ref):
        i = pl.program_id(0)
        vec = vec_ref[...]

        @pl.when(i == 0)
        def _():
            sums_ref[...] = jnp.zeros_like(sums_ref)
            db1_ref[...] = jnp.zeros_like(db1_ref)
            wacc_ref[...] = jnp.zeros_like(wacc_ref)

        dext_ref[0:tm, :] = dcw_ref[...]
        dext_ref[tm:, :] = jnp.where(i < nsteps - 1, dhalo_ref[...], 0.0)
        gext_ref[0:CONV_HALO, :] = jnp.where(i > 0, ghalo_ref[...], 0.0)
        gext_ref[CONV_HALO:, :] = glu_ref[...]

        def tile(r0, c0):
            lanes = slice(c0, c0 + LANES)
            dwin = dext_ref[pl.ds(r0, TILE_ROWS + CONV_HALO), lanes]
            acc = jnp.zeros((TILE_ROWS // SUBLANES, SUBLANES, LANES), F32)
            for b in range(SUBLANES):
                sh = _shift_up(dwin, b)
                for a8 in range(CONV_HALO // SUBLANES):
                    k = CONV_WIDTH - 1 - (SUBLANES * a8 + b)
                    if 0 <= k < CONV_WIDTH:
                        acc = acc + w8_ref[k, :, lanes][None] * _tiles3(sh[SUBLANES * a8:SUBLANES * a8 + TILE_ROWS])
            dglu_ref[pl.ds(r0, TILE_ROWS), lanes] = acc.reshape(TILE_ROWS, LANES)

            dc3 = _tiles3(dwin[:TILE_ROWS])
            gwin = gext_ref[pl.ds(r0, TILE_ROWS + CONV_HALO), lanes]
            for b in range(SUBLANES):
                sh = _shift_up(gwin, b)
                for a8 in range(CONV_HALO // SUBLANES + 1):
                    k = SUBLANES * a8 + b - lead
                    if 0 <= k < CONV_WIDTH:
                        part = jnp.sum(dc3 * _tiles3(sh[SUBLANES * a8:SUBLANES * a8 + TILE_ROWS]), axis=0)
                        wacc_ref[k, :, lanes] += part

        _for_tiles(tm, 0, d, tile)

        dglu = dglu_ref[...]
        av, gv = a_ref[...].astype(F32), g_ref[...].astype(F32)
        sg = _sigmoid(gv)
        da = dglu * sg
        dg = dglu * av * sg * (1.0 - sg)
        du_ref[:, 0:d] = da.astype(_LOWP)
        du_ref[:, d:2 * d] = dg.astype(_LOWP)
        db1_ref[:, 0:d] += _colsum(da)
        db1_ref[:, d:2 * d] += _colsum(dg)
        dh = jnp.zeros((tm, d), F32)
        for q in range(nq):
            dh = dh + _dot_nt(du_ref[:, q * hq:(q + 1) * hq], w1_ref[q])
        xh, r, n = _rms(x_ref[...], vec[0:1])
        sums_ref[2:3, :] += _colsum(dh)
        sums_ref[1:2, :] += _colsum(dh * n)
        dn = dh * (1.0 + vec[1:2])
        sums_ref[0:1, :] += _colsum(dn * xh)
        dx_ref[...] = dxo_ref[...] + _rms_bwd(dn, xh, r, vec[0:1])

        @pl.when(i == nsteps - 1)
        def _():
            dwdw_ref[...] = jnp.sum(wacc_ref[...], axis=1)

    return _call(
        body, dcw, dcw, glu, glu, a, g, x, dxo, vec, w8, w1, carry=carry, name="conv_a_bwd", grid=(nsteps,),
        in_specs=[_rows_spec(tm, d),
                  pl.BlockSpec((CONV_HALO, d), lambda i: (jnp.minimum((i + 1) * hb, last_halo), 0)),
                  _rows_spec(tm, d),
                  pl.BlockSpec((CONV_HALO, d), lambda i: (jnp.maximum(i * hb - 1, 0), 0)),
                  _rows_spec(tm, d), _rows_spec(tm, d), _rows_spec(tm, d), _rows_spec(tm, d),
                  _const_spec(vec.shape), _const_spec(w8.shape), _const_spec(w1.shape)],
        out_specs=[_rows_spec(tm, 2 * d), _rows_spec(tm, d), _const_spec((3, d)), _const_spec((1, 2 * d)),
                   _const_spec((CONV_WIDTH, d))],
        out_shape=[SDS((s, 2 * d), _LOWP), SDS((s, d), F32), SDS((3, d), F32), SDS((1, 2 * d), F32),
                   SDS((CONV_WIDTH, d), F32)],
        scratch_shapes=[pltpu.VMEM((tm + CONV_HALO, d), F32), pltpu.VMEM((tm + CONV_HALO, d), F32),
                        pltpu.VMEM((tm, d), F32), pltpu.VMEM((CONV_WIDTH, SUBLANES, d), F32)],
        sem=("arbitrary",))


def _adamw(w, g, m, v, name):
    r, c = w.shape
    rb = r
    for cand in (512, 352, 256, 184, 128, 64, 32, 16, 8):
        if r % cand == 0 and cand * c * 4 <= ADAM_BLOCK_BYTES:
            rb = cand
            break

    def body(w_ref, g_ref, m_ref, v_ref, d_ref, mo_ref, vo_ref):
        gv = g_ref[...]
        mn = ADAM_B1 * m_ref[...] + (1.0 - ADAM_B1) * gv
        vn = ADAM_B2 * v_ref[...] + (1.0 - ADAM_B2) * (gv * gv)
        m_hat = mn / (1.0 - ADAM_B1 ** ADAM_STEP)
        v_hat = vn / (1.0 - ADAM_B2 ** ADAM_STEP)
        d_ref[...] = -ADAM_LR * (m_hat / (jnp.sqrt(v_hat) + ADAM_EPS) + ADAM_WD * w_ref[...])
        mo_ref[...] = mn
        vo_ref[...] = vn

    spec = pl.BlockSpec((rb, c), lambda i: (i, 0))
    return pl.pallas_call(
        body, name=name, grid=(r // rb,), in_specs=[spec] * 4, out_specs=[spec] * 3,
        out_shape=[SDS((r, c), F32)] * 3, compiler_params=_cparams("parallel"),
    )(w, g, m, v)


def _adamw_nd(w, g, m, v, name):
    shape = w.shape
    two_d = (-1, shape[-1])
    outs = _adamw(w.reshape(two_d), g.reshape(two_d), m.reshape(two_d), v.reshape(two_d), name)
    return [o.reshape(shape) for o in outs]


def _adamw_from_slots(w, m, v, segs, pos, name, carry=None):
    r, c_ = w.shape
    rb = _divisor_block(r, (256, 176, 128, 64, 32, 16))
    for slots, _, first, rows in segs:
        half = slots.shape[1] // 2
        while first % rb or rows % rb or half % rb:
            rb //= 2
    assert rb % 16 == 0, (name, rb)
    plan, start = [], 0
    for slots, _, first, rows in segs:
        plan.append((start, rows // rb, first // rb, slots.shape[1] // 2 // rb))
        start += rows // rb
    assert start * rb == r, (name, start, rb, r)

    def body(pos_ref, w_ref, m_ref, v_ref, *rest):
        seg_refs, (g_ref, d_ref, mo_ref, vo_ref) = rest[:2 * len(segs)], rest[2 * len(segs):]
        i = pl.program_id(0)
        for k, (s0, nblk, b0, nbh) in enumerate(plan):
            @pl.when((i >= s0) & (i < s0 + nblk))
            def _(k=k, s0=s0, b0=b0, nbh=nbh):
                slots = seg_refs[2 * k][...].astype(F32)
                blk = b0 + i - s0
                in_my_half = (blk >= pos_ref[0] * nbh) & (blk < (pos_ref[0] + 1) * nbh)
                own = jnp.where(in_my_half, seg_refs[2 * k + 1][0].astype(F32), slots[3])
                gv = ((own + slots[0]) + slots[1]) + slots[2]
                mn = ADAM_B1 * m_ref[...] + (1.0 - ADAM_B1) * gv
                vn = ADAM_B2 * v_ref[...] + (1.0 - ADAM_B2) * (gv * gv)
                m_hat = mn / (1.0 - ADAM_B1 ** ADAM_STEP)
                v_hat = vn / (1.0 - ADAM_B2 ** ADAM_STEP)
                g_ref[...] = gv
                d_ref[...] = -ADAM_LR * (m_hat / (jnp.sqrt(v_hat) + ADAM_EPS) + ADAM_WD * w_ref[...])
                mo_ref[...] = mn
                vo_ref[...] = vn

    spec = pl.BlockSpec((rb, c_), lambda i, pos: (i, 0))
    seg_specs, seg_args = [], []
    for (slots, mine, _, _), (s0, nblk, b0, nbh) in zip(segs, plan):
        def in_seg(i, s0=s0, nblk=nblk):
            return jnp.clip(i - s0, 0, nblk - 1)
        seg_specs.append(pl.BlockSpec((N_CHIPS, rb, c_), lambda i, pos, f=in_seg, b0=b0: (0, b0 + f(i), 0)))
        seg_specs.append(pl.BlockSpec(
            (1, rb, c_), lambda i, pos, f=in_seg, b0=b0, nbh=nbh: (pos[1], jnp.clip(b0 + f(i) - pos[0] * nbh, 0, nbh - 1), 0)))
        seg_args += [slots, mine]
    return _call(body, w, m, v, *seg_args, carry=carry, prefetch=pos, name=name, grid=(r // rb,),
                 in_specs=[spec] * 3 + seg_specs, out_specs=[spec] * 4, out_shape=[SDS((r, c_), F32)] * 4,
                 sem=("arbitrary",))


def _rows_of(*vs):
    return jnp.concatenate([v.reshape(-1, v.shape[-1]) for v in vs], axis=0)


def kernel(x, c, ada_w, ada_b, norm_mix_g, norm_ffn_g, conv_w1, conv_b1, conv_wdw, conv_bdw, conv_ln_g, conv_ln_b, conv_w2, conv_b2, pool_w, pool_ls, ffn_w_gate, ffn_w_up, ffn_w_down, final_g, loss_target, m_ada_w, m_ada_b, m_norm_mix_g, m_norm_ffn_g, m_conv_w1, m_conv_b1, m_conv_wdw, m_conv_bdw, m_conv_ln_g, m_conv_ln_b, m_conv_w2, m_conv_b2, m_pool_w, m_pool_ls, m_ffn_w_gate, m_ffn_w_up, m_ffn_w_down, m_final_g, v_ada_w, v_ada_b, v_norm_mix_g, v_norm_ffn_g, v_conv_w1, v_conv_b1, v_conv_wdw, v_conv_bdw, v_conv_ln_g, v_conv_ln_b, v_conv_w2, v_conv_b2, v_pool_w, v_pool_ls, v_ffn_w_gate, v_ffn_w_up, v_ffn_w_down, v_final_g):
    s, d = x.shape[1], x.shape[2]
    depth = ada_w.shape[0]
    assert depth == 2 and conv_w1.shape[0] == 1 and pool_w.shape[0] == 1, "one conv layer then one pool layer"
    fq = ffn_w_gate.shape[2]
    n_ada = ada_b.shape[1] // d
    ada_cols = ada_w.shape[2]
    ng, gq, gd = pool_w.shape[1], pool_w.shape[2], pool_w.shape[3]
    cq = conv_wdw.shape[2]
    ax, ay, ac = _mesh_pos()
    myq = 2 * ax + ay
    me = 4 * ax + 2 * ay + ac
    x0 = x.reshape(s, d)
    tgt = loss_target.reshape(s, d)

    n_c = d // LANES
    small_in = jnp.concatenate(
        [c.reshape(n_c, LANES), conv_wdw[0].reshape(-1, LANES), pool_ls.reshape(-1, LANES)], axis=0)
    small_in_all = _all_gather8(small_in, "gather_c_and_small_shards").reshape(N_DEV, -1, LANES)
    c_all = small_in_all[:, :n_c].reshape(N_DEV, d)
    ada_b_cols = lax.dynamic_slice_in_dim(ada_b, myq * ada_cols, ada_cols, axis=1).reshape(depth, 1, ada_cols)
    w1_shard, w2_shard = [conv_w1[0].astype(_LOWP)], [conv_w2[0].astype(_LOWP)]
    (mod_part,), w1_all = _ada_forward(c_all, ada_w, ada_b_cols, carry=(_Exchange("gather", w1_shard), w1_shard))
    mod_all = _all_gather8(mod_part.reshape(-1, LANES), "gather_mod").reshape(N_DEV, depth, N_DEV, ada_cols)
    mod_mine = lax.dynamic_index_in_dim(mod_all[0::2], me, axis=2, keepdims=False)
    mod = jnp.transpose(mod_mine, (1, 0, 2)).reshape(depth, n_ada, d)

    def gathered(ex_out, shards):
        return [lax.dynamic_update_slice(g, sh[None], (myq, 0, 0)) for g, sh in zip(ex_out, shards)]

    pos = jnp.stack([ac, myq]).astype(jnp.int32)
    gate_t, up_t = jnp.swapaxes(ffn_w_gate, 1, 2), jnp.swapaxes(ffn_w_up, 1, 2)
    ffn0_shards = [gate_t[0].astype(_LOWP), up_t[0].astype(_LOWP), ffn_w_down[0].astype(_LOWP)]
    l1_shards = [pool_w[0].reshape(ng * gq, gd).astype(_LOWP), gate_t[1].astype(_LOWP), up_t[1].astype(_LOWP),
                 ffn_w_down[1].astype(_LOWP)]
    (w1,) = gathered(w1_all, w1_shard)
    wdw_full_shards = small_in_all[0::2, n_c:].reshape(N_CHIPS, CONV_WIDTH + 1, cq)
    wdw = jnp.transpose(wdw_full_shards[:, :CONV_WIDTH], (1, 0, 2)).reshape(CONV_WIDTH, d)
    pls = wdw_full_shards[:, CONV_WIDTH].reshape(1, d)
    w8 = jnp.broadcast_to(wdw[:, None, :], (CONV_WIDTH, SUBLANES, d))

    vec_ca = _rows_of(norm_mix_g[0], mod[0, 1], mod[0, 0])
    (h0, a0, g0, glu), w2_all = _conv_a_fwd(x0, vec_ca, w1, conv_b1, carry=(_Exchange("gather", w2_shard), w2_shard))
    w2 = gathered(w2_all, w2_shard)[0].reshape(d, d)
    vec_cb = _rows_of(conv_bdw, conv_ln_g, conv_ln_b, conv_b2, mod[0, 2])
    (cw, s0, y0, x1), ffn0_w = _conv_b_fwd(glu, w8, vec_cb, w2, x0, carry=(_Exchange("gather", ffn0_shards), ffn0_shards))
    wg0, wu0, wd0 = gathered(ffn0_w, ffn0_shards)
    vec_f0 = _rows_of(norm_ffn_g[0], mod[0, 4], mod[0, 3], mod[0, 5])
    (h2a, gta, upa, acta, y2a, x2), l1_w = _ffn_fwd(x1, vec_f0, wg0, wu0, wd0, "ffn0_fwd",
                                                   carry=(_Exchange("gather", l1_shards), l1_shards))
    pw_all, wg1, wu1, wd1 = gathered(l1_w, l1_shards)
    pw = jnp.transpose(pw_all.reshape(N_CHIPS, ng, gq, gd), (1, 0, 2, 3)).reshape(ng, N_CHIPS * gq, gd)
    vec_p = _rows_of(norm_mix_g[1], mod[1, 1], mod[1, 0], mod[1, 2], pls)
    mixed, yp, x3 = _pool_fwd(x2, vec_p, pw)
    vec_f1 = _rows_of(norm_ffn_g[1], mod[1, 4], mod[1, 3], mod[1, 5])
    (h2b, gtb, upb, actb, y2b, dx4, fin_sums), _ = _ffn_fwd(x3, vec_f1, wg1, wu1, wd1, "ffn1_fwd_loss",
                                                           head=(tgt, final_g.reshape(1, d)))

    def add_halves(partials, swapped, tag):
        return [_add_sibling_half(p, r, pos, f"grad_add_{tag}{k}") for k, (p, r) in enumerate(zip(partials, swapped))]

    def ffn_wgrads(h2, dgt, dup, act, dy, tag):
        gate_up = _wgrad(dgt, h2, _quarter_major(fq), _shared_rows(d), N_CHIPS, fq, d, s, "wgrad_gate" + tag, stack_blocks=2)
        gate_up = _wgrad(dup, h2, _quarter_major(fq), _shared_rows(d), N_CHIPS, fq, d, s, "wgrad_up" + tag,
                         stack=gate_up, stack_blocks=2, block=1)
        down = _wgrad(act, dy, _quarter_major(fq), _shared_rows(d), N_CHIPS, fq, d, s, "wgrad_down" + tag)
        return [gate_up, down]

    (dgtb, dupb, dyb, dx3, sums_f1), _ = _ffn_bwd(dx4, x3, vec_f1, gtb, upb, y2b, wg1, wu1, wd1, "ffn1_bwd")
    part_l1 = ffn_wgrads(h2b, dgtb, dupb, actb, dyb, "1")
    (dyp, dx2, sums_p), swapped_l1 = _pool_bwd(dx3, x2, yp, vec_p, pw, carry=(_SiblingSwap(part_l1), part_l1))
    cs_l1 = add_halves(part_l1, swapped_l1, "l1")
    dpw = _wgrad(mixed, dyp, _column_block(gd), _column_block(gd), ng, gd, gd, s, "wgrad_pool")
    dpw_q = jnp.transpose(dpw.reshape(ng, N_CHIPS, gq, gd), (1, 0, 2, 3)).reshape(N_CHIPS, ng * gq, gd)
    (dgta, dupa, dya, dx1, sums_f0), slots_l1 = _ffn_bwd(dx2, x1, vec_f0, gta, upa, y2a, wg0, wu0, wd0, "ffn0_bwd",
                                                        carry=(_Exchange("reduce", cs_l1), cs_l1))
    part_l0 = ffn_wgrads(h2a, dgta, dupa, acta, dya, "0") + [dpw_q]
    (dy0, dcw, sums_cb), swapped_l0 = _conv_b_bwd(dx1, cw, y0, vec_cb, w2, carry=(_SiblingSwap(part_l0), part_l0))
    dw2 = _wgrad(s0, dy0, _column_block(d // N_CHIPS), _shared_rows(d), N_CHIPS, d // N_CHIPS, d, s, "wgrad_conv2")
    cs_l0 = add_halves(part_l0, swapped_l0, "l0") + add_halves([dw2], _SiblingSwap([dw2]).alone([dw2], "grad_swap_w2"), "w2")
    (du, dx0, sums_ca, db1, dwdw), slots_l0 = _conv_a_bwd(dcw, glu, a0, g0, x0, dx1, vec_ca, w8, w1,
                                                          carry=(_Exchange("reduce", cs_l0), cs_l0))
    dw1 = _wgrad(h0, du, _shared_rows(d), _column_block(2 * d // N_CHIPS), N_CHIPS, d, 2 * d // N_CHIPS, s, "wgrad_conv1")
    cs_conv = add_halves([dw1], _SiblingSwap([dw1]).alone([dw1], "grad_swap_w1"), "w1")

    dmod = jnp.stack([
        _rows_of(sums_ca[2], sums_ca[1], sums_cb[0], sums_f0[2], sums_f0[1], sums_f0[3]),
        _rows_of(sums_p[2], sums_p[1], sums_p[3], sums_f1[2], sums_f1[1], sums_f1[3])])
    small = _rows_of(dmod.reshape(-1, d), sums_ca[0], sums_p[0], sums_f0[0], sums_f1[0], db1.reshape(2, d),
                     sums_cb[4], sums_cb[2], sums_cb[3], sums_cb[1], fin_sums[0], sums_p[4], dwdw, fin_sums[1])
    n_small = small.shape[0]
    pad = (-n_small) % SUBLANES
    small = jnp.concatenate([small, jnp.zeros((pad, d), F32)], axis=0) if pad else small
    small_all = _all_gather8(small.reshape(-1, LANES), "gather_small_grads").reshape(N_DEV, -1, LANES)
    small_sum = _sum_devices(small_all).reshape(-1, d)
    n_dm = depth * n_ada
    g_ada_b = small_sum[0:n_dm].reshape(depth, n_ada * d)
    g_norm_mix = small_sum[n_dm:n_dm + 2]
    g_norm_ffn = small_sum[n_dm + 2:n_dm + 4]
    g_b1 = small_sum[n_dm + 4:n_dm + 6].reshape(1, 2 * d)
    g_bdw, g_lng, g_lnb, g_b2 = (small_sum[n_dm + 6 + k].reshape(1, d) for k in range(4))
    g_final = small_sum[n_dm + 10]
    g_pls = lax.dynamic_slice_in_dim(small_sum[n_dm + 11].reshape(1, d), myq * cq, cq, axis=1)
    g_wdw = lax.dynamic_slice_in_dim(small_sum[n_dm + 12:n_dm + 12 + CONV_WIDTH], myq * cq, cq, axis=1).reshape(conv_wdw.shape)
    loss = small_sum[n_dm + 12 + CONV_WIDTH, 0]
    dmod_all = small_all.reshape(N_DEV, -1, d)[:, 0:n_dm].reshape(N_DEV, depth, n_ada * d)
    dmod_cols = lax.dynamic_slice_in_dim(jnp.transpose(dmod_all, (1, 0, 2)), myq * ada_cols, ada_cols, axis=2)
    g_ada_w = _ada_backward(c_all.T, dmod_cols)

    def small_pack(*vs):
        return jnp.concatenate([v.reshape(-1) for v in vs]).reshape(-1, LANES)

    names = ("ada_b", "norm_mix_g", "norm_ffn_g", "conv_b1", "conv_bdw", "conv_ln_g", "conv_ln_b", "conv_b2", "final_g")
    small_w = (ada_b, norm_mix_g, norm_ffn_g, conv_b1, conv_bdw, conv_ln_g, conv_ln_b, conv_b2, final_g)
    small_g = (g_ada_b, g_norm_mix, g_norm_ffn, g_b1, g_bdw, g_lng, g_lnb, g_b2, g_final)
    small_m = (m_ada_b, m_norm_mix_g, m_norm_ffn_g, m_conv_b1, m_conv_bdw, m_conv_ln_g, m_conv_ln_b, m_conv_b2, m_final_g)
    small_v = (v_ada_b, v_norm_mix_g, v_norm_ffn_g, v_conv_b1, v_conv_bdw, v_conv_ln_g, v_conv_ln_b, v_conv_b2, v_final_g)
    sd, sm, sv = _adamw(small_pack(*small_w), small_pack(*small_g), small_pack(*small_m), small_pack(*small_v), "adamw_small")

    def unpack(flat2d):
        flat = flat2d.reshape(-1)
        out, off = {}, 0
        for nm, wv in zip(names, small_w):
            out[nm] = flat[off:off + wv.size].reshape(wv.shape)
            off += wv.size
        return out

    sd, sm, sv = unpack(sd), unpack(sm), unpack(sv)
    grads = {
        "ada_w": g_ada_w, "ada_b": g_ada_b, "norm_mix_g": g_norm_mix, "norm_ffn_g": g_norm_ffn,
        "conv_b1": g_b1, "conv_wdw": g_wdw, "conv_bdw": g_bdw, "conv_ln_g": g_lng,
        "conv_ln_b": g_lnb, "conv_b2": g_b2, "pool_ls": g_pls, "final_g": g_final,
    }
    plain = {
        "ada_w": (ada_w, m_ada_w, v_ada_w), "conv_wdw": (conv_wdw, m_conv_wdw, v_conv_wdw),
        "pool_ls": (pool_ls, m_pool_ls, v_pool_ls),
    }
    from_slots = {
        "ffn_w_gate": (ffn_w_gate, m_ffn_w_gate, v_ffn_w_gate,
                       lambda: [(slots_l0[0], cs_l0[0], 0, fq), (slots_l1[0], cs_l1[0], 0, fq)]),
        "ffn_w_up": (ffn_w_up, m_ffn_w_up, v_ffn_w_up,
                     lambda: [(slots_l0[0], cs_l0[0], fq, fq), (slots_l1[0], cs_l1[0], fq, fq)]),
        "ffn_w_down": (ffn_w_down, m_ffn_w_down, v_ffn_w_down,
                       lambda: [(slots_l0[1], cs_l0[1], 0, fq), (slots_l1[1], cs_l1[1], 0, fq)]),
        "pool_w": (pool_w, m_pool_w, v_pool_w, lambda: [(slots_l0[2], cs_l0[2], 0, ng * gq)]),
        "conv_w2": (conv_w2, m_conv_w2, v_conv_w2, lambda: [(slots_l0[3], cs_l0[3], 0, d // N_CHIPS)]),
        "conv_w1": (conv_w1, m_conv_w1, v_conv_w1, lambda: [(slots_conv[0], cs_conv[0], 0, d)]),
    }
    order = ("ada_w", "ada_b", "norm_mix_g", "norm_ffn_g", "conv_w1", "conv_b1", "conv_wdw", "conv_bdw", "conv_ln_g",
             "conv_ln_b", "conv_w2", "conv_b2", "pool_w", "pool_ls", "ffn_w_gate", "ffn_w_up", "ffn_w_down", "final_g")
    delta, new_m, new_v = {}, {}, {}
    transposed = ("ffn_w_gate", "ffn_w_up")
    for nm, (wv, mv, vv, segs) in from_slots.items():
        if nm in transposed:
            wv, mv, vv = (jnp.swapaxes(t, 1, 2) for t in (wv, mv, vv))
        two_d = (-1, wv.shape[-1])
        carry = (_Exchange("reduce", cs_conv), cs_conv) if nm == "ffn_w_gate" else None
        outs, brought = _adamw_from_slots(wv.reshape(two_d), mv.reshape(two_d), vv.reshape(two_d), segs(), pos,
                                          "adamw_" + nm, carry=carry)
        if carry is not None:
            slots_conv = brought
        outs = [o.reshape(wv.shape) for o in outs]
        if nm in transposed:
            outs = [jnp.swapaxes(o, 1, 2) for o in outs]
        grads[nm], delta[nm], new_m[nm], new_v[nm] = outs
    for nm in order:
        if nm in from_slots:
            continue
        elif nm in plain:
            wv, mv, vv = plain[nm]
            grads[nm] = grads[nm].reshape(wv.shape)
            delta[nm], new_m[nm], new_v[nm] = _adamw_nd(wv, grads[nm], mv, vv, "adamw_" + nm)
        else:
            delta[nm], new_m[nm], new_v[nm] = sd[nm], sm[nm], sv[nm]
            grads[nm] = grads[nm].reshape(dict(zip(names, small_w))[nm].shape)

    return (loss, dx0.reshape(x.shape), *[grads[n] for n in order], *[delta[n] for n in order],
            *[new_m[n] for n in order], *[new_v[n] for n in order])
```
